```python
import jax
import jax.numpy as jnp
from jax import lax
import numpy as np

D_MODEL = 1024
BATCH = 2
SEQ = 8192
DEPTH = 1

CONV_WIDTH = 512
CONV_K = 31
HGRN_HEADS = 4
HGRN_DK = 128
HGRN_DV = 128
HGRN_KW = HGRN_HEADS * HGRN_DK
HGRN_VW = HGRN_HEADS * HGRN_DV
CHUNK = 64
MEM_LEN = 256
XA_HEADS = 4
XA_HEAD_DIM = D_MODEL // XA_HEADS
N_GROUPS = 4
EXPERTS_PER_GROUP = 8
N_EXPERTS = N_GROUPS * EXPERTS_PER_GROUP
TOP_K = 2
D_FF = 512
EPS = 1e-6
ROUTER_BIAS_SCALE = 0.01

_COLS = (CONV_WIDTH, CONV_WIDTH, HGRN_KW, HGRN_KW, HGRN_VW, HGRN_VW, D_MODEL, D_MODEL)
IN_COLS = sum(_COLS)
SPLITS = [sum(_COLS[:i]) for i in range(1, len(_COLS))]

kernel_name = 'hybrid_conv_hgrn2_xattn_hmoe_layer'


def rms_norm(x, g):
    xf = x.astype(jnp.float32)
    y = xf * lax.rsqrt(jnp.mean(xf * xf, axis=-1, keepdims=True) + EPS)
    return (y * g.astype(jnp.float32)).astype(x.dtype)


def layer_norm(x, g, b):
    xf = x.astype(jnp.float32)
    mu = jnp.mean(xf, axis=-1, keepdims=True)
    var = jnp.mean(jnp.square(xf - mu), axis=-1, keepdims=True)
    y = (xf - mu) * lax.rsqrt(var + EPS)
    return (y * g.astype(jnp.float32) + b.astype(jnp.float32)).astype(x.dtype)


def causal_depthwise_conv(u, w, b):
    y = lax.conv_general_dilated(
        u, w[:, None, :].astype(u.dtype), window_strides=(1,),
        padding=[(CONV_K - 1, 0)], dimension_numbers=('NWC', 'WIO', 'NWC'),
        feature_group_count=u.shape[-1])
    return y + b.astype(u.dtype)


def conformer_conv_branch(a, gate, w_dw, b_dw, ln_g, ln_b, w_proj):
    u = a * jax.nn.sigmoid(gate)
    u = causal_depthwise_conv(u, w_dw, b_dw)
    u = jax.nn.silu(layer_norm(u, ln_g, ln_b))
    return u @ w_proj


def hgrn2_chunkwise(q, k, v, logf):
    b, s, h, dk = q.shape
    dv = v.shape[-1]
    nc = s // CHUNK

    def to_chunks(t):
        return t.reshape(b, nc, CHUNK, h, t.shape[-1]).transpose(1, 0, 3, 2, 4)

    causal = jnp.tril(jnp.ones((CHUNK, CHUNK), dtype=bool))[None, None, :, :, None]

    def step(state, inp):
        qb, kb, vb, fb = inp
        cum = jnp.cumsum(fb, axis=2)
        o_inter = jnp.einsum('bhtk,bhkv->bhtv', qb * jnp.exp(cum), state)
        diff = cum[:, :, :, None, :] - cum[:, :, None, :, :]
        decay = jnp.exp(jnp.where(causal, diff, -jnp.inf))
        scores = jnp.einsum('bhtk,bhtsk,bhsk->bhts', qb, decay, kb)
        o = o_inter + jnp.einsum('bhts,bhsv->bhtv', scores, vb)
        last = cum[:, :, -1:, :]
        state = (jnp.exp(last[:, :, 0, :])[..., None] * state
                 + jnp.einsum('bhsk,bhsv->bhkv', kb * jnp.exp(last - cum), vb))
        return state, o

    s0 = jnp.zeros((b, h, dk, dv), jnp.float32)
    _, o = lax.scan(step, s0, (to_chunks(q), to_chunks(k), to_chunks(v), to_chunks(logf)))
    return o.transpose(1, 0, 3, 2, 4).reshape(b, s, h, dv)


def hgrn2_branch(q_raw, f_raw, i_raw, g_raw, lower_bound, onorm_g, w_o):
    b, s, _ = q_raw.shape

    def heads(t, d):
        return t.reshape(b, s, HGRN_HEADS, d)

    f = lower_bound + (1.0 - lower_bound) * jax.nn.sigmoid(f_raw.astype(jnp.float32))
    q = jax.nn.silu(q_raw.astype(jnp.float32))
    o = hgrn2_chunkwise(heads(q, HGRN_DK), heads(1.0 - f, HGRN_DK),
                        heads(i_raw.astype(jnp.float32), HGRN_DV), heads(jnp.log(f), HGRN_DK))
    o = rms_norm(o, onorm_g) * jax.nn.silu(heads(g_raw.astype(jnp.float32), HGRN_DV))
    return o.reshape(b, s, HGRN_VW).astype(q_raw.dtype) @ w_o


def memory_cross_attention(h, mem_n, w_q, w_k, w_v, w_o):
    b, s, d = h.shape
    m = mem_n.shape[1]
    q = (h @ w_q).reshape(b, s, XA_HEADS, XA_HEAD_DIM)
    k = (mem_n @ w_k).reshape(b, m, XA_HEADS, XA_HEAD_DIM)
    v = (mem_n @ w_v).reshape(b, m, XA_HEADS, XA_HEAD_DIM)
    scores = jnp.einsum('bshd,bmhd->bhsm', q, k).astype(jnp.float32) * (XA_HEAD_DIM ** -0.5)
    p = jax.nn.softmax(scores, axis=-1).astype(v.dtype)
    o = jnp.einsum('bhsm,bmhd->bshd', p, v).reshape(b, s, d)
    return o @ w_o


def hierarchical_moe(h, w_group, b_group, w_expert, b_expert, w_gate, w_up, w_down):
    b, s, d = h.shape
    t = h.reshape(b * s, d)
    n = t.shape[0]
    group_probs = jax.nn.softmax((t @ w_group).astype(jnp.float32) + b_group.astype(jnp.float32), axis=-1)
    g_p, g_idx = lax.top_k(group_probs, 1)
    e_logits = ((t @ w_expert).astype(jnp.float32) + b_expert.astype(jnp.float32))
    e_logits = e_logits.reshape(n, N_GROUPS, EXPERTS_PER_GROUP)
    in_group = jnp.take_along_axis(e_logits, g_idx[:, :, None], axis=1)[:, 0]
    e_p, e_idx = lax.top_k(jax.nn.softmax(in_group, axis=-1), TOP_K)
    e_p = e_p / jnp.sum(e_p, axis=-1, keepdims=True)
    weights = g_p * e_p
    expert_ids = g_idx * EXPERTS_PER_GROUP + e_idx
    combine = jnp.zeros((n, N_EXPERTS), jnp.float32).at[
        jnp.arange(n)[:, None], expert_ids].add(weights).astype(t.dtype)
    y = jnp.zeros_like(t)
    for e in range(N_EXPERTS):
        hid = jax.nn.silu(t @ w_gate[e]) * (t @ w_up[e])
        y = y + combine[:, e:e + 1] * (hid @ w_down[e])
    return y.reshape(b, s, d)


def setup_inputs(seed: int = 0) -> dict:
    key = jax.random.key(seed)
    ks = iter(jax.random.split(key, 32))

    def nrm(shape, fan_in):
        return jax.random.normal(next(ks), shape, jnp.float32) * (fan_in ** -0.5)

    def gain(shape):
        return 1.0 + 0.02 * jax.random.normal(next(ks), shape, jnp.float32)

    def small(shape, scale):
        return scale * jax.random.normal(next(ks), shape, jnp.float32)

    L = DEPTH
    return {
        'x': jax.random.normal(next(ks), (BATCH, SEQ, D_MODEL), jnp.float32),
        'mem': jax.random.normal(next(ks), (BATCH, MEM_LEN, D_MODEL), jnp.float32),
        'norm_mix_g': gain((L, D_MODEL)),
        'w_in': nrm((L, D_MODEL, IN_COLS), D_MODEL),
        'conv_w': nrm((L, CONV_K, CONV_WIDTH), CONV_K),
        'conv_b': small((L, CONV_WIDTH), 0.02),
        'conv_ln_g': gain((L, CONV_WIDTH)),
        'conv_ln_b': small((L, CONV_WIDTH), 0.02),
        'conv_w_out': nrm((L, CONV_WIDTH, D_MODEL), CONV_WIDTH),
        'hgrn_lb_logits': small((L + 1, HGRN_KW), 0.5),
        'hgrn_onorm_g': gain((L, HGRN_DV)),
        'hgrn_w_out': nrm((L, HGRN_VW, D_MODEL), HGRN_VW),
        'w_mix_out': nrm((L, D_MODEL, D_MODEL), D_MODEL),
        'norm_xa_g': gain((L, D_MODEL)),
        'norm_mem_g': gain((L, D_MODEL)),
        'xa_w_q': nrm((L, D_MODEL, D_MODEL), D_MODEL),
        'xa_w_k': nrm((L, D_MODEL, D_MODEL), D_MODEL),
        'xa_w_v': nrm((L, D_MODEL, D_MODEL), D_MODEL),
        'xa_w_o': nrm((L, D_MODEL, D_MODEL), D_MODEL),
        'norm_ffn_g': gain((L, D_MODEL)),
        'router_group_w': nrm((L, D_MODEL, N_GROUPS), D_MODEL),
        'router_group_b': small((L, N_GROUPS), ROUTER_BIAS_SCALE),
        'router_expert_w': nrm((L, D_MODEL, N_EXPERTS), D_MODEL),
        'router_expert_b': small((L, N_EXPERTS), ROUTER_BIAS_SCALE),
        'moe_w_gate': nrm((L, N_EXPERTS, D_MODEL, D_FF), D_MODEL),
        'moe_w_up': nrm((L, N_EXPERTS, D_MODEL, D_FF), D_MODEL),
        'moe_w_down': nrm((L, N_EXPERTS, D_FF, D_MODEL), D_FF),
        'final_norm_g': gain((D_MODEL,)),
    }


def reference(x, mem, norm_mix_g, w_in, conv_w, conv_b, conv_ln_g, conv_ln_b, conv_w_out,
              hgrn_lb_logits, hgrn_onorm_g, hgrn_w_out, w_mix_out, norm_xa_g, norm_mem_g,
              xa_w_q, xa_w_k, xa_w_v, xa_w_o, norm_ffn_g, router_group_w, router_group_b,
              router_expert_w, router_expert_b, moe_w_gate, moe_w_up, moe_w_down, final_norm_g):
    lower_bounds = jnp.cumsum(jax.nn.softmax(hgrn_lb_logits.astype(jnp.float32), axis=0), axis=0)
    for l in range(DEPTH):
        h = rms_norm(x, norm_mix_g[l])
        proj = h @ w_in[l]
        c_a, c_g, q_r, f_r, i_r, g_r, gate_c, gate_r = jnp.split(proj, SPLITS, axis=-1)
        y_conv = conformer_conv_branch(c_a, c_g, conv_w[l], conv_b[l], conv_ln_g[l],
                                       conv_ln_b[l], conv_w_out[l])
        y_rec = hgrn2_branch(q_r, f_r, i_r, g_r, lower_bounds[l], hgrn_onorm_g[l], hgrn_w_out[l])
        merged = jax.nn.sigmoid(gate_c) * y_conv + jax.nn.sigmoid(gate_r) * y_rec
        x = x + merged @ w_mix_out[l]
        h = rms_norm(x, norm_xa_g[l])
        mem_n = rms_norm(mem, norm_mem_g[l])
        x = x + memory_cross_attention(h, mem_n, xa_w_q[l], xa_w_k[l], xa_w_v[l], xa_w_o[l])
        h = rms_norm(x, norm_ffn_g[l])
        x = x + hierarchical_moe(h, router_group_w[l], router_group_b[l], router_expert_w[l],
                                 router_expert_b[l], moe_w_gate[l], moe_w_up[l], moe_w_down[l])
    return rms_norm(x, final_norm_g)
```

```python
import functools

import jax
import jax.numpy as jnp
from jax import lax
from jax.experimental import pallas as pl
from jax.experimental.pallas import tpu as pltpu

F32 = jnp.float32
BF16 = jnp.bfloat16
I32 = jnp.int32

EPS = 1e-6
CONV_K = 31
CONV_HALO = 32
CONV_ROWS = 64
HGRN_HEADS = 4
HGRN_DIM = 128
CHUNK = 64
SUB = 16
XA_HEADS = 4
N_GROUPS = 4
EXPERTS_PER_GROUP = 8
N_EXPERTS = N_GROUPS * EXPERTS_PER_GROUP
TOP_K = 2
ROUTE_ROWS = 40
TOKEN_BLOCK = 512
MOE_TILE = 256
RANK_BLOCK = 1024
COMBINE_BLOCK = 256
VMEM_LIMIT_BYTES = 48 * 1024 * 1024


def _rms(x, g):
    return x * lax.rsqrt(jnp.mean(x * x, axis=-1, keepdims=True) + EPS) * g


def _dot(a, b):
    return jnp.dot(a, b, preferred_element_type=F32)


def _dot_nt(a, b):
    return lax.dot_general(a, b, (((1,), (1,)), ((), ())), preferred_element_type=F32)


def _dot_tn(a, b):
    return lax.dot_general(a, b, (((0,), (0,)), ((), ())), preferred_element_type=F32)


def _split_bf16(x):
    hi = x.astype(BF16)
    lo = (x - hi.astype(F32)).astype(BF16)
    return hi, lo


def _params(*sem):
    return pltpu.CompilerParams(dimension_semantics=sem, vmem_limit_bytes=VMEM_LIMIT_BYTES)


def _inproj_kernel(x_ref, g_ref, w_ref, u_ref, q_ref, fr_ref, iv_ref, og_ref, gates_ref, *, cw, kw):
    hb = _rms(x_ref[...], g_ref[...]).astype(BF16)

    def proj(lo, width):
        return _dot(hb, w_ref[:, lo:lo + width])

    u_ref[...] = (proj(0, cw) * jax.nn.sigmoid(proj(cw, cw))).astype(BF16)
    base = 2 * cw
    q_ref[...] = jax.nn.silu(proj(base, kw)).astype(BF16)
    fr_ref[...] = proj(base + kw, kw)
    iv_ref[...] = proj(base + 2 * kw, kw).astype(BF16)
    og_ref[...] = jax.nn.silu(proj(base + 3 * kw, kw)).astype(BF16)
    base += 4 * kw
    for c in range(gates_ref.shape[1] // kw):
        gates_ref[:, c * kw:(c + 1) * kw] = jax.nn.sigmoid(proj(base + c * kw, kw)).astype(BF16)


def _inproj(x2d, g, w_bf, cw, kw):
    n, d = x2d.shape
    tb = TOKEN_BLOCK
    row = lambda i: (i, 0)
    fixed = lambda i: (0, 0)
    outs = [((n, cw), BF16), ((n, kw), BF16), ((n, kw), F32), ((n, kw), BF16), ((n, kw), BF16), ((n, 2 * d), BF16)]
    return pl.pallas_call(
        functools.partial(_inproj_kernel, cw=cw, kw=kw),
        grid=(n // tb,),
        in_specs=[pl.BlockSpec((tb, d), row), pl.BlockSpec((1, d), fixed), pl.BlockSpec(w_bf.shape, fixed)],
        out_specs=[pl.BlockSpec((tb, s[1]), row) for s, _ in outs],
        out_shape=[jax.ShapeDtypeStruct(s, t) for s, t in outs],
        compiler_params=_params("parallel"),
        name="inproj",
    )(x2d, g, w_bf)


def _conv_kernel(u_ref, cw_ref, cb_ref, lg_ref, lb_ref, wo_ref, gate_ref, y_ref, ext_ref, act_ref):
    tb, c = u_ref.shape

    @pl.when(pl.program_id(1) == 0)
    def _():
        ext_ref[0:CONV_HALO, :] = jnp.zeros((CONV_HALO, c), F32)

    ext_ref[CONV_HALO:CONV_HALO + tb, :] = u_ref[...].astype(F32)
    for r in range(tb // CONV_ROWS):
        acc = jnp.broadcast_to(cb_ref[...], (CONV_ROWS, c))
        for j in range(CONV_K):
            off = r * CONV_ROWS + CONV_HALO - (CONV_K - 1) + j
            acc = acc + cw_ref[j:j + 1, :] * ext_ref[off:off + CONV_ROWS, :]
        mu = jnp.mean(acc, axis=-1, keepdims=True)
        cen = acc - mu
        var = jnp.mean(cen * cen, axis=-1, keepdims=True)
        ln = cen * lax.rsqrt(var + EPS) * lg_ref[...] + lb_ref[...]
        act_ref[r * CONV_ROWS:(r + 1) * CONV_ROWS, :] = jax.nn.silu(ln).astype(BF16)
    ext_ref[0:CONV_HALO, :] = ext_ref[tb:tb + CONV_HALO, :]
    y_ref[...] = (gate_ref[...].astype(F32) * _dot(act_ref[...], wo_ref[...])).astype(BF16)


def _conv(u, conv_w, conv_b, ln_g, ln_b, w_out_bf, gates, batch):
    n, c = u.shape
    d = w_out_bf.shape[1]
    tb = TOKEN_BLOCK
    nsb = n // batch // tb
    row = lambda b, s: (b * nsb + s, 0)
    fixed = lambda b, s: (0, 0)
    return pl.pallas_call(
        _conv_kernel,
        grid=(batch, nsb),
        in_specs=[pl.BlockSpec((tb, c), row), pl.BlockSpec(conv_w.shape, fixed), pl.BlockSpec((1, c), fixed),
                  pl.BlockSpec((1, c), fixed), pl.BlockSpec((1, c), fixed), pl.BlockSpec((c, d), fixed),
                  pl.BlockSpec((tb, d), row)],
        out_specs=pl.BlockSpec((tb, d), row),
        out_shape=jax.ShapeDtypeStruct((n, d), BF16),
        scratch_shapes=[pltpu.VMEM((tb + CONV_HALO, c), F32), pltpu.VMEM((tb, c), BF16)],
        compiler_params=_params("arbitrary", "arbitrary"),
        name="conv",
    )(u, conv_w, conv_b, ln_g, ln_b, w_out_bf, gates)


def _hgrn_chunk_head(q, fr, v, lb, st, tri):
    f = lb + (1.0 - lb) * jax.nn.sigmoid(fr)
    lf = jnp.log(f)
    kk = 1.0 - f
    lf_hi, lf_lo = _split_bf16(lf)
    cum = _dot(tri, lf_hi) + _dot(tri, lf_lo)
    last = cum[CHUNK - 1:CHUNK, :]

    qe = (q * jnp.exp(cum)).astype(BF16)
    o_inter = _dot_nt(qe, st.astype(BF16))
    kd = (kk * jnp.exp(last - cum)).astype(BF16)
    st_new = st * jnp.exp(last) + _dot_tn(v, kd)

    parts = []
    for i in range(CHUNK // SUB):
        rs = i * SUB
        ne = rs + SUB
        ref = cum[rs - 1:rs, :] if i else jnp.zeros_like(last)
        qt = (q[rs:ne] * jnp.exp(cum[rs:ne] - ref)).astype(BF16)
        kt = (kk[0:ne] * jnp.exp(ref - cum[0:ne])).astype(BF16)
        a = _dot_nt(qt, kt)
        trow = lax.broadcasted_iota(I32, (SUB, ne), 0) + rs
        scol = lax.broadcasted_iota(I32, (SUB, ne), 1)
        a = jnp.where(scol <= trow, a, 0.0).astype(BF16)
        parts.append(_dot(a, v[0:ne]))
    return o_inter + jnp.concatenate(parts, axis=0), st_new


def _hgrn_kernel(q_ref, fr_ref, iv_ref, og_ref, gate_ref, yc_ref, x_ref, lbl_ref, on_ref, wo_ref, wm_ref,
                 x1_ref, st_ref, ob_ref):
    tb = q_ref.shape[0]

    @pl.when(pl.program_id(1) == 0)
    def _():
        st_ref[...] = jnp.zeros_like(st_ref)

    l0, l1 = lbl_ref[0:1, :], lbl_ref[1:2, :]
    m = jnp.maximum(l0, l1)
    e0, e1 = jnp.exp(l0 - m), jnp.exp(l1 - m)
    lb_all = e0 / (e0 + e1)
    trow = lax.broadcasted_iota(I32, (CHUNK, CHUNK), 0)
    tcol = lax.broadcasted_iota(I32, (CHUNK, CHUNK), 1)
    tri = jnp.where(tcol <= trow, 1.0, 0.0).astype(BF16)

    def chunk(ci, carry):
        rows = pl.ds(pl.multiple_of(ci * CHUNK, CHUNK), CHUNK)
        for h in range(HGRN_HEADS):
            hs = slice(h * HGRN_DIM, (h + 1) * HGRN_DIM)
            o, st_new = _hgrn_chunk_head(q_ref[rows, hs].astype(F32), fr_ref[rows, hs], iv_ref[rows, hs],
                                         lb_all[:, hs], st_ref[h], tri)
            st_ref[h] = st_new
            on = _rms(o, on_ref[...]) * og_ref[rows, hs].astype(F32)
            ob_ref[rows, hs] = on.astype(BF16)
        return carry

    lax.fori_loop(0, tb // CHUNK, chunk, 0)
    y_rec = _dot(ob_ref[...], wo_ref[...])
    merged = yc_ref[...].astype(F32) + gate_ref[...].astype(F32) * y_rec
    x1_ref[...] = x_ref[...] + _dot(merged.astype(BF16), wm_ref[...])


def _hgrn(q, fr, iv, og, gates, yc, x2d, lb_logits, onorm_g, w_o_bf, w_mix_bf, batch):
    n, kw = q.shape
    d = x2d.shape[1]
    tb = TOKEN_BLOCK
    nsb = n // batch // tb
    row = lambda b, s: (b * nsb + s, 0)
    fixed = lambda b, s: (0, 0)
    return pl.pallas_call(
        _hgrn_kernel,
        grid=(batch, nsb),
        in_specs=[pl.BlockSpec((tb, kw), row), pl.BlockSpec((tb, kw), row), pl.BlockSpec((tb, kw), row),
                  pl.BlockSpec((tb, kw), row),
                  pl.BlockSpec((tb, d), lambda b, s: (b * nsb + s, 1)),
                  pl.BlockSpec((tb, d), row), pl.BlockSpec((tb, d), row),
                  pl.BlockSpec(lb_logits.shape, fixed), pl.BlockSpec((1, HGRN_DIM), fixed),
                  pl.BlockSpec(w_o_bf.shape, fixed), pl.BlockSpec(w_mix_bf.shape, fixed)],
        out_specs=pl.BlockSpec((tb, d), row),
        out_shape=jax.ShapeDtypeStruct((n, d), F32),
        scratch_shapes=[pltpu.VMEM((HGRN_HEADS, HGRN_DIM, HGRN_DIM), F32), pltpu.VMEM((tb, kw), BF16)],
        compiler_params=_params("arbitrary", "arbitrary"),
        name="hgrn",
    )(q, fr, iv, og, gates, yc, x2d, lb_logits, onorm_g, w_o_bf, w_mix_bf)


def _memkv_kernel(mem_ref, g_ref, wk_ref, wv_ref, k_ref, v_ref):
    mb = _rms(mem_ref[...], g_ref[...]).astype(BF16)
    k_ref[...] = _dot(mb, wk_ref[...]).astype(BF16)
    v_ref[...] = _dot(mb, wv_ref[...]).astype(BF16)


def _memkv(mem2d, g, wk_bf, wv_bf, batch):
    n, d = mem2d.shape
    m = n // batch
    row = lambda b: (b, 0)
    fixed = lambda b: (0, 0)
    return pl.pallas_call(
        _memkv_kernel,
        grid=(batch,),
        in_specs=[pl.BlockSpec((m, d), row), pl.BlockSpec((1, d), fixed), pl.BlockSpec((d, d), fixed),
                  pl.BlockSpec((d, d), fixed)],
        out_specs=[pl.BlockSpec((m, d), row)] * 2,
        out_shape=[jax.ShapeDtypeStruct((n, d), BF16)] * 2,
        compiler_params=_params("parallel"),
        name="memkv",
    )(mem2d, g, wk_bf, wv_bf)


def _route(lt):
    def row(r):
        return lt[r:r + 1, :]

    gl = [row(g) for g in range(N_GROUPS)]
    gmax = functools.reduce(jnp.maximum, gl)
    g_p = 1.0 / functools.reduce(jnp.add, [jnp.exp(l - gmax) for l in gl])
    gidx = jnp.full(gmax.shape, N_GROUPS - 1, I32)
    for g in range(N_GROUPS - 2, -1, -1):
        gidx = jnp.where(gl[g] == gmax, g, gidx)

    el = []
    for j in range(EXPERTS_PER_GROUP):
        v = row(N_GROUPS + (N_GROUPS - 1) * EXPERTS_PER_GROUP + j)
        for g in range(N_GROUPS - 2, -1, -1):
            v = jnp.where(gidx == g, row(N_GROUPS + g * EXPERTS_PER_GROUP + j), v)
        el.append(v)

    def argmax(vals):
        mx = functools.reduce(jnp.maximum, vals)
        idx = jnp.full(mx.shape, EXPERTS_PER_GROUP - 1, I32)
        for j in range(EXPERTS_PER_GROUP - 2, -1, -1):
            idx = jnp.where(vals[j] == mx, j, idx)
        return mx, idx

    m1, i1 = argmax(el)
    m2, i2 = argmax([jnp.where(i1 == j, -jnp.inf, el[j]) for j in range(EXPERTS_PER_GROUP)])
    r = jnp.exp(m2 - m1)
    w1 = g_p / (1.0 + r)
    w2 = g_p * r / (1.0 + r)
    base = gidx * EXPERTS_PER_GROUP
    return jnp.concatenate([base + i1, base + i2], axis=0), jnp.concatenate([w1, w2], axis=0)


def _attn_kernel(x1_ref, gxa_ref, wq_ref, k_ref, v_ref, wo_ref, gffn_ref, wr_ref, br_ref,
                 x2_ref, h2_ref, eid_ref, wts_ref):
    x1 = x1_ref[...]
    d = x1.shape[1]
    hd = d // XA_HEADS
    q = _dot(_rms(x1, gxa_ref[...]).astype(BF16), wq_ref[...]).astype(BF16)
    heads = []
    for h in range(XA_HEADS):
        hs = slice(h * hd, (h + 1) * hd)
        sc = _dot_nt(q[:, hs], k_ref[:, hs]) * (hd ** -0.5)
        p = jnp.exp(sc - jnp.max(sc, axis=-1, keepdims=True))
        p = p / jnp.sum(p, axis=-1, keepdims=True)
        heads.append(_dot(p.astype(BF16), v_ref[:, hs]).astype(BF16))
    x2 = x1 + _dot(jnp.concatenate(heads, axis=1), wo_ref[...])
    x2_ref[...] = x2
    h2 = _rms(x2, gffn_ref[...])
    h2_ref[...] = h2
    h_hi, h_lo = _split_bf16(h2)
    w_hi, w_lo = _split_bf16(wr_ref[...])
    lt = _dot_nt(w_hi, h_hi) + (_dot_nt(w_hi, h_lo) + _dot_nt(w_lo, h_hi)) + br_ref[...]
    eid, wts = _route(lt)
    eid_ref[...] = eid
    wts_ref[...] = wts


def _attn(x1, gxa, wq_bf, k_bf, v_bf, wo_bf, gffn, w_route_t, b_route, batch):
    n, d = x1.shape
    m = k_bf.shape[0] // batch
    tb = TOKEN_BLOCK
    nsb = n // batch // tb
    row = lambda b, s: (b * nsb + s, 0)
    lane = lambda b, s: (0, b * nsb + s)
    fixed = lambda b, s: (0, 0)
    mem = lambda b, s: (b, 0)
    return pl.pallas_call(
        _attn_kernel,
        grid=(batch, nsb),
        in_specs=[pl.BlockSpec((tb, d), row), pl.BlockSpec((1, d), fixed), pl.BlockSpec((d, d), fixed),
                  pl.BlockSpec((m, d), mem), pl.BlockSpec((m, d), mem), pl.BlockSpec((d, d), fixed),
                  pl.BlockSpec((1, d), fixed), pl.BlockSpec((ROUTE_ROWS, d), fixed),
                  pl.BlockSpec((ROUTE_ROWS, 1), fixed)],
        out_specs=[pl.BlockSpec((tb, d), row), pl.BlockSpec((tb, d), row),
                   pl.BlockSpec((TOP_K, tb), lane), pl.BlockSpec((TOP_K, tb), lane)],
        out_shape=[jax.ShapeDtypeStruct((n, d), F32), jax.ShapeDtypeStruct((n, d), F32),
                   jax.ShapeDtypeStruct((TOP_K, n), I32), jax.ShapeDtypeStruct((TOP_K, n), F32)],
        compiler_params=_params("parallel", "parallel"),
        name="attn",
    )(x1, gxa, wq_bf, k_bf, v_bf, wo_bf, gffn, w_route_t, b_route)


def _rank_kernel(eid_ref, rank_ref, cnt_ref, run_ref):
    i = pl.program_id(0)
    ab = eid_ref.shape[1]

    @pl.when(i == 0)
    def _():
        run_ref[...] = jnp.zeros_like(run_ref)

    onehot = lax.broadcasted_iota(I32, (N_EXPERTS, ab), 0) == eid_ref[...]
    before = lax.broadcasted_iota(I32, (ab, ab), 0) < lax.broadcasted_iota(I32, (ab, ab), 1)
    excl = _dot(jnp.where(onehot, 1.0, 0.0).astype(BF16), jnp.where(before, 1.0, 0.0).astype(BF16))
    run = run_ref[...]
    rank_ref[...] = jnp.sum(jnp.where(onehot, excl + run, 0.0), axis=0, keepdims=True).astype(I32)
    run = run + jnp.sum(jnp.where(onehot, 1.0, 0.0), axis=1, keepdims=True)
    run_ref[...] = run
    cnt_ref[...] = run


def _rank(eid_flat):
    na = eid_flat.shape[1]
    ab = RANK_BLOCK
    lane = lambda i: (0, i)
    fixed = lambda i: (0, 0)
    return pl.pallas_call(
        _rank_kernel,
        grid=(na // ab,),
        in_specs=[pl.BlockSpec((1, ab), lane)],
        out_specs=[pl.BlockSpec((1, ab), lane), pl.BlockSpec((N_EXPERTS, 1), fixed)],
        out_shape=[jax.ShapeDtypeStruct((1, na), I32), jax.ShapeDtypeStruct((N_EXPERTS, 1), F32)],
        scratch_shapes=[pltpu.VMEM((N_EXPERTS, 1), F32)],
        compiler_params=_params("arbitrary"),
        name="rank",
    )(eid_flat)


def _plan_kernel(cnt_ref, eid_ref, rank_ref, pos_ref, tile_ref):
    ab = eid_ref.shape[1]
    nt = tile_ref.shape[1]
    padded = jnp.floor((cnt_ref[...] + (MOE_TILE - 1)) * (1.0 / MOE_TILE)) * MOE_TILE
    lower = lax.broadcasted_iota(I32, (N_EXPERTS, N_EXPERTS), 1) < lax.broadcasted_iota(I32, (N_EXPERTS, N_EXPERTS), 0)
    p_hi, p_lo = _split_bf16(jnp.broadcast_to(padded, (N_EXPERTS, 128)))
    lower_bf = jnp.where(lower, 1.0, 0.0).astype(BF16)
    start = (_dot(lower_bf, p_hi) + _dot(lower_bf, p_lo))[:, 0:1]
    onehot = lax.broadcasted_iota(I32, (N_EXPERTS, ab), 0) == eid_ref[...]
    pos_ref[...] = rank_ref[...] + jnp.sum(jnp.where(onehot, start, 0.0), axis=0, keepdims=True).astype(I32)
    end_tile = (start + padded) * (1.0 / MOE_TILE)
    tile = lax.broadcasted_iota(I32, (N_EXPERTS, nt), 1).astype(F32)
    tile_ref[...] = jnp.sum(jnp.where(tile >= end_tile, 1.0, 0.0), axis=0, keepdims=True).astype(I32)


def _plan(cnt, eid_flat, rank, n_tiles_padded):
    na = eid_flat.shape[1]
    ab = 2048
    lane = lambda i: (0, i)
    fixed = lambda i: (0, 0)
    return pl.pallas_call(
        _plan_kernel,
        grid=(na // ab,),
        in_specs=[pl.BlockSpec((N_EXPERTS, 1), fixed), pl.BlockSpec((1, ab), lane), pl.BlockSpec((1, ab), lane)],
        out_specs=[pl.BlockSpec((1, ab), lane), pl.BlockSpec((1, n_tiles_padded), fixed)],
        out_shape=[jax.ShapeDtypeStruct((1, na), I32), jax.ShapeDtypeStruct((1, n_tiles_padded), I32)],
        compiler_params=_params("arbitrary"),
        name="plan",
    )(cnt, eid_flat, rank)


def _row_copy(src, src_row, dst, dst_row, sem):
    return pltpu.make_async_copy(src.at[pl.ds(src_row, 1)], dst.at[pl.ds(dst_row, 1)], sem)


def _dispatch_kernel(pos_ref, h_ref, xs_in_ref, xs_ref, sem):
    del xs_in_ref
    tb = h_ref.shape[0]

    def issue(r, carry):
        for k in range(TOP_K):
            _row_copy(h_ref, r, xs_ref, pos_ref[0, k, r], sem).start()
        return carry

    lax.fori_loop(0, tb, issue, 0)

    def drain(r, carry):
        for k in range(TOP_K):
            _row_copy(h_ref, r, xs_ref, pos_ref[0, k, r], sem).wait()
        return carry

    lax.fori_loop(0, tb, drain, 0)


def _dispatch(pos_blocks, h2, xs_zero):
    n, d = h2.shape
    tb = pos_blocks.shape[2]
    return pl.pallas_call(
        _dispatch_kernel,
        grid=(n // tb,),
        in_specs=[pl.BlockSpec((1, TOP_K, tb), lambda i: (i, 0, 0), memory_space=pltpu.SMEM),
                  pl.BlockSpec((tb, d), lambda i: (i, 0)),
                  pl.BlockSpec(memory_space=pl.ANY)],
        out_specs=pl.BlockSpec(memory_space=pl.ANY),
        out_shape=jax.ShapeDtypeStruct(xs_zero.shape, xs_zero.dtype),
        scratch_shapes=[pltpu.SemaphoreType.DMA(())],
        input_output_aliases={2: 0},
        compiler_params=_params("arbitrary"),
        name="dispatch",
    )(pos_blocks, h2, xs_zero)


def _experts_kernel(te_ref, xs_ref, wg_ref, wu_ref, wd_ref, ys_ref, wg_bf, wu_bf, wd_bf):
    t = pl.program_id(0)
    e = te_ref[t]
    prev = te_ref[jnp.maximum(t - 1, 0)]

    @pl.when(jnp.logical_and(e < N_EXPERTS, jnp.logical_or(t == 0, e != prev)))
    def _():
        wg_bf[...] = wg_ref[...].astype(BF16)
        wu_bf[...] = wu_ref[...].astype(BF16)
        wd_bf[...] = wd_ref[...].astype(BF16)

    @pl.when(e < N_EXPERTS)
    def _():
        x = xs_ref[...].astype(BF16)
        hid = jax.nn.silu(_dot(x, wg_bf[...])) * _dot(x, wu_bf[...])
        ys_ref[...] = _dot(hid.astype(BF16), wd_bf[...])

    @pl.when(e >= N_EXPERTS)
    def _():
        ys_ref[...] = jnp.zeros_like(ys_ref)


def _experts(tile_expert, xs, w_gate, w_up, w_down, n_tiles):
    _, d = xs.shape
    ff = w_gate.shape[2]
    tm = MOE_TILE
    tile = lambda t, te: (t, 0)
    expert = lambda t, te: (jnp.minimum(te[t], N_EXPERTS - 1), 0, 0)
    return pl.pallas_call(
        _experts_kernel,
        grid_spec=pltpu.PrefetchScalarGridSpec(
            num_scalar_prefetch=1,
            grid=(n_tiles,),
            in_specs=[pl.BlockSpec((tm, d), tile), pl.BlockSpec((None, d, ff), expert),
                      pl.BlockSpec((None, d, ff), expert), pl.BlockSpec((None, ff, d), expert)],
            out_specs=pl.BlockSpec((tm, d), tile),
            scratch_shapes=[pltpu.VMEM((d, ff), BF16), pltpu.VMEM((d, ff), BF16), pltpu.VMEM((ff, d), BF16)],
        ),
        out_shape=jax.ShapeDtypeStruct(xs.shape, F32),
        compiler_params=_params("arbitrary"),
        name="experts",
    )(tile_expert, xs, w_gate, w_up, w_down)


def _combine_kernel(pos_ref, w_ref, x2_ref, g_ref, ys_ref, out_ref, rows_ref, sem):
    tb = x2_ref.shape[0]

    def issue(r, carry):
        for k in range(TOP_K):
            _row_copy(ys_ref, pos_ref[0, k, r], rows_ref.at[k], r, sem).start()
        return carry

    lax.fori_loop(0, tb, issue, 0)

    def drain(r, carry):
        for k in range(TOP_K):
            _row_copy(ys_ref, pos_ref[0, k, r], rows_ref.at[k], r, sem).wait()
        return carry

    lax.fori_loop(0, tb, drain, 0)
    y = w_ref[:, 0:1] * rows_ref[0] + w_ref[:, 1:2] * rows_ref[1]
    out_ref[...] = _rms(x2_ref[...] + y, g_ref[...])


def _combine(pos_blocks, wts_t, x2, g_final, ys):
    n, d = x2.shape
    tb = pos_blocks.shape[2]
    row = lambda i: (i, 0)
    return pl.pallas_call(
        _combine_kernel,
        grid=(n // tb,),
        in_specs=[pl.BlockSpec((1, TOP_K, tb), lambda i: (i, 0, 0), memory_space=pltpu.SMEM),
                  pl.BlockSpec((tb, TOP_K), row), pl.BlockSpec((tb, d), row), pl.BlockSpec((1, d), lambda i: (0, 0)),
                  pl.BlockSpec(memory_space=pl.ANY)],
        out_specs=pl.BlockSpec((tb, d), row),
        out_shape=jax.ShapeDtypeStruct((n, d), F32),
        scratch_shapes=[pltpu.VMEM((TOP_K, tb, d), F32), pltpu.SemaphoreType.DMA(())],
        compiler_params=_params("arbitrary"),
        name="combine",
    )(pos_blocks, wts_t, x2, g_final, ys)


def _blocked(pos_flat, n, tb):
    return pos_flat.reshape(TOP_K, n // tb, tb).transpose(1, 0, 2)


def kernel(x, mem, norm_mix_g, w_in, conv_w, conv_b, conv_ln_g, conv_ln_b, conv_w_out, hgrn_lb_logits, hgrn_onorm_g, hgrn_w_out, w_mix_out, norm_xa_g, norm_mem_g, xa_w_q, xa_w_k, xa_w_v, xa_w_o, norm_ffn_g, router_group_w, router_group_b, router_expert_w, router_expert_b, moe_w_gate, moe_w_up, moe_w_down, final_norm_g):
    batch, seq, d = x.shape
    n = batch * seq
    assert w_in.shape[0] == 1, "the final RMSNorm is fused into the single layer's combine step"
    cw = conv_w.shape[2]
    kw = hgrn_w_out.shape[1]
    assert kw == HGRN_HEADS * HGRN_DIM and conv_w.shape[1] == CONV_K
    assert seq % TOKEN_BLOCK == 0 and TOKEN_BLOCK % CHUNK == 0 and TOKEN_BLOCK % CONV_ROWS == 0
    assert (TOP_K * n) % 2048 == 0 and n % COMBINE_BLOCK == 0
    assert moe_w_gate.shape[1] == N_EXPERTS and router_group_w.shape[2] == N_GROUPS

    n_tiles = (TOP_K * n) // MOE_TILE + N_EXPERTS
    n_tiles_padded = -(-n_tiles // 128) * 128
    vec = lambda p: p.reshape(1, -1)

    x2d = x.reshape(n, d)
    mem2d = mem.reshape(-1, d)
    for l in range(1):
        u, q, fr, iv, og, gates = _inproj(x2d, vec(norm_mix_g[l]), w_in[l].astype(BF16), cw, kw)
        yc = _conv(u, conv_w[l], vec(conv_b[l]), vec(conv_ln_g[l]), vec(conv_ln_b[l]),
                   conv_w_out[l].astype(BF16), gates, batch)
        x1 = _hgrn(q, fr, iv, og, gates, yc, x2d, hgrn_lb_logits[l:l + 2], vec(hgrn_onorm_g[l]),
                   hgrn_w_out[l].astype(BF16), w_mix_out[l].astype(BF16), batch)
        k_bf, v_bf = _memkv(mem2d, vec(norm_mem_g[l]), xa_w_k[l].astype(BF16), xa_w_v[l].astype(BF16), batch)
        w_route_t = jnp.zeros((ROUTE_ROWS, d), F32).at[:N_GROUPS + N_EXPERTS].set(
            jnp.concatenate([router_group_w[l], router_expert_w[l]], axis=1).T)
        b_route = jnp.zeros((ROUTE_ROWS, 1), F32).at[:N_GROUPS + N_EXPERTS, 0].set(
            jnp.concatenate([router_group_b[l], router_expert_b[l]]))
        x2, h2, eid, wts = _attn(x1, vec(norm_xa_g[l]), xa_w_q[l].astype(BF16), k_bf, v_bf,
                                 xa_w_o[l].astype(BF16), vec(norm_ffn_g[l]), w_route_t, b_route, batch)
        eid_flat = eid.reshape(1, TOP_K * n)
        rank, cnt = _rank(eid_flat)
        pos, tile_expert = _plan(cnt, eid_flat, rank, n_tiles_padded)
        xs = _dispatch(_blocked(pos, n, TOKEN_BLOCK), h2, jnp.zeros((n_tiles * MOE_TILE, d), F32))
        ys = _experts(tile_expert.reshape(-1), xs, moe_w_gate[l], moe_w_up[l], moe_w_down[l], n_tiles)
        out = _combine(_blocked(pos, n, COMBINE_BLOCK), wts.T, x2, vec(final_norm_g), ys)
    return out.reshape(batch, seq, d)
```

```python
import functools

import jax
import jax.numpy as jnp
from jax import lax
from jax.experimental import pallas as pl
from jax.experimental.pallas import tpu as pltpu

F32 = jnp.float32
BF16 = jnp.bfloat16
I32 = jnp.int32

EPS = 1e-6
CONV_K = 31
CONV_HALO = 32
CONV_ROWS = 64
HGRN_HEADS = 4
HGRN_DIM = 128
CHUNK = 64
SUB = 16
XA_HEADS = 4
N_GROUPS = 4
EXPERTS_PER_GROUP = 8
N_EXPERTS = N_GROUPS * EXPERTS_PER_GROUP
TOP_K = 2
ROUTE_ROWS = 40
TOKEN_BLOCK = 512
MOE_TILE = 256
RANK_BLOCK = 1024
COMBINE_BLOCK = 256
VMEM_LIMIT_BYTES = 48 * 1024 * 1024


def _rms(x, g):
    return x * lax.rsqrt(jnp.mean(x * x, axis=-1, keepdims=True) + EPS) * g


def _dot(a, b):
    return jnp.dot(a, b, preferred_element_type=F32)


def _dot_nt(a, b):
    return lax.dot_general(a, b, (((1,), (1,)), ((), ())), preferred_element_type=F32)


def _dot_tn(a, b):
    return lax.dot_general(a, b, (((0,), (0,)), ((), ())), preferred_element_type=F32)


def _split_bf16(x):
    hi = x.astype(BF16)
    lo = (x - hi.astype(F32)).astype(BF16)
    return hi, lo


def _params(*sem):
    return pltpu.CompilerParams(dimension_semantics=sem, vmem_limit_bytes=VMEM_LIMIT_BYTES)


def _inproj_kernel(x_ref, g_ref, w_ref, u_ref, q_ref, fr_ref, iv_ref, og_ref, gates_ref, *, cw, kw):
    hb = _rms(x_ref[...], g_ref[...]).astype(BF16)

    def proj(lo, width):
        return _dot(hb, w_ref[:, lo:lo + width])

    u_ref[...] = (proj(0, cw) * jax.nn.sigmoid(proj(cw, cw))).astype(BF16)
    base = 2 * cw
    q_ref[...] = jax.nn.silu(proj(base, kw)).astype(BF16)
    fr_ref[...] = proj(base + kw, kw)
    iv_ref[...] = proj(base + 2 * kw, kw).astype(BF16)
    og_ref[...] = jax.nn.silu(proj(base + 3 * kw, kw)).astype(BF16)
    base += 4 * kw
    for c in range(gates_ref.shape[1] // kw):
        gates_ref[:, c * kw:(c + 1) * kw] = jax.nn.sigmoid(proj(base + c * kw, kw)).astype(BF16)


def _inproj(x2d, g, w_bf, cw, kw):
    n, d = x2d.shape
    tb = TOKEN_BLOCK
    row = lambda i: (i, 0)
    fixed = lambda i: (0, 0)
    outs = [((n, cw), BF16), ((n, kw), BF16), ((n, kw), F32), ((n, kw), BF16), ((n, kw), BF16), ((n, 2 * d), BF16)]
    return pl.pallas_call(
        functools.partial(_inproj_kernel, cw=cw, kw=kw),
        grid=(n // tb,),
        in_specs=[pl.BlockSpec((tb, d), row), pl.BlockSpec((1, d), fixed), pl.BlockSpec(w_bf.shape, fixed)],
        out_specs=[pl.BlockSpec((tb, s[1]), row) for s, _ in outs],
        out_shape=[jax.ShapeDtypeStruct(s, t) for s, t in outs],
        compiler_params=_params("parallel"),
        name="inproj",
    )(x2d, g, w_bf)


def _conv_kernel(u_ref, cw_ref, cb_ref, lg_ref, lb_ref, wo_ref, gate_ref, y_ref, ext_ref, act_ref):
    tb, c = u_ref.shape

    @pl.when(pl.program_id(1) == 0)
    def _():
        ext_ref[0:CONV_HALO, :] = jnp.zeros((CONV_HALO, c), F32)

    ext_ref[CONV_HALO:CONV_HALO + tb, :] = u_ref[...].astype(F32)
    for r in range(tb // CONV_ROWS):
        acc = jnp.broadcast_to(cb_ref[...], (CONV_ROWS, c))
        for j in range(CONV_K):
            off = r * CONV_ROWS + CONV_HALO - (CONV_K - 1) + j
            acc = acc + cw_ref[j:j + 1, :] * ext_ref[off:off + CONV_ROWS, :]
        mu = jnp.mean(acc, axis=-1, keepdims=True)
        cen = acc - mu
        var = jnp.mean(cen * cen, axis=-1, keepdims=True)
        ln = cen * lax.rsqrt(var + EPS) * lg_ref[...] + lb_ref[...]
        act_ref[r * CONV_ROWS:(r + 1) * CONV_ROWS, :] = jax.nn.silu(ln).astype(BF16)
    ext_ref[0:CONV_HALO, :] = ext_ref[tb:tb + CONV_HALO, :]
    y_ref[...] = (gate_ref[...].astype(F32) * _dot(act_ref[...], wo_ref[...])).astype(BF16)


def _conv(u, conv_w, conv_b, ln_g, ln_b, w_out_bf, gates, batch):
    n, c = u.shape
    d = w_out_bf.shape[1]
    tb = TOKEN_BLOCK
    nsb = n // batch // tb
    row = lambda b, s: (b * nsb + s, 0)
    fixed = lambda b, s: (0, 0)
    return pl.pallas_call(
        _conv_kernel,
        grid=(batch, nsb),
        in_specs=[pl.BlockSpec((tb, c), row), pl.BlockSpec(conv_w.shape, fixed), pl.BlockSpec((1, c), fixed),
                  pl.BlockSpec((1, c), fixed), pl.BlockSpec((1, c), fixed), pl.BlockSpec((c, d), fixed),
                  pl.BlockSpec((tb, d), row)],
        out_specs=pl.BlockSpec((tb, d), row),
        out_shape=jax.ShapeDtypeStruct((n, d), BF16),
        scratch_shapes=[pltpu.VMEM((tb + CONV_HALO, c), F32), pltpu.VMEM((tb, c), BF16)],
        compiler_params=_params("arbitrary", "arbitrary"),
        name="conv",
    )(u, conv_w, conv_b, ln_g, ln_b, w_out_bf, gates)


def _hgrn_chunk(q, fr, v, lb, st_ref, tri):
    heads = [slice(h * HGRN_DIM, (h + 1) * HGRN_DIM) for h in range(HGRN_HEADS)]
    f = lb + (1.0 - lb) * jax.nn.sigmoid(fr)
    lf = jnp.log(f)
    kk = 1.0 - f
    lf_hi, lf_lo = _split_bf16(lf)
    cum = _dot(tri, lf_hi) + _dot(tri, lf_lo)
    last = cum[CHUNK - 1:CHUNK, :]
    qe = (q * jnp.exp(cum)).astype(BF16)
    kd = (kk * jnp.exp(last - cum)).astype(BF16)
    decay = jnp.exp(last)

    blocks = []
    for i in range(CHUNK // SUB):
        rs, ne = i * SUB, (i + 1) * SUB
        ref = cum[rs - 1:rs, :] if i else jnp.zeros_like(last)
        qt = (q[rs:ne] * jnp.exp(cum[rs:ne] - ref)).astype(BF16)
        kt = (kk[0:ne] * jnp.exp(ref - cum[0:ne])).astype(BF16)
        blocks.append((qt, kt))

    states = [st_ref[h] for h in range(HGRN_HEADS)]
    o_inter = [_dot_nt(qe[:, hs], st.astype(BF16)) for hs, st in zip(heads, states)]
    scores = [[_dot_nt(qt[:, hs], kt[:, hs]) for qt, kt in blocks] for hs in heads]
    for h, hs in enumerate(heads):
        st_ref[h] = states[h] * decay[:, hs] + _dot_tn(v[:, hs], kd[:, hs])

    outs = []
    for h, hs in enumerate(heads):
        parts = []
        for i, a in enumerate(scores[h]):
            rs, ne = i * SUB, (i + 1) * SUB
            trow = lax.broadcasted_iota(I32, (SUB, ne), 0) + rs
            scol = lax.broadcasted_iota(I32, (SUB, ne), 1)
            a = jnp.where(scol <= trow, a, 0.0).astype(BF16)
            parts.append(_dot(a, v[0:ne, hs]))
        outs.append(o_inter[h] + jnp.concatenate(parts, axis=0))
    return jnp.concatenate(outs, axis=1)


def _hgrn_kernel(q_ref, fr_ref, iv_ref, og_ref, gate_ref, yc_ref, x_ref, lbl_ref, on_ref, wo_ref, wm_ref,
                 x1_ref, st_ref, ob_ref):
    tb = q_ref.shape[0]

    @pl.when(pl.program_id(1) == 0)
    def _():
        st_ref[...] = jnp.zeros_like(st_ref)

    l0, l1 = lbl_ref[0:1, :], lbl_ref[1:2, :]
    m = jnp.maximum(l0, l1)
    e0, e1 = jnp.exp(l0 - m), jnp.exp(l1 - m)
    lb_all = e0 / (e0 + e1)
    trow = lax.broadcasted_iota(I32, (CHUNK, CHUNK), 0)
    tcol = lax.broadcasted_iota(I32, (CHUNK, CHUNK), 1)
    tri = jnp.where(tcol <= trow, 1.0, 0.0).astype(BF16)

    def chunk(ci, carry):
        rows = pl.ds(pl.multiple_of(ci * CHUNK, CHUNK), CHUNK)
        o = _hgrn_chunk(q_ref[rows, :].astype(F32), fr_ref[rows, :], iv_ref[rows, :], lb_all, st_ref, tri)
        og = og_ref[rows, :].astype(F32)
        for h in range(HGRN_HEADS):
            hs = slice(h * HGRN_DIM, (h + 1) * HGRN_DIM)
            ob_ref[rows, hs] = (_rms(o[:, hs], on_ref[...]) * og[:, hs]).astype(BF16)
        return carry

    lax.fori_loop(0, tb // CHUNK, chunk, 0, unroll=4)
    y_rec = _dot(ob_ref[...], wo_ref[...])
    merged = yc_ref[...].astype(F32) + gate_ref[...].astype(F32) * y_rec
    x1_ref[...] = x_ref[...] + _dot(merged.astype(BF16), wm_ref[...])


def _hgrn(q, fr, iv, og, gates, yc, x2d, lb_logits, onorm_g, w_o_bf, w_mix_bf, batch):
    n, kw = q.shape
    d = x2d.shape[1]
    tb = TOKEN_BLOCK
    nsb = n // batch // tb
    row = lambda b, s: (b * nsb + s, 0)
    fixed = lambda b, s: (0, 0)
    return pl.pallas_call(
        _hgrn_kernel,
        grid=(batch, nsb),
        in_specs=[pl.BlockSpec((tb, kw), row), pl.BlockSpec((tb, kw), row), pl.BlockSpec((tb, kw), row),
                  pl.BlockSpec((tb, kw), row),
                  pl.BlockSpec((tb, d), lambda b, s: (b * nsb + s, 1)),
                  pl.BlockSpec((tb, d), row), pl.BlockSpec((tb, d), row),
                  pl.BlockSpec(lb_logits.shape, fixed), pl.BlockSpec((1, HGRN_DIM), fixed),
                  pl.BlockSpec(w_o_bf.shape, fixed), pl.BlockSpec(w_mix_bf.shape, fixed)],
        out_specs=pl.BlockSpec((tb, d), row),
        out_shape=jax.ShapeDtypeStruct((n, d), F32),
        scratch_shapes=[pltpu.VMEM((HGRN_HEADS, HGRN_DIM, HGRN_DIM), F32), pltpu.VMEM((tb, kw), BF16)],
        compiler_params=_params("arbitrary", "arbitrary"),
        name="hgrn",
    )(q, fr, iv, og, gates, yc, x2d, lb_logits, onorm_g, w_o_bf, w_mix_bf)


def _memkv_kernel(mem_ref, g_ref, wk_ref, wv_ref, k_ref, v_ref):
    mb = _rms(mem_ref[...], g_ref[...]).astype(BF16)
    k_ref[...] = _dot(mb, wk_ref[...]).astype(BF16)
    v_ref[...] = _dot(mb, wv_ref[...]).astype(BF16)


def _memkv(mem2d, g, wk_bf, wv_bf, batch):
    n, d = mem2d.shape
    m = n // batch
    row = lambda b: (b, 0)
    fixed = lambda b: (0, 0)
    return pl.pallas_call(
        _memkv_kernel,
        grid=(batch,),
        in_specs=[pl.BlockSpec((m, d), row), pl.BlockSpec((1, d), fixed), pl.BlockSpec((d, d), fixed),
                  pl.BlockSpec((d, d), fixed)],
        out_specs=[pl.BlockSpec((m, d), row)] * 2,
        out_shape=[jax.ShapeDtypeStruct((n, d), BF16)] * 2,
        compiler_params=_params("parallel"),
        name="memkv",
    )(mem2d, g, wk_bf, wv_bf)


def _route(lt):
    def row(r):
        return lt[r:r + 1, :]

    gl = [row(g) for g in range(N_GROUPS)]
    gmax = functools.reduce(jnp.maximum, gl)
    g_p = 1.0 / functools.reduce(jnp.add, [jnp.exp(l - gmax) for l in gl])
    gidx = jnp.full(gmax.shape, N_GROUPS - 1, I32)
    for g in range(N_GROUPS - 2, -1, -1):
        gidx = jnp.where(gl[g] == gmax, g, gidx)

    el = []
    for j in range(EXPERTS_PER_GROUP):
        v = row(N_GROUPS + (N_GROUPS - 1) * EXPERTS_PER_GROUP + j)
        for g in range(N_GROUPS - 2, -1, -1):
            v = jnp.where(gidx == g, row(N_GROUPS + g * EXPERTS_PER_GROUP + j), v)
        el.append(v)

    def argmax(vals):
        mx = functools.reduce(jnp.maximum, vals)
        idx = jnp.full(mx.shape, EXPERTS_PER_GROUP - 1, I32)
        for j in range(EXPERTS_PER_GROUP - 2, -1, -1):
            idx = jnp.where(vals[j] == mx, j, idx)
        return mx, idx

    m1, i1 = argmax(el)
    m2, i2 = argmax([jnp.where(i1 == j, -jnp.inf, el[j]) for j in range(EXPERTS_PER_GROUP)])
    r = jnp.exp(m2 - m1)
    w1 = g_p / (1.0 + r)
    w2 = g_p * r / (1.0 + r)
    base = gidx * EXPERTS_PER_GROUP
    return jnp.concatenate([base + i1, base + i2], axis=0), jnp.concatenate([w1, w2], axis=0)


def _attn_kernel(x1_ref, gxa_ref, wq_ref, k_ref, v_ref, wo_ref, gffn_ref, wr_ref, br_ref,
                 x2_ref, h2_ref, eid_ref, wts_ref):
    x1 = x1_ref[...]
    d = x1.shape[1]
    hd = d // XA_HEADS
    q = _dot(_rms(x1, gxa_ref[...]).astype(BF16), wq_ref[...]).astype(BF16)
    heads = []
    for h in range(XA_HEADS):
        hs = slice(h * hd, (h + 1) * hd)
        sc = _dot_nt(q[:, hs], k_ref[:, hs]) * (hd ** -0.5)
        p = jnp.exp(sc - jnp.max(sc, axis=-1, keepdims=True))
        p = p / jnp.sum(p, axis=-1, keepdims=True)
        heads.append(_dot(p.astype(BF16), v_ref[:, hs]).astype(BF16))
    x2 = x1 + _dot(jnp.concatenate(heads, axis=1), wo_ref[...])
    x2_ref[...] = x2
    h2 = _rms(x2, gffn_ref[...])
    h2_ref[...] = h2
    h_hi, h_lo = _split_bf16(h2)
    w_hi, w_lo = _split_bf16(wr_ref[...])
    lt = _dot_nt(w_hi, h_hi) + (_dot_nt(w_hi, h_lo) + _dot_nt(w_lo, h_hi)) + br_ref[...]
    eid, wts = _route(lt)
    eid_ref[...] = eid
    wts_ref[...] = wts


def _attn(x1, gxa, wq_bf, k_bf, v_bf, wo_bf, gffn, w_route_t, b_route, batch):
    n, d = x1.shape
    m = k_bf.shape[0] // batch
    tb = TOKEN_BLOCK
    nsb = n // batch // tb
    row = lambda b, s: (b * nsb + s, 0)
    lane = lambda b, s: (0, b * nsb + s)
    fixed = lambda b, s: (0, 0)
    mem = lambda b, s: (b, 0)
    return pl.pallas_call(
        _attn_kernel,
        grid=(batch, nsb),
        in_specs=[pl.BlockSpec((tb, d), row), pl.BlockSpec((1, d), fixed), pl.BlockSpec((d, d), fixed),
                  pl.BlockSpec((m, d), mem), pl.BlockSpec((m, d), mem), pl.BlockSpec((d, d), fixed),
                  pl.BlockSpec((1, d), fixed), pl.BlockSpec((ROUTE_ROWS, d), fixed),
                  pl.BlockSpec((ROUTE_ROWS, 1), fixed)],
        out_specs=[pl.BlockSpec((tb, d), row), pl.BlockSpec((tb, d), row),
                   pl.BlockSpec((TOP_K, tb), lane), pl.BlockSpec((TOP_K, tb), lane)],
        out_shape=[jax.ShapeDtypeStruct((n, d), F32), jax.ShapeDtypeStruct((n, d), F32),
                   jax.ShapeDtypeStruct((TOP_K, n), I32), jax.ShapeDtypeStruct((TOP_K, n), F32)],
        compiler_params=_params("parallel", "parallel"),
        name="attn",
    )(x1, gxa, wq_bf, k_bf, v_bf, wo_bf, gffn, w_route_t, b_route)


def _rank_kernel(eid_ref, rank_ref, cnt_ref, run_ref):
    i = pl.program_id(0)
    ab = eid_ref.shape[1]

    @pl.when(i == 0)
    def _():
        run_ref[...] = jnp.zeros_like(run_ref)

    onehot = lax.broadcasted_iota(I32, (N_EXPERTS, ab), 0) == eid_ref[...]
    before = lax.broadcasted_iota(I32, (ab, ab), 0) < lax.broadcasted_iota(I32, (ab, ab), 1)
    excl = _dot(jnp.where(onehot, 1.0, 0.0).astype(BF16), jnp.where(before, 1.0, 0.0).astype(BF16))
    run = run_ref[...]
    rank_ref[...] = jnp.sum(jnp.where(onehot, excl + run, 0.0), axis=0, keepdims=True).astype(I32)
    run = run + jnp.sum(jnp.where(onehot, 1.0, 0.0), axis=1, keepdims=True)
    run_ref[...] = run
    cnt_ref[...] = run


def _rank(eid_flat):
    na = eid_flat.shape[1]
    ab = RANK_BLOCK
    lane = lambda i: (0, i)
    fixed = lambda i: (0, 0)
    return pl.pallas_call(
        _rank_kernel,
        grid=(na // ab,),
        in_specs=[pl.BlockSpec((1, ab), lane)],
        out_specs=[pl.BlockSpec((1, ab), lane), pl.BlockSpec((N_EXPERTS, 1), fixed)],
        out_shape=[jax.ShapeDtypeStruct((1, na), I32), jax.ShapeDtypeStruct((N_EXPERTS, 1), F32)],
        scratch_shapes=[pltpu.VMEM((N_EXPERTS, 1), F32)],
        compiler_params=_params("arbitrary"),
        name="rank",
    )(eid_flat)


def _plan_kernel(cnt_ref, eid_ref, rank_ref, pos_ref, tile_ref):
    ab = eid_ref.shape[1]
    nt = tile_ref.shape[1]
    padded = jnp.floor((cnt_ref[...] + (MOE_TILE - 1)) * (1.0 / MOE_TILE)) * MOE_TILE
    lower = lax.broadcasted_iota(I32, (N_EXPERTS, N_EXPERTS), 1) < lax.broadcasted_iota(I32, (N_EXPERTS, N_EXPERTS), 0)
    p_hi, p_lo = _split_bf16(jnp.broadcast_to(padded, (N_EXPERTS, 128)))
    lower_bf = jnp.where(lower, 1.0, 0.0).astype(BF16)
    start = (_dot(lower_bf, p_hi) + _dot(lower_bf, p_lo))[:, 0:1]
    onehot = lax.broadcasted_iota(I32, (N_EXPERTS, ab), 0) == eid_ref[...]
    pos_ref[...] = rank_ref[...] + jnp.sum(jnp.where(onehot, start, 0.0), axis=0, keepdims=True).astype(I32)
    end_tile = (start + padded) * (1.0 / MOE_TILE)
    tile = lax.broadcasted_iota(I32, (N_EXPERTS, nt), 1).astype(F32)
    tile_ref[...] = jnp.sum(jnp.where(tile >= end_tile, 1.0, 0.0), axis=0, keepdims=True).astype(I32)


def _plan(cnt, eid_flat, rank, n_tiles_padded):
    na = eid_flat.shape[1]
    ab = 2048
    lane = lambda i: (0, i)
    fixed = lambda i: (0, 0)
    return pl.pallas_call(
        _plan_kernel,
        grid=(na // ab,),
        in_specs=[pl.BlockSpec((N_EXPERTS, 1), fixed), pl.BlockSpec((1, ab), lane), pl.BlockSpec((1, ab), lane)],
        out_specs=[pl.BlockSpec((1, ab), lane), pl.BlockSpec((1, n_tiles_padded), fixed)],
        out_shape=[jax.ShapeDtypeStruct((1, na), I32), jax.ShapeDtypeStruct((1, n_tiles_padded), I32)],
        compiler_params=_params("arbitrary"),
        name="plan",
    )(cnt, eid_flat, rank)


def _row_copy(src, src_row, dst, dst_row, sem):
    return pltpu.make_async_copy(src.at[pl.ds(src_row, 1)], dst.at[pl.ds(dst_row, 1)], sem)


def _dispatch_kernel(pos_ref, h_ref, xs_in_ref, xs_ref, sem):
    del xs_in_ref
    tb = h_ref.shape[0]

    def issue(r, carry):
        for k in range(TOP_K):
            _row_copy(h_ref, r, xs_ref, pos_ref[0, k, r], sem).start()
        return carry

    lax.fori_loop(0, tb, issue, 0)

    def drain(r, carry):
        for k in range(TOP_K):
            _row_copy(h_ref, r, xs_ref, pos_ref[0, k, r], sem).wait()
        return carry

    lax.fori_loop(0, tb, drain, 0)


def _dispatch(pos_blocks, h2, xs_zero):
    n, d = h2.shape
    tb = pos_blocks.shape[2]
    return pl.pallas_call(
        _dispatch_kernel,
        grid=(n // tb,),
        in_specs=[pl.BlockSpec((1, TOP_K, tb), lambda i: (i, 0, 0), memory_space=pltpu.SMEM),
                  pl.BlockSpec((tb, d), lambda i: (i, 0)),
                  pl.BlockSpec(memory_space=pl.ANY)],
        out_specs=pl.BlockSpec(memory_space=pl.ANY),
        out_shape=jax.ShapeDtypeStruct(xs_zero.shape, xs_zero.dtype),
        scratch_shapes=[pltpu.SemaphoreType.DMA(())],
        input_output_aliases={2: 0},
        compiler_params=_params("arbitrary"),
        name="dispatch",
    )(pos_blocks, h2, xs_zero)


def _experts_kernel(te_ref, xs_ref, wg_ref, wu_ref, wd_ref, ys_ref, wg_bf, wu_bf, wd_bf):
    t = pl.program_id(0)
    e = te_ref[t]
    prev = te_ref[jnp.maximum(t - 1, 0)]

    @pl.when(jnp.logical_and(e < N_EXPERTS, jnp.logical_or(t == 0, e != prev)))
    def _():
        wg_bf[...] = wg_ref[...].astype(BF16)
        wu_bf[...] = wu_ref[...].astype(BF16)
        wd_bf[...] = wd_ref[...].astype(BF16)

    @pl.when(e < N_EXPERTS)
    def _():
        x = xs_ref[...].astype(BF16)
        hid = jax.nn.silu(_dot(x, wg_bf[...])) * _dot(x, wu_bf[...])
        ys_ref[...] = _dot(hid.astype(BF16), wd_bf[...])

    @pl.when(e >= N_EXPERTS)
    def _():
        ys_ref[...] = jnp.zeros_like(ys_ref)


def _experts(tile_expert, xs, w_gate, w_up, w_down, n_tiles):
    _, d = xs.shape
    ff = w_gate.shape[2]
    tm = MOE_TILE
    tile = lambda t, te: (t, 0)
    expert = lambda t, te: (jnp.minimum(te[t], N_EXPERTS - 1), 0, 0)
    return pl.pallas_call(
        _experts_kernel,
        grid_spec=pltpu.PrefetchScalarGridSpec(
            num_scalar_prefetch=1,
            grid=(n_tiles,),
            in_specs=[pl.BlockSpec((tm, d), tile), pl.BlockSpec((None, d, ff), expert),
                      pl.BlockSpec((None, d, ff), expert), pl.BlockSpec((None, ff, d), expert)],
            out_specs=pl.BlockSpec((tm, d), tile),
            scratch_shapes=[pltpu.VMEM((d, ff), BF16), pltpu.VMEM((d, ff), BF16), pltpu.VMEM((ff, d), BF16)],
        ),
        out_shape=jax.ShapeDtypeStruct(xs.shape, F32),
        compiler_params=_params("arbitrary"),
        name="experts",
    )(tile_expert, xs, w_gate, w_up, w_down)


def _combine_kernel(pos_ref, w_ref, x2_ref, g_ref, ys_ref, out_ref, rows_ref, sem):
    tb = x2_ref.shape[0]

    def issue(r, carry):
        for k in range(TOP_K):
            _row_copy(ys_ref, pos_ref[0, k, r], rows_ref.at[k], r, sem).start()
        return carry

    lax.fori_loop(0, tb, issue, 0)

    def drain(r, carry):
        for k in range(TOP_K):
            _row_copy(ys_ref, pos_ref[0, k, r], rows_ref.at[k], r, sem).wait()
        return carry

    lax.fori_loop(0, tb, drain, 0)
    y = w_ref[:, 0:1] * rows_ref[0] + w_ref[:, 1:2] * rows_ref[1]
    out_ref[...] = _rms(x2_ref[...] + y, g_ref[...])


def _combine(pos_blocks, wts_t, x2, g_final, ys):
    n, d = x2.shape
    tb = pos_blocks.shape[2]
    row = lambda i: (i, 0)
    return pl.pallas_call(
        _combine_kernel,
        grid=(n // tb,),
        in_specs=[pl.BlockSpec((1, TOP_K, tb), lambda i: (i, 0, 0), memory_space=pltpu.SMEM),
                  pl.BlockSpec((tb, TOP_K), row), pl.BlockSpec((tb, d), row), pl.BlockSpec((1, d), lambda i: (0, 0)),
                  pl.BlockSpec(memory_space=pl.ANY)],
        out_specs=pl.BlockSpec((tb, d), row),
        out_shape=jax.ShapeDtypeStruct((n, d), F32),
        scratch_shapes=[pltpu.VMEM((TOP_K, tb, d), F32), pltpu.SemaphoreType.DMA(())],
        compiler_params=_params("arbitrary"),
        name="combine",
    )(pos_blocks, wts_t, x2, g_final, ys)


def _blocked(pos_flat, n, tb):
    return pos_flat.reshape(TOP_K, n // tb, tb).transpose(1, 0, 2)


def kernel(x, mem, norm_mix_g, w_in, conv_w, conv_b, conv_ln_g, conv_ln_b, conv_w_out, hgrn_lb_logits, hgrn_onorm_g, hgrn_w_out, w_mix_out, norm_xa_g, norm_mem_g, xa_w_q, xa_w_k, xa_w_v, xa_w_o, norm_ffn_g, router_group_w, router_group_b, router_expert_w, router_expert_b, moe_w_gate, moe_w_up, moe_w_down, final_norm_g):
    batch, seq, d = x.shape
    n = batch * seq
    assert w_in.shape[0] == 1, "the final RMSNorm is fused into the single layer's combine step"
    cw = conv_w.shape[2]
    kw = hgrn_w_out.shape[1]
    assert kw == HGRN_HEADS * HGRN_DIM and conv_w.shape[1] == CONV_K
    assert seq % TOKEN_BLOCK == 0 and TOKEN_BLOCK % CHUNK == 0 and TOKEN_BLOCK % CONV_ROWS == 0
    assert (TOP_K * n) % 2048 == 0 and n % COMBINE_BLOCK == 0
    assert moe_w_gate.shape[1] == N_EXPERTS and router_group_w.shape[2] == N_GROUPS

    n_tiles = (TOP_K * n) // MOE_TILE + N_EXPERTS
    n_tiles_padded = -(-n_tiles // 128) * 128
    vec = lambda p: p.reshape(1, -1)

    x2d = x.reshape(n, d)
    mem2d = mem.reshape(-1, d)
    for l in range(1):
        u, q, fr, iv, og, gates = _inproj(x2d, vec(norm_mix_g[l]), w_in[l].astype(BF16), cw, kw)
        yc = _conv(u, conv_w[l], vec(conv_b[l]), vec(conv_ln_g[l]), vec(conv_ln_b[l]),
                   conv_w_out[l].astype(BF16), gates, batch)
        x1 = _hgrn(q, fr, iv, og, gates, yc, x2d, hgrn_lb_logits[l:l + 2], vec(hgrn_onorm_g[l]),
                   hgrn_w_out[l].astype(BF16), w_mix_out[l].astype(BF16), batch)
        k_bf, v_bf = _memkv(mem2d, vec(norm_mem_g[l]), xa_w_k[l].astype(BF16), xa_w_v[l].astype(BF16), batch)
        w_route_t = jnp.zeros((ROUTE_ROWS, d), F32).at[:N_GROUPS + N_EXPERTS].set(
            jnp.concatenate([router_group_w[l], router_expert_w[l]], axis=1).T)
        b_route = jnp.zeros((ROUTE_ROWS, 1), F32).at[:N_GROUPS + N_EXPERTS, 0].set(
            jnp.concatenate([router_group_b[l], router_expert_b[l]]))
        x2, h2, eid, wts = _attn(x1, vec(norm_xa_g[l]), xa_w_q[l].astype(BF16), k_bf, v_bf,
                                 xa_w_o[l].astype(BF16), vec(norm_ffn_g[l]), w_route_t, b_route, batch)
        eid_flat = eid.reshape(1, TOP_K * n)
        rank, cnt = _rank(eid_flat)
        pos, tile_expert = _plan(cnt, eid_flat, rank, n_tiles_padded)
        xs = _dispatch(_blocked(pos, n, TOKEN_BLOCK), h2, jnp.zeros((n_tiles * MOE_TILE, d), F32))
        ys = _experts(tile_expert.reshape(-1), xs, moe_w_gate[l], moe_w_up[l], moe_w_down[l], n_tiles)
        out = _combine(_blocked(pos, n, COMBINE_BLOCK), wts.T, x2, vec(final_norm_g), ys)
    return out.reshape(batch, seq, d)
```

```python
import functools

import jax
import jax.numpy as jnp
from jax import lax
from jax.experimental import pallas as pl
from jax.experimental.pallas import tpu as pltpu

F32 = jnp.float32
BF16 = jnp.bfloat16
I32 = jnp.int32

EPS = 1e-6
CONV_K = 31
CONV_HALO = 32
CONV_ROWS = 64
HGRN_HEADS = 4
HGRN_DIM = 128
CHUNK = 64
SUB = 16
XA_HEADS = 4
N_GROUPS = 4
EXPERTS_PER_GROUP = 8
N_EXPERTS = N_GROUPS * EXPERTS_PER_GROUP
TOP_K = 2
ROUTE_ROWS = 40
TOKEN_BLOCK = 512
MOE_TILE = 256
GROUP = 8
LOCAL_ROWS = 1280
SORT_CHUNK = 256
W_LANES = 128
VMEM_LIMIT_BYTES = 48 * 1024 * 1024


def _rms(x, g):
    return x * lax.rsqrt(jnp.mean(x * x, axis=-1, keepdims=True) + EPS) * g


def _dot(a, b):
    return jnp.dot(a, b, preferred_element_type=F32)


def _dot_nt(a, b):
    return lax.dot_general(a, b, (((1,), (1,)), ((), ())), preferred_element_type=F32)


def _dot_tn(a, b):
    return lax.dot_general(a, b, (((0,), (0,)), ((), ())), preferred_element_type=F32)


def _split_bf16(x):
    hi = x.astype(BF16)
    lo = (x - hi.astype(F32)).astype(BF16)
    return hi, lo


def _params(*sem):
    return pltpu.CompilerParams(dimension_semantics=sem, vmem_limit_bytes=VMEM_LIMIT_BYTES)


def _inproj_kernel(x_ref, g_ref, w_ref, u_ref, q_ref, fr_ref, iv_ref, og_ref, gates_ref, *, cw, kw):
    hb = _rms(x_ref[...], g_ref[...]).astype(BF16)

    def proj(lo, width):
        return _dot(hb, w_ref[:, lo:lo + width])

    u_ref[...] = (proj(0, cw) * jax.nn.sigmoid(proj(cw, cw))).astype(BF16)
    base = 2 * cw
    q_ref[...] = jax.nn.silu(proj(base, kw)).astype(BF16)
    fr_ref[...] = proj(base + kw, kw)
    iv_ref[...] = proj(base + 2 * kw, kw).astype(BF16)
    og_ref[...] = jax.nn.silu(proj(base + 3 * kw, kw)).astype(BF16)
    base += 4 * kw
    for c in range(gates_ref.shape[1] // kw):
        gates_ref[:, c * kw:(c + 1) * kw] = jax.nn.sigmoid(proj(base + c * kw, kw)).astype(BF16)


def _inproj(x2d, g, w_bf, cw, kw):
    n, d = x2d.shape
    tb = TOKEN_BLOCK
    row = lambda i: (i, 0)
    fixed = lambda i: (0, 0)
    outs = [((n, cw), BF16), ((n, kw), BF16), ((n, kw), F32), ((n, kw), BF16), ((n, kw), BF16), ((n, 2 * d), BF16)]
    return pl.pallas_call(
        functools.partial(_inproj_kernel, cw=cw, kw=kw),
        grid=(n // tb,),
        in_specs=[pl.BlockSpec((tb, d), row), pl.BlockSpec((1, d), fixed), pl.BlockSpec(w_bf.shape, fixed)],
        out_specs=[pl.BlockSpec((tb, s[1]), row) for s, _ in outs],
        out_shape=[jax.ShapeDtypeStruct(s, t) for s, t in outs],
        compiler_params=_params("parallel"),
        name="inproj",
    )(x2d, g, w_bf)


def _conv_kernel(u_ref, cw_ref, cb_ref, lg_ref, lb_ref, wo_ref, gate_ref, y_ref, ext_ref, halo_ref, perm_ref, act_ref):
    tb, c = u_ref.shape
    nt = tb // 8
    slabs = c // 128
    hr = CONV_HALO * 8
    lanes = [slice(l * 128, (l + 1) * 128) for l in range(slabs)]

    @pl.when(pl.program_id(1) == 0)
    def _():
        halo_ref[...] = jnp.zeros_like(halo_ref)

    un = u_ref[...].astype(F32)
    per = nt // 8
    for j in range(nt):
        start = hr + (j % per) * 64 + j // per
        for l in range(slabs):
            ext_ref[l, pl.ds(start, 8, stride=8), :] = un[8 * j:8 * j + 8, lanes[l]]
    first = lax.broadcasted_iota(I32, (hr, 128), 0) % 8 == 0
    for l in range(slabs):
        cur = ext_ref[l, nt * 8:nt * 8 + hr, :]
        ext_ref[l, 0:hr, :] = jnp.where(first, pltpu.roll(halo_ref[l], hr - 7, axis=0), pltpu.roll(cur, 1, axis=0))
        halo_ref[l] = cur

    for r in range(tb // CONV_ROWS):
        accs = []
        for l in range(slabs):
            acc = jnp.broadcast_to(cb_ref[:, lanes[l]], (CONV_ROWS, 128))
            for dt in range(CONV_K):
                off = hr + r * CONV_ROWS - dt * 8
                acc = acc + cw_ref[CONV_K - 1 - dt:CONV_K - dt, lanes[l]] * ext_ref[l, off:off + CONV_ROWS, :]
            accs.append(acc)
        mu = functools.reduce(jnp.add, [jnp.sum(a, axis=-1, keepdims=True) for a in accs]) * (1.0 / c)
        cens = [a - mu for a in accs]
        var = functools.reduce(jnp.add, [jnp.sum(a * a, axis=-1, keepdims=True) for a in cens]) * (1.0 / c)
        inv = lax.rsqrt(var + EPS)
        for l in range(slabs):
            ln = cens[l] * inv * lg_ref[:, lanes[l]] + lb_ref[:, lanes[l]]
            perm_ref[l, r * CONV_ROWS:(r + 1) * CONV_ROWS, :] = jax.nn.silu(ln)
    for j in range(nt):
        start = (j % per) * 64 + j // per
        for l in range(slabs):
            act_ref[8 * j:8 * j + 8, lanes[l]] = perm_ref[l, pl.ds(start, 8, stride=8), :]
    y_ref[...] = (gate_ref[...].astype(F32) * _dot(act_ref[...].astype(BF16), wo_ref[...])).astype(BF16)


def _conv(u, conv_w, conv_b, ln_g, ln_b, w_out_bf, gates, batch):
    n, c = u.shape
    d = w_out_bf.shape[1]
    tb = TOKEN_BLOCK
    nsb = n // batch // tb
    row = lambda b, s: (b * nsb + s, 0)
    fixed = lambda b, s: (0, 0)
    return pl.pallas_call(
        _conv_kernel,
        grid=(batch, nsb),
        in_specs=[pl.BlockSpec((tb, c), row), pl.BlockSpec(conv_w.shape, fixed), pl.BlockSpec((1, c), fixed),
                  pl.BlockSpec((1, c), fixed), pl.BlockSpec((1, c), fixed), pl.BlockSpec((c, d), fixed),
                  pl.BlockSpec((tb, d), row)],
        out_specs=pl.BlockSpec((tb, d), row),
        out_shape=jax.ShapeDtypeStruct((n, d), BF16),
        scratch_shapes=[pltpu.VMEM((c // 128, tb + CONV_HALO * 8, 128), F32),
                        pltpu.VMEM((c // 128, CONV_HALO * 8, 128), F32),
                        pltpu.VMEM((c // 128, tb, 128), F32), pltpu.VMEM((tb, c), F32)],
        compiler_params=_params("arbitrary", "arbitrary"),
        name="conv",
    )(u, conv_w, conv_b, ln_g, ln_b, w_out_bf, gates)


def _hgrn_chunk(q, fr, v, lb, st_ref, tri):
    heads = [slice(h * HGRN_DIM, (h + 1) * HGRN_DIM) for h in range(HGRN_HEADS)]
    f = lb + (1.0 - lb) * jax.nn.sigmoid(fr)
    lf = jnp.log(f)
    kk = 1.0 - f
    lf_hi, lf_lo = _split_bf16(lf)
    cum = _dot(tri, lf_hi) + _dot(tri, lf_lo)
    last = cum[CHUNK - 1:CHUNK, :]
    qe = (q * jnp.exp(cum)).astype(BF16)
    kd = (kk * jnp.exp(last - cum)).astype(BF16)
    decay = jnp.exp(last)

    blocks = []
    for i in range(CHUNK // SUB):
        rs, ne = i * SUB, (i + 1) * SUB
        ref = cum[rs - 1:rs, :] if i else jnp.zeros_like(last)
        qt = (q[rs:ne] * jnp.exp(cum[rs:ne] - ref)).astype(BF16)
        kt = (kk[0:ne] * jnp.exp(ref - cum[0:ne])).astype(BF16)
        blocks.append((qt, kt))

    states = [st_ref[h] for h in range(HGRN_HEADS)]
    o_inter = [_dot_nt(qe[:, hs], st.astype(BF16)) for hs, st in zip(heads, states)]
    scores = [[_dot_nt(qt[:, hs], kt[:, hs]) for qt, kt in blocks] for hs in heads]
    for h, hs in enumerate(heads):
        st_ref[h] = states[h] * decay[:, hs] + _dot_tn(v[:, hs], kd[:, hs])

    outs = []
    for h, hs in enumerate(heads):
        parts = []
        for i, a in enumerate(scores[h]):
            rs, ne = i * SUB, (i + 1) * SUB
            trow = lax.broadcasted_iota(I32, (SUB, ne), 0) + rs
            scol = lax.broadcasted_iota(I32, (SUB, ne), 1)
            a = jnp.where(scol <= trow, a, 0.0).astype(BF16)
            parts.append(_dot(a, v[0:ne, hs]))
        outs.append(o_inter[h] + jnp.concatenate(parts, axis=0))
    return jnp.concatenate(outs, axis=1)


def _hgrn_kernel(q_ref, fr_ref, iv_ref, og_ref, gate_ref, yc_ref, x_ref, lbl_ref, on_ref, wo_ref, wm_ref,
                 x1_ref, st_ref, ob_ref):
    tb = q_ref.shape[0]

    @pl.when(pl.program_id(1) == 0)
    def _():
        st_ref[...] = jnp.zeros_like(st_ref)

    l0, l1 = lbl_ref[0:1, :], lbl_ref[1:2, :]
    m = jnp.maximum(l0, l1)
    e0, e1 = jnp.exp(l0 - m), jnp.exp(l1 - m)
    lb_all = e0 / (e0 + e1)
    trow = lax.broadcasted_iota(I32, (CHUNK, CHUNK), 0)
    tcol = lax.broadcasted_iota(I32, (CHUNK, CHUNK), 1)
    tri = jnp.where(tcol <= trow, 1.0, 0.0).astype(BF16)

    def chunk(ci, carry):
        rows = pl.ds(pl.multiple_of(ci * CHUNK, CHUNK), CHUNK)
        o = _hgrn_chunk(q_ref[rows, :].astype(F32), fr_ref[rows, :], iv_ref[rows, :], lb_all, st_ref, tri)
        og = og_ref[rows, :].astype(F32)
        for h in range(HGRN_HEADS):
            hs = slice(h * HGRN_DIM, (h + 1) * HGRN_DIM)
            ob_ref[rows, hs] = (_rms(o[:, hs], on_ref[...]) * og[:, hs]).astype(BF16)
        return carry

    lax.fori_loop(0, tb // CHUNK, chunk, 0, unroll=4)
    y_rec = _dot(ob_ref[...], wo_ref[...])
    merged = yc_ref[...].astype(F32) + gate_ref[...].astype(F32) * y_rec
    x1_ref[...] = x_ref[...] + _dot(merged.astype(BF16), wm_ref[...])


def _hgrn(q, fr, iv, og, gates, yc, x2d, lb_logits, onorm_g, w_o_bf, w_mix_bf, batch):
    n, kw = q.shape
    d = x2d.shape[1]
    tb = TOKEN_BLOCK
    nsb = n // batch // tb
    row = lambda b, s: (b * nsb + s, 0)
    fixed = lambda b, s: (0, 0)
    return pl.pallas_call(
        _hgrn_kernel,
        grid=(batch, nsb),
        in_specs=[pl.BlockSpec((tb, kw), row), pl.BlockSpec((tb, kw), row), pl.BlockSpec((tb, kw), row),
                  pl.BlockSpec((tb, kw), row),
                  pl.BlockSpec((tb, d), lambda b, s: (b * nsb + s, 1)),
                  pl.BlockSpec((tb, d), row), pl.BlockSpec((tb, d), row),
                  pl.BlockSpec(lb_logits.shape, fixed), pl.BlockSpec((1, HGRN_DIM), fixed),
                  pl.BlockSpec(w_o_bf.shape, fixed), pl.BlockSpec(w_mix_bf.shape, fixed)],
        out_specs=pl.BlockSpec((tb, d), row),
        out_shape=jax.ShapeDtypeStruct((n, d), F32),
        scratch_shapes=[pltpu.VMEM((HGRN_HEADS, HGRN_DIM, HGRN_DIM), F32), pltpu.VMEM((tb, kw), BF16)],
        compiler_params=_params("arbitrary", "arbitrary"),
        name="hgrn",
    )(q, fr, iv, og, gates, yc, x2d, lb_logits, onorm_g, w_o_bf, w_mix_bf)


def _memkv_kernel(mem_ref, g_ref, wk_ref, wv_ref, k_ref, v_ref):
    mb = _rms(mem_ref[...], g_ref[...]).astype(BF16)
    k_ref[...] = _dot(mb, wk_ref[...]).astype(BF16)
    v_ref[...] = _dot(mb, wv_ref[...]).astype(BF16)


def _memkv(mem2d, g, wk_bf, wv_bf, batch):
    n, d = mem2d.shape
    m = n // batch
    row = lambda b: (b, 0)
    fixed = lambda b: (0, 0)
    return pl.pallas_call(
        _memkv_kernel,
        grid=(batch,),
        in_specs=[pl.BlockSpec((m, d), row), pl.BlockSpec((1, d), fixed), pl.BlockSpec((d, d), fixed),
                  pl.BlockSpec((d, d), fixed)],
        out_specs=[pl.BlockSpec((m, d), row)] * 2,
        out_shape=[jax.ShapeDtypeStruct((n, d), BF16)] * 2,
        compiler_params=_params("parallel"),
        name="memkv",
    )(mem2d, g, wk_bf, wv_bf)


def _route(lt):
    def row(r):
        return lt[r:r + 1, :]

    gl = [row(g) for g in range(N_GROUPS)]
    gmax = functools.reduce(jnp.maximum, gl)
    g_p = 1.0 / functools.reduce(jnp.add, [jnp.exp(l - gmax) for l in gl])
    gidx = jnp.full(gmax.shape, N_GROUPS - 1, I32)
    for g in range(N_GROUPS - 2, -1, -1):
        gidx = jnp.where(gl[g] == gmax, g, gidx)

    el = []
    for j in range(EXPERTS_PER_GROUP):
        v = row(N_GROUPS + (N_GROUPS - 1) * EXPERTS_PER_GROUP + j)
        for g in range(N_GROUPS - 2, -1, -1):
            v = jnp.where(gidx == g, row(N_GROUPS + g * EXPERTS_PER_GROUP + j), v)
        el.append(v)

    def argmax(vals):
        mx = functools.reduce(jnp.maximum, vals)
        idx = jnp.full(mx.shape, EXPERTS_PER_GROUP - 1, I32)
        for j in range(EXPERTS_PER_GROUP - 2, -1, -1):
            idx = jnp.where(vals[j] == mx, j, idx)
        return mx, idx

    m1, i1 = argmax(el)
    m2, i2 = argmax([jnp.where(i1 == j, -jnp.inf, el[j]) for j in range(EXPERTS_PER_GROUP)])
    r = jnp.exp(m2 - m1)
    w1 = g_p / (1.0 + r)
    w2 = g_p * r / (1.0 + r)
    base = gidx * EXPERTS_PER_GROUP
    return jnp.concatenate([base + i1, base + i2], axis=0), jnp.concatenate([w1, w2], axis=0)


def _excl_cumsum_rows(col):
    r = col.shape[0]
    lower = lax.broadcasted_iota(I32, (r, r), 1) < lax.broadcasted_iota(I32, (r, r), 0)
    lower_bf = jnp.where(lower, 1.0, 0.0).astype(BF16)
    hi, lo = _split_bf16(jnp.broadcast_to(col, (r, 128)))
    return (_dot(lower_bf, hi) + _dot(lower_bf, lo))[:, 0:1]


def _slot_masks(lpos, chunk, tb):
    slot = lax.broadcasted_iota(I32, (SORT_CHUNK, tb), 0) + chunk * SORT_CHUNK
    return slot == lpos[:, 0:tb], slot == lpos[:, tb:2 * tb]


def _local_sort(eid, wts, h2, upper_ref, xb_ref, cnt_ref, lpos_ref):
    tb, d = h2.shape
    na = TOP_K * tb
    e_all = jnp.concatenate([eid[k:k + 1] for k in range(TOP_K)], axis=1)
    onehot = lax.broadcasted_iota(I32, (N_EXPERTS, na), 0) == e_all
    ones = jnp.where(onehot, 1.0, 0.0)
    cnt = jnp.sum(ones, axis=1, keepdims=True)
    cnt_pad = jnp.floor((cnt + (GROUP - 1)) * (1.0 / GROUP)) * GROUP
    earlier = _dot(ones.astype(BF16), upper_ref[...])
    start = _excl_cumsum_rows(cnt_pad)
    lpos = jnp.sum(jnp.where(onehot, start + earlier, 0.0), axis=0, keepdims=True).astype(I32)
    hb = h2.astype(BF16)
    for c in range(LOCAL_ROWS // SORT_CHUNK):
        m0, m1 = _slot_masks(lpos, c, tb)
        p = jnp.where(m0, 1.0, jnp.where(m1, 1.0, 0.0)).astype(BF16)
        pw = jnp.where(m0, wts[0:1], jnp.where(m1, wts[1:2], 0.0))
        rows = slice(c * SORT_CHUNK, (c + 1) * SORT_CHUNK)
        xb_ref[rows, 0:d] = _dot(p, hb)
        xb_ref[rows, d:d + W_LANES] = jnp.broadcast_to(jnp.sum(pw, axis=1, keepdims=True), (SORT_CHUNK, W_LANES))
    cnt_ref[...] = cnt_pad.astype(I32)
    lpos_ref[...] = lpos


def _attn_kernel(x1_ref, gxa_ref, wq_ref, k_ref, v_ref, wo_ref, gffn_ref, wr_ref, br_ref, upper_ref,
                 x2_ref, xb_ref, cnt_ref, lpos_ref):
    x1 = x1_ref[...]
    d = x1.shape[1]
    hd = d // XA_HEADS
    q = _dot(_rms(x1, gxa_ref[...]).astype(BF16), wq_ref[...]).astype(BF16)
    heads = []
    for h in range(XA_HEADS):
        hs = slice(h * hd, (h + 1) * hd)
        sc = _dot_nt(q[:, hs], k_ref[:, hs]) * (hd ** -0.5)
        p = jnp.exp(sc - jnp.max(sc, axis=-1, keepdims=True))
        p = p / jnp.sum(p, axis=-1, keepdims=True)
        heads.append(_dot(p.astype(BF16), v_ref[:, hs]).astype(BF16))
    x2 = x1 + _dot(jnp.concatenate(heads, axis=1), wo_ref[...])
    x2_ref[...] = x2
    h2 = _rms(x2, gffn_ref[...])
    h_hi, h_lo = _split_bf16(h2)
    w_hi, w_lo = _split_bf16(wr_ref[...])
    lt = _dot_nt(w_hi, h_hi) + (_dot_nt(w_hi, h_lo) + _dot_nt(w_lo, h_hi)) + br_ref[...]
    eid, wts = _route(lt)
    _local_sort(eid, wts, h2, upper_ref, xb_ref, cnt_ref, lpos_ref)


def _attn(x1, gxa, wq_bf, k_bf, v_bf, wo_bf, gffn, w_route_t, b_route, upper, batch):
    n, d = x1.shape
    m = k_bf.shape[0] // batch
    tb = TOKEN_BLOCK
    nsb = n // batch // tb
    nb = n // tb
    row = lambda b, s: (b * nsb + s, 0)
    blk3 = lambda b, s: (b * nsb + s, 0, 0)
    fixed = lambda b, s: (0, 0)
    mem = lambda b, s: (b, 0)
    return pl.pallas_call(
        _attn_kernel,
        grid=(batch, nsb),
        in_specs=[pl.BlockSpec((tb, d), row), pl.BlockSpec((1, d), fixed), pl.BlockSpec((d, d), fixed),
                  pl.BlockSpec((m, d), mem), pl.BlockSpec((m, d), mem), pl.BlockSpec((d, d), fixed),
                  pl.BlockSpec((1, d), fixed), pl.BlockSpec((ROUTE_ROWS, d), fixed),
                  pl.BlockSpec((ROUTE_ROWS, 1), fixed), pl.BlockSpec(upper.shape, fixed)],
        out_specs=[pl.BlockSpec((tb, d), row), pl.BlockSpec((LOCAL_ROWS, d + W_LANES), row),
                   pl.BlockSpec((None, N_EXPERTS, 1), blk3), pl.BlockSpec((None, 1, TOP_K * tb), blk3)],
        out_shape=[jax.ShapeDtypeStruct((n, d), F32), jax.ShapeDtypeStruct((nb * LOCAL_ROWS, d + W_LANES), F32),
                   jax.ShapeDtypeStruct((nb, N_EXPERTS, 1), I32), jax.ShapeDtypeStruct((nb, 1, TOP_K * tb), I32)],
        compiler_params=_params("parallel", "parallel"),
        name="attn",
    )(x1, gxa, wq_bf, k_bf, v_bf, wo_bf, gffn, w_route_t, b_route, upper)


def _plan_kernel(cnt_ref, te_ref, src_ref, dst_ref, nact_ref, lrun_ref):
    nb = cnt_ref.shape[0]
    lg = LOCAL_ROWS // GROUP
    tg = MOE_TILE // GROUP
    shift_g = GROUP.bit_length() - 1
    shift_t = tg.bit_length() - 1

    def fill(ref, lo, hi, val):
        def body(i, c):
            ref[i] = jnp.int32(val)
            return c
        lax.fori_loop(lo, hi, body, 0)

    zero_group = lg - 1
    fill(lrun_ref, 0, nb, 0)

    def expert(e, g0):
        def block(b, g):
            ng = lax.shift_right_logical(cnt_ref[b, e], jnp.int32(shift_g))
            l0 = b * lg + lrun_ref[b]

            def group(i, c):
                src_ref[g + i] = l0 + i
                dst_ref[l0 + i] = g + i
                return c

            lax.fori_loop(0, ng, group, 0)
            lrun_ref[b] = lrun_ref[b] + ng
            return g + ng

        g1 = lax.fori_loop(0, nb, block, g0)
        nt = lax.shift_right_logical(g1 - g0 + (tg - 1), jnp.int32(shift_t))
        t0 = lax.shift_right_logical(g0, jnp.int32(shift_t))

        def tile(t, c):
            te_ref[t0 + t] = e
            return c

        lax.fori_loop(0, nt, tile, 0)
        fill(src_ref, g1, g0 + nt * tg, zero_group)
        return g0 + nt * tg

    g_end = lax.fori_loop(0, N_EXPERTS, expert, jnp.int32(0))
    nact = lax.shift_right_logical(g_end, jnp.int32(shift_t))
    fill(src_ref, g_end, src_ref.shape[0], zero_group)
    fill(te_ref, nact, te_ref.shape[0], N_EXPERTS)

    def unused(b, c):
        fill(dst_ref, b * lg + lrun_ref[b], (b + 1) * lg, 0)
        return c

    lax.fori_loop(0, nb, unused, 0)
    nact_ref[0] = nact


def _plan(cnt, n_tiles):
    nb = cnt.shape[0]
    smem = pl.BlockSpec(memory_space=pltpu.SMEM)
    return pl.pallas_call(
        _plan_kernel,
        in_specs=[smem],
        out_specs=[smem] * 4,
        out_shape=[jax.ShapeDtypeStruct((n_tiles,), I32), jax.ShapeDtypeStruct((n_tiles * (MOE_TILE // GROUP),), I32),
                   jax.ShapeDtypeStruct((nb * (LOCAL_ROWS // GROUP),), I32), jax.ShapeDtypeStruct((1,), I32)],
        scratch_shapes=[pltpu.SMEM((nb,), I32)],
        name="plan",
    )(cnt)


def _group_copy(src_hbm, src_group, dst_buf, slot, index, sem):
    start = src_group * GROUP
    rows = pl.ds(start if isinstance(start, int) else pl.multiple_of(start, GROUP), GROUP)
    return pltpu.make_async_copy(src_hbm.at[rows], dst_buf.at[slot, pl.ds(index * GROUP, GROUP)], sem.at[slot])


def _experts_kernel(te_ref, src_ref, nact_ref, xb_ref, wg_ref, wu_ref, wd_ref, ys_ref,
                    xbuf, sem, wg_bf, wu_bf, wd_bf):
    t = pl.program_id(0)
    tg = MOE_TILE // GROUP
    d = ys_ref.shape[1]
    nact = nact_ref[0]
    e = te_ref[t]
    slot = lax.rem(t, 2)

    def start_gather(tile):
        for i in range(tg):
            _group_copy(xb_ref, src_ref[tile * tg + i], xbuf, lax.rem(tile, 2), i, sem).start()

    @pl.when(jnp.logical_and(t == 0, nact > 0))
    def _():
        start_gather(t)

    @pl.when(t + 1 < nact)
    def _():
        start_gather(t + 1)

    @pl.when(jnp.logical_and(t < nact, jnp.logical_or(t == 0, e != te_ref[jnp.maximum(t - 1, 0)])))
    def _():
        wg_bf[...] = wg_ref[...].astype(BF16)
        wu_bf[...] = wu_ref[...].astype(BF16)
        wd_bf[...] = wd_ref[...].astype(BF16)

    @pl.when(t < nact)
    def _():
        for i in range(tg):
            _group_copy(xb_ref, 0, xbuf, slot, i, sem).wait()
        x = xbuf[slot, :, 0:d].astype(BF16)
        hid = jax.nn.silu(_dot(x, wg_bf[...])) * _dot(x, wu_bf[...])
        ys_ref[...] = _dot(hid.astype(BF16), wd_bf[...]) * xbuf[slot, :, d:d + 1]

    @pl.when(t >= nact)
    def _():
        ys_ref[...] = jnp.zeros_like(ys_ref)


def _experts(te, src, nact, xb, w_gate, w_up, w_down):
    n_tiles = te.shape[0]
    dw = xb.shape[1]
    d = dw - W_LANES
    ff = w_gate.shape[2]
    tm = MOE_TILE
    expert = lambda t, te, src, nact: (jnp.minimum(te[t], N_EXPERTS - 1), 0, 0)
    return pl.pallas_call(
        _experts_kernel,
        grid_spec=pltpu.PrefetchScalarGridSpec(
            num_scalar_prefetch=3,
            grid=(n_tiles,),
            in_specs=[pl.BlockSpec(memory_space=pl.ANY), pl.BlockSpec((None, d, ff), expert),
                      pl.BlockSpec((None, d, ff), expert), pl.BlockSpec((None, ff, d), expert)],
            out_specs=pl.BlockSpec((tm, d), lambda t, te, src, nact: (t, 0)),
            scratch_shapes=[pltpu.VMEM((2, tm, dw), F32), pltpu.SemaphoreType.DMA((2,)),
                            pltpu.VMEM((d, ff), BF16), pltpu.VMEM((d, ff), BF16), pltpu.VMEM((ff, d), BF16)],
        ),
        out_shape=jax.ShapeDtypeStruct((n_tiles * tm, d), F32),
        compiler_params=_params("arbitrary"),
        name="experts",
    )(te, src, nact, xb, w_gate, w_up, w_down)


def _combine_kernel(dst_ref, lpos_ref, x2_ref, g_ref, ys_ref, out_ref, ybuf, sem):
    b = pl.program_id(0)
    tb = x2_ref.shape[0]
    lg = LOCAL_ROWS // GROUP
    slot = lax.rem(b, 2)

    def start_gather(blk):
        for i in range(lg):
            _group_copy(ys_ref, dst_ref[blk * lg + i], ybuf, lax.rem(blk, 2), i, sem).start()

    @pl.when(b == 0)
    def _():
        start_gather(b)

    @pl.when(b + 1 < pl.num_programs(0))
    def _():
        start_gather(b + 1)

    for i in range(lg):
        _group_copy(ys_ref, 0, ybuf, slot, i, sem).wait()
    lpos = lpos_ref[...]
    y = jnp.zeros(x2_ref.shape, F32)
    for c in range(LOCAL_ROWS // SORT_CHUNK):
        m0, m1 = _slot_masks(lpos, c, tb)
        p = jnp.where(m0, 1.0, jnp.where(m1, 1.0, 0.0)).astype(BF16)
        y = y + _dot_tn(p, ybuf[slot, c * SORT_CHUNK:(c + 1) * SORT_CHUNK, :].astype(BF16))
    out_ref[...] = _rms(x2_ref[...] + y, g_ref[...])


def _combine(dst, lpos, x2, g_final, ys):
    n, d = x2.shape
    tb = TOKEN_BLOCK
    return pl.pallas_call(
        _combine_kernel,
        grid_spec=pltpu.PrefetchScalarGridSpec(
            num_scalar_prefetch=1,
            grid=(n // tb,),
            in_specs=[pl.BlockSpec((None, 1, TOP_K * tb), lambda i, dst: (i, 0, 0)),
                      pl.BlockSpec((tb, d), lambda i, dst: (i, 0)), pl.BlockSpec((1, d), lambda i, dst: (0, 0)),
                      pl.BlockSpec(memory_space=pl.ANY)],
            out_specs=pl.BlockSpec((tb, d), lambda i, dst: (i, 0)),
            scratch_shapes=[pltpu.VMEM((2, LOCAL_ROWS, d), F32), pltpu.SemaphoreType.DMA((2,))],
        ),
        out_shape=jax.ShapeDtypeStruct((n, d), F32),
        compiler_params=_params("arbitrary"),
        name="combine",
    )(dst, lpos, x2, g_final, ys)


def kernel(x, mem, norm_mix_g, w_in, conv_w, conv_b, conv_ln_g, conv_ln_b, conv_w_out, hgrn_lb_logits, hgrn_onorm_g, hgrn_w_out, w_mix_out, norm_xa_g, norm_mem_g, xa_w_q, xa_w_k, xa_w_v, xa_w_o, norm_ffn_g, router_group_w, router_group_b, router_expert_w, router_expert_b, moe_w_gate, moe_w_up, moe_w_down, final_norm_g):
    batch, seq, d = x.shape
    n = batch * seq
    assert w_in.shape[0] == 1, "the final RMSNorm is fused into the single layer's combine step"
    cw = conv_w.shape[2]
    kw = hgrn_w_out.shape[1]
    assert kw == HGRN_HEADS * HGRN_DIM and conv_w.shape[1] == CONV_K
    assert seq % TOKEN_BLOCK == 0 and TOKEN_BLOCK % CHUNK == 0 and TOKEN_BLOCK % CONV_ROWS == 0
    assert moe_w_gate.shape[1] == N_EXPERTS and router_group_w.shape[2] == N_GROUPS and TOP_K == 2
    na = TOP_K * TOKEN_BLOCK
    assert LOCAL_ROWS % SORT_CHUNK == 0 and LOCAL_ROWS >= na + N_EXPERTS * (GROUP - 1) + GROUP

    n_tiles = -(-(n // TOKEN_BLOCK) * (na + N_EXPERTS * (GROUP - 1)) // MOE_TILE) + N_EXPERTS
    vec = lambda p: p.reshape(1, -1)
    l = 0

    x2d = x.reshape(n, d)
    u, q, fr, iv, og, gates = _inproj(x2d, vec(norm_mix_g[l]), w_in[l].astype(BF16), cw, kw)
    yc = _conv(u, conv_w[l], vec(conv_b[l]), vec(conv_ln_g[l]), vec(conv_ln_b[l]),
               conv_w_out[l].astype(BF16), gates, batch)
    x1 = _hgrn(q, fr, iv, og, gates, yc, x2d, hgrn_lb_logits[l:l + 2], vec(hgrn_onorm_g[l]),
               hgrn_w_out[l].astype(BF16), w_mix_out[l].astype(BF16), batch)
    k_bf, v_bf = _memkv(mem.reshape(-1, d), vec(norm_mem_g[l]), xa_w_k[l].astype(BF16), xa_w_v[l].astype(BF16), batch)
    pad = ROUTE_ROWS - N_GROUPS - N_EXPERTS
    w_route_t = jnp.pad(jnp.concatenate([router_group_w[l], router_expert_w[l]], axis=1).T, ((0, pad), (0, 0)))
    b_route = jnp.pad(jnp.concatenate([router_group_b[l], router_expert_b[l]]), (0, pad)).reshape(ROUTE_ROWS, 1)
    upper = (lax.broadcasted_iota(I32, (na, na), 0) < lax.broadcasted_iota(I32, (na, na), 1)).astype(BF16)
    x2, xb, cnt, lpos = _attn(x1, vec(norm_xa_g[l]), xa_w_q[l].astype(BF16), k_bf, v_bf, xa_w_o[l].astype(BF16),
                              vec(norm_ffn_g[l]), w_route_t, b_route, upper, batch)
    te, src, dst, nact = _plan(cnt.reshape(cnt.shape[0], N_EXPERTS), n_tiles)
    ys = _experts(te, src, nact, xb, moe_w_gate[l], moe_w_up[l], moe_w_down[l])
    out = _combine(dst, lpos, x2, vec(final_norm_g), ys)
    return out.reshape(batch, seq, d)
```

```python
import functools

import jax
import jax.numpy as jnp
from jax import lax
from jax.experimental import pallas as pl
from jax.experimental.pallas import tpu as pltpu

F32 = jnp.float32
BF16 = jnp.bfloat16
I32 = jnp.int32

EPS = 1e-6
CONV_K = 31
CONV_HALO = 32
CONV_ROWS = 64
HGRN_HEADS = 4
HGRN_DIM = 128
CHUNK = 64
SUB = 16
XA_HEADS = 4
N_GROUPS = 4
EXPERTS_PER_GROUP = 8
N_EXPERTS = N_GROUPS * EXPERTS_PER_GROUP
TOP_K = 2
ROUTE_ROWS = 40
TOKEN_BLOCK = 512
MOE_TILE = 256
GROUP = 8
LOCAL_ROWS = 1280
SORT_CHUNK = 256
W_LANES = 128
PLAN_CHUNKS = (512, 256, 128)
VMEM_LIMIT_BYTES = 48 * 1024 * 1024


def _rms(x, g):
    return x * lax.rsqrt(jnp.mean(x * x, axis=-1, keepdims=True) + EPS) * g


def _dot(a, b):
    return jnp.dot(a, b, preferred_element_type=F32)


def _dot_nt(a, b):
    return lax.dot_general(a, b, (((1,), (1,)), ((), ())), preferred_element_type=F32)


def _dot_tn(a, b):
    return lax.dot_general(a, b, (((0,), (0,)), ((), ())), preferred_element_type=F32)


def _split_bf16(x):
    hi = x.astype(BF16)
    lo = (x - hi.astype(F32)).astype(BF16)
    return hi, lo


def _params(*sem):
    return pltpu.CompilerParams(dimension_semantics=sem, vmem_limit_bytes=VMEM_LIMIT_BYTES)


def _inproj_kernel(x_ref, g_ref, w_ref, u_ref, q_ref, fr_ref, iv_ref, og_ref, gates_ref, *, cw, kw):
    hb = _rms(x_ref[...], g_ref[...]).astype(BF16)

    def proj(lo, width):
        return _dot(hb, w_ref[:, lo:lo + width])

    u_ref[...] = (proj(0, cw) * jax.nn.sigmoid(proj(cw, cw))).astype(BF16)
    base = 2 * cw
    q_ref[...] = jax.nn.silu(proj(base, kw)).astype(BF16)
    fr_ref[...] = proj(base + kw, kw)
    iv_ref[...] = proj(base + 2 * kw, kw).astype(BF16)
    og_ref[...] = jax.nn.silu(proj(base + 3 * kw, kw)).astype(BF16)
    base += 4 * kw
    for c in range(gates_ref.shape[1] // kw):
        gates_ref[:, c * kw:(c + 1) * kw] = jax.nn.sigmoid(proj(base + c * kw, kw)).astype(BF16)


def _inproj(x2d, g, w_bf, cw, kw):
    n, d = x2d.shape
    tb = TOKEN_BLOCK
    row = lambda i: (i, 0)
    fixed = lambda i: (0, 0)
    outs = [((n, cw), BF16), ((n, kw), BF16), ((n, kw), F32), ((n, kw), BF16), ((n, kw), BF16), ((n, 2 * d), BF16)]
    return pl.pallas_call(
        functools.partial(_inproj_kernel, cw=cw, kw=kw),
        grid=(n // tb,),
        in_specs=[pl.BlockSpec((tb, d), row), pl.BlockSpec((1, d), fixed), pl.BlockSpec(w_bf.shape, fixed)],
        out_specs=[pl.BlockSpec((tb, s[1]), row) for s, _ in outs],
        out_shape=[jax.ShapeDtypeStruct(s, t) for s, t in outs],
        compiler_params=_params("parallel"),
        name="inproj",
    )(x2d, g, w_bf)


def _conv_kernel(u_ref, cw_ref, cb_ref, lg_ref, lb_ref, wo_ref, gate_ref, y_ref, ext_ref, halo_ref, perm_ref, act_ref):
    tb, c = u_ref.shape
    nt = tb // 8
    slabs = c // 128
    hr = CONV_HALO * 8
    lanes = [slice(l * 128, (l + 1) * 128) for l in range(slabs)]

    @pl.when(pl.program_id(1) == 0)
    def _():
        halo_ref[...] = jnp.zeros_like(halo_ref)

    un = u_ref[...].astype(F32)
    per = nt // 8
    for j in range(nt):
        start = hr + (j % per) * 64 + j // per
        for l in range(slabs):
            ext_ref[l, pl.ds(start, 8, stride=8), :] = un[8 * j:8 * j + 8, lanes[l]]
    first = lax.broadcasted_iota(I32, (hr, 128), 0) % 8 == 0
    for l in range(slabs):
        cur = ext_ref[l, nt * 8:nt * 8 + hr, :]
        ext_ref[l, 0:hr, :] = jnp.where(first, pltpu.roll(halo_ref[l], hr - 7, axis=0), pltpu.roll(cur, 1, axis=0))
        halo_ref[l] = cur

    for r in range(tb // CONV_ROWS):
        accs = []
        for l in range(slabs):
            acc = jnp.broadcast_to(cb_ref[:, lanes[l]], (CONV_ROWS, 128))
            for dt in range(CONV_K):
                off = hr + r * CONV_ROWS - dt * 8
                acc = acc + cw_ref[CONV_K - 1 - dt:CONV_K - dt, lanes[l]] * ext_ref[l, off:off + CONV_ROWS, :]
            accs.append(acc)
        mu = functools.reduce(jnp.add, [jnp.sum(a, axis=-1, keepdims=True) for a in accs]) * (1.0 / c)
        cens = [a - mu for a in accs]
        var = functools.reduce(jnp.add, [jnp.sum(a * a, axis=-1, keepdims=True) for a in cens]) * (1.0 / c)
        inv = lax.rsqrt(var + EPS)
        for l in range(slabs):
            ln = cens[l] * inv * lg_ref[:, lanes[l]] + lb_ref[:, lanes[l]]
            perm_ref[l, r * CONV_ROWS:(r + 1) * CONV_ROWS, :] = jax.nn.silu(ln)
    for j in range(nt):
        start = (j % per) * 64 + j // per
        for l in range(slabs):
            act_ref[8 * j:8 * j + 8, lanes[l]] = perm_ref[l, pl.ds(start, 8, stride=8), :]
    y_ref[...] = (gate_ref[...].astype(F32) * _dot(act_ref[...].astype(BF16), wo_ref[...])).astype(BF16)


def _conv(u, conv_w, conv_b, ln_g, ln_b, w_out_bf, gates, batch):
    n, c = u.shape
    d = w_out_bf.shape[1]
    tb = TOKEN_BLOCK
    nsb = n // batch // tb
    row = lambda b, s: (b * nsb + s, 0)
    fixed = lambda b, s: (0, 0)
    return pl.pallas_call(
        _conv_kernel,
        grid=(batch, nsb),
        in_specs=[pl.BlockSpec((tb, c), row), pl.BlockSpec(conv_w.shape, fixed), pl.BlockSpec((1, c), fixed),
                  pl.BlockSpec((1, c), fixed), pl.BlockSpec((1, c), fixed), pl.BlockSpec((c, d), fixed),
                  pl.BlockSpec((tb, d), row)],
        out_specs=pl.BlockSpec((tb, d), row),
        out_shape=jax.ShapeDtypeStruct((n, d), BF16),
        scratch_shapes=[pltpu.VMEM((c // 128, tb + CONV_HALO * 8, 128), F32),
                        pltpu.VMEM((c // 128, CONV_HALO * 8, 128), F32),
                        pltpu.VMEM((c // 128, tb, 128), F32), pltpu.VMEM((tb, c), F32)],
        compiler_params=_params("arbitrary", "arbitrary"),
        name="conv",
    )(u, conv_w, conv_b, ln_g, ln_b, w_out_bf, gates)


def _hgrn_chunk(q, fr, v, lb, st_ref, tri):
    heads = [slice(h * HGRN_DIM, (h + 1) * HGRN_DIM) for h in range(HGRN_HEADS)]
    f = lb + (1.0 - lb) * jax.nn.sigmoid(fr)
    lf = jnp.log(f)
    kk = 1.0 - f
    lf_hi, lf_lo = _split_bf16(lf)
    cum = _dot(tri, lf_hi) + _dot(tri, lf_lo)
    last = cum[CHUNK - 1:CHUNK, :]
    qe = (q * jnp.exp(cum)).astype(BF16)
    kd = (kk * jnp.exp(last - cum)).astype(BF16)
    decay = jnp.exp(last)

    blocks = []
    for i in range(CHUNK // SUB):
        rs, ne = i * SUB, (i + 1) * SUB
        ref = cum[rs - 1:rs, :] if i else jnp.zeros_like(last)
        qt = (q[rs:ne] * jnp.exp(cum[rs:ne] - ref)).astype(BF16)
        kt = (kk[0:ne] * jnp.exp(ref - cum[0:ne])).astype(BF16)
        blocks.append((qt, kt))

    states = [st_ref[h] for h in range(HGRN_HEADS)]
    o_inter = [_dot_nt(qe[:, hs], st.astype(BF16)) for hs, st in zip(heads, states)]
    scores = [[_dot_nt(qt[:, hs], kt[:, hs]) for qt, kt in blocks] for hs in heads]
    for h, hs in enumerate(heads):
        st_ref[h] = states[h] * decay[:, hs] + _dot_tn(v[:, hs], kd[:, hs])

    outs = []
    for h, hs in enumerate(heads):
        parts = []
        for i, a in enumerate(scores[h]):
            rs, ne = i * SUB, (i + 1) * SUB
            trow = lax.broadcasted_iota(I32, (SUB, ne), 0) + rs
            scol = lax.broadcasted_iota(I32, (SUB, ne), 1)
            a = jnp.where(scol <= trow, a, 0.0).astype(BF16)
            parts.append(_dot(a, v[0:ne, hs]))
        outs.append(o_inter[h] + jnp.concatenate(parts, axis=0))
    return jnp.concatenate(outs, axis=1)


def _hgrn_kernel(q_ref, fr_ref, iv_ref, og_ref, gate_ref, yc_ref, x_ref, lbl_ref, on_ref, wo_ref, wm_ref,
                 x1_ref, st_ref, ob_ref):
    tb = q_ref.shape[0]

    @pl.when(pl.program_id(1) == 0)
    def _():
        st_ref[...] = jnp.zeros_like(st_ref)

    l0, l1 = lbl_ref[0:1, :], lbl_ref[1:2, :]
    m = jnp.maximum(l0, l1)
    e0, e1 = jnp.exp(l0 - m), jnp.exp(l1 - m)
    lb_all = e0 / (e0 + e1)
    trow = lax.broadcasted_iota(I32, (CHUNK, CHUNK), 0)
    tcol = lax.broadcasted_iota(I32, (CHUNK, CHUNK), 1)
    tri = jnp.where(tcol <= trow, 1.0, 0.0).astype(BF16)

    def chunk(ci, carry):
        rows = pl.ds(pl.multiple_of(ci * CHUNK, CHUNK), CHUNK)
        o = _hgrn_chunk(q_ref[rows, :].astype(F32), fr_ref[rows, :], iv_ref[rows, :], lb_all, st_ref, tri)
        og = og_ref[rows, :].astype(F32)
        for h in range(HGRN_HEADS):
            hs = slice(h * HGRN_DIM, (h + 1) * HGRN_DIM)
            ob_ref[rows, hs] = (_rms(o[:, hs], on_ref[...]) * og[:, hs]).astype(BF16)
        return carry

    lax.fori_loop(0, tb // CHUNK, chunk, 0, unroll=4)
    y_rec = _dot(ob_ref[...], wo_ref[...])
    merged = yc_ref[...].astype(F32) + gate_ref[...].astype(F32) * y_rec
    x1_ref[...] = x_ref[...] + _dot(merged.astype(BF16), wm_ref[...])


def _hgrn(q, fr, iv, og, gates, yc, x2d, lb_logits, onorm_g, w_o_bf, w_mix_bf, batch):
    n, kw = q.shape
    d = x2d.shape[1]
    tb = TOKEN_BLOCK
    nsb = n // batch // tb
    row = lambda b, s: (b * nsb + s, 0)
    fixed = lambda b, s: (0, 0)
    return pl.pallas_call(
        _hgrn_kernel,
        grid=(batch, nsb),
        in_specs=[pl.BlockSpec((tb, kw), row), pl.BlockSpec((tb, kw), row), pl.BlockSpec((tb, kw), row),
                  pl.BlockSpec((tb, kw), row),
                  pl.BlockSpec((tb, d), lambda b, s: (b * nsb + s, 1)),
                  pl.BlockSpec((tb, d), row), pl.BlockSpec((tb, d), row),
                  pl.BlockSpec(lb_logits.shape, fixed), pl.BlockSpec((1, HGRN_DIM), fixed),
                  pl.BlockSpec(w_o_bf.shape, fixed), pl.BlockSpec(w_mix_bf.shape, fixed)],
        out_specs=pl.BlockSpec((tb, d), row),
        out_shape=jax.ShapeDtypeStruct((n, d), F32),
        scratch_shapes=[pltpu.VMEM((HGRN_HEADS, HGRN_DIM, HGRN_DIM), F32), pltpu.VMEM((tb, kw), BF16)],
        compiler_params=_params("arbitrary", "arbitrary"),
        name="hgrn",
    )(q, fr, iv, og, gates, yc, x2d, lb_logits, onorm_g, w_o_bf, w_mix_bf)


def _memkv_kernel(mem_ref, g_ref, wk_ref, wv_ref, k_ref, v_ref):
    mb = _rms(mem_ref[...], g_ref[...]).astype(BF16)
    k_ref[...] = _dot(mb, wk_ref[...]).astype(BF16)
    v_ref[...] = _dot(mb, wv_ref[...]).astype(BF16)


def _memkv(mem2d, g, wk_bf, wv_bf, batch):
    n, d = mem2d.shape
    m = n // batch
    row = lambda b: (b, 0)
    fixed = lambda b: (0, 0)
    return pl.pallas_call(
        _memkv_kernel,
        grid=(batch,),
        in_specs=[pl.BlockSpec((m, d), row), pl.BlockSpec((1, d), fixed), pl.BlockSpec((d, d), fixed),
                  pl.BlockSpec((d, d), fixed)],
        out_specs=[pl.BlockSpec((m, d), row)] * 2,
        out_shape=[jax.ShapeDtypeStruct((n, d), BF16)] * 2,
        compiler_params=_params("parallel"),
        name="memkv",
    )(mem2d, g, wk_bf, wv_bf)


def _route(lt):
    def row(r):
        return lt[r:r + 1, :]

    gl = [row(g) for g in range(N_GROUPS)]
    gmax = functools.reduce(jnp.maximum, gl)
    g_p = 1.0 / functools.reduce(jnp.add, [jnp.exp(l - gmax) for l in gl])
    gidx = jnp.full(gmax.shape, N_GROUPS - 1, I32)
    for g in range(N_GROUPS - 2, -1, -1):
        gidx = jnp.where(gl[g] == gmax, g, gidx)

    el = []
    for j in range(EXPERTS_PER_GROUP):
        v = row(N_GROUPS + (N_GROUPS - 1) * EXPERTS_PER_GROUP + j)
        for g in range(N_GROUPS - 2, -1, -1):
            v = jnp.where(gidx == g, row(N_GROUPS + g * EXPERTS_PER_GROUP + j), v)
        el.append(v)

    def argmax(vals):
        mx = functools.reduce(jnp.maximum, vals)
        idx = jnp.full(mx.shape, EXPERTS_PER_GROUP - 1, I32)
        for j in range(EXPERTS_PER_GROUP - 2, -1, -1):
            idx = jnp.where(vals[j] == mx, j, idx)
        return mx, idx

    m1, i1 = argmax(el)
    m2, i2 = argmax([jnp.where(i1 == j, -jnp.inf, el[j]) for j in range(EXPERTS_PER_GROUP)])
    r = jnp.exp(m2 - m1)
    w1 = g_p / (1.0 + r)
    w2 = g_p * r / (1.0 + r)
    base = gidx * EXPERTS_PER_GROUP
    return jnp.concatenate([base + i1, base + i2], axis=0), jnp.concatenate([w1, w2], axis=0)


def _excl_cumsum_rows(col):
    r = col.shape[0]
    lower = lax.broadcasted_iota(I32, (r, r), 1) < lax.broadcasted_iota(I32, (r, r), 0)
    lower_bf = jnp.where(lower, 1.0, 0.0).astype(BF16)
    hi, lo = _split_bf16(jnp.broadcast_to(col, (r, 128)))
    return (_dot(lower_bf, hi) + _dot(lower_bf, lo))[:, 0:1]


def _slot_masks(lpos, chunk, tb):
    slot = lax.broadcasted_iota(I32, (SORT_CHUNK, tb), 0) + chunk * SORT_CHUNK
    return slot == lpos[:, 0:tb], slot == lpos[:, tb:2 * tb]


def _local_sort(eid, wts, h2, upper_ref, xb_ref, cnt_ref, lpos_ref):
    tb, d = h2.shape
    na = TOP_K * tb
    e_all = jnp.concatenate([eid[k:k + 1] for k in range(TOP_K)], axis=1)
    onehot = lax.broadcasted_iota(I32, (N_EXPERTS, na), 0) == e_all
    ones = jnp.where(onehot, 1.0, 0.0)
    cnt = jnp.sum(ones, axis=1, keepdims=True)
    cnt_pad = jnp.floor((cnt + (GROUP - 1)) * (1.0 / GROUP)) * GROUP
    earlier = _dot(ones.astype(BF16), upper_ref[...])
    start = _excl_cumsum_rows(cnt_pad)
    lpos = jnp.sum(jnp.where(onehot, start + earlier, 0.0), axis=0, keepdims=True).astype(I32)
    hb = h2.astype(BF16)
    for c in range(LOCAL_ROWS // SORT_CHUNK):
        m0, m1 = _slot_masks(lpos, c, tb)
        p = jnp.where(m0, 1.0, jnp.where(m1, 1.0, 0.0)).astype(BF16)
        pw = jnp.where(m0, wts[0:1], jnp.where(m1, wts[1:2], 0.0))
        rows = slice(c * SORT_CHUNK, (c + 1) * SORT_CHUNK)
        xb_ref[rows, 0:d] = _dot(p, hb)
        xb_ref[rows, d:d + W_LANES] = jnp.broadcast_to(jnp.sum(pw, axis=1, keepdims=True), (SORT_CHUNK, W_LANES))
    cnt_ref[...] = cnt_pad.astype(I32)
    lpos_ref[...] = lpos


def _attn_kernel(x1_ref, gxa_ref, wq_ref, k_ref, v_ref, wo_ref, gffn_ref, wr_ref, br_ref, upper_ref,
                 x2_ref, xb_ref, cnt_ref, lpos_ref):
    x1 = x1_ref[...]
    d = x1.shape[1]
    hd = d // XA_HEADS
    q = _dot(_rms(x1, gxa_ref[...]).astype(BF16), wq_ref[...]).astype(BF16)
    heads = []
    for h in range(XA_HEADS):
        hs = slice(h * hd, (h + 1) * hd)
        sc = _dot_nt(q[:, hs], k_ref[:, hs]) * (hd ** -0.5)
        p = jnp.exp(sc - jnp.max(sc, axis=-1, keepdims=True))
        p = p / jnp.sum(p, axis=-1, keepdims=True)
        heads.append(_dot(p.astype(BF16), v_ref[:, hs]).astype(BF16))
    x2 = x1 + _dot(jnp.concatenate(heads, axis=1), wo_ref[...])
    x2_ref[...] = x2
    h2 = _rms(x2, gffn_ref[...])
    h_hi, h_lo = _split_bf16(h2)
    w_hi, w_lo = _split_bf16(wr_ref[...])
    lt = _dot_nt(w_hi, h_hi) + (_dot_nt(w_hi, h_lo) + _dot_nt(w_lo, h_hi)) + br_ref[...]
    eid, wts = _route(lt)
    _local_sort(eid, wts, h2, upper_ref, xb_ref, cnt_ref, lpos_ref)


def _attn(x1, gxa, wq_bf, k_bf, v_bf, wo_bf, gffn, w_route_t, b_route, upper, batch):
    n, d = x1.shape
    m = k_bf.shape[0] // batch
    tb = TOKEN_BLOCK
    nsb = n // batch // tb
    nb = n // tb
    row = lambda b, s: (b * nsb + s, 0)
    blk3 = lambda b, s: (b * nsb + s, 0, 0)
    fixed = lambda b, s: (0, 0)
    mem = lambda b, s: (b, 0)
    return pl.pallas_call(
        _attn_kernel,
        grid=(batch, nsb),
        in_specs=[pl.BlockSpec((tb, d), row), pl.BlockSpec((1, d), fixed), pl.BlockSpec((d, d), fixed),
                  pl.BlockSpec((m, d), mem), pl.BlockSpec((m, d), mem), pl.BlockSpec((d, d), fixed),
                  pl.BlockSpec((1, d), fixed), pl.BlockSpec((ROUTE_ROWS, d), fixed),
                  pl.BlockSpec((ROUTE_ROWS, 1), fixed), pl.BlockSpec(upper.shape, fixed)],
        out_specs=[pl.BlockSpec((tb, d), row), pl.BlockSpec((LOCAL_ROWS, d + W_LANES), row),
                   pl.BlockSpec((None, N_EXPERTS, 1), blk3), pl.BlockSpec((None, 1, TOP_K * tb), blk3)],
        out_shape=[jax.ShapeDtypeStruct((n, d), F32), jax.ShapeDtypeStruct((nb * LOCAL_ROWS, d + W_LANES), F32),
                   jax.ShapeDtypeStruct((nb, N_EXPERTS, 1), I32), jax.ShapeDtypeStruct((nb, 1, TOP_K * tb), I32)],
        compiler_params=_params("parallel", "parallel"),
        name="attn",
    )(x1, gxa, wq_bf, k_bf, v_bf, wo_bf, gffn, w_route_t, b_route, upper)


def _plan_kernel(cnt_ref, te_ref, src_ref, dst_ref, nact_ref):
    runs = cnt_ref.shape[0]
    lg = LOCAL_ROWS // GROUP
    tg = MOE_TILE // GROUP
    sh = N_EXPERTS.bit_length() - 1
    emask = N_EXPERTS - 1
    zero_group = lg - 1

    length = cnt_ref[...].astype(F32) * (1.0 / GROUP)
    len_bf = jnp.broadcast_to(length, (runs, 128)).astype(BF16)
    ri = lax.broadcasted_iota(I32, (runs, runs), 0)
    ci = lax.broadcasted_iota(I32, (runs, runs), 1)
    r_e, c_e = ri & emask, ci & emask
    r_b, c_b = lax.shift_right_logical(ri, sh), lax.shift_right_logical(ci, sh)
    same_expert_earlier = jnp.where(r_e == c_e, jnp.where(c_b < r_b, 1.0, 0.0), 0.0).astype(BF16)
    same_block_earlier = jnp.where(r_b == c_b, jnp.where(c_e < r_e, 1.0, 0.0), 0.0).astype(BF16)
    before = _dot(same_expert_earlier, len_bf)[:, 0:1]
    local = _dot(same_block_earlier, len_bf)[:, 0:1]

    of_expert = (lax.broadcasted_iota(I32, (N_EXPERTS, runs), 1) & emask) == lax.broadcasted_iota(
        I32, (N_EXPERTS, runs), 0)
    total = _dot(jnp.where(of_expert, 1.0, 0.0).astype(BF16), len_bf)[:, 0:1]
    tiles = jnp.floor((total + (tg - 1)) * (1.0 / tg))
    tile0 = _excl_cumsum_rows(tiles)
    to_run = (lax.broadcasted_iota(I32, (runs, N_EXPERTS), 0) & emask) == lax.broadcasted_iota(
        I32, (runs, N_EXPERTS), 1)
    to_run_bf = jnp.where(to_run, 1.0, 0.0).astype(BF16)
    base_hi, base_lo = _split_bf16(jnp.broadcast_to(tile0 * tg, (N_EXPERTS, 128)))
    g_start = (_dot(to_run_bf, base_hi) + _dot(to_run_bf, base_lo))[:, 0:1] + before
    block = lax.shift_right_logical(lax.broadcasted_iota(I32, (runs, 1), 0), sh).astype(F32)
    l_start = block * lg + local

    def cover(out_ref, start, offset, default):
        stop = start + length
        shift = offset - default
        n_out = out_ref.shape[1]
        chunk = next(c for c in PLAN_CHUNKS if n_out % c == 0)
        for c in range(n_out // chunk):
            j = (lax.broadcasted_iota(I32, (runs, chunk), 1) + c * chunk).astype(F32)
            hit = jnp.where(start <= j, jnp.where(j < stop, j + shift, 0.0), 0.0)
            out = jnp.sum(hit, axis=0, keepdims=True) + default
            out_ref[:, c * chunk:(c + 1) * chunk] = out.astype(I32)

    cover(src_ref, g_start, l_start - g_start, float(zero_group))
    cover(dst_ref, l_start, g_start - l_start, 0.0)
    t = lax.broadcasted_iota(I32, (N_EXPERTS, te_ref.shape[1]), 1).astype(F32)
    te_ref[...] = jnp.sum(jnp.where(t >= tile0 + tiles, 1.0, 0.0), axis=0, keepdims=True).astype(I32)
    nact_ref[...] = jnp.sum(jnp.broadcast_to(tiles, (N_EXPERTS, 128)), axis=0, keepdims=True).astype(I32)


def _plan(cnt, n_tiles):
    runs = cnt.shape[0]
    n_src = n_tiles * (MOE_TILE // GROUP)
    n_dst = runs // N_EXPERTS * (LOCAL_ROWS // GROUP)
    n_te = -(-n_tiles // 128) * 128
    te, src, dst, nact = pl.pallas_call(
        _plan_kernel,
        out_shape=[jax.ShapeDtypeStruct((1, n_te), I32), jax.ShapeDtypeStruct((1, n_src), I32),
                   jax.ShapeDtypeStruct((1, n_dst), I32), jax.ShapeDtypeStruct((1, 128), I32)],
        compiler_params=_params(),
        name="plan",
    )(cnt)
    return te.reshape(-1), src.reshape(-1), dst.reshape(-1), nact[0, :1]


def _group_copy(src_hbm, src_group, dst_buf, slot, index, sem):
    start = src_group * GROUP
    rows = pl.ds(start if isinstance(start, int) else pl.multiple_of(start, GROUP), GROUP)
    return pltpu.make_async_copy(src_hbm.at[rows], dst_buf.at[slot, pl.ds(index * GROUP, GROUP)], sem.at[slot])


def _experts_kernel(te_ref, src_ref, nact_ref, xb_ref, wg_hbm, wu_hbm, wd_hbm, ys_ref,
                    xbuf, sem, wg_st, wu_st, wd_st, wsem, wg_bf, wu_bf, wd_bf):
    t = pl.program_id(0)
    tg = MOE_TILE // GROUP
    d = ys_ref.shape[1]
    nact = nact_ref[0]
    e = te_ref[t]
    slot = lax.rem(t, 2)
    last = te_ref.shape[0] - 1

    def start_gather(tile):
        for i in range(tg):
            _group_copy(xb_ref, src_ref[tile * tg + i], xbuf, lax.rem(tile, 2), i, sem).start()

    def weight_copies(expert):
        pairs = ((wg_hbm, wg_st), (wu_hbm, wu_st), (wd_hbm, wd_st))
        return [pltpu.make_async_copy(w.at[expert], st, wsem.at[i]) for i, (w, st) in enumerate(pairs)]

    @pl.when(jnp.logical_and(t == 0, nact > 0))
    def _():
        start_gather(t)
        for c in weight_copies(e):
            c.start()

    @pl.when(t + 1 < nact)
    def _():
        start_gather(t + 1)

    @pl.when(jnp.logical_and(t < nact, jnp.logical_or(t == 0, e != te_ref[jnp.maximum(t - 1, 0)])))
    def _():
        for c in weight_copies(e):
            c.wait()
        wg_bf[...] = wg_st[...].astype(BF16)
        wu_bf[...] = wu_st[...].astype(BF16)
        wd_bf[...] = wd_st[...].astype(BF16)
        nxt = lax.while_loop(lambda j: jnp.logical_and(j < nact, te_ref[jnp.minimum(j, last)] == e),
                             lambda j: j + 1, t + 1)

        @pl.when(nxt < nact)
        def _():
            for c in weight_copies(te_ref[jnp.minimum(nxt, last)]):
                c.start()

    @pl.when(t < nact)
    def _():
        for i in range(tg):
            _group_copy(xb_ref, 0, xbuf, slot, i, sem).wait()
        x = xbuf[slot, :, 0:d].astype(BF16)
        hid = jax.nn.silu(_dot(x, wg_bf[...])) * _dot(x, wu_bf[...])
        ys_ref[...] = _dot(hid.astype(BF16), wd_bf[...]) * xbuf[slot, :, d:d + 1]

    @pl.when(t >= nact)
    def _():
        ys_ref[...] = jnp.zeros_like(ys_ref)


def _experts(te, src, nact, xb, w_gate, w_up, w_down):
    n_tiles = te.shape[0]
    dw = xb.shape[1]
    d = dw - W_LANES
    ff = w_gate.shape[2]
    tm = MOE_TILE
    hbm = pl.BlockSpec(memory_space=pl.ANY)
    return pl.pallas_call(
        _experts_kernel,
        grid_spec=pltpu.PrefetchScalarGridSpec(
            num_scalar_prefetch=3,
            grid=(n_tiles,),
            in_specs=[hbm, hbm, hbm, hbm],
            out_specs=pl.BlockSpec((tm, d), lambda t, te, src, nact: (t, 0)),
            scratch_shapes=[pltpu.VMEM((2, tm, dw), F32), pltpu.SemaphoreType.DMA((2,)),
                            pltpu.VMEM((d, ff), F32), pltpu.VMEM((d, ff), F32), pltpu.VMEM((ff, d), F32),
                            pltpu.SemaphoreType.DMA((3,)),
                            pltpu.VMEM((d, ff), BF16), pltpu.VMEM((d, ff), BF16), pltpu.VMEM((ff, d), BF16)],
        ),
        out_shape=jax.ShapeDtypeStruct((n_tiles * tm, d), F32),
        compiler_params=_params("arbitrary"),
        name="experts",
    )(te, src, nact, xb, w_gate, w_up, w_down)


def _combine_kernel(dst_ref, lpos_ref, x2_ref, g_ref, ys_ref, out_ref, ybuf, sem):
    b = pl.program_id(0)
    tb = x2_ref.shape[0]
    lg = LOCAL_ROWS // GROUP
    slot = lax.rem(b, 2)

    def start_gather(blk):
        for i in range(lg):
            _group_copy(ys_ref, dst_ref[blk * lg + i], ybuf, lax.rem(blk, 2), i, sem).start()

    @pl.when(b == 0)
    def _():
        start_gather(b)

    @pl.when(b + 1 < pl.num_programs(0))
    def _():
        start_gather(b + 1)

    for i in range(lg):
        _group_copy(ys_ref, 0, ybuf, slot, i, sem).wait()
    lpos = lpos_ref[...]
    y = jnp.zeros(x2_ref.shape, F32)
    for c in range(LOCAL_ROWS // SORT_CHUNK):
        m0, m1 = _slot_masks(lpos, c, tb)
        p = jnp.where(m0, 1.0, jnp.where(m1, 1.0, 0.0)).astype(BF16)
        y = y + _dot_tn(p, ybuf[slot, c * SORT_CHUNK:(c + 1) * SORT_CHUNK, :].astype(BF16))
    out_ref[...] = _rms(x2_ref[...] + y, g_ref[...])


def _combine(dst, lpos, x2, g_final, ys):
    n, d = x2.shape
    tb = TOKEN_BLOCK
    return pl.pallas_call(
        _combine_kernel,
        grid_spec=pltpu.PrefetchScalarGridSpec(
            num_scalar_prefetch=1,
            grid=(n // tb,),
            in_specs=[pl.BlockSpec((None, 1, TOP_K * tb), lambda i, dst: (i, 0, 0)),
                      pl.BlockSpec((tb, d), lambda i, dst: (i, 0)), pl.BlockSpec((1, d), lambda i, dst: (0, 0)),
                      pl.BlockSpec(memory_space=pl.ANY)],
            out_specs=pl.BlockSpec((tb, d), lambda i, dst: (i, 0)),
            scratch_shapes=[pltpu.VMEM((2, LOCAL_ROWS, d), F32), pltpu.SemaphoreType.DMA((2,))],
        ),
        out_shape=jax.ShapeDtypeStruct((n, d), F32),
        compiler_params=_params("arbitrary"),
        name="combine",
    )(dst, lpos, x2, g_final, ys)


def kernel(x, mem, norm_mix_g, w_in, conv_w, conv_b, conv_ln_g, conv_ln_b, conv_w_out, hgrn_lb_logits, hgrn_onorm_g, hgrn_w_out, w_mix_out, norm_xa_g, norm_mem_g, xa_w_q, xa_w_k, xa_w_v, xa_w_o, norm_ffn_g, router_group_w, router_group_b, router_expert_w, router_expert_b, moe_w_gate, moe_w_up, moe_w_down, final_norm_g):
    batch, seq, d = x.shape
    n = batch * seq
    assert w_in.shape[0] == 1, "the final RMSNorm is fused into the single layer's combine step"
    cw = conv_w.shape[2]
    kw = hgrn_w_out.shape[1]
    assert kw == HGRN_HEADS * HGRN_DIM and conv_w.shape[1] == CONV_K
    assert seq % TOKEN_BLOCK == 0 and TOKEN_BLOCK % CHUNK == 0 and TOKEN_BLOCK % CONV_ROWS == 0
    assert moe_w_gate.shape[1] == N_EXPERTS and router_group_w.shape[2] == N_GROUPS and TOP_K == 2
    na = TOP_K * TOKEN_BLOCK
    assert LOCAL_ROWS % SORT_CHUNK == 0 and LOCAL_ROWS >= na + N_EXPERTS * (GROUP - 1) + GROUP

    n_tiles = -(-(n // TOKEN_BLOCK) * (na + N_EXPERTS * (GROUP - 1)) // MOE_TILE) + N_EXPERTS
    n_tiles = -(-n_tiles // 16) * 16
    vec = lambda p: p.reshape(1, -1)
    l = 0

    x2d = x.reshape(n, d)
    u, q, fr, iv, og, gates = _inproj(x2d, vec(norm_mix_g[l]), w_in[l].astype(BF16), cw, kw)
    yc = _conv(u, conv_w[l], vec(conv_b[l]), vec(conv_ln_g[l]), vec(conv_ln_b[l]),
               conv_w_out[l].astype(BF16), gates, batch)
    x1 = _hgrn(q, fr, iv, og, gates, yc, x2d, hgrn_lb_logits[l:l + 2], vec(hgrn_onorm_g[l]),
               hgrn_w_out[l].astype(BF16), w_mix_out[l].astype(BF16), batch)
    k_bf, v_bf = _memkv(mem.reshape(-1, d), vec(norm_mem_g[l]), xa_w_k[l].astype(BF16), xa_w_v[l].astype(BF16), batch)
    pad = ROUTE_ROWS - N_GROUPS - N_EXPERTS
    w_route_t = jnp.pad(jnp.concatenate([router_group_w[l], router_expert_w[l]], axis=1).T, ((0, pad), (0, 0)))
    b_route = jnp.pad(jnp.concatenate([router_group_b[l], router_expert_b[l]]), (0, pad)).reshape(ROUTE_ROWS, 1)
    upper = (lax.broadcasted_iota(I32, (na, na), 0) < lax.broadcasted_iota(I32, (na, na), 1)).astype(BF16)
    x2, xb, cnt, lpos = _attn(x1, vec(norm_xa_g[l]), xa_w_q[l].astype(BF16), k_bf, v_bf, xa_w_o[l].astype(BF16),
                              vec(norm_ffn_g[l]), w_route_t, b_route, upper, batch)
    te, src, dst, nact = _plan(cnt.reshape(-1, 1), n_tiles)
    ys = _experts(te, src, nact, xb, moe_w_gate[l], moe_w_up[l], moe_w_down[l])
    out = _combine(dst, lpos, x2, vec(final_norm_g), ys)
    return out.reshape(batch, seq, d)
```

```python
import functools

import jax
import jax.numpy as jnp
from jax import lax
from jax.experimental import pallas as pl
from jax.experimental.pallas import tpu as pltpu

F32 = jnp.float32
BF16 = jnp.bfloat16
I32 = jnp.int32

EPS = 1e-6
CONV_K = 31
CONV_HALO = 32
CONV_ROWS = 64
HGRN_HEADS = 4
HGRN_DIM = 128
CHUNK = 64
SUB = 16
XA_HEADS = 4
N_GROUPS = 4
EXPERTS_PER_GROUP = 8
N_EXPERTS = N_GROUPS * EXPERTS_PER_GROUP
TOP_K = 2
ROUTE_ROWS = 40
TOKEN_BLOCK = 512
MOE_TILE = 256
GROUP = 8
LOCAL_ROWS = 1280
SORT_CHUNK = 256
W_LANES = 128
PLAN_CHUNKS = (512, 256, 128)
VMEM_LIMIT_BYTES = 48 * 1024 * 1024


def _rms(x, g):
    return x * lax.rsqrt(jnp.mean(x * x, axis=-1, keepdims=True) + EPS) * g


def _dot(a, b):
    return jnp.dot(a, b, preferred_element_type=F32)


def _dot_nt(a, b):
    return lax.dot_general(a, b, (((1,), (1,)), ((), ())), preferred_element_type=F32)


def _dot_tn(a, b):
    return lax.dot_general(a, b, (((0,), (0,)), ((), ())), preferred_element_type=F32)


def _split_bf16(x):
    hi = x.astype(BF16)
    lo = (x - hi.astype(F32)).astype(BF16)
    return hi, lo


def _params(*sem):
    return pltpu.CompilerParams(dimension_semantics=sem, vmem_limit_bytes=VMEM_LIMIT_BYTES)


def _inproj_kernel(x_ref, g_ref, w_ref, u_ref, q_ref, fr_ref, iv_ref, og_ref, gates_ref, *, cw, kw):
    hb = _rms(x_ref[...], g_ref[...]).astype(BF16)

    def proj(lo, width):
        return _dot(hb, w_ref[:, lo:lo + width])

    u_ref[...] = (proj(0, cw) * jax.nn.sigmoid(proj(cw, cw))).astype(BF16)
    base = 2 * cw
    q_ref[...] = jax.nn.silu(proj(base, kw)).astype(BF16)
    fr_ref[...] = proj(base + kw, kw)
    iv_ref[...] = proj(base + 2 * kw, kw).astype(BF16)
    og_ref[...] = jax.nn.silu(proj(base + 3 * kw, kw)).astype(BF16)
    base += 4 * kw
    for c in range(gates_ref.shape[1] // kw):
        gates_ref[:, c * kw:(c + 1) * kw] = jax.nn.sigmoid(proj(base + c * kw, kw)).astype(BF16)


def _inproj(x2d, g, w_bf, cw, kw):
    n, d = x2d.shape
    tb = TOKEN_BLOCK
    row = lambda i: (i, 0)
    fixed = lambda i: (0, 0)
    outs = [((n, cw), BF16), ((n, kw), BF16), ((n, kw), F32), ((n, kw), BF16), ((n, kw), BF16), ((n, 2 * d), BF16)]
    return pl.pallas_call(
        functools.partial(_inproj_kernel, cw=cw, kw=kw),
        grid=(n // tb,),
        in_specs=[pl.BlockSpec((tb, d), row), pl.BlockSpec((1, d), fixed), pl.BlockSpec(w_bf.shape, fixed)],
        out_specs=[pl.BlockSpec((tb, s[1]), row) for s, _ in outs],
        out_shape=[jax.ShapeDtypeStruct(s, t) for s, t in outs],
        compiler_params=_params("parallel"),
        name="inproj",
    )(x2d, g, w_bf)


def _conv_kernel(u_ref, cw_ref, cb_ref, lg_ref, lb_ref, wo_ref, gate_ref, y_ref, ext_ref, halo_ref, perm_ref, act_ref):
    tb, c = u_ref.shape
    nt = tb // 8
    slabs = c // 128
    hr = CONV_HALO * 8
    lanes = [slice(l * 128, (l + 1) * 128) for l in range(slabs)]

    @pl.when(pl.program_id(1) == 0)
    def _():
        halo_ref[...] = jnp.zeros_like(halo_ref)

    un = u_ref[...].astype(F32)
    per = nt // 8
    for j in range(nt):
        start = hr + (j % per) * 64 + j // per
        for l in range(slabs):
            ext_ref[l, pl.ds(start, 8, stride=8), :] = un[8 * j:8 * j + 8, lanes[l]]
    first = lax.broadcasted_iota(I32, (hr, 128), 0) % 8 == 0
    for l in range(slabs):
        cur = ext_ref[l, nt * 8:nt * 8 + hr, :]
        ext_ref[l, 0:hr, :] = jnp.where(first, pltpu.roll(halo_ref[l], hr - 7, axis=0), pltpu.roll(cur, 1, axis=0))
        halo_ref[l] = cur

    for r in range(tb // CONV_ROWS):
        accs = []
        for l in range(slabs):
            acc = jnp.broadcast_to(cb_ref[:, lanes[l]], (CONV_ROWS, 128))
            for dt in range(CONV_K):
                off = hr + r * CONV_ROWS - dt * 8
                acc = acc + cw_ref[CONV_K - 1 - dt:CONV_K - dt, lanes[l]] * ext_ref[l, off:off + CONV_ROWS, :]
            accs.append(acc)
        mu = functools.reduce(jnp.add, [jnp.sum(a, axis=-1, keepdims=True) for a in accs]) * (1.0 / c)
        cens = [a - mu for a in accs]
        var = functools.reduce(jnp.add, [jnp.sum(a * a, axis=-1, keepdims=True) for a in cens]) * (1.0 / c)
        inv = lax.rsqrt(var + EPS)
        for l in range(slabs):
            ln = cens[l] * inv * lg_ref[:, lanes[l]] + lb_ref[:, lanes[l]]
            perm_ref[l, r * CONV_ROWS:(r + 1) * CONV_ROWS, :] = jax.nn.silu(ln)
    for j in range(nt):
        start = (j % per) * 64 + j // per
        for l in range(slabs):
            act_ref[8 * j:8 * j + 8, lanes[l]] = perm_ref[l, pl.ds(start, 8, stride=8), :]
    y_ref[...] = (gate_ref[...].astype(F32) * _dot(act_ref[...].astype(BF16), wo_ref[...])).astype(BF16)


def _conv(u, conv_w, conv_b, ln_g, ln_b, w_out_bf, gates, batch):
    n, c = u.shape
    d = w_out_bf.shape[1]
    tb = TOKEN_BLOCK
    nsb = n // batch // tb
    row = lambda b, s: (b * nsb + s, 0)
    fixed = lambda b, s: (0, 0)
    return pl.pallas_call(
        _conv_kernel,
        grid=(batch, nsb),
        in_specs=[pl.BlockSpec((tb, c), row), pl.BlockSpec(conv_w.shape, fixed), pl.BlockSpec((1, c), fixed),
                  pl.BlockSpec((1, c), fixed), pl.BlockSpec((1, c), fixed), pl.BlockSpec((c, d), fixed),
                  pl.BlockSpec((tb, d), row)],
        out_specs=pl.BlockSpec((tb, d), row),
        out_shape=jax.ShapeDtypeStruct((n, d), BF16),
        scratch_shapes=[pltpu.VMEM((c // 128, tb + CONV_HALO * 8, 128), F32),
                        pltpu.VMEM((c // 128, CONV_HALO * 8, 128), F32),
                        pltpu.VMEM((c // 128, tb, 128), F32), pltpu.VMEM((tb, c), F32)],
        compiler_params=_params("arbitrary", "arbitrary"),
        name="conv",
    )(u, conv_w, conv_b, ln_g, ln_b, w_out_bf, gates)


def _hgrn_chunk(q, fr, v, lb, st_ref, tri):
    heads = [slice(h * HGRN_DIM, (h + 1) * HGRN_DIM) for h in range(HGRN_HEADS)]
    f = lb + (1.0 - lb) * jax.nn.sigmoid(fr)
    lf = jnp.log(f)
    kk = 1.0 - f
    lf_hi, lf_lo = _split_bf16(lf)
    cum = _dot(tri, lf_hi) + _dot(tri, lf_lo)
    last = cum[CHUNK - 1:CHUNK, :]
    qe = (q * jnp.exp(cum)).astype(BF16)
    kd = (kk * jnp.exp(last - cum)).astype(BF16)
    decay = jnp.exp(last)

    blocks = []
    for i in range(CHUNK // SUB):
        rs, ne = i * SUB, (i + 1) * SUB
        ref = cum[rs - 1:rs, :] if i else jnp.zeros_like(last)
        qt = (q[rs:ne] * jnp.exp(cum[rs:ne] - ref)).astype(BF16)
        kt = (kk[0:ne] * jnp.exp(ref - cum[0:ne])).astype(BF16)
        blocks.append((qt, kt))

    states = [st_ref[h] for h in range(HGRN_HEADS)]
    o_inter = [_dot_nt(qe[:, hs], st.astype(BF16)) for hs, st in zip(heads, states)]
    scores = [[_dot_nt(qt[:, hs], kt[:, hs]) for qt, kt in blocks] for hs in heads]
    for h, hs in enumerate(heads):
        st_ref[h] = states[h] * decay[:, hs] + _dot_tn(v[:, hs], kd[:, hs])

    outs = []
    for h, hs in enumerate(heads):
        parts = []
        for i, a in enumerate(scores[h]):
            rs, ne = i * SUB, (i + 1) * SUB
            trow = lax.broadcasted_iota(I32, (SUB, ne), 0) + rs
            scol = lax.broadcasted_iota(I32, (SUB, ne), 1)
            a = jnp.where(scol <= trow, a, 0.0).astype(BF16)
            parts.append(_dot(a, v[0:ne, hs]))
        outs.append(o_inter[h] + jnp.concatenate(parts, axis=0))
    return jnp.concatenate(outs, axis=1)


def _hgrn_kernel(q_ref, fr_ref, iv_ref, og_ref, gate_ref, yc_ref, x_ref, lbl_ref, on_ref, wo_ref, wm_ref,
                 x1_ref, st_ref, ob_ref):
    tb = q_ref.shape[0]

    @pl.when(pl.program_id(1) == 0)
    def _():
        st_ref[...] = jnp.zeros_like(st_ref)

    l0, l1 = lbl_ref[0:1, :], lbl_ref[1:2, :]
    m = jnp.maximum(l0, l1)
    e0, e1 = jnp.exp(l0 - m), jnp.exp(l1 - m)
    lb_all = e0 / (e0 + e1)
    trow = lax.broadcasted_iota(I32, (CHUNK, CHUNK), 0)
    tcol = lax.broadcasted_iota(I32, (CHUNK, CHUNK), 1)
    tri = jnp.where(tcol <= trow, 1.0, 0.0).astype(BF16)

    def chunk(ci, carry):
        rows = pl.ds(pl.multiple_of(ci * CHUNK, CHUNK), CHUNK)
        o = _hgrn_chunk(q_ref[rows, :].astype(F32), fr_ref[rows, :], iv_ref[rows, :], lb_all, st_ref, tri)
        og = og_ref[rows, :].astype(F32)
        for h in range(HGRN_HEADS):
            hs = slice(h * HGRN_DIM, (h + 1) * HGRN_DIM)
            ob_ref[rows, hs] = (_rms(o[:, hs], on_ref[...]) * og[:, hs]).astype(BF16)
        return carry

    lax.fori_loop(0, tb // CHUNK, chunk, 0, unroll=4)
    y_rec = _dot(ob_ref[...], wo_ref[...])
    merged = yc_ref[...].astype(F32) + gate_ref[...].astype(F32) * y_rec
    x1_ref[...] = x_ref[...] + _dot(merged.astype(BF16), wm_ref[...])


def _hgrn(q, fr, iv, og, gates, yc, x2d, lb_logits, onorm_g, w_o_bf, w_mix_bf, batch):
    n, kw = q.shape
    d = x2d.shape[1]
    tb = TOKEN_BLOCK
    nsb = n // batch // tb
    row = lambda b, s: (b * nsb + s, 0)
    fixed = lambda b, s: (0, 0)
    return pl.pallas_call(
        _hgrn_kernel,
        grid=(batch, nsb),
        in_specs=[pl.BlockSpec((tb, kw), row), pl.BlockSpec((tb, kw), row), pl.BlockSpec((tb, kw), row),
                  pl.BlockSpec((tb, kw), row),
                  pl.BlockSpec((tb, d), lambda b, s: (b * nsb + s, 1)),
                  pl.BlockSpec((tb, d), row), pl.BlockSpec((tb, d), row),
                  pl.BlockSpec(lb_logits.shape, fixed), pl.BlockSpec((1, HGRN_DIM), fixed),
                  pl.BlockSpec(w_o_bf.shape, fixed), pl.BlockSpec(w_mix_bf.shape, fixed)],
        out_specs=pl.BlockSpec((tb, d), row),
        out_shape=jax.ShapeDtypeStruct((n, d), F32),
        scratch_shapes=[pltpu.VMEM((HGRN_HEADS, HGRN_DIM, HGRN_DIM), F32), pltpu.VMEM((tb, kw), BF16)],
        compiler_params=_params("arbitrary", "arbitrary"),
        name="hgrn",
    )(q, fr, iv, og, gates, yc, x2d, lb_logits, onorm_g, w_o_bf, w_mix_bf)


def _memkv_kernel(mem_ref, g_ref, wk_ref, wv_ref, k_ref, v_ref):
    mb = _rms(mem_ref[...], g_ref[...]).astype(BF16)
    k_ref[...] = _dot(mb, wk_ref[...]).astype(BF16)
    v_ref[...] = _dot(mb, wv_ref[...]).astype(BF16)


def _memkv(mem2d, g, wk_bf, wv_bf, batch):
    n, d = mem2d.shape
    m = n // batch
    row = lambda b: (b, 0)
    fixed = lambda b: (0, 0)
    return pl.pallas_call(
        _memkv_kernel,
        grid=(batch,),
        in_specs=[pl.BlockSpec((m, d), row), pl.BlockSpec((1, d), fixed), pl.BlockSpec((d, d), fixed),
                  pl.BlockSpec((d, d), fixed)],
        out_specs=[pl.BlockSpec((m, d), row)] * 2,
        out_shape=[jax.ShapeDtypeStruct((n, d), BF16)] * 2,
        compiler_params=_params("parallel"),
        name="memkv",
    )(mem2d, g, wk_bf, wv_bf)


def _route(lt):
    def row(r):
        return lt[r:r + 1, :]

    gl = [row(g) for g in range(N_GROUPS)]
    gmax = functools.reduce(jnp.maximum, gl)
    g_p = 1.0 / functools.reduce(jnp.add, [jnp.exp(l - gmax) for l in gl])
    gidx = jnp.full(gmax.shape, N_GROUPS - 1, I32)
    for g in range(N_GROUPS - 2, -1, -1):
        gidx = jnp.where(gl[g] == gmax, g, gidx)

    el = []
    for j in range(EXPERTS_PER_GROUP):
        v = row(N_GROUPS + (N_GROUPS - 1) * EXPERTS_PER_GROUP + j)
        for g in range(N_GROUPS - 2, -1, -1):
            v = jnp.where(gidx == g, row(N_GROUPS + g * EXPERTS_PER_GROUP + j), v)
        el.append(v)

    def argmax(vals):
        mx = functools.reduce(jnp.maximum, vals)
        idx = jnp.full(mx.shape, EXPERTS_PER_GROUP - 1, I32)
        for j in range(EXPERTS_PER_GROUP - 2, -1, -1):
            idx = jnp.where(vals[j] == mx, j, idx)
        return mx, idx

    m1, i1 = argmax(el)
    m2, i2 = argmax([jnp.where(i1 == j, -jnp.inf, el[j]) for j in range(EXPERTS_PER_GROUP)])
    r = jnp.exp(m2 - m1)
    w1 = g_p / (1.0 + r)
    w2 = g_p * r / (1.0 + r)
    base = gidx * EXPERTS_PER_GROUP
    return jnp.concatenate([base + i1, base + i2], axis=0), jnp.concatenate([w1, w2], axis=0)


def _excl_cumsum_rows(col):
    r = col.shape[0]
    lower = lax.broadcasted_iota(I32, (r, r), 1) < lax.broadcasted_iota(I32, (r, r), 0)
    lower_bf = jnp.where(lower, 1.0, 0.0).astype(BF16)
    hi, lo = _split_bf16(jnp.broadcast_to(col, (r, 128)))
    return (_dot(lower_bf, hi) + _dot(lower_bf, lo))[:, 0:1]


def _slot_masks(lpos, chunk, tb):
    slot = lax.broadcasted_iota(I32, (SORT_CHUNK, tb), 0) + chunk * SORT_CHUNK
    return slot == lpos[:, 0:tb], slot == lpos[:, tb:2 * tb]


def _local_sort(eid, wts, h2, upper_ref, xb_ref, cnt_ref, lpos_ref):
    tb, d = h2.shape
    na = TOP_K * tb
    e_all = jnp.concatenate([eid[k:k + 1] for k in range(TOP_K)], axis=1)
    onehot = lax.broadcasted_iota(I32, (N_EXPERTS, na), 0) == e_all
    ones = jnp.where(onehot, 1.0, 0.0)
    cnt = jnp.sum(ones, axis=1, keepdims=True)
    cnt_pad = jnp.floor((cnt + (GROUP - 1)) * (1.0 / GROUP)) * GROUP
    earlier = _dot(ones.astype(BF16), upper_ref[...])
    start = _excl_cumsum_rows(cnt_pad)
    lpos = jnp.sum(jnp.where(onehot, start + earlier, 0.0), axis=0, keepdims=True).astype(I32)
    hb = h2.astype(BF16)
    for c in range(LOCAL_ROWS // SORT_CHUNK):
        m0, m1 = _slot_masks(lpos, c, tb)
        p = jnp.where(m0, 1.0, jnp.where(m1, 1.0, 0.0)).astype(BF16)
        pw = jnp.where(m0, wts[0:1], jnp.where(m1, wts[1:2], 0.0))
        rows = slice(c * SORT_CHUNK, (c + 1) * SORT_CHUNK)
        xb_ref[rows, 0:d] = _dot(p, hb)
        xb_ref[rows, d:d + W_LANES] = jnp.broadcast_to(jnp.sum(pw, axis=1, keepdims=True), (SORT_CHUNK, W_LANES))
    cnt_ref[...] = cnt_pad.astype(I32)
    lpos_ref[...] = lpos


def _attn_kernel(x1_ref, gxa_ref, wq_ref, k_ref, v_ref, wo_ref, gffn_ref, wr_ref, br_ref, upper_ref,
                 x2_ref, xb_ref, cnt_ref, lpos_ref):
    x1 = x1_ref[...]
    d = x1.shape[1]
    hd = d // XA_HEADS
    q = _dot(_rms(x1, gxa_ref[...]).astype(BF16), wq_ref[...]).astype(BF16)
    heads = []
    for h in range(XA_HEADS):
        hs = slice(h * hd, (h + 1) * hd)
        sc = _dot_nt(q[:, hs], k_ref[:, hs]) * (hd ** -0.5)
        p = jnp.exp(sc - jnp.max(sc, axis=-1, keepdims=True))
        p = p / jnp.sum(p, axis=-1, keepdims=True)
        heads.append(_dot(p.astype(BF16), v_ref[:, hs]).astype(BF16))
    x2 = x1 + _dot(jnp.concatenate(heads, axis=1), wo_ref[...])
    x2_ref[...] = x2
    h2 = _rms(x2, gffn_ref[...])
    h_hi, h_lo = _split_bf16(h2)
    w_hi, w_lo = _split_bf16(wr_ref[...])
    lt = _dot_nt(w_hi, h_hi) + (_dot_nt(w_hi, h_lo) + _dot_nt(w_lo, h_hi)) + br_ref[...]
    eid, wts = _route(lt)
    _local_sort(eid, wts, h2, upper_ref, xb_ref, cnt_ref, lpos_ref)


def _attn(x1, gxa, wq_bf, k_bf, v_bf, wo_bf, gffn, w_route_t, b_route, upper, batch):
    n, d = x1.shape
    m = k_bf.shape[0] // batch
    tb = TOKEN_BLOCK
    nsb = n // batch // tb
    nb = n // tb
    row = lambda b, s: (b * nsb + s, 0)
    blk3 = lambda b, s: (b * nsb + s, 0, 0)
    fixed = lambda b, s: (0, 0)
    mem = lambda b, s: (b, 0)
    return pl.pallas_call(
        _attn_kernel,
        grid=(batch, nsb),
        in_specs=[pl.BlockSpec((tb, d), row), pl.BlockSpec((1, d), fixed), pl.BlockSpec((d, d), fixed),
                  pl.BlockSpec((m, d), mem), pl.BlockSpec((m, d), mem), pl.BlockSpec((d, d), fixed),
                  pl.BlockSpec((1, d), fixed), pl.BlockSpec((ROUTE_ROWS, d), fixed),
                  pl.BlockSpec((ROUTE_ROWS, 1), fixed), pl.BlockSpec(upper.shape, fixed)],
        out_specs=[pl.BlockSpec((tb, d), row), pl.BlockSpec((LOCAL_ROWS, d + W_LANES), row),
                   pl.BlockSpec((None, N_EXPERTS, 1), blk3), pl.BlockSpec((None, 1, TOP_K * tb), blk3)],
        out_shape=[jax.ShapeDtypeStruct((n, d), F32), jax.ShapeDtypeStruct((nb * LOCAL_ROWS, d + W_LANES), F32),
                   jax.ShapeDtypeStruct((nb, N_EXPERTS, 1), I32), jax.ShapeDtypeStruct((nb, 1, TOP_K * tb), I32)],
        compiler_params=_params("parallel", "parallel"),
        name="attn",
    )(x1, gxa, wq_bf, k_bf, v_bf, wo_bf, gffn, w_route_t, b_route, upper)


def _plan_kernel(cnt_ref, te_ref, src_ref, dst_ref, nact_ref):
    runs = cnt_ref.shape[0]
    lg = LOCAL_ROWS // GROUP
    tg = MOE_TILE // GROUP
    sh = N_EXPERTS.bit_length() - 1
    emask = N_EXPERTS - 1
    zero_group = lg - 1

    length = cnt_ref[...].astype(F32) * (1.0 / GROUP)
    len_bf = jnp.broadcast_to(length, (runs, 128)).astype(BF16)
    ri = lax.broadcasted_iota(I32, (runs, runs), 0)
    ci = lax.broadcasted_iota(I32, (runs, runs), 1)
    r_e, c_e = ri & emask, ci & emask
    r_b, c_b = lax.shift_right_logical(ri, sh), lax.shift_right_logical(ci, sh)
    same_expert_earlier = jnp.where(r_e == c_e, jnp.where(c_b < r_b, 1.0, 0.0), 0.0).astype(BF16)
    same_block_earlier = jnp.where(r_b == c_b, jnp.where(c_e < r_e, 1.0, 0.0), 0.0).astype(BF16)
    before = _dot(same_expert_earlier, len_bf)[:, 0:1]
    local = _dot(same_block_earlier, len_bf)[:, 0:1]

    of_expert = (lax.broadcasted_iota(I32, (N_EXPERTS, runs), 1) & emask) == lax.broadcasted_iota(
        I32, (N_EXPERTS, runs), 0)
    total = _dot(jnp.where(of_expert, 1.0, 0.0).astype(BF16), len_bf)[:, 0:1]
    tiles = jnp.floor((total + (tg - 1)) * (1.0 / tg))
    tile0 = _excl_cumsum_rows(tiles)
    to_run = (lax.broadcasted_iota(I32, (runs, N_EXPERTS), 0) & emask) == lax.broadcasted_iota(
        I32, (runs, N_EXPERTS), 1)
    to_run_bf = jnp.where(to_run, 1.0, 0.0).astype(BF16)
    base_hi, base_lo = _split_bf16(jnp.broadcast_to(tile0 * tg, (N_EXPERTS, 128)))
    g_start = (_dot(to_run_bf, base_hi) + _dot(to_run_bf, base_lo))[:, 0:1] + before
    block = lax.shift_right_logical(lax.broadcasted_iota(I32, (runs, 1), 0), sh).astype(F32)
    l_start = block * lg + local

    def cover(out_ref, start, offset, default):
        stop = start + length
        shift = offset - default
        n_out = out_ref.shape[1]
        chunk = next(c for c in PLAN_CHUNKS if n_out % c == 0)
        for c in range(n_out // chunk):
            j = (lax.broadcasted_iota(I32, (runs, chunk), 1) + c * chunk).astype(F32)
            hit = jnp.where(start <= j, jnp.where(j < stop, j + shift, 0.0), 0.0)
            out = jnp.sum(hit, axis=0, keepdims=True) + default
            out_ref[:, c * chunk:(c + 1) * chunk] = out.astype(I32)

    cover(src_ref, g_start, l_start - g_start, float(zero_group))
    cover(dst_ref, l_start, g_start - l_start, 0.0)
    t = lax.broadcasted_iota(I32, (N_EXPERTS, te_ref.shape[1]), 1).astype(F32)
    te_ref[...] = jnp.sum(jnp.where(t >= tile0 + tiles, 1.0, 0.0), axis=0, keepdims=True).astype(I32)
    nact_ref[...] = jnp.sum(jnp.broadcast_to(tiles, (N_EXPERTS, 128)), axis=0, keepdims=True).astype(I32)


def _plan(cnt, n_tiles):
    runs = cnt.shape[0]
    n_src = n_tiles * (MOE_TILE // GROUP)
    n_dst = runs // N_EXPERTS * (LOCAL_ROWS // GROUP)
    n_te = -(-n_tiles // 128) * 128
    te, src, dst, nact = pl.pallas_call(
        _plan_kernel,
        out_shape=[jax.ShapeDtypeStruct((1, n_te), I32), jax.ShapeDtypeStruct((1, n_src), I32),
                   jax.ShapeDtypeStruct((1, n_dst), I32), jax.ShapeDtypeStruct((1, 128), I32)],
        compiler_params=_params(),
        name="plan",
    )(cnt)
    return te.reshape(-1), src.reshape(-1), dst.reshape(-1), nact[0, :1]


def _group_copy(src_hbm, src_group, dst_buf, slot, index, sem):
    start = src_group * GROUP
    rows = pl.ds(start if isinstance(start, int) else pl.multiple_of(start, GROUP), GROUP)
    return pltpu.make_async_copy(src_hbm.at[rows], dst_buf.at[slot, pl.ds(index * GROUP, GROUP)], sem.at[slot])


def _experts_kernel(te_ref, src_ref, nact_ref, xb_ref, wg_hbm, wu_hbm, wd_hbm, ys_ref,
                    xbuf, sem, wg_st, wu_st, wd_st, wsem, wg_bf, wu_bf, wd_bf):
    t = pl.program_id(0)
    tg = MOE_TILE // GROUP
    d = ys_ref.shape[1]
    nact = nact_ref[0]
    e = te_ref[t]
    slot = lax.rem(t, 2)
    last = te_ref.shape[0] - 1

    def start_gather(tile):
        for i in range(tg):
            _group_copy(xb_ref, src_ref[tile * tg + i], xbuf, lax.rem(tile, 2), i, sem).start(priority=i % 2)

    def weight_copies(expert):
        pairs = ((wg_hbm, wg_st), (wu_hbm, wu_st), (wd_hbm, wd_st))
        return [pltpu.make_async_copy(w.at[expert], st, wsem.at[i]) for i, (w, st) in enumerate(pairs)]

    @pl.when(jnp.logical_and(t == 0, nact > 0))
    def _():
        start_gather(t)
        for c in weight_copies(e):
            c.start()

    @pl.when(t + 1 < nact)
    def _():
        start_gather(t + 1)

    @pl.when(jnp.logical_and(t < nact, jnp.logical_or(t == 0, e != te_ref[jnp.maximum(t - 1, 0)])))
    def _():
        for c in weight_copies(e):
            c.wait()
        wg_bf[...] = wg_st[...].astype(BF16)
        wu_bf[...] = wu_st[...].astype(BF16)
        wd_bf[...] = wd_st[...].astype(BF16)
        nxt = lax.while_loop(lambda j: jnp.logical_and(j < nact, te_ref[jnp.minimum(j, last)] == e),
                             lambda j: j + 1, t + 1)

        @pl.when(nxt < nact)
        def _():
            for c in weight_copies(te_ref[jnp.minimum(nxt, last)]):
                c.start()

    @pl.when(t < nact)
    def _():
        for i in range(tg):
            _group_copy(xb_ref, 0, xbuf, slot, i, sem).wait()
        x = xbuf[slot, :, 0:d].astype(BF16)
        hid = jax.nn.silu(_dot(x, wg_bf[...])) * _dot(x, wu_bf[...])
        ys_ref[...] = _dot(hid.astype(BF16), wd_bf[...]) * xbuf[slot, :, d:d + 1]

    @pl.when(t >= nact)
    def _():
        ys_ref[...] = jnp.zeros_like(ys_ref)


def _experts(te, src, nact, xb, w_gate, w_up, w_down):
    n_tiles = te.shape[0]
    dw = xb.shape[1]
    d = dw - W_LANES
    ff = w_gate.shape[2]
    tm = MOE_TILE
    hbm = pl.BlockSpec(memory_space=pl.ANY)
    return pl.pallas_call(
        _experts_kernel,
        grid_spec=pltpu.PrefetchScalarGridSpec(
            num_scalar_prefetch=3,
            grid=(n_tiles,),
            in_specs=[hbm, hbm, hbm, hbm],
            out_specs=pl.BlockSpec((tm, d), lambda t, te, src, nact: (t, 0)),
            scratch_shapes=[pltpu.VMEM((2, tm, dw), F32), pltpu.SemaphoreType.DMA((2,)),
                            pltpu.VMEM((d, ff), F32), pltpu.VMEM((d, ff), F32), pltpu.VMEM((ff, d), F32),
                            pltpu.SemaphoreType.DMA((3,)),
                            pltpu.VMEM((d, ff), BF16), pltpu.VMEM((d, ff), BF16), pltpu.VMEM((ff, d), BF16)],
        ),
        out_shape=jax.ShapeDtypeStruct((n_tiles * tm, d), F32),
        compiler_params=_params("arbitrary"),
        name="experts",
    )(te, src, nact, xb, w_gate, w_up, w_down)


def _combine_kernel(dst_ref, lpos_ref, x2_ref, g_ref, ys_ref, out_ref, ybuf, sem):
    b = pl.program_id(0)
    tb = x2_ref.shape[0]
    lg = LOCAL_ROWS // GROUP
    slot = lax.rem(b, 2)

    def start_gather(blk):
        for i in range(lg):
            _group_copy(ys_ref, dst_ref[blk * lg + i], ybuf, lax.rem(blk, 2), i, sem).start(priority=i % 2)

    @pl.when(b == 0)
    def _():
        start_gather(b)

    @pl.when(b + 1 < pl.num_programs(0))
    def _():
        start_gather(b + 1)

    for i in range(lg):
        _group_copy(ys_ref, 0, ybuf, slot, i, sem).wait()
    lpos = lpos_ref[...]
    y = jnp.zeros(x2_ref.shape, F32)
    for c in range(LOCAL_ROWS // SORT_CHUNK):
        m0, m1 = _slot_masks(lpos, c, tb)
        p = jnp.where(m0, 1.0, jnp.where(m1, 1.0, 0.0)).astype(BF16)
        y = y + _dot_tn(p, ybuf[slot, c * SORT_CHUNK:(c + 1) * SORT_CHUNK, :].astype(BF16))
    out_ref[...] = _rms(x2_ref[...] + y, g_ref[...])


def _combine(dst, lpos, x2, g_final, ys):
    n, d = x2.shape
    tb = TOKEN_BLOCK
    return pl.pallas_call(
        _combine_kernel,
        grid_spec=pltpu.PrefetchScalarGridSpec(
            num_scalar_prefetch=1,
            grid=(n // tb,),
            in_specs=[pl.BlockSpec((None, 1, TOP_K * tb), lambda i, dst: (i, 0, 0)),
                      pl.BlockSpec((tb, d), lambda i, dst: (i, 0)), pl.BlockSpec((1, d), lambda i, dst: (0, 0)),
                      pl.BlockSpec(memory_space=pl.ANY)],
            out_specs=pl.BlockSpec((tb, d), lambda i, dst: (i, 0)),
            scratch_shapes=[pltpu.VMEM((2, LOCAL_ROWS, d), F32), pltpu.SemaphoreType.DMA((2,))],
        ),
        out_shape=jax.ShapeDtypeStruct((n, d), F32),
        compiler_params=_params("arbitrary"),
        name="combine",
    )(dst, lpos, x2, g_final, ys)


def kernel(x, mem, norm_mix_g, w_in, conv_w, conv_b, conv_ln_g, conv_ln_b, conv_w_out, hgrn_lb_logits, hgrn_onorm_g, hgrn_w_out, w_mix_out, norm_xa_g, norm_mem_g, xa_w_q, xa_w_k, xa_w_v, xa_w_o, norm_ffn_g, router_group_w, router_group_b, router_expert_w, router_expert_b, moe_w_gate, moe_w_up, moe_w_down, final_norm_g):
    batch, seq, d = x.shape
    n = batch * seq
    assert w_in.shape[0] == 1, "the final RMSNorm is fused into the single layer's combine step"
    cw = conv_w.shape[2]
    kw = hgrn_w_out.shape[1]
    assert kw == HGRN_HEADS * HGRN_DIM and conv_w.shape[1] == CONV_K
    assert seq % TOKEN_BLOCK == 0 and TOKEN_BLOCK % CHUNK == 0 and TOKEN_BLOCK % CONV_ROWS == 0
    assert moe_w_gate.shape[1] == N_EXPERTS and router_group_w.shape[2] == N_GROUPS and TOP_K == 2
    na = TOP_K * TOKEN_BLOCK
    assert LOCAL_ROWS % SORT_CHUNK == 0 and LOCAL_ROWS >= na + N_EXPERTS * (GROUP - 1) + GROUP

    n_tiles = -(-(n // TOKEN_BLOCK) * (na + N_EXPERTS * (GROUP - 1)) // MOE_TILE) + N_EXPERTS
    n_tiles = -(-n_tiles // 16) * 16
    vec = lambda p: p.reshape(1, -1)
    l = 0

    x2d = x.reshape(n, d)
    u, q, fr, iv, og, gates = _inproj(x2d, vec(norm_mix_g[l]), w_in[l].astype(BF16), cw, kw)
    yc = _conv(u, conv_w[l], vec(conv_b[l]), vec(conv_ln_g[l]), vec(conv_ln_b[l]),
               conv_w_out[l].astype(BF16), gates, batch)
    x1 = _hgrn(q, fr, iv, og, gates, yc, x2d, hgrn_lb_logits[l:l + 2], vec(hgrn_onorm_g[l]),
               hgrn_w_out[l].astype(BF16), w_mix_out[l].astype(BF16), batch)
    k_bf, v_bf = _memkv(mem.reshape(-1, d), vec(norm_mem_g[l]), xa_w_k[l].astype(BF16), xa_w_v[l].astype(BF16), batch)
    pad = ROUTE_ROWS - N_GROUPS - N_EXPERTS
    w_route_t = jnp.pad(jnp.concatenate([router_group_w[l], router_expert_w[l]], axis=1).T, ((0, pad), (0, 0)))
    b_route = jnp.pad(jnp.concatenate([router_group_b[l], router_expert_b[l]]), (0, pad)).reshape(ROUTE_ROWS, 1)
    upper = (lax.broadcasted_iota(I32, (na, na), 0) < lax.broadcasted_iota(I32, (na, na), 1)).astype(BF16)
    x2, xb, cnt, lpos = _attn(x1, vec(norm_xa_g[l]), xa_w_q[l].astype(BF16), k_bf, v_bf, xa_w_o[l].astype(BF16),
                              vec(norm_ffn_g[l]), w_route_t, b_route, upper, batch)
    te, src, dst, nact = _plan(cnt.reshape(-1, 1), n_tiles)
    ys = _experts(te, src, nact, xb, moe_w_gate[l], moe_w_up[l], moe_w_down[l])
    out = _combine(dst, lpos, x2, vec(final_norm_g), ys)
    return out.reshape(batch, seq, d)
```

```python
import functools

import jax
import jax.numpy as jnp
from jax import lax
from jax.experimental import pallas as pl
from jax.experimental.pallas import tpu as pltpu

F32 = jnp.float32
BF16 = jnp.bfloat16
I32 = jnp.int32

EPS = 1e-6
CONV_K = 31
CONV_HALO = 32
CONV_ROWS = 64
HGRN_HEADS = 4
HGRN_DIM = 128
CHUNK = 64
SUB = 16
XA_HEADS = 4
N_GROUPS = 4
EXPERTS_PER_GROUP = 8
N_EXPERTS = N_GROUPS * EXPERTS_PER_GROUP
TOP_K = 2
ROUTE_ROWS = 40
TOKEN_BLOCK = 512
MOE_TILE = 256
GROUP = 16
LOCAL_ROWS = 1536
SORT_CHUNK = 256
W_LANES = 128
PLAN_CHUNKS = (512, 256, 128)
VMEM_LIMIT_BYTES = 48 * 1024 * 1024


def _rms(x, g):
    return x * lax.rsqrt(jnp.mean(x * x, axis=-1, keepdims=True) + EPS) * g


def _dot(a, b):
    return jnp.dot(a, b, preferred_element_type=F32)


def _dot_nt(a, b):
    return lax.dot_general(a, b, (((1,), (1,)), ((), ())), preferred_element_type=F32)


def _dot_tn(a, b):
    return lax.dot_general(a, b, (((0,), (0,)), ((), ())), preferred_element_type=F32)


def _split_bf16(x):
    hi = x.astype(BF16)
    lo = (x - hi.astype(F32)).astype(BF16)
    return hi, lo


def _params(*sem):
    return pltpu.CompilerParams(dimension_semantics=sem, vmem_limit_bytes=VMEM_LIMIT_BYTES)


def _inproj_kernel(x_ref, g_ref, w_ref, u_ref, q_ref, fr_ref, iv_ref, og_ref, gates_ref, *, cw, kw):
    hb = _rms(x_ref[...], g_ref[...]).astype(BF16)

    def proj(lo, width):
        return _dot(hb, w_ref[:, lo:lo + width])

    u_ref[...] = (proj(0, cw) * jax.nn.sigmoid(proj(cw, cw))).astype(BF16)
    base = 2 * cw
    q_ref[...] = jax.nn.silu(proj(base, kw)).astype(BF16)
    fr_ref[...] = proj(base + kw, kw)
    iv_ref[...] = proj(base + 2 * kw, kw).astype(BF16)
    og_ref[...] = jax.nn.silu(proj(base + 3 * kw, kw)).astype(BF16)
    base += 4 * kw
    for c in range(gates_ref.shape[1] // kw):
        gates_ref[:, c * kw:(c + 1) * kw] = jax.nn.sigmoid(proj(base + c * kw, kw)).astype(BF16)


def _inproj(x2d, g, w_bf, cw, kw):
    n, d = x2d.shape
    tb = TOKEN_BLOCK
    row = lambda i: (i, 0)
    fixed = lambda i: (0, 0)
    outs = [((n, cw), BF16), ((n, kw), BF16), ((n, kw), F32), ((n, kw), BF16), ((n, kw), BF16), ((n, 2 * d), BF16)]
    return pl.pallas_call(
        functools.partial(_inproj_kernel, cw=cw, kw=kw),
        grid=(n // tb,),
        in_specs=[pl.BlockSpec((tb, d), row), pl.BlockSpec((1, d), fixed), pl.BlockSpec(w_bf.shape, fixed)],
        out_specs=[pl.BlockSpec((tb, s[1]), row) for s, _ in outs],
        out_shape=[jax.ShapeDtypeStruct(s, t) for s, t in outs],
        compiler_params=_params("parallel"),
        name="inproj",
    )(x2d, g, w_bf)


def _conv_kernel(u_ref, cw_ref, cb_ref, lg_ref, lb_ref, wo_ref, gate_ref, y_ref, ext_ref, halo_ref, perm_ref, act_ref):
    tb, c = u_ref.shape
    nt = tb // 8
    slabs = c // 128
    hr = CONV_HALO * 8
    lanes = [slice(l * 128, (l + 1) * 128) for l in range(slabs)]

    @pl.when(pl.program_id(1) == 0)
    def _():
        halo_ref[...] = jnp.zeros_like(halo_ref)

    un = u_ref[...].astype(F32)
    per = nt // 8
    for j in range(nt):
        start = hr + (j % per) * 64 + j // per
        for l in range(slabs):
            ext_ref[l, pl.ds(start, 8, stride=8), :] = un[8 * j:8 * j + 8, lanes[l]]
    first = lax.broadcasted_iota(I32, (hr, 128), 0) % 8 == 0
    for l in range(slabs):
        cur = ext_ref[l, nt * 8:nt * 8 + hr, :]
        ext_ref[l, 0:hr, :] = jnp.where(first, pltpu.roll(halo_ref[l], hr - 7, axis=0), pltpu.roll(cur, 1, axis=0))
        halo_ref[l] = cur

    for r in range(tb // CONV_ROWS):
        accs = []
        for l in range(slabs):
            acc = jnp.broadcast_to(cb_ref[:, lanes[l]], (CONV_ROWS, 128))
            for dt in range(CONV_K):
                off = hr + r * CONV_ROWS - dt * 8
                acc = acc + cw_ref[CONV_K - 1 - dt:CONV_K - dt, lanes[l]] * ext_ref[l, off:off + CONV_ROWS, :]
            accs.append(acc)
        mu = functools.reduce(jnp.add, [jnp.sum(a, axis=-1, keepdims=True) for a in accs]) * (1.0 / c)
        cens = [a - mu for a in accs]
        var = functools.reduce(jnp.add, [jnp.sum(a * a, axis=-1, keepdims=True) for a in cens]) * (1.0 / c)
        inv = lax.rsqrt(var + EPS)
        for l in range(slabs):
            ln = cens[l] * inv * lg_ref[:, lanes[l]] + lb_ref[:, lanes[l]]
            perm_ref[l, r * CONV_ROWS:(r + 1) * CONV_ROWS, :] = jax.nn.silu(ln)
    for j in range(nt):
        start = (j % per) * 64 + j // per
        for l in range(slabs):
            act_ref[8 * j:8 * j + 8, lanes[l]] = perm_ref[l, pl.ds(start, 8, stride=8), :]
    y_ref[...] = (gate_ref[...].astype(F32) * _dot(act_ref[...].astype(BF16), wo_ref[...])).astype(BF16)


def _conv(u, conv_w, conv_b, ln_g, ln_b, w_out_bf, gates, batch):
    n, c = u.shape
    d = w_out_bf.shape[1]
    tb = TOKEN_BLOCK
    nsb = n // batch // tb
    row = lambda b, s: (b * nsb + s, 0)
    fixed = lambda b, s: (0, 0)
    return pl.pallas_call(
        _conv_kernel,
        grid=(batch, nsb),
        in_specs=[pl.BlockSpec((tb, c), row), pl.BlockSpec(conv_w.shape, fixed), pl.BlockSpec((1, c), fixed),
                  pl.BlockSpec((1, c), fixed), pl.BlockSpec((1, c), fixed), pl.BlockSpec((c, d), fixed),
                  pl.BlockSpec((tb, d), row)],
        out_specs=pl.BlockSpec((tb, d), row),
        out_shape=jax.ShapeDtypeStruct((n, d), BF16),
        scratch_shapes=[pltpu.VMEM((c // 128, tb + CONV_HALO * 8, 128), F32),
                        pltpu.VMEM((c // 128, CONV_HALO * 8, 128), F32),
                        pltpu.VMEM((c // 128, tb, 128), F32), pltpu.VMEM((tb, c), F32)],
        compiler_params=_params("arbitrary", "arbitrary"),
        name="conv",
    )(u, conv_w, conv_b, ln_g, ln_b, w_out_bf, gates)


def _hgrn_chunk(q, fr, v, lb, st_ref, tri):
    heads = [slice(h * HGRN_DIM, (h + 1) * HGRN_DIM) for h in range(HGRN_HEADS)]
    f = lb + (1.0 - lb) * jax.nn.sigmoid(fr)
    lf = jnp.log(f)
    kk = 1.0 - f
    lf_hi, lf_lo = _split_bf16(lf)
    cum = _dot(tri, lf_hi) + _dot(tri, lf_lo)
    last = cum[CHUNK - 1:CHUNK, :]
    qe = (q * jnp.exp(cum)).astype(BF16)
    kd = (kk * jnp.exp(last - cum)).astype(BF16)
    decay = jnp.exp(last)

    blocks = []
    for i in range(CHUNK // SUB):
        rs, ne = i * SUB, (i + 1) * SUB
        ref = cum[rs - 1:rs, :] if i else jnp.zeros_like(last)
        qt = (q[rs:ne] * jnp.exp(cum[rs:ne] - ref)).astype(BF16)
        kt = (kk[0:ne] * jnp.exp(ref - cum[0:ne])).astype(BF16)
        blocks.append((qt, kt))

    states = [st_ref[h] for h in range(HGRN_HEADS)]
    o_inter = [_dot_nt(qe[:, hs], st.astype(BF16)) for hs, st in zip(heads, states)]
    scores = [[_dot_nt(qt[:, hs], kt[:, hs]) for qt, kt in blocks] for hs in heads]
    for h, hs in enumerate(heads):
        st_ref[h] = states[h] * decay[:, hs] + _dot_tn(v[:, hs], kd[:, hs])

    outs = []
    for h, hs in enumerate(heads):
        parts = []
        for i, a in enumerate(scores[h]):
            rs, ne = i * SUB, (i + 1) * SUB
            trow = lax.broadcasted_iota(I32, (SUB, ne), 0) + rs
            scol = lax.broadcasted_iota(I32, (SUB, ne), 1)
            a = jnp.where(scol <= trow, a, 0.0).astype(BF16)
            parts.append(_dot(a, v[0:ne, hs]))
        outs.append(o_inter[h] + jnp.concatenate(parts, axis=0))
    return jnp.concatenate(outs, axis=1)


def _hgrn_kernel(q_ref, fr_ref, iv_ref, og_ref, gate_ref, yc_ref, x_ref, lbl_ref, on_ref, wo_ref, wm_ref,
                 x1_ref, st_ref, ob_ref):
    tb = q_ref.shape[0]

    @pl.when(pl.program_id(1) == 0)
    def _():
        st_ref[...] = jnp.zeros_like(st_ref)

    l0, l1 = lbl_ref[0:1, :], lbl_ref[1:2, :]
    m = jnp.maximum(l0, l1)
    e0, e1 = jnp.exp(l0 - m), jnp.exp(l1 - m)
    lb_all = e0 / (e0 + e1)
    trow = lax.broadcasted_iota(I32, (CHUNK, CHUNK), 0)
    tcol = lax.broadcasted_iota(I32, (CHUNK, CHUNK), 1)
    tri = jnp.where(tcol <= trow, 1.0, 0.0).astype(BF16)

    def chunk(ci, carry):
        rows = pl.ds(pl.multiple_of(ci * CHUNK, CHUNK), CHUNK)
        o = _hgrn_chunk(q_ref[rows, :].astype(F32), fr_ref[rows, :], iv_ref[rows, :], lb_all, st_ref, tri)
        og = og_ref[rows, :].astype(F32)
        for h in range(HGRN_HEADS):
            hs = slice(h * HGRN_DIM, (h + 1) * HGRN_DIM)
            ob_ref[rows, hs] = (_rms(o[:, hs], on_ref[...]) * og[:, hs]).astype(BF16)
        return carry

    lax.fori_loop(0, tb // CHUNK, chunk, 0, unroll=4)
    y_rec = _dot(ob_ref[...], wo_ref[...])
    merged = yc_ref[...].astype(F32) + gate_ref[...].astype(F32) * y_rec
    x1_ref[...] = x_ref[...] + _dot(merged.astype(BF16), wm_ref[...])


def _hgrn(q, fr, iv, og, gates, yc, x2d, lb_logits, onorm_g, w_o_bf, w_mix_bf, batch):
    n, kw = q.shape
    d = x2d.shape[1]
    tb = TOKEN_BLOCK
    nsb = n // batch // tb
    row = lambda b, s: (b * nsb + s, 0)
    fixed = lambda b, s: (0, 0)
    return pl.pallas_call(
        _hgrn_kernel,
        grid=(batch, nsb),
        in_specs=[pl.BlockSpec((tb, kw), row), pl.BlockSpec((tb, kw), row), pl.BlockSpec((tb, kw), row),
                  pl.BlockSpec((tb, kw), row),
                  pl.BlockSpec((tb, d), lambda b, s: (b * nsb + s, 1)),
                  pl.BlockSpec((tb, d), row), pl.BlockSpec((tb, d), row),
                  pl.BlockSpec(lb_logits.shape, fixed), pl.BlockSpec((1, HGRN_DIM), fixed),
                  pl.BlockSpec(w_o_bf.shape, fixed), pl.BlockSpec(w_mix_bf.shape, fixed)],
        out_specs=pl.BlockSpec((tb, d), row),
        out_shape=jax.ShapeDtypeStruct((n, d), F32),
        scratch_shapes=[pltpu.VMEM((HGRN_HEADS, HGRN_DIM, HGRN_DIM), F32), pltpu.VMEM((tb, kw), BF16)],
        compiler_params=_params("arbitrary", "arbitrary"),
        name="hgrn",
    )(q, fr, iv, og, gates, yc, x2d, lb_logits, onorm_g, w_o_bf, w_mix_bf)


def _memkv_kernel(mem_ref, g_ref, wk_ref, wv_ref, k_ref, v_ref):
    mb = _rms(mem_ref[...], g_ref[...]).astype(BF16)
    k_ref[...] = _dot(mb, wk_ref[...]).astype(BF16)
    v_ref[...] = _dot(mb, wv_ref[...]).astype(BF16)


def _memkv(mem2d, g, wk_bf, wv_bf, batch):
    n, d = mem2d.shape
    m = n // batch
    row = lambda b: (b, 0)
    fixed = lambda b: (0, 0)
    return pl.pallas_call(
        _memkv_kernel,
        grid=(batch,),
        in_specs=[pl.BlockSpec((m, d), row), pl.BlockSpec((1, d), fixed), pl.BlockSpec((d, d), fixed),
                  pl.BlockSpec((d, d), fixed)],
        out_specs=[pl.BlockSpec((m, d), row)] * 2,
        out_shape=[jax.ShapeDtypeStruct((n, d), BF16)] * 2,
        compiler_params=_params("parallel"),
        name="memkv",
    )(mem2d, g, wk_bf, wv_bf)


def _route(lt):
    def row(r):
        return lt[r:r + 1, :]

    gl = [row(g) for g in range(N_GROUPS)]
    gmax = functools.reduce(jnp.maximum, gl)
    g_p = 1.0 / functools.reduce(jnp.add, [jnp.exp(l - gmax) for l in gl])
    gidx = jnp.full(gmax.shape, N_GROUPS - 1, I32)
    for g in range(N_GROUPS - 2, -1, -1):
        gidx = jnp.where(gl[g] == gmax, g, gidx)

    el = []
    for j in range(EXPERTS_PER_GROUP):
        v = row(N_GROUPS + (N_GROUPS - 1) * EXPERTS_PER_GROUP + j)
        for g in range(N_GROUPS - 2, -1, -1):
            v = jnp.where(gidx == g, row(N_GROUPS + g * EXPERTS_PER_GROUP + j), v)
        el.append(v)

    def argmax(vals):
        mx = functools.reduce(jnp.maximum, vals)
        idx = jnp.full(mx.shape, EXPERTS_PER_GROUP - 1, I32)
        for j in range(EXPERTS_PER_GROUP - 2, -1, -1):
            idx = jnp.where(vals[j] == mx, j, idx)
        return mx, idx

    m1, i1 = argmax(el)
    m2, i2 = argmax([jnp.where(i1 == j, -jnp.inf, el[j]) for j in range(EXPERTS_PER_GROUP)])
    r = jnp.exp(m2 - m1)
    w1 = g_p / (1.0 + r)
    w2 = g_p * r / (1.0 + r)
    base = gidx * EXPERTS_PER_GROUP
    return jnp.concatenate([base + i1, base + i2], axis=0), jnp.concatenate([w1, w2], axis=0)


def _excl_cumsum_rows(col):
    r = col.shape[0]
    lower = lax.broadcasted_iota(I32, (r, r), 1) < lax.broadcasted_iota(I32, (r, r), 0)
    lower_bf = jnp.where(lower, 1.0, 0.0).astype(BF16)
    hi, lo = _split_bf16(jnp.broadcast_to(col, (r, 128)))
    return (_dot(lower_bf, hi) + _dot(lower_bf, lo))[:, 0:1]


def _slot_masks(lpos, chunk, tb):
    slot = lax.broadcasted_iota(I32, (SORT_CHUNK, tb), 0) + chunk * SORT_CHUNK
    return slot == lpos[:, 0:tb], slot == lpos[:, tb:2 * tb]


def _local_sort(eid, wts, h2, upper_ref, xb_ref, cnt_ref, lpos_ref):
    tb, d = h2.shape
    na = TOP_K * tb
    e_all = jnp.concatenate([eid[k:k + 1] for k in range(TOP_K)], axis=1)
    onehot = lax.broadcasted_iota(I32, (N_EXPERTS, na), 0) == e_all
    ones = jnp.where(onehot, 1.0, 0.0)
    cnt = jnp.sum(ones, axis=1, keepdims=True)
    cnt_pad = jnp.floor((cnt + (GROUP - 1)) * (1.0 / GROUP)) * GROUP
    earlier = _dot(ones.astype(BF16), upper_ref[...])
    start = _excl_cumsum_rows(cnt_pad)
    lpos = jnp.sum(jnp.where(onehot, start + earlier, 0.0), axis=0, keepdims=True).astype(I32)
    hb = h2.astype(BF16)
    for c in range(LOCAL_ROWS // SORT_CHUNK):
        m0, m1 = _slot_masks(lpos, c, tb)
        p = jnp.where(m0, 1.0, jnp.where(m1, 1.0, 0.0)).astype(BF16)
        pw = jnp.where(m0, wts[0:1], jnp.where(m1, wts[1:2], 0.0))
        rows = slice(c * SORT_CHUNK, (c + 1) * SORT_CHUNK)
        xb_ref[rows, 0:d] = _dot(p, hb).astype(BF16)
        w_row = jnp.broadcast_to(jnp.sum(pw, axis=1, keepdims=True), (SORT_CHUNK, W_LANES))
        w_hi = w_row.astype(BF16).astype(F32)
        low_half = lax.broadcasted_iota(I32, (SORT_CHUNK, W_LANES), 1) < W_LANES // 2
        xb_ref[rows, d:d + W_LANES] = jnp.where(low_half, w_hi, w_row - w_hi).astype(BF16)
    cnt_ref[...] = cnt_pad.astype(I32)
    lpos_ref[...] = lpos


def _attn_kernel(x1_ref, gxa_ref, wq_ref, k_ref, v_ref, wo_ref, gffn_ref, wr_ref, br_ref, upper_ref,
                 x2_ref, xb_ref, cnt_ref, lpos_ref):
    x1 = x1_ref[...]
    d = x1.shape[1]
    hd = d // XA_HEADS
    q = _dot(_rms(x1, gxa_ref[...]).astype(BF16), wq_ref[...]).astype(BF16)
    heads = []
    for h in range(XA_HEADS):
        hs = slice(h * hd, (h + 1) * hd)
        sc = _dot_nt(q[:, hs], k_ref[:, hs]) * (hd ** -0.5)
        p = jnp.exp(sc - jnp.max(sc, axis=-1, keepdims=True))
        p = p / jnp.sum(p, axis=-1, keepdims=True)
        heads.append(_dot(p.astype(BF16), v_ref[:, hs]).astype(BF16))
    x2 = x1 + _dot(jnp.concatenate(heads, axis=1), wo_ref[...])
    x2_ref[...] = x2
    h2 = _rms(x2, gffn_ref[...])
    h_hi, h_lo = _split_bf16(h2)
    w_hi, w_lo = _split_bf16(wr_ref[...])
    lt = _dot_nt(w_hi, h_hi) + (_dot_nt(w_hi, h_lo) + _dot_nt(w_lo, h_hi)) + br_ref[...]
    eid, wts = _route(lt)
    _local_sort(eid, wts, h2, upper_ref, xb_ref, cnt_ref, lpos_ref)


def _attn(x1, gxa, wq_bf, k_bf, v_bf, wo_bf, gffn, w_route_t, b_route, upper, batch):
    n, d = x1.shape
    m = k_bf.shape[0] // batch
    tb = TOKEN_BLOCK
    nsb = n // batch // tb
    nb = n // tb
    row = lambda b, s: (b * nsb + s, 0)
    blk3 = lambda b, s: (b * nsb + s, 0, 0)
    fixed = lambda b, s: (0, 0)
    mem = lambda b, s: (b, 0)
    return pl.pallas_call(
        _attn_kernel,
        grid=(batch, nsb),
        in_specs=[pl.BlockSpec((tb, d), row), pl.BlockSpec((1, d), fixed), pl.BlockSpec((d, d), fixed),
                  pl.BlockSpec((m, d), mem), pl.BlockSpec((m, d), mem), pl.BlockSpec((d, d), fixed),
                  pl.BlockSpec((1, d), fixed), pl.BlockSpec((ROUTE_ROWS, d), fixed),
                  pl.BlockSpec((ROUTE_ROWS, 1), fixed), pl.BlockSpec(upper.shape, fixed)],
        out_specs=[pl.BlockSpec((tb, d), row), pl.BlockSpec((LOCAL_ROWS, d + W_LANES), row),
                   pl.BlockSpec((None, N_EXPERTS, 1), blk3), pl.BlockSpec((None, 1, TOP_K * tb), blk3)],
        out_shape=[jax.ShapeDtypeStruct((n, d), F32), jax.ShapeDtypeStruct((nb * LOCAL_ROWS, d + W_LANES), BF16),
                   jax.ShapeDtypeStruct((nb, N_EXPERTS, 1), I32), jax.ShapeDtypeStruct((nb, 1, TOP_K * tb), I32)],
        compiler_params=_params("parallel", "parallel"),
        name="attn",
    )(x1, gxa, wq_bf, k_bf, v_bf, wo_bf, gffn, w_route_t, b_route, upper)


def _plan_kernel(cnt_ref, te_ref, src_ref, dst_ref, nact_ref):
    runs = cnt_ref.shape[0]
    lg = LOCAL_ROWS // GROUP
    tg = MOE_TILE // GROUP
    sh = N_EXPERTS.bit_length() - 1
    emask = N_EXPERTS - 1
    zero_group = lg - 1

    length = cnt_ref[...].astype(F32) * (1.0 / GROUP)
    len_bf = jnp.broadcast_to(length, (runs, 128)).astype(BF16)
    ri = lax.broadcasted_iota(I32, (runs, runs), 0)
    ci = lax.broadcasted_iota(I32, (runs, runs), 1)
    r_e, c_e = ri & emask, ci & emask
    r_b, c_b = lax.shift_right_logical(ri, sh), lax.shift_right_logical(ci, sh)
    same_expert_earlier = jnp.where(r_e == c_e, jnp.where(c_b < r_b, 1.0, 0.0), 0.0).astype(BF16)
    same_block_earlier = jnp.where(r_b == c_b, jnp.where(c_e < r_e, 1.0, 0.0), 0.0).astype(BF16)
    before = _dot(same_expert_earlier, len_bf)[:, 0:1]
    local = _dot(same_block_earlier, len_bf)[:, 0:1]

    of_expert = (lax.broadcasted_iota(I32, (N_EXPERTS, runs), 1) & emask) == lax.broadcasted_iota(
        I32, (N_EXPERTS, runs), 0)
    total = _dot(jnp.where(of_expert, 1.0, 0.0).astype(BF16), len_bf)[:, 0:1]
    tiles = jnp.floor((total + (tg - 1)) * (1.0 / tg))
    tile0 = _excl_cumsum_rows(tiles)
    to_run = (lax.broadcasted_iota(I32, (runs, N_EXPERTS), 0) & emask) == lax.broadcasted_iota(
        I32, (runs, N_EXPERTS), 1)
    to_run_bf = jnp.where(to_run, 1.0, 0.0).astype(BF16)
    base_hi, base_lo = _split_bf16(jnp.broadcast_to(tile0 * tg, (N_EXPERTS, 128)))
    g_start = (_dot(to_run_bf, base_hi) + _dot(to_run_bf, base_lo))[:, 0:1] + before
    block = lax.shift_right_logical(lax.broadcasted_iota(I32, (runs, 1), 0), sh).astype(F32)
    l_start = block * lg + local

    def cover(out_ref, start, offset, default):
        stop = start + length
        shift = offset - default
        n_out = out_ref.shape[1]
        chunk = next(c for c in PLAN_CHUNKS if n_out % c == 0)
        for c in range(n_out // chunk):
            j = (lax.broadcasted_iota(I32, (runs, chunk), 1) + c * chunk).astype(F32)
            hit = jnp.where(start <= j, jnp.where(j < stop, j + shift, 0.0), 0.0)
            out = jnp.sum(hit, axis=0, keepdims=True) + default
            out_ref[:, c * chunk:(c + 1) * chunk] = out.astype(I32)

    cover(src_ref, g_start, l_start - g_start, float(zero_group))
    cover(dst_ref, l_start, g_start - l_start, 0.0)
    t = lax.broadcasted_iota(I32, (N_EXPERTS, te_ref.shape[1]), 1).astype(F32)
    te_ref[...] = jnp.sum(jnp.where(t >= tile0 + tiles, 1.0, 0.0), axis=0, keepdims=True).astype(I32)
    nact_ref[...] = jnp.sum(jnp.broadcast_to(tiles, (N_EXPERTS, 128)), axis=0, keepdims=True).astype(I32)


def _plan(cnt, n_tiles):
    runs = cnt.shape[0]
    n_src = n_tiles * (MOE_TILE // GROUP)
    n_dst = runs // N_EXPERTS * (LOCAL_ROWS // GROUP)
    n_te = -(-n_tiles // 128) * 128
    te, src, dst, nact = pl.pallas_call(
        _plan_kernel,
        out_shape=[jax.ShapeDtypeStruct((1, n_te), I32), jax.ShapeDtypeStruct((1, n_src), I32),
                   jax.ShapeDtypeStruct((1, n_dst), I32), jax.ShapeDtypeStruct((1, 128), I32)],
        compiler_params=_params(),
        name="plan",
    )(cnt)
    return te.reshape(-1), src.reshape(-1), dst.reshape(-1), nact[0, :1]


def _group_copy(src_hbm, src_group, dst_buf, slot, index, sem):
    start = src_group * GROUP
    rows = pl.ds(start if isinstance(start, int) else pl.multiple_of(start, GROUP), GROUP)
    return pltpu.make_async_copy(src_hbm.at[rows], dst_buf.at[slot, pl.ds(index * GROUP, GROUP)], sem.at[slot])


def _experts_kernel(te_ref, src_ref, nact_ref, xb_ref, wg_hbm, wu_hbm, wd_hbm, ys_ref,
                    xbuf, sem, wg_st, wu_st, wd_st, wsem, wg_bf, wu_bf, wd_bf):
    t = pl.program_id(0)
    tg = MOE_TILE // GROUP
    d = ys_ref.shape[1]
    nact = nact_ref[0]
    e = te_ref[t]
    slot = lax.rem(t, 2)
    last = te_ref.shape[0] - 1

    def start_gather(tile):
        for i in range(tg):
            _group_copy(xb_ref, src_ref[tile * tg + i], xbuf, lax.rem(tile, 2), i, sem).start(priority=i % 2)

    def weight_copies(expert):
        pairs = ((wg_hbm, wg_st), (wu_hbm, wu_st), (wd_hbm, wd_st))
        return [pltpu.make_async_copy(w.at[expert], st, wsem.at[i]) for i, (w, st) in enumerate(pairs)]

    @pl.when(jnp.logical_and(t == 0, nact > 0))
    def _():
        start_gather(t)
        for c in weight_copies(e):
            c.start()

    @pl.when(t + 1 < nact)
    def _():
        start_gather(t + 1)

    @pl.when(jnp.logical_and(t < nact, jnp.logical_or(t == 0, e != te_ref[jnp.maximum(t - 1, 0)])))
    def _():
        for c in weight_copies(e):
            c.wait()
        wg_bf[...] = wg_st[...].astype(BF16)
        wu_bf[...] = wu_st[...].astype(BF16)
        wd_bf[...] = wd_st[...].astype(BF16)
        nxt = lax.while_loop(lambda j: jnp.logical_and(j < nact, te_ref[jnp.minimum(j, last)] == e),
                             lambda j: j + 1, t + 1)

        @pl.when(nxt < nact)
        def _():
            for c in weight_copies(te_ref[jnp.minimum(nxt, last)]):
                c.start()

    @pl.when(t < nact)
    def _():
        for i in range(tg):
            _group_copy(xb_ref, 0, xbuf, slot, i, sem).wait()
        x = xbuf[slot, :, 0:d]
        hid = jax.nn.silu(_dot(x, wg_bf[...])) * _dot(x, wu_bf[...])
        weight = (xbuf[slot, :, d:d + 1].astype(F32)
                  + xbuf[slot, :, d + W_LANES // 2:d + W_LANES // 2 + 1].astype(F32))
        ys_ref[...] = (_dot(hid.astype(BF16), wd_bf[...]) * weight).astype(BF16)

    @pl.when(t >= nact)
    def _():
        ys_ref[...] = jnp.zeros_like(ys_ref)


def _experts(te, src, nact, xb, w_gate, w_up, w_down):
    n_tiles = te.shape[0]
    dw = xb.shape[1]
    d = dw - W_LANES
    ff = w_gate.shape[2]
    tm = MOE_TILE
    hbm = pl.BlockSpec(memory_space=pl.ANY)
    return pl.pallas_call(
        _experts_kernel,
        grid_spec=pltpu.PrefetchScalarGridSpec(
            num_scalar_prefetch=3,
            grid=(n_tiles,),
            in_specs=[hbm, hbm, hbm, hbm],
            out_specs=pl.BlockSpec((tm, d), lambda t, te, src, nact: (t, 0)),
            scratch_shapes=[pltpu.VMEM((2, tm, dw), BF16), pltpu.SemaphoreType.DMA((2,)),
                            pltpu.VMEM((d, ff), F32), pltpu.VMEM((d, ff), F32), pltpu.VMEM((ff, d), F32),
                            pltpu.SemaphoreType.DMA((3,)),
                            pltpu.VMEM((d, ff), BF16), pltpu.VMEM((d, ff), BF16), pltpu.VMEM((ff, d), BF16)],
        ),
        out_shape=jax.ShapeDtypeStruct((n_tiles * tm, d), BF16),
        compiler_params=_params("arbitrary"),
        name="experts",
    )(te, src, nact, xb, w_gate, w_up, w_down)


def _combine_kernel(dst_ref, lpos_ref, x2_ref, g_ref, ys_ref, out_ref, ybuf, sem):
    b = pl.program_id(0)
    tb = x2_ref.shape[0]
    lg = LOCAL_ROWS // GROUP
    slot = lax.rem(b, 2)

    def start_gather(blk):
        for i in range(lg):
            _group_copy(ys_ref, dst_ref[blk * lg + i], ybuf, lax.rem(blk, 2), i, sem).start(priority=i % 2)

    @pl.when(b == 0)
    def _():
        start_gather(b)

    @pl.when(b + 1 < pl.num_programs(0))
    def _():
        start_gather(b + 1)

    for i in range(lg):
        _group_copy(ys_ref, 0, ybuf, slot, i, sem).wait()
    lpos = lpos_ref[...]
    y = jnp.zeros(x2_ref.shape, F32)
    for c in range(LOCAL_ROWS // SORT_CHUNK):
        m0, m1 = _slot_masks(lpos, c, tb)
        p = jnp.where(m0, 1.0, jnp.where(m1, 1.0, 0.0)).astype(BF16)
        y = y + _dot_tn(p, ybuf[slot, c * SORT_CHUNK:(c + 1) * SORT_CHUNK, :])
    out_ref[...] = _rms(x2_ref[...] + y, g_ref[...])


def _combine(dst, lpos, x2, g_final, ys):
    n, d = x2.shape
    tb = TOKEN_BLOCK
    return pl.pallas_call(
        _combine_kernel,
        grid_spec=pltpu.PrefetchScalarGridSpec(
            num_scalar_prefetch=1,
            grid=(n // tb,),
            in_specs=[pl.BlockSpec((None, 1, TOP_K * tb), lambda i, dst: (i, 0, 0)),
                      pl.BlockSpec((tb, d), lambda i, dst: (i, 0)), pl.BlockSpec((1, d), lambda i, dst: (0, 0)),
                      pl.BlockSpec(memory_space=pl.ANY)],
            out_specs=pl.BlockSpec((tb, d), lambda i, dst: (i, 0)),
            scratch_shapes=[pltpu.VMEM((2, LOCAL_ROWS, d), BF16), pltpu.SemaphoreType.DMA((2,))],
        ),
        out_shape=jax.ShapeDtypeStruct((n, d), F32),
        compiler_params=_params("arbitrary"),
        name="combine",
    )(dst, lpos, x2, g_final, ys)


def kernel(x, mem, norm_mix_g, w_in, conv_w, conv_b, conv_ln_g, conv_ln_b, conv_w_out, hgrn_lb_logits, hgrn_onorm_g, hgrn_w_out, w_mix_out, norm_xa_g, norm_mem_g, xa_w_q, xa_w_k, xa_w_v, xa_w_o, norm_ffn_g, router_group_w, router_group_b, router_expert_w, router_expert_b, moe_w_gate, moe_w_up, moe_w_down, final_norm_g):
    batch, seq, d = x.shape
    n = batch * seq
    assert w_in.shape[0] == 1, "the final RMSNorm is fused into the single layer's combine step"
    cw = conv_w.shape[2]
    kw = hgrn_w_out.shape[1]
    assert kw == HGRN_HEADS * HGRN_DIM and conv_w.shape[1] == CONV_K
    assert seq % TOKEN_BLOCK == 0 and TOKEN_BLOCK % CHUNK == 0 and TOKEN_BLOCK % CONV_ROWS == 0
    assert moe_w_gate.shape[1] == N_EXPERTS and router_group_w.shape[2] == N_GROUPS and TOP_K == 2
    na = TOP_K * TOKEN_BLOCK
    assert LOCAL_ROWS % SORT_CHUNK == 0 and LOCAL_ROWS >= na + N_EXPERTS * (GROUP - 1) + GROUP

    n_tiles = -(-(n // TOKEN_BLOCK) * (na + N_EXPERTS * (GROUP - 1)) // MOE_TILE) + N_EXPERTS
    n_tiles = -(-n_tiles // 16) * 16
    vec = lambda p: p.reshape(1, -1)
    l = 0

    x2d = x.reshape(n, d)
    u, q, fr, iv, og, gates = _inproj(x2d, vec(norm_mix_g[l]), w_in[l].astype(BF16), cw, kw)
    yc = _conv(u, conv_w[l], vec(conv_b[l]), vec(conv_ln_g[l]), vec(conv_ln_b[l]),
               conv_w_out[l].astype(BF16), gates, batch)
    x1 = _hgrn(q, fr, iv, og, gates, yc, x2d, hgrn_lb_logits[l:l + 2], vec(hgrn_onorm_g[l]),
               hgrn_w_out[l].astype(BF16), w_mix_out[l].astype(BF16), batch)
    k_bf, v_bf = _memkv(mem.reshape(-1, d), vec(norm_mem_g[l]), xa_w_k[l].astype(BF16), xa_w_v[l].astype(BF16), batch)
    pad = ROUTE_ROWS - N_GROUPS - N_EXPERTS
    w_route_t = jnp.pad(jnp.concatenate([router_group_w[l], router_expert_w[l]], axis=1).T, ((0, pad), (0, 0)))
    b_route = jnp.pad(jnp.concatenate([router_group_b[l], router_expert_b[l]]), (0, pad)).reshape(ROUTE_ROWS, 1)
    upper = (lax.broadcasted_iota(I32, (na, na), 0) < lax.broadcasted_iota(I32, (na, na), 1)).astype(BF16)
    x2, xb, cnt, lpos = _attn(x1, vec(norm_xa_g[l]), xa_w_q[l].astype(BF16), k_bf, v_bf, xa_w_o[l].astype(BF16),
                              vec(norm_ffn_g[l]), w_route_t, b_route, upper, batch)
    te, src, dst, nact = _plan(cnt.reshape(-1, 1), n_tiles)
    ys = _experts(te, src, nact, xb, moe_w_gate[l], moe_w_up[l], moe_w_down[l])
    out = _combine(dst, lpos, x2, vec(final_norm_g), ys)
    return out.reshape(batch, seq, d)
```

```python
import functools

import jax
import jax.numpy as jnp
from jax import lax
from jax.experimental import pallas as pl
from jax.experimental.pallas import tpu as pltpu

F32 = jnp.float32
BF16 = jnp.bfloat16
I32 = jnp.int32

EPS = 1e-6
CONV_K = 31
CONV_HALO = 32
CONV_ROWS = 64
HGRN_HEADS = 4
HGRN_DIM = 128
CHUNK = 64
SUB = 16
XA_HEADS = 4
N_GROUPS = 4
EXPERTS_PER_GROUP = 8
N_EXPERTS = N_GROUPS * EXPERTS_PER_GROUP
TOP_K = 2
ROUTE_ROWS = 40
TOKEN_BLOCK = 512
MOE_TILE = 256
GROUP = 16
LOCAL_ROWS = 1536
SORT_CHUNK = 256
W_LANES = 128
PLAN_CHUNKS = (512, 256, 128)
VMEM_LIMIT_BYTES = 48 * 1024 * 1024


def _rms(x, g):
    return x * lax.rsqrt(jnp.mean(x * x, axis=-1, keepdims=True) + EPS) * g


def _dot(a, b):
    return jnp.dot(a, b, preferred_element_type=F32)


def _dot_nt(a, b):
    return lax.dot_general(a, b, (((1,), (1,)), ((), ())), preferred_element_type=F32)


def _dot_tn(a, b):
    return lax.dot_general(a, b, (((0,), (0,)), ((), ())), preferred_element_type=F32)


def _split_bf16(x):
    hi = x.astype(BF16)
    lo = (x - hi.astype(F32)).astype(BF16)
    return hi, lo


def _params(*sem):
    return pltpu.CompilerParams(dimension_semantics=sem, vmem_limit_bytes=VMEM_LIMIT_BYTES)


def _inproj_kernel(x_ref, g_ref, w_ref, u_ref, q_ref, fr_ref, iv_ref, og_ref, gates_ref, *, cw, kw):
    hb = _rms(x_ref[...], g_ref[...]).astype(BF16)

    def proj(lo, width):
        return _dot(hb, w_ref[:, lo:lo + width])

    u_ref[...] = (proj(0, cw) * jax.nn.sigmoid(proj(cw, cw))).astype(BF16)
    base = 2 * cw
    q_ref[...] = jax.nn.silu(proj(base, kw)).astype(BF16)
    fr_ref[...] = proj(base + kw, kw)
    iv_ref[...] = proj(base + 2 * kw, kw).astype(BF16)
    og_ref[...] = jax.nn.silu(proj(base + 3 * kw, kw)).astype(BF16)
    base += 4 * kw
    for c in range(gates_ref.shape[1] // kw):
        gates_ref[:, c * kw:(c + 1) * kw] = jax.nn.sigmoid(proj(base + c * kw, kw)).astype(BF16)


def _inproj(x2d, g, w_bf, cw, kw):
    n, d = x2d.shape
    tb = TOKEN_BLOCK
    row = lambda i: (i, 0)
    fixed = lambda i: (0, 0)
    outs = [((n, cw), BF16), ((n, kw), BF16), ((n, kw), F32), ((n, kw), BF16), ((n, kw), BF16), ((n, 2 * d), BF16)]
    return pl.pallas_call(
        functools.partial(_inproj_kernel, cw=cw, kw=kw),
        grid=(n // tb,),
        in_specs=[pl.BlockSpec((tb, d), row), pl.BlockSpec((1, d), fixed), pl.BlockSpec(w_bf.shape, fixed)],
        out_specs=[pl.BlockSpec((tb, s[1]), row) for s, _ in outs],
        out_shape=[jax.ShapeDtypeStruct(s, t) for s, t in outs],
        compiler_params=_params("parallel"),
        name="inproj",
    )(x2d, g, w_bf)


def _conv_kernel(u_ref, cw_ref, cb_ref, lg_ref, lb_ref, wo_ref, gate_ref, y_ref, ext_ref, halo_ref, perm_ref, act_ref):
    tb, c = u_ref.shape
    nt = tb // 8
    slabs = c // 128
    hr = CONV_HALO * 8
    lanes = [slice(l * 128, (l + 1) * 128) for l in range(slabs)]

    @pl.when(pl.program_id(1) == 0)
    def _():
        halo_ref[...] = jnp.zeros_like(halo_ref)

    un = u_ref[...].astype(F32)
    per = nt // 8
    for j in range(nt):
        start = hr + (j % per) * 64 + j // per
        for l in range(slabs):
            ext_ref[l, pl.ds(start, 8, stride=8), :] = un[8 * j:8 * j + 8, lanes[l]]
    first = lax.broadcasted_iota(I32, (hr, 128), 0) % 8 == 0
    for l in range(slabs):
        cur = ext_ref[l, nt * 8:nt * 8 + hr, :]
        ext_ref[l, 0:hr, :] = jnp.where(first, pltpu.roll(halo_ref[l], hr - 7, axis=0), pltpu.roll(cur, 1, axis=0))
        halo_ref[l] = cur

    for r in range(tb // CONV_ROWS):
        accs = []
        for l in range(slabs):
            acc = jnp.broadcast_to(cb_ref[:, lanes[l]], (CONV_ROWS, 128))
            for dt in range(CONV_K):
                off = hr + r * CONV_ROWS - dt * 8
                acc = acc + cw_ref[CONV_K - 1 - dt:CONV_K - dt, lanes[l]] * ext_ref[l, off:off + CONV_ROWS, :]
            accs.append(acc)
        mu = functools.reduce(jnp.add, [jnp.sum(a, axis=-1, keepdims=True) for a in accs]) * (1.0 / c)
        cens = [a - mu for a in accs]
        var = functools.reduce(jnp.add, [jnp.sum(a * a, axis=-1, keepdims=True) for a in cens]) * (1.0 / c)
        inv = lax.rsqrt(var + EPS)
        for l in range(slabs):
            ln = cens[l] * inv * lg_ref[:, lanes[l]] + lb_ref[:, lanes[l]]
            perm_ref[l, r * CONV_ROWS:(r + 1) * CONV_ROWS, :] = jax.nn.silu(ln)
    for j in range(nt):
        start = (j % per) * 64 + j // per
        for l in range(slabs):
            act_ref[8 * j:8 * j + 8, lanes[l]] = perm_ref[l, pl.ds(start, 8, stride=8), :]
    y_ref[...] = (gate_ref[...].astype(F32) * _dot(act_ref[...].astype(BF16), wo_ref[...])).astype(BF16)


def _conv(u, conv_w, conv_b, ln_g, ln_b, w_out_bf, gates, batch):
    n, c = u.shape
    d = w_out_bf.shape[1]
    tb = TOKEN_BLOCK
    nsb = n // batch // tb
    row = lambda b, s: (b * nsb + s, 0)
    fixed = lambda b, s: (0, 0)
    return pl.pallas_call(
        _conv_kernel,
        grid=(batch, nsb),
        in_specs=[pl.BlockSpec((tb, c), row), pl.BlockSpec(conv_w.shape, fixed), pl.BlockSpec((1, c), fixed),
                  pl.BlockSpec((1, c), fixed), pl.BlockSpec((1, c), fixed), pl.BlockSpec((c, d), fixed),
                  pl.BlockSpec((tb, d), row)],
        out_specs=pl.BlockSpec((tb, d), row),
        out_shape=jax.ShapeDtypeStruct((n, d), BF16),
        scratch_shapes=[pltpu.VMEM((c // 128, tb + CONV_HALO * 8, 128), F32),
                        pltpu.VMEM((c // 128, CONV_HALO * 8, 128), F32),
                        pltpu.VMEM((c // 128, tb, 128), F32), pltpu.VMEM((tb, c), F32)],
        compiler_params=_params("arbitrary", "arbitrary"),
        name="conv",
    )(u, conv_w, conv_b, ln_g, ln_b, w_out_bf, gates)


def _hgrn_chunk(q, fr, v, lb, st_ref, tri):
    heads = [slice(h * HGRN_DIM, (h + 1) * HGRN_DIM) for h in range(HGRN_HEADS)]
    f = lb + (1.0 - lb) * jax.nn.sigmoid(fr)
    lf = jnp.log(f)
    kk = 1.0 - f
    lf_hi, lf_lo = _split_bf16(lf)
    cum = _dot(tri, lf_hi) + _dot(tri, lf_lo)
    last = cum[CHUNK - 1:CHUNK, :]
    qe = (q * jnp.exp(cum)).astype(BF16)
    kd = (kk * jnp.exp(last - cum)).astype(BF16)
    decay = jnp.exp(last)

    blocks = []
    for i in range(CHUNK // SUB):
        rs, ne = i * SUB, (i + 1) * SUB
        ref = cum[rs - 1:rs, :] if i else jnp.zeros_like(last)
        qt = (q[rs:ne] * jnp.exp(cum[rs:ne] - ref)).astype(BF16)
        kt = (kk[0:ne] * jnp.exp(ref - cum[0:ne])).astype(BF16)
        blocks.append((qt, kt))

    states = [st_ref[h] for h in range(HGRN_HEADS)]
    o_inter = [_dot_nt(qe[:, hs], st.astype(BF16)) for hs, st in zip(heads, states)]
    scores = [[_dot_nt(qt[:, hs], kt[:, hs]) for qt, kt in blocks] for hs in heads]
    for h, hs in enumerate(heads):
        st_ref[h] = states[h] * decay[:, hs] + _dot_tn(v[:, hs], kd[:, hs])

    outs = []
    for h, hs in enumerate(heads):
        parts = []
        for i, a in enumerate(scores[h]):
            rs, ne = i * SUB, (i + 1) * SUB
            trow = lax.broadcasted_iota(I32, (SUB, ne), 0) + rs
            scol = lax.broadcasted_iota(I32, (SUB, ne), 1)
            a = jnp.where(scol <= trow, a, 0.0).astype(BF16)
            parts.append(_dot(a, v[0:ne, hs]))
        outs.append(o_inter[h] + jnp.concatenate(parts, axis=0))
    return jnp.concatenate(outs, axis=1)


def _hgrn_kernel(q_ref, fr_ref, iv_ref, og_ref, gate_ref, yc_ref, x_ref, lbl_ref, on_ref, wo_ref, wm_ref,
                 x1_ref, st_ref, ob_ref):
    tb = q_ref.shape[0]

    @pl.when(pl.program_id(1) == 0)
    def _():
        st_ref[...] = jnp.zeros_like(st_ref)

    l0, l1 = lbl_ref[0:1, :], lbl_ref[1:2, :]
    m = jnp.maximum(l0, l1)
    e0, e1 = jnp.exp(l0 - m), jnp.exp(l1 - m)
    lb_all = e0 / (e0 + e1)
    trow = lax.broadcasted_iota(I32, (CHUNK, CHUNK), 0)
    tcol = lax.broadcasted_iota(I32, (CHUNK, CHUNK), 1)
    tri = jnp.where(tcol <= trow, 1.0, 0.0).astype(BF16)

    def chunk(ci, carry):
        rows = pl.ds(pl.multiple_of(ci * CHUNK, CHUNK), CHUNK)
        o = _hgrn_chunk(q_ref[rows, :].astype(F32), fr_ref[rows, :], iv_ref[rows, :], lb_all, st_ref, tri)
        og = og_ref[rows, :].astype(F32)
        for h in range(HGRN_HEADS):
            hs = slice(h * HGRN_DIM, (h + 1) * HGRN_DIM)
            ob_ref[rows, hs] = (_rms(o[:, hs], on_ref[...]) * og[:, hs]).astype(BF16)
        return carry

    lax.fori_loop(0, tb // CHUNK, chunk, 0, unroll=4)
    y_rec = _dot(ob_ref[...], wo_ref[...])
    merged = yc_ref[...].astype(F32) + gate_ref[...].astype(F32) * y_rec
    x1_ref[...] = x_ref[...] + _dot(merged.astype(BF16), wm_ref[...])


def _hgrn(q, fr, iv, og, gates, yc, x2d, lb_logits, onorm_g, w_o_bf, w_mix_bf, batch):
    n, kw = q.shape
    d = x2d.shape[1]
    tb = TOKEN_BLOCK
    nsb = n // batch // tb
    row = lambda b, s: (b * nsb + s, 0)
    fixed = lambda b, s: (0, 0)
    return pl.pallas_call(
        _hgrn_kernel,
        grid=(batch, nsb),
        in_specs=[pl.BlockSpec((tb, kw), row), pl.BlockSpec((tb, kw), row), pl.BlockSpec((tb, kw), row),
                  pl.BlockSpec((tb, kw), row),
                  pl.BlockSpec((tb, d), lambda b, s: (b * nsb + s, 1)),
                  pl.BlockSpec((tb, d), row), pl.BlockSpec((tb, d), row),
                  pl.BlockSpec(lb_logits.shape, fixed), pl.BlockSpec((1, HGRN_DIM), fixed),
                  pl.BlockSpec(w_o_bf.shape, fixed), pl.BlockSpec(w_mix_bf.shape, fixed)],
        out_specs=pl.BlockSpec((tb, d), row),
        out_shape=jax.ShapeDtypeStruct((n, d), F32),
        scratch_shapes=[pltpu.VMEM((HGRN_HEADS, HGRN_DIM, HGRN_DIM), F32), pltpu.VMEM((tb, kw), BF16)],
        compiler_params=_params("arbitrary", "arbitrary"),
        name="hgrn",
    )(q, fr, iv, og, gates, yc, x2d, lb_logits, onorm_g, w_o_bf, w_mix_bf)


def _memkv_kernel(mem_ref, g_ref, wk_ref, wv_ref, k_ref, v_ref):
    mb = _rms(mem_ref[...], g_ref[...]).astype(BF16)
    k_ref[...] = _dot(mb, wk_ref[...]).astype(BF16)
    v_ref[...] = _dot(mb, wv_ref[...]).astype(BF16)


def _memkv(mem2d, g, wk_bf, wv_bf, batch):
    n, d = mem2d.shape
    m = n // batch
    row = lambda b: (b, 0)
    fixed = lambda b: (0, 0)
    return pl.pallas_call(
        _memkv_kernel,
        grid=(batch,),
        in_specs=[pl.BlockSpec((m, d), row), pl.BlockSpec((1, d), fixed), pl.BlockSpec((d, d), fixed),
                  pl.BlockSpec((d, d), fixed)],
        out_specs=[pl.BlockSpec((m, d), row)] * 2,
        out_shape=[jax.ShapeDtypeStruct((n, d), BF16)] * 2,
        compiler_params=_params("parallel"),
        name="memkv",
    )(mem2d, g, wk_bf, wv_bf)


def _route(lt):
    def row(r):
        return lt[r:r + 1, :]

    gl = [row(g) for g in range(N_GROUPS)]
    gmax = functools.reduce(jnp.maximum, gl)
    g_p = 1.0 / functools.reduce(jnp.add, [jnp.exp(l - gmax) for l in gl])
    gidx = jnp.full(gmax.shape, N_GROUPS - 1, I32)
    for g in range(N_GROUPS - 2, -1, -1):
        gidx = jnp.where(gl[g] == gmax, g, gidx)

    el = []
    for j in range(EXPERTS_PER_GROUP):
        v = row(N_GROUPS + (N_GROUPS - 1) * EXPERTS_PER_GROUP + j)
        for g in range(N_GROUPS - 2, -1, -1):
            v = jnp.where(gidx == g, row(N_GROUPS + g * EXPERTS_PER_GROUP + j), v)
        el.append(v)

    def argmax(vals):
        mx = functools.reduce(jnp.maximum, vals)
        idx = jnp.full(mx.shape, EXPERTS_PER_GROUP - 1, I32)
        for j in range(EXPERTS_PER_GROUP - 2, -1, -1):
            idx = jnp.where(vals[j] == mx, j, idx)
        return mx, idx

    m1, i1 = argmax(el)
    m2, i2 = argmax([jnp.where(i1 == j, -jnp.inf, el[j]) for j in range(EXPERTS_PER_GROUP)])
    r = jnp.exp(m2 - m1)
    w1 = g_p / (1.0 + r)
    w2 = g_p * r / (1.0 + r)
    base = gidx * EXPERTS_PER_GROUP
    return jnp.concatenate([base + i1, base + i2], axis=0), jnp.concatenate([w1, w2], axis=0)


def _excl_cumsum_rows(col):
    r = col.shape[0]
    lower = lax.broadcasted_iota(I32, (r, r), 1) < lax.broadcasted_iota(I32, (r, r), 0)
    lower_bf = jnp.where(lower, 1.0, 0.0).astype(BF16)
    hi, lo = _split_bf16(jnp.broadcast_to(col, (r, 128)))
    return (_dot(lower_bf, hi) + _dot(lower_bf, lo))[:, 0:1]


def _slot_masks(lpos, chunk, tb):
    slot = lax.broadcasted_iota(I32, (SORT_CHUNK, tb), 0) + chunk * SORT_CHUNK
    return slot == lpos[:, 0:tb], slot == lpos[:, tb:2 * tb]


def _local_sort(eid, wts, h2, upper_ref, xb_ref, cnt_ref, lpos_ref):
    tb, d = h2.shape
    na = TOP_K * tb
    e_all = jnp.concatenate([eid[k:k + 1] for k in range(TOP_K)], axis=1)
    onehot = lax.broadcasted_iota(I32, (N_EXPERTS, na), 0) == e_all
    ones = jnp.where(onehot, 1.0, 0.0)
    cnt = jnp.sum(ones, axis=1, keepdims=True)
    cnt_pad = jnp.floor((cnt + (GROUP - 1)) * (1.0 / GROUP)) * GROUP
    earlier = _dot(ones.astype(BF16), upper_ref[...])
    start = _excl_cumsum_rows(cnt_pad)
    lpos = jnp.sum(jnp.where(onehot, start + earlier, 0.0), axis=0, keepdims=True).astype(I32)
    hb = h2.astype(BF16)
    for c in range(LOCAL_ROWS // SORT_CHUNK):
        m0, m1 = _slot_masks(lpos, c, tb)
        p = jnp.where(m0, 1.0, jnp.where(m1, 1.0, 0.0)).astype(BF16)
        pw = jnp.where(m0, wts[0:1], jnp.where(m1, wts[1:2], 0.0))
        rows = slice(c * SORT_CHUNK, (c + 1) * SORT_CHUNK)
        xb_ref[rows, 0:d] = _dot(p, hb).astype(BF16)
        w_row = jnp.broadcast_to(jnp.sum(pw, axis=1, keepdims=True), (SORT_CHUNK, W_LANES))
        w_hi = w_row.astype(BF16).astype(F32)
        low_half = lax.broadcasted_iota(I32, (SORT_CHUNK, W_LANES), 1) < W_LANES // 2
        xb_ref[rows, d:d + W_LANES] = jnp.where(low_half, w_hi, w_row - w_hi).astype(BF16)
    cnt_ref[...] = cnt_pad.astype(I32)
    lpos_ref[...] = lpos


def _attn_kernel(x1_ref, gxa_ref, wq_ref, k_ref, v_ref, wo_ref, gffn_ref, wr_ref, br_ref, upper_ref,
                 x2_ref, xb_ref, cnt_ref, lpos_ref):
    x1 = x1_ref[...]
    d = x1.shape[1]
    hd = d // XA_HEADS
    q = _dot(_rms(x1, gxa_ref[...]).astype(BF16), wq_ref[...]).astype(BF16)
    heads = []
    for h in range(XA_HEADS):
        hs = slice(h * hd, (h + 1) * hd)
        sc = _dot_nt(q[:, hs], k_ref[:, hs]) * (hd ** -0.5)
        p = jnp.exp(sc - jnp.max(sc, axis=-1, keepdims=True))
        p = p / jnp.sum(p, axis=-1, keepdims=True)
        heads.append(_dot(p.astype(BF16), v_ref[:, hs]).astype(BF16))
    x2 = x1 + _dot(jnp.concatenate(heads, axis=1), wo_ref[...])
    x2_ref[...] = x2
    h2 = _rms(x2, gffn_ref[...])
    h_hi, h_lo = _split_bf16(h2)
    w_hi, w_lo = _split_bf16(wr_ref[...])
    lt = _dot_nt(w_hi, h_hi) + (_dot_nt(w_hi, h_lo) + _dot_nt(w_lo, h_hi)) + br_ref[...]
    eid, wts = _route(lt)
    _local_sort(eid, wts, h2, upper_ref, xb_ref, cnt_ref, lpos_ref)


def _attn(x1, gxa, wq_bf, k_bf, v_bf, wo_bf, gffn, w_route_t, b_route, upper, batch):
    n, d = x1.shape
    m = k_bf.shape[0] // batch
    tb = TOKEN_BLOCK
    nsb = n // batch // tb
    nb = n // tb
    row = lambda b, s: (b * nsb + s, 0)
    blk3 = lambda b, s: (b * nsb + s, 0, 0)
    fixed = lambda b, s: (0, 0)
    mem = lambda b, s: (b, 0)
    return pl.pallas_call(
        _attn_kernel,
        grid=(batch, nsb),
        in_specs=[pl.BlockSpec((tb, d), row), pl.BlockSpec((1, d), fixed), pl.BlockSpec((d, d), fixed),
                  pl.BlockSpec((m, d), mem), pl.BlockSpec((m, d), mem), pl.BlockSpec((d, d), fixed),
                  pl.BlockSpec((1, d), fixed), pl.BlockSpec((ROUTE_ROWS, d), fixed),
                  pl.BlockSpec((ROUTE_ROWS, 1), fixed), pl.BlockSpec(upper.shape, fixed)],
        out_specs=[pl.BlockSpec((tb, d), row), pl.BlockSpec((LOCAL_ROWS, d + W_LANES), row),
                   pl.BlockSpec((None, N_EXPERTS, 1), blk3), pl.BlockSpec((None, 1, TOP_K * tb), blk3)],
        out_shape=[jax.ShapeDtypeStruct((n, d), F32), jax.ShapeDtypeStruct((nb * LOCAL_ROWS, d + W_LANES), BF16),
                   jax.ShapeDtypeStruct((nb, N_EXPERTS, 1), I32), jax.ShapeDtypeStruct((nb, 1, TOP_K * tb), I32)],
        compiler_params=_params("parallel", "parallel"),
        name="attn",
    )(x1, gxa, wq_bf, k_bf, v_bf, wo_bf, gffn, w_route_t, b_route, upper)


def _plan_kernel(cnt_ref, te_ref, src_ref, dst_ref, nact_ref):
    runs = cnt_ref.shape[0]
    lg = LOCAL_ROWS // GROUP
    tg = MOE_TILE // GROUP
    sh = N_EXPERTS.bit_length() - 1
    emask = N_EXPERTS - 1
    zero_group = lg - 1

    length = cnt_ref[...].astype(F32) * (1.0 / GROUP)
    len_bf = jnp.broadcast_to(length, (runs, 128)).astype(BF16)
    ri = lax.broadcasted_iota(I32, (runs, runs), 0)
    ci = lax.broadcasted_iota(I32, (runs, runs), 1)
    r_e, c_e = ri & emask, ci & emask
    r_b, c_b = lax.shift_right_logical(ri, sh), lax.shift_right_logical(ci, sh)
    same_expert_earlier = jnp.where(r_e == c_e, jnp.where(c_b < r_b, 1.0, 0.0), 0.0).astype(BF16)
    same_block_earlier = jnp.where(r_b == c_b, jnp.where(c_e < r_e, 1.0, 0.0), 0.0).astype(BF16)
    before = _dot(same_expert_earlier, len_bf)[:, 0:1]
    local = _dot(same_block_earlier, len_bf)[:, 0:1]

    of_expert = (lax.broadcasted_iota(I32, (N_EXPERTS, runs), 1) & emask) == lax.broadcasted_iota(
        I32, (N_EXPERTS, runs), 0)
    total = _dot(jnp.where(of_expert, 1.0, 0.0).astype(BF16), len_bf)[:, 0:1]
    tiles = jnp.floor((total + (tg - 1)) * (1.0 / tg))
    tile0 = _excl_cumsum_rows(tiles)
    to_run = (lax.broadcasted_iota(I32, (runs, N_EXPERTS), 0) & emask) == lax.broadcasted_iota(
        I32, (runs, N_EXPERTS), 1)
    to_run_bf = jnp.where(to_run, 1.0, 0.0).astype(BF16)
    base_hi, base_lo = _split_bf16(jnp.broadcast_to(tile0 * tg, (N_EXPERTS, 128)))
    g_start = (_dot(to_run_bf, base_hi) + _dot(to_run_bf, base_lo))[:, 0:1] + before
    block = lax.shift_right_logical(lax.broadcasted_iota(I32, (runs, 1), 0), sh).astype(F32)
    l_start = block * lg + local

    def cover(out_ref, start, offset, default):
        stop = start + length
        shift = offset - default
        n_out = out_ref.shape[1]
        chunk = next(c for c in PLAN_CHUNKS if n_out % c == 0)
        for c in range(n_out // chunk):
            j = (lax.broadcasted_iota(I32, (runs, chunk), 1) + c * chunk).astype(F32)
            hit = jnp.where(start <= j, jnp.where(j < stop, j + shift, 0.0), 0.0)
            out = jnp.sum(hit, axis=0, keepdims=True) + default
            out_ref[:, c * chunk:(c + 1) * chunk] = out.astype(I32)

    cover(src_ref, g_start, l_start - g_start, float(zero_group))
    cover(dst_ref, l_start, g_start - l_start, 0.0)
    t = lax.broadcasted_iota(I32, (N_EXPERTS, te_ref.shape[1]), 1).astype(F32)
    te_ref[...] = jnp.sum(jnp.where(t >= tile0 + tiles, 1.0, 0.0), axis=0, keepdims=True).astype(I32)
    nact_ref[...] = jnp.sum(jnp.broadcast_to(tiles, (N_EXPERTS, 128)), axis=0, keepdims=True).astype(I32)


def _plan(cnt, n_tiles):
    runs = cnt.shape[0]
    n_src = n_tiles * (MOE_TILE // GROUP)
    n_dst = runs // N_EXPERTS * (LOCAL_ROWS // GROUP)
    n_te = -(-n_tiles // 128) * 128
    te, src, dst, nact = pl.pallas_call(
        _plan_kernel,
        out_shape=[jax.ShapeDtypeStruct((1, n_te), I32), jax.ShapeDtypeStruct((1, n_src), I32),
                   jax.ShapeDtypeStruct((1, n_dst), I32), jax.ShapeDtypeStruct((1, 128), I32)],
        compiler_params=_params(),
        name="plan",
    )(cnt)
    return te.reshape(-1), src.reshape(-1), dst.reshape(-1), nact[0, :1]


def _group_copy(src_hbm, src_group, dst_buf, slot, index, sem):
    start = src_group * GROUP
    rows = pl.ds(start if isinstance(start, int) else pl.multiple_of(start, GROUP), GROUP)
    return pltpu.make_async_copy(src_hbm.at[rows], dst_buf.at[slot, pl.ds(index * GROUP, GROUP)], sem.at[slot])


def _experts_kernel(te_ref, src_ref, nact_ref, xb_ref, wg_hbm, wu_hbm, wd_hbm, ys_ref,
                    xbuf, sem, obuf, osem, wg_st, wu_st, wd_st, wsem, wg_bf, wu_bf, wd_bf):
    tg = MOE_TILE // GROUP
    tm, d = obuf.shape[1], obuf.shape[2]
    nact = nact_ref[0]
    n_tiles = ys_ref.shape[0] // tm

    def start_gather(tile):
        for i in range(tg):
            _group_copy(xb_ref, src_ref[tile * tg + i], xbuf, lax.rem(tile, 2), i, sem).start(priority=i % 2)

    def weight_copies(expert):
        pairs = ((wg_hbm, wg_st), (wu_hbm, wu_st), (wd_hbm, wd_st))
        return [pltpu.make_async_copy(w.at[expert], st, wsem.at[i]) for i, (w, st) in enumerate(pairs)]

    def out_copy(tile, slot):
        rows = pl.ds(pl.multiple_of(tile * tm, tm), tm)
        return pltpu.make_async_copy(obuf.at[slot], ys_ref.at[rows], osem.at[slot])

    @pl.when(nact > 0)
    def _():
        start_gather(0)
        for c in weight_copies(te_ref[0]):
            c.start()

    def tile_step(t, carry):
        e = te_ref[t]
        slot = lax.rem(t, 2)

        @pl.when(t + 1 < nact)
        def _():
            start_gather(t + 1)

        @pl.when(jnp.logical_or(t == 0, e != te_ref[jnp.maximum(t - 1, 0)]))
        def _():
            for c in weight_copies(e):
                c.wait()
            wg_bf[...] = wg_st[...].astype(BF16)
            wu_bf[...] = wu_st[...].astype(BF16)
            wd_bf[...] = wd_st[...].astype(BF16)
            nxt = lax.while_loop(lambda j: jnp.logical_and(j < nact, te_ref[jnp.minimum(j, n_tiles - 1)] == e),
                                 lambda j: j + 1, t + 1)

            @pl.when(nxt < nact)
            def _():
                for c in weight_copies(te_ref[jnp.minimum(nxt, n_tiles - 1)]):
                    c.start()

        for i in range(tg):
            _group_copy(xb_ref, 0, xbuf, slot, i, sem).wait()

        @pl.when(t >= 2)
        def _():
            out_copy(t - 2, slot).wait()

        x = xbuf[slot, :, 0:d]
        hid = jax.nn.silu(_dot(x, wg_bf[...])) * _dot(x, wu_bf[...])
        weight = (xbuf[slot, :, d:d + 1].astype(F32)
                  + xbuf[slot, :, d + W_LANES // 2:d + W_LANES // 2 + 1].astype(F32))
        obuf[slot] = (_dot(hid.astype(BF16), wd_bf[...]) * weight).astype(BF16)
        out_copy(t, slot).start()
        return carry

    lax.fori_loop(0, nact, tile_step, 0)
    for back in (2, 1):
        @pl.when(nact >= back)
        def _():
            out_copy(nact - back, lax.rem(nact - back, 2)).wait()

    obuf[0] = jnp.zeros((tm, d), BF16)

    def zero_start(t, carry):
        out_copy(t, 0).start()
        return carry

    def zero_wait(t, carry):
        out_copy(t, 0).wait()
        return carry

    lax.fori_loop(nact, n_tiles, zero_start, 0)
    lax.fori_loop(nact, n_tiles, zero_wait, 0)


def _experts(te, src, nact, xb, w_gate, w_up, w_down, n_tiles):
    dw = xb.shape[1]
    d = dw - W_LANES
    ff = w_gate.shape[2]
    tm = MOE_TILE
    hbm = pl.BlockSpec(memory_space=pl.ANY)
    return pl.pallas_call(
        _experts_kernel,
        grid_spec=pltpu.PrefetchScalarGridSpec(
            num_scalar_prefetch=3,
            grid=(1,),
            in_specs=[hbm, hbm, hbm, hbm],
            out_specs=hbm,
            scratch_shapes=[pltpu.VMEM((2, tm, dw), BF16), pltpu.SemaphoreType.DMA((2,)),
                            pltpu.VMEM((2, tm, d), BF16), pltpu.SemaphoreType.DMA((2,)),
                            pltpu.VMEM((d, ff), F32), pltpu.VMEM((d, ff), F32), pltpu.VMEM((ff, d), F32),
                            pltpu.SemaphoreType.DMA((3,)),
                            pltpu.VMEM((d, ff), BF16), pltpu.VMEM((d, ff), BF16), pltpu.VMEM((ff, d), BF16)],
        ),
        out_shape=jax.ShapeDtypeStruct((n_tiles * tm, d), BF16),
        compiler_params=_params("arbitrary"),
        name="experts",
    )(te, src, nact, xb, w_gate, w_up, w_down)


def _combine_kernel(dst_ref, lpos_ref, x2_ref, g_ref, ys_ref, out_ref, ybuf, sem):
    b = pl.program_id(0)
    tb = x2_ref.shape[0]
    lg = LOCAL_ROWS // GROUP
    slot = lax.rem(b, 2)

    def start_gather(blk):
        for i in range(lg):
            _group_copy(ys_ref, dst_ref[blk * lg + i], ybuf, lax.rem(blk, 2), i, sem).start(priority=i % 2)

    @pl.when(b == 0)
    def _():
        start_gather(b)

    @pl.when(b + 1 < pl.num_programs(0))
    def _():
        start_gather(b + 1)

    for i in range(lg):
        _group_copy(ys_ref, 0, ybuf, slot, i, sem).wait()
    lpos = lpos_ref[...]
    y = jnp.zeros(x2_ref.shape, F32)
    for c in range(LOCAL_ROWS // SORT_CHUNK):
        m0, m1 = _slot_masks(lpos, c, tb)
        p = jnp.where(m0, 1.0, jnp.where(m1, 1.0, 0.0)).astype(BF16)
        y = y + _dot_tn(p, ybuf[slot, c * SORT_CHUNK:(c + 1) * SORT_CHUNK, :])
    out_ref[...] = _rms(x2_ref[...] + y, g_ref[...])


def _combine(dst, lpos, x2, g_final, ys):
    n, d = x2.shape
    tb = TOKEN_BLOCK
    return pl.pallas_call(
        _combine_kernel,
        grid_spec=pltpu.PrefetchScalarGridSpec(
            num_scalar_prefetch=1,
            grid=(n // tb,),
            in_specs=[pl.BlockSpec((None, 1, TOP_K * tb), lambda i, dst: (i, 0, 0)),
                      pl.BlockSpec((tb, d), lambda i, dst: (i, 0)), pl.BlockSpec((1, d), lambda i, dst: (0, 0)),
                      pl.BlockSpec(memory_space=pl.ANY)],
            out_specs=pl.BlockSpec((tb, d), lambda i, dst: (i, 0)),
            scratch_shapes=[pltpu.VMEM((2, LOCAL_ROWS, d), BF16), pltpu.SemaphoreType.DMA((2,))],
        ),
        out_shape=jax.ShapeDtypeStruct((n, d), F32),
        compiler_params=_params("arbitrary"),
        name="combine",
    )(dst, lpos, x2, g_final, ys)


def kernel(x, mem, norm_mix_g, w_in, conv_w, conv_b, conv_ln_g, conv_ln_b, conv_w_out, hgrn_lb_logits, hgrn_onorm_g, hgrn_w_out, w_mix_out, norm_xa_g, norm_mem_g, xa_w_q, xa_w_k, xa_w_v, xa_w_o, norm_ffn_g, router_group_w, router_group_b, router_expert_w, router_expert_b, moe_w_gate, moe_w_up, moe_w_down, final_norm_g):
    batch, seq, d = x.shape
    n = batch * seq
    assert w_in.shape[0] == 1, "the final RMSNorm is fused into the single layer's combine step"
    cw = conv_w.shape[2]
    kw = hgrn_w_out.shape[1]
    assert kw == HGRN_HEADS * HGRN_DIM and conv_w.shape[1] == CONV_K
    assert seq % TOKEN_BLOCK == 0 and TOKEN_BLOCK % CHUNK == 0 and TOKEN_BLOCK % CONV_ROWS == 0
    assert moe_w_gate.shape[1] == N_EXPERTS and router_group_w.shape[2] == N_GROUPS and TOP_K == 2
    na = TOP_K * TOKEN_BLOCK
    assert LOCAL_ROWS % SORT_CHUNK == 0 and LOCAL_ROWS >= na + N_EXPERTS * (GROUP - 1) + GROUP

    n_tiles = -(-(n // TOKEN_BLOCK) * (na + N_EXPERTS * (GROUP - 1)) // MOE_TILE) + N_EXPERTS
    n_tiles = -(-n_tiles // 16) * 16
    vec = lambda p: p.reshape(1, -1)
    l = 0

    x2d = x.reshape(n, d)
    u, q, fr, iv, og, gates = _inproj(x2d, vec(norm_mix_g[l]), w_in[l].astype(BF16), cw, kw)
    yc = _conv(u, conv_w[l], vec(conv_b[l]), vec(conv_ln_g[l]), vec(conv_ln_b[l]),
               conv_w_out[l].astype(BF16), gates, batch)
    x1 = _hgrn(q, fr, iv, og, gates, yc, x2d, hgrn_lb_logits[l:l + 2], vec(hgrn_onorm_g[l]),
               hgrn_w_out[l].astype(BF16), w_mix_out[l].astype(BF16), batch)
    k_bf, v_bf = _memkv(mem.reshape(-1, d), vec(norm_mem_g[l]), xa_w_k[l].astype(BF16), xa_w_v[l].astype(BF16), batch)
    pad = ROUTE_ROWS - N_GROUPS - N_EXPERTS
    w_route_t = jnp.pad(jnp.concatenate([router_group_w[l], router_expert_w[l]], axis=1).T, ((0, pad), (0, 0)))
    b_route = jnp.pad(jnp.concatenate([router_group_b[l], router_expert_b[l]]), (0, pad)).reshape(ROUTE_ROWS, 1)
    upper = (lax.broadcasted_iota(I32, (na, na), 0) < lax.broadcasted_iota(I32, (na, na), 1)).astype(BF16)
    x2, xb, cnt, lpos = _attn(x1, vec(norm_xa_g[l]), xa_w_q[l].astype(BF16), k_bf, v_bf, xa_w_o[l].astype(BF16),
                              vec(norm_ffn_g[l]), w_route_t, b_route, upper, batch)
    te, src, dst, nact = _plan(cnt.reshape(-1, 1), n_tiles)
    ys = _experts(te, src, nact, xb, moe_w_gate[l], moe_w_up[l], moe_w_down[l], n_tiles)
    out = _combine(dst, lpos, x2, vec(final_norm_g), ys)
    return out.reshape(batch, seq, d)
```

```python
import functools

import jax
import jax.numpy as jnp
from jax import lax
from jax.experimental import pallas as pl
from jax.experimental.pallas import tpu as pltpu

F32 = jnp.float32
BF16 = jnp.bfloat16
I32 = jnp.int32

EPS = 1e-6
CONV_K = 31
CONV_HALO = 32
CONV_ROWS = 64
HGRN_HEADS = 4
HGRN_DIM = 128
CHUNK = 64
SUB = 16
XA_HEADS = 4
N_GROUPS = 4
EXPERTS_PER_GROUP = 8
N_EXPERTS = N_GROUPS * EXPERTS_PER_GROUP
TOP_K = 2
ROUTE_ROWS = 40
TOKEN_BLOCK = 512
MOE_TILE = 256
GROUP = 16
LOCAL_ROWS = 1536
SORT_CHUNK = 256
W_LANES = 128
PLAN_CHUNKS = (512, 256, 128)
VMEM_LIMIT_BYTES = 48 * 1024 * 1024


def _rms(x, g):
    return x * lax.rsqrt(jnp.mean(x * x, axis=-1, keepdims=True) + EPS) * g


def _dot(a, b):
    return jnp.dot(a, b, preferred_element_type=F32)


def _dot_nt(a, b):
    return lax.dot_general(a, b, (((1,), (1,)), ((), ())), preferred_element_type=F32)


def _dot_tn(a, b):
    return lax.dot_general(a, b, (((0,), (0,)), ((), ())), preferred_element_type=F32)


def _split_bf16(x):
    hi = x.astype(BF16)
    lo = (x - hi.astype(F32)).astype(BF16)
    return hi, lo


def _params(*sem):
    return pltpu.CompilerParams(dimension_semantics=sem, vmem_limit_bytes=VMEM_LIMIT_BYTES)


def _inproj_kernel(x_ref, g_ref, w_ref, u_ref, q_ref, fr_ref, iv_ref, og_ref, gates_ref, *, cw, kw):
    hb = _rms(x_ref[...], g_ref[...]).astype(BF16)

    def proj(lo, width):
        return _dot(hb, w_ref[:, lo:lo + width])

    u_ref[...] = (proj(0, cw) * jax.nn.sigmoid(proj(cw, cw))).astype(BF16)
    base = 2 * cw
    q_ref[...] = jax.nn.silu(proj(base, kw)).astype(BF16)
    fr_ref[...] = proj(base + kw, kw)
    iv_ref[...] = proj(base + 2 * kw, kw).astype(BF16)
    og_ref[...] = jax.nn.silu(proj(base + 3 * kw, kw)).astype(BF16)
    base += 4 * kw
    for c in range(gates_ref.shape[1] // kw):
        gates_ref[:, c * kw:(c + 1) * kw] = jax.nn.sigmoid(proj(base + c * kw, kw)).astype(BF16)


def _inproj(x2d, g, w_bf, cw, kw):
    n, d = x2d.shape
    tb = TOKEN_BLOCK
    row = lambda i: (i, 0)
    fixed = lambda i: (0, 0)
    outs = [((n, cw), BF16), ((n, kw), BF16), ((n, kw), F32), ((n, kw), BF16), ((n, kw), BF16), ((n, 2 * d), BF16)]
    return pl.pallas_call(
        functools.partial(_inproj_kernel, cw=cw, kw=kw),
        grid=(n // tb,),
        in_specs=[pl.BlockSpec((tb, d), row), pl.BlockSpec((1, d), fixed), pl.BlockSpec(w_bf.shape, fixed)],
        out_specs=[pl.BlockSpec((tb, s[1]), row) for s, _ in outs],
        out_shape=[jax.ShapeDtypeStruct(s, t) for s, t in outs],
        compiler_params=_params("parallel"),
        name="inproj",
    )(x2d, g, w_bf)


def _conv_kernel(u_ref, cw_ref, cb_ref, lg_ref, lb_ref, wo_ref, gate_ref, y_ref, ext_ref, halo_ref, perm_ref, act_ref):
    tb, c = u_ref.shape
    nt = tb // 8
    slabs = c // 128
    hr = CONV_HALO * 8
    lanes = [slice(l * 128, (l + 1) * 128) for l in range(slabs)]

    @pl.when(pl.program_id(1) == 0)
    def _():
        halo_ref[...] = jnp.zeros_like(halo_ref)

    un = u_ref[...].astype(F32)
    per = nt // 8
    for j in range(nt):
        start = hr + (j % per) * 64 + j // per
        for l in range(slabs):
            ext_ref[l, pl.ds(start, 8, stride=8), :] = un[8 * j:8 * j + 8, lanes[l]]
    first = lax.broadcasted_iota(I32, (hr, 128), 0) % 8 == 0
    for l in range(slabs):
        cur = ext_ref[l, nt * 8:nt * 8 + hr, :]
        ext_ref[l, 0:hr, :] = jnp.where(first, pltpu.roll(halo_ref[l], hr - 7, axis=0), pltpu.roll(cur, 1, axis=0))
        halo_ref[l] = cur

    for r in range(tb // CONV_ROWS):
        accs = []
        for l in range(slabs):
            acc = jnp.broadcast_to(cb_ref[:, lanes[l]], (CONV_ROWS, 128))
            for dt in range(CONV_K):
                off = hr + r * CONV_ROWS - dt * 8
                acc = acc + cw_ref[CONV_K - 1 - dt:CONV_K - dt, lanes[l]] * ext_ref[l, off:off + CONV_ROWS, :]
            accs.append(acc)
        mu = functools.reduce(jnp.add, [jnp.sum(a, axis=-1, keepdims=True) for a in accs]) * (1.0 / c)
        cens = [a - mu for a in accs]
        var = functools.reduce(jnp.add, [jnp.sum(a * a, axis=-1, keepdims=True) for a in cens]) * (1.0 / c)
        inv = lax.rsqrt(var + EPS)
        for l in range(slabs):
            ln = cens[l] * inv * lg_ref[:, lanes[l]] + lb_ref[:, lanes[l]]
            perm_ref[l, r * CONV_ROWS:(r + 1) * CONV_ROWS, :] = jax.nn.silu(ln)
    for j in range(nt):
        start = (j % per) * 64 + j // per
        for l in range(slabs):
            act_ref[8 * j:8 * j + 8, lanes[l]] = perm_ref[l, pl.ds(start, 8, stride=8), :]
    y_ref[...] = (gate_ref[...].astype(F32) * _dot(act_ref[...].astype(BF16), wo_ref[...])).astype(BF16)


def _conv(u, conv_w, conv_b, ln_g, ln_b, w_out_bf, gates, batch):
    n, c = u.shape
    d = w_out_bf.shape[1]
    tb = TOKEN_BLOCK
    nsb = n // batch // tb
    row = lambda b, s: (b * nsb + s, 0)
    fixed = lambda b, s: (0, 0)
    return pl.pallas_call(
        _conv_kernel,
        grid=(batch, nsb),
        in_specs=[pl.BlockSpec((tb, c), row), pl.BlockSpec(conv_w.shape, fixed), pl.BlockSpec((1, c), fixed),
                  pl.BlockSpec((1, c), fixed), pl.BlockSpec((1, c), fixed), pl.BlockSpec((c, d), fixed),
                  pl.BlockSpec((tb, d), row)],
        out_specs=pl.BlockSpec((tb, d), row),
        out_shape=jax.ShapeDtypeStruct((n, d), BF16),
        scratch_shapes=[pltpu.VMEM((c // 128, tb + CONV_HALO * 8, 128), F32),
                        pltpu.VMEM((c // 128, CONV_HALO * 8, 128), F32),
                        pltpu.VMEM((c // 128, tb, 128), F32), pltpu.VMEM((tb, c), F32)],
        compiler_params=_params("arbitrary", "arbitrary"),
        name="conv",
    )(u, conv_w, conv_b, ln_g, ln_b, w_out_bf, gates)


def _hgrn_block(q_ref, fr_ref, iv_ref, lb, st_ref, tri):
    tb = q_ref.shape[0]
    chunks = [slice(c * CHUNK, (c + 1) * CHUNK) for c in range(tb // CHUNK)]
    heads = [slice(h * HGRN_DIM, (h + 1) * HGRN_DIM) for h in range(HGRN_HEADS)]

    qs, vs, kks, cums = [], [], [], []
    for rows in chunks:
        f = lb + (1.0 - lb) * jax.nn.sigmoid(fr_ref[rows, :])
        lf_hi, lf_lo = _split_bf16(jnp.log(f))
        cums.append(_dot(tri, lf_hi) + _dot(tri, lf_lo))
        kks.append(1.0 - f)
        qs.append(q_ref[rows, :].astype(F32))
        vs.append(iv_ref[rows, :])

    qes, kds, decays, blocks = [], [], [], []
    for q, kk, cum in zip(qs, kks, cums):
        last = cum[CHUNK - 1:CHUNK, :]
        qes.append((q * jnp.exp(cum)).astype(BF16))
        kds.append((kk * jnp.exp(last - cum)).astype(BF16))
        decays.append(jnp.exp(last))
        sub = []
        for i in range(CHUNK // SUB):
            rs, ne = i * SUB, (i + 1) * SUB
            ref = cum[rs + SUB // 2 - 1:rs + SUB // 2, :]
            qt = (q[rs:ne] * jnp.exp(cum[rs:ne] - ref)).astype(BF16)
            kt = (kk[0:ne] * jnp.exp(ref - cum[0:ne])).astype(BF16)
            sub.append((qt, kt))
        blocks.append(sub)

    updates = [[_dot_tn(v[:, hs], kd[:, hs]) for hs in heads] for v, kd in zip(vs, kds)]
    scores = [[[_dot_nt(qt[:, hs], kt[:, hs]) for qt, kt in sub] for hs in heads] for sub in blocks]

    states = [st_ref[h] for h in range(HGRN_HEADS)]
    inter = []
    for qe, decay, upd in zip(qes, decays, updates):
        inter.append([_dot_nt(qe[:, hs], st.astype(BF16)) for hs, st in zip(heads, states)])
        states = [st * decay[:, hs] + u for st, hs, u in zip(states, heads, upd)]
    for h in range(HGRN_HEADS):
        st_ref[h] = states[h]

    outs = []
    for v, sc, o_inter in zip(vs, scores, inter):
        per_head = []
        for h, hs in enumerate(heads):
            parts = []
            for i, a in enumerate(sc[h]):
                rs, ne = i * SUB, (i + 1) * SUB
                trow = lax.broadcasted_iota(I32, (SUB, ne), 0) + rs
                scol = lax.broadcasted_iota(I32, (SUB, ne), 1)
                a = jnp.where(scol <= trow, a, 0.0).astype(BF16)
                parts.append(_dot(a, v[0:ne, hs]))
            per_head.append(o_inter[h] + jnp.concatenate(parts, axis=0))
        outs.append(jnp.concatenate(per_head, axis=1))
    return outs


def _hgrn_kernel(q_ref, fr_ref, iv_ref, og_ref, gate_ref, yc_ref, x_ref, lbl_ref, on_ref, wo_ref, wm_ref,
                 x1_ref, st_ref, ob_ref):
    tb = q_ref.shape[0]

    @pl.when(pl.program_id(1) == 0)
    def _():
        st_ref[...] = jnp.zeros_like(st_ref)

    l0, l1 = lbl_ref[0:1, :], lbl_ref[1:2, :]
    m = jnp.maximum(l0, l1)
    e0, e1 = jnp.exp(l0 - m), jnp.exp(l1 - m)
    lb_all = e0 / (e0 + e1)
    trow = lax.broadcasted_iota(I32, (CHUNK, CHUNK), 0)
    tcol = lax.broadcasted_iota(I32, (CHUNK, CHUNK), 1)
    tri = jnp.where(tcol <= trow, 1.0, 0.0).astype(BF16)

    for c, o in enumerate(_hgrn_block(q_ref, fr_ref, iv_ref, lb_all, st_ref, tri)):
        rows = slice(c * CHUNK, (c + 1) * CHUNK)
        og = og_ref[rows, :].astype(F32)
        for h in range(HGRN_HEADS):
            hs = slice(h * HGRN_DIM, (h + 1) * HGRN_DIM)
            ob_ref[rows, hs] = (_rms(o[:, hs], on_ref[...]) * og[:, hs]).astype(BF16)
    y_rec = _dot(ob_ref[...], wo_ref[...])
    merged = yc_ref[...].astype(F32) + gate_ref[...].astype(F32) * y_rec
    x1_ref[...] = x_ref[...] + _dot(merged.astype(BF16), wm_ref[...])


def _hgrn(q, fr, iv, og, gates, yc, x2d, lb_logits, onorm_g, w_o_bf, w_mix_bf, batch):
    n, kw = q.shape
    d = x2d.shape[1]
    tb = TOKEN_BLOCK
    nsb = n // batch // tb
    row = lambda b, s: (b * nsb + s, 0)
    fixed = lambda b, s: (0, 0)
    return pl.pallas_call(
        _hgrn_kernel,
        grid=(batch, nsb),
        in_specs=[pl.BlockSpec((tb, kw), row), pl.BlockSpec((tb, kw), row), pl.BlockSpec((tb, kw), row),
                  pl.BlockSpec((tb, kw), row),
                  pl.BlockSpec((tb, d), lambda b, s: (b * nsb + s, 1)),
                  pl.BlockSpec((tb, d), row), pl.BlockSpec((tb, d), row),
                  pl.BlockSpec(lb_logits.shape, fixed), pl.BlockSpec((1, HGRN_DIM), fixed),
                  pl.BlockSpec(w_o_bf.shape, fixed), pl.BlockSpec(w_mix_bf.shape, fixed)],
        out_specs=pl.BlockSpec((tb, d), row),
        out_shape=jax.ShapeDtypeStruct((n, d), F32),
        scratch_shapes=[pltpu.VMEM((HGRN_HEADS, HGRN_DIM, HGRN_DIM), F32), pltpu.VMEM((tb, kw), BF16)],
        compiler_params=_params("arbitrary", "arbitrary"),
        name="hgrn",
    )(q, fr, iv, og, gates, yc, x2d, lb_logits, onorm_g, w_o_bf, w_mix_bf)


def _memkv_kernel(mem_ref, g_ref, wk_ref, wv_ref, k_ref, v_ref):
    mb = _rms(mem_ref[...], g_ref[...]).astype(BF16)
    k_ref[...] = _dot(mb, wk_ref[...]).astype(BF16)
    v_ref[...] = _dot(mb, wv_ref[...]).astype(BF16)


def _memkv(mem2d, g, wk_bf, wv_bf, batch):
    n, d = mem2d.shape
    m = n // batch
    row = lambda b: (b, 0)
    fixed = lambda b: (0, 0)
    return pl.pallas_call(
        _memkv_kernel,
        grid=(batch,),
        in_specs=[pl.BlockSpec((m, d), row), pl.BlockSpec((1, d), fixed), pl.BlockSpec((d, d), fixed),
                  pl.BlockSpec((d, d), fixed)],
        out_specs=[pl.BlockSpec((m, d), row)] * 2,
        out_shape=[jax.ShapeDtypeStruct((n, d), BF16)] * 2,
        compiler_params=_params("parallel"),
        name="memkv",
    )(mem2d, g, wk_bf, wv_bf)


def _route(lt):
    def row(r):
        return lt[r:r + 1, :]

    gl = [row(g) for g in range(N_GROUPS)]
    gmax = functools.reduce(jnp.maximum, gl)
    g_p = 1.0 / functools.reduce(jnp.add, [jnp.exp(l - gmax) for l in gl])
    gidx = jnp.full(gmax.shape, N_GROUPS - 1, I32)
    for g in range(N_GROUPS - 2, -1, -1):
        gidx = jnp.where(gl[g] == gmax, g, gidx)

    el = []
    for j in range(EXPERTS_PER_GROUP):
        v = row(N_GROUPS + (N_GROUPS - 1) * EXPERTS_PER_GROUP + j)
        for g in range(N_GROUPS - 2, -1, -1):
            v = jnp.where(gidx == g, row(N_GROUPS + g * EXPERTS_PER_GROUP + j), v)
        el.append(v)

    def argmax(vals):
        mx = functools.reduce(jnp.maximum, vals)
        idx = jnp.full(mx.shape, EXPERTS_PER_GROUP - 1, I32)
        for j in range(EXPERTS_PER_GROUP - 2, -1, -1):
            idx = jnp.where(vals[j] == mx, j, idx)
        return mx, idx

    m1, i1 = argmax(el)
    m2, i2 = argmax([jnp.where(i1 == j, -jnp.inf, el[j]) for j in range(EXPERTS_PER_GROUP)])
    r = jnp.exp(m2 - m1)
    w1 = g_p / (1.0 + r)
    w2 = g_p * r / (1.0 + r)
    base = gidx * EXPERTS_PER_GROUP
    return jnp.concatenate([base + i1, base + i2], axis=0), jnp.concatenate([w1, w2], axis=0)


def _excl_cumsum_rows(col):
    r = col.shape[0]
    lower = lax.broadcasted_iota(I32, (r, r), 1) < lax.broadcasted_iota(I32, (r, r), 0)
    lower_bf = jnp.where(lower, 1.0, 0.0).astype(BF16)
    hi, lo = _split_bf16(jnp.broadcast_to(col, (r, 128)))
    return (_dot(lower_bf, hi) + _dot(lower_bf, lo))[:, 0:1]


def _slot_masks(lpos, chunk, tb):
    slot = lax.broadcasted_iota(I32, (SORT_CHUNK, tb), 0) + chunk * SORT_CHUNK
    return slot == lpos[:, 0:tb], slot == lpos[:, tb:2 * tb]


def _local_sort(eid, wts, h2, upper_ref, xb_ref, cnt_ref, lpos_ref):
    tb, d = h2.shape
    na = TOP_K * tb
    e_all = jnp.concatenate([eid[k:k + 1] for k in range(TOP_K)], axis=1)
    onehot = lax.broadcasted_iota(I32, (N_EXPERTS, na), 0) == e_all
    ones = jnp.where(onehot, 1.0, 0.0)
    cnt = jnp.sum(ones, axis=1, keepdims=True)
    cnt_pad = jnp.floor((cnt + (GROUP - 1)) * (1.0 / GROUP)) * GROUP
    earlier = _dot(ones.astype(BF16), upper_ref[...])
    start = _excl_cumsum_rows(cnt_pad)
    lpos = jnp.sum(jnp.where(onehot, start + earlier, 0.0), axis=0, keepdims=True).astype(I32)
    hb = h2.astype(BF16)
    for c in range(LOCAL_ROWS // SORT_CHUNK):
        m0, m1 = _slot_masks(lpos, c, tb)
        p = jnp.where(m0, 1.0, jnp.where(m1, 1.0, 0.0)).astype(BF16)
        pw = jnp.where(m0, wts[0:1], jnp.where(m1, wts[1:2], 0.0))
        rows = slice(c * SORT_CHUNK, (c + 1) * SORT_CHUNK)
        xb_ref[rows, 0:d] = _dot(p, hb).astype(BF16)
        w_row = jnp.broadcast_to(jnp.sum(pw, axis=1, keepdims=True), (SORT_CHUNK, W_LANES))
        w_hi = w_row.astype(BF16).astype(F32)
        low_half = lax.broadcasted_iota(I32, (SORT_CHUNK, W_LANES), 1) < W_LANES // 2
        xb_ref[rows, d:d + W_LANES] = jnp.where(low_half, w_hi, w_row - w_hi).astype(BF16)
    cnt_ref[...] = cnt_pad.astype(I32)
    lpos_ref[...] = lpos


def _attn_kernel(x1_ref, gxa_ref, wq_ref, k_ref, v_ref, wo_ref, gffn_ref, wr_ref, br_ref, upper_ref,
                 x2_ref, xb_ref, cnt_ref, lpos_ref):
    x1 = x1_ref[...]
    d = x1.shape[1]
    hd = d // XA_HEADS
    q = _dot(_rms(x1, gxa_ref[...]).astype(BF16), wq_ref[...]).astype(BF16)
    heads = []
    for h in range(XA_HEADS):
        hs = slice(h * hd, (h + 1) * hd)
        sc = _dot_nt(q[:, hs], k_ref[:, hs]) * (hd ** -0.5)
        p = jnp.exp(sc - jnp.max(sc, axis=-1, keepdims=True))
        p = p / jnp.sum(p, axis=-1, keepdims=True)
        heads.append(_dot(p.astype(BF16), v_ref[:, hs]).astype(BF16))
    x2 = x1 + _dot(jnp.concatenate(heads, axis=1), wo_ref[...])
    x2_ref[...] = x2
    h2 = _rms(x2, gffn_ref[...])
    h_hi, h_lo = _split_bf16(h2)
    w_hi, w_lo = _split_bf16(wr_ref[...])
    lt = _dot_nt(w_hi, h_hi) + (_dot_nt(w_hi, h_lo) + _dot_nt(w_lo, h_hi)) + br_ref[...]
    eid, wts = _route(lt)
    _local_sort(eid, wts, h2, upper_ref, xb_ref, cnt_ref, lpos_ref)


def _attn(x1, gxa, wq_bf, k_bf, v_bf, wo_bf, gffn, w_route_t, b_route, upper, batch):
    n, d = x1.shape
    m = k_bf.shape[0] // batch
    tb = TOKEN_BLOCK
    nsb = n // batch // tb
    nb = n // tb
    row = lambda b, s: (b * nsb + s, 0)
    blk3 = lambda b, s: (b * nsb + s, 0, 0)
    fixed = lambda b, s: (0, 0)
    mem = lambda b, s: (b, 0)
    return pl.pallas_call(
        _attn_kernel,
        grid=(batch, nsb),
        in_specs=[pl.BlockSpec((tb, d), row), pl.BlockSpec((1, d), fixed), pl.BlockSpec((d, d), fixed),
                  pl.BlockSpec((m, d), mem), pl.BlockSpec((m, d), mem), pl.BlockSpec((d, d), fixed),
                  pl.BlockSpec((1, d), fixed), pl.BlockSpec((ROUTE_ROWS, d), fixed),
                  pl.BlockSpec((ROUTE_ROWS, 1), fixed), pl.BlockSpec(upper.shape, fixed)],
        out_specs=[pl.BlockSpec((tb, d), row), pl.BlockSpec((LOCAL_ROWS, d + W_LANES), row),
                   pl.BlockSpec((None, N_EXPERTS, 1), blk3), pl.BlockSpec((None, 1, TOP_K * tb), blk3)],
        out_shape=[jax.ShapeDtypeStruct((n, d), F32), jax.ShapeDtypeStruct((nb * LOCAL_ROWS, d + W_LANES), BF16),
                   jax.ShapeDtypeStruct((nb, N_EXPERTS, 1), I32), jax.ShapeDtypeStruct((nb, 1, TOP_K * tb), I32)],
        compiler_params=_params("parallel", "parallel"),
        name="attn",
    )(x1, gxa, wq_bf, k_bf, v_bf, wo_bf, gffn, w_route_t, b_route, upper)


def _plan_kernel(cnt_ref, te_ref, src_ref, dst_ref, nact_ref):
    runs = cnt_ref.shape[0]
    lg = LOCAL_ROWS // GROUP
    tg = MOE_TILE // GROUP
    sh = N_EXPERTS.bit_length() - 1
    emask = N_EXPERTS - 1
    zero_group = lg - 1

    length = cnt_ref[...].astype(F32) * (1.0 / GROUP)
    len_bf = jnp.broadcast_to(length, (runs, 128)).astype(BF16)
    ri = lax.broadcasted_iota(I32, (runs, runs), 0)
    ci = lax.broadcasted_iota(I32, (runs, runs), 1)
    r_e, c_e = ri & emask, ci & emask
    r_b, c_b = lax.shift_right_logical(ri, sh), lax.shift_right_logical(ci, sh)
    same_expert_earlier = jnp.where(r_e == c_e, jnp.where(c_b < r_b, 1.0, 0.0), 0.0).astype(BF16)
    same_block_earlier = jnp.where(r_b == c_b, jnp.where(c_e < r_e, 1.0, 0.0), 0.0).astype(BF16)
    before = _dot(same_expert_earlier, len_bf)[:, 0:1]
    local = _dot(same_block_earlier, len_bf)[:, 0:1]

    of_expert = (lax.broadcasted_iota(I32, (N_EXPERTS, runs), 1) & emask) == lax.broadcasted_iota(
        I32, (N_EXPERTS, runs), 0)
    total = _dot(jnp.where(of_expert, 1.0, 0.0).astype(BF16), len_bf)[:, 0:1]
    tiles = jnp.floor((total + (tg - 1)) * (1.0 / tg))
    tile0 = _excl_cumsum_rows(tiles)
    to_run = (lax.broadcasted_iota(I32, (runs, N_EXPERTS), 0) & emask) == lax.broadcasted_iota(
        I32, (runs, N_EXPERTS), 1)
    to_run_bf = jnp.where(to_run, 1.0, 0.0).astype(BF16)
    base_hi, base_lo = _split_bf16(jnp.broadcast_to(tile0 * tg, (N_EXPERTS, 128)))
    g_start = (_dot(to_run_bf, base_hi) + _dot(to_run_bf, base_lo))[:, 0:1] + before
    block = lax.shift_right_logical(lax.broadcasted_iota(I32, (runs, 1), 0), sh).astype(F32)
    l_start = block * lg + local

    def cover(out_ref, start, offset, default):
        stop = start + length
        shift = offset - default
        n_out = out_ref.shape[1]
        chunk = next(c for c in PLAN_CHUNKS if n_out % c == 0)
        for c in range(n_out // chunk):
            j = (lax.broadcasted_iota(I32, (runs, chunk), 1) + c * chunk).astype(F32)
            hit = jnp.where(start <= j, jnp.where(j < stop, j + shift, 0.0), 0.0)
            out = jnp.sum(hit, axis=0, keepdims=True) + default
            out_ref[:, c * chunk:(c + 1) * chunk] = out.astype(I32)

    cover(src_ref, g_start, l_start - g_start, float(zero_group))
    cover(dst_ref, l_start, g_start - l_start, 0.0)
    t = lax.broadcasted_iota(I32, (N_EXPERTS, te_ref.shape[1]), 1).astype(F32)
    te_ref[...] = jnp.sum(jnp.where(t >= tile0 + tiles, 1.0, 0.0), axis=0, keepdims=True).astype(I32)
    nact_ref[...] = jnp.sum(jnp.broadcast_to(tiles, (N_EXPERTS, 128)), axis=0, keepdims=True).astype(I32)


def _plan(cnt, n_tiles):
    runs = cnt.shape[0]
    n_src = n_tiles * (MOE_TILE // GROUP)
    n_dst = runs // N_EXPERTS * (LOCAL_ROWS // GROUP)
    n_te = -(-n_tiles // 128) * 128
    te, src, dst, nact = pl.pallas_call(
        _plan_kernel,
        out_shape=[jax.ShapeDtypeStruct((1, n_te), I32), jax.ShapeDtypeStruct((1, n_src), I32),
                   jax.ShapeDtypeStruct((1, n_dst), I32), jax.ShapeDtypeStruct((1, 128), I32)],
        compiler_params=_params(),
        name="plan",
    )(cnt)
    return te.reshape(-1), src.reshape(-1), dst.reshape(-1), nact[0, :1]


def _group_copy(src_hbm, src_group, dst_buf, slot, index, sem):
    start = src_group * GROUP
    rows = pl.ds(start if isinstance(start, int) else pl.multiple_of(start, GROUP), GROUP)
    return pltpu.make_async_copy(src_hbm.at[rows], dst_buf.at[slot, pl.ds(index * GROUP, GROUP)], sem.at[slot])


def _experts_kernel(te_ref, src_ref, nact_ref, xb_ref, wg_hbm, wu_hbm, wd_hbm, ys_ref,
                    xbuf, sem, obuf, osem, wg_st, wu_st, wd_st, wsem, wg_bf, wu_bf, wd_bf):
    tg = MOE_TILE // GROUP
    tm, d = obuf.shape[1], obuf.shape[2]
    nact = nact_ref[0]
    n_tiles = ys_ref.shape[0] // tm

    def start_gather(tile):
        for i in range(tg):
            _group_copy(xb_ref, src_ref[tile * tg + i], xbuf, lax.rem(tile, 2), i, sem).start()

    def weight_copies(expert):
        pairs = ((wg_hbm, wg_st), (wu_hbm, wu_st), (wd_hbm, wd_st))
        return [pltpu.make_async_copy(w.at[expert], st, wsem.at[i]) for i, (w, st) in enumerate(pairs)]

    def out_copy(tile, slot):
        rows = pl.ds(pl.multiple_of(tile * tm, tm), tm)
        return pltpu.make_async_copy(obuf.at[slot], ys_ref.at[rows], osem.at[slot])

    @pl.when(nact > 0)
    def _():
        start_gather(0)
        for c in weight_copies(te_ref[0]):
            c.start()

    def tile_step(t, carry):
        e = te_ref[t]
        slot = lax.rem(t, 2)

        @pl.when(t + 1 < nact)
        def _():
            start_gather(t + 1)

        @pl.when(jnp.logical_or(t == 0, e != te_ref[jnp.maximum(t - 1, 0)]))
        def _():
            for c in weight_copies(e):
                c.wait()
            wg_bf[...] = wg_st[...].astype(BF16)
            wu_bf[...] = wu_st[...].astype(BF16)
            wd_bf[...] = wd_st[...].astype(BF16)
            nxt = lax.while_loop(lambda j: jnp.logical_and(j < nact, te_ref[jnp.minimum(j, n_tiles - 1)] == e),
                                 lambda j: j + 1, t + 1)

            @pl.when(nxt < nact)
            def _():
                for c in weight_copies(te_ref[jnp.minimum(nxt, n_tiles - 1)]):
                    c.start(priority=1)

        for i in range(tg):
            _group_copy(xb_ref, 0, xbuf, slot, i, sem).wait()

        @pl.when(t >= 2)
        def _():
            out_copy(t - 2, slot).wait()

        x = xbuf[slot, :, 0:d]
        hid = jax.nn.silu(_dot(x, wg_bf[...])) * _dot(x, wu_bf[...])
        weight = (xbuf[slot, :, d:d + 1].astype(F32)
                  + xbuf[slot, :, d + W_LANES // 2:d + W_LANES // 2 + 1].astype(F32))
        obuf[slot] = (_dot(hid.astype(BF16), wd_bf[...]) * weight).astype(BF16)
        out_copy(t, slot).start()
        return carry

    lax.fori_loop(0, nact, tile_step, 0)
    for back in (2, 1):
        @pl.when(nact >= back)
        def _():
            out_copy(nact - back, lax.rem(nact - back, 2)).wait()

    obuf[0] = jnp.zeros((tm, d), BF16)

    def zero_start(t, carry):
        out_copy(t, 0).start()
        return carry

    def zero_wait(t, carry):
        out_copy(t, 0).wait()
        return carry

    lax.fori_loop(nact, n_tiles, zero_start, 0)
    lax.fori_loop(nact, n_tiles, zero_wait, 0)


def _experts(te, src, nact, xb, w_gate, w_up, w_down, n_tiles):
    dw = xb.shape[1]
    d = dw - W_LANES
    ff = w_gate.shape[2]
    tm = MOE_TILE
    hbm = pl.BlockSpec(memory_space=pl.ANY)
    return pl.pallas_call(
        _experts_kernel,
        grid_spec=pltpu.PrefetchScalarGridSpec(
            num_scalar_prefetch=3,
            grid=(1,),
            in_specs=[hbm, hbm, hbm, hbm],
            out_specs=hbm,
            scratch_shapes=[pltpu.VMEM((2, tm, dw), BF16), pltpu.SemaphoreType.DMA((2,)),
                            pltpu.VMEM((2, tm, d), BF16), pltpu.SemaphoreType.DMA((2,)),
                            pltpu.VMEM((d, ff), F32), pltpu.VMEM((d, ff), F32), pltpu.VMEM((ff, d), F32),
                            pltpu.SemaphoreType.DMA((3,)),
                            pltpu.VMEM((d, ff), BF16), pltpu.VMEM((d, ff), BF16), pltpu.VMEM((ff, d), BF16)],
        ),
        out_shape=jax.ShapeDtypeStruct((n_tiles * tm, d), BF16),
        compiler_params=_params("arbitrary"),
        name="experts",
    )(te, src, nact, xb, w_gate, w_up, w_down)


def _combine_kernel(dst_ref, lpos_ref, x2_ref, g_ref, ys_ref, out_ref, ybuf, sem):
    b = pl.program_id(0)
    tb = x2_ref.shape[0]
    lg = LOCAL_ROWS // GROUP
    slot = lax.rem(b, 2)

    def start_gather(blk):
        for i in range(lg):
            _group_copy(ys_ref, dst_ref[blk * lg + i], ybuf, lax.rem(blk, 2), i, sem).start()

    @pl.when(b == 0)
    def _():
        start_gather(b)

    @pl.when(b + 1 < pl.num_programs(0))
    def _():
        start_gather(b + 1)

    for i in range(lg):
        _group_copy(ys_ref, 0, ybuf, slot, i, sem).wait()
    lpos = lpos_ref[...]
    y = jnp.zeros(x2_ref.shape, F32)
    for c in range(LOCAL_ROWS // SORT_CHUNK):
        m0, m1 = _slot_masks(lpos, c, tb)
        p = jnp.where(m0, 1.0, jnp.where(m1, 1.0, 0.0)).astype(BF16)
        y = y + _dot_tn(p, ybuf[slot, c * SORT_CHUNK:(c + 1) * SORT_CHUNK, :])
    out_ref[...] = _rms(x2_ref[...] + y, g_ref[...])


def _combine(dst, lpos, x2, g_final, ys):
    n, d = x2.shape
    tb = TOKEN_BLOCK
    return pl.pallas_call(
        _combine_kernel,
        grid_spec=pltpu.PrefetchScalarGridSpec(
            num_scalar_prefetch=1,
            grid=(n // tb,),
            in_specs=[pl.BlockSpec((None, 1, TOP_K * tb), lambda i, dst: (i, 0, 0)),
                      pl.BlockSpec((tb, d), lambda i, dst: (i, 0)), pl.BlockSpec((1, d), lambda i, dst: (0, 0)),
                      pl.BlockSpec(memory_space=pl.ANY)],
            out_specs=pl.BlockSpec((tb, d), lambda i, dst: (i, 0)),
            scratch_shapes=[pltpu.VMEM((2, LOCAL_ROWS, d), BF16), pltpu.SemaphoreType.DMA((2,))],
        ),
        out_shape=jax.ShapeDtypeStruct((n, d), F32),
        compiler_params=_params("arbitrary"),
        name="combine",
    )(dst, lpos, x2, g_final, ys)


def kernel(x, mem, norm_mix_g, w_in, conv_w, conv_b, conv_ln_g, conv_ln_b, conv_w_out, hgrn_lb_logits, hgrn_onorm_g, hgrn_w_out, w_mix_out, norm_xa_g, norm_mem_g, xa_w_q, xa_w_k, xa_w_v, xa_w_o, norm_ffn_g, router_group_w, router_group_b, router_expert_w, router_expert_b, moe_w_gate, moe_w_up, moe_w_down, final_norm_g):
    batch, seq, d = x.shape
    n = batch * seq
    assert w_in.shape[0] == 1, "the final RMSNorm is fused into the single layer's combine step"
    cw = conv_w.shape[2]
    kw = hgrn_w_out.shape[1]
    assert kw == HGRN_HEADS * HGRN_DIM and conv_w.shape[1] == CONV_K
    assert seq % TOKEN_BLOCK == 0 and TOKEN_BLOCK % CHUNK == 0 and TOKEN_BLOCK % CONV_ROWS == 0
    assert moe_w_gate.shape[1] == N_EXPERTS and router_group_w.shape[2] == N_GROUPS and TOP_K == 2
    na = TOP_K * TOKEN_BLOCK
    assert LOCAL_ROWS % SORT_CHUNK == 0 and LOCAL_ROWS >= na + N_EXPERTS * (GROUP - 1) + GROUP

    n_tiles = -(-(n // TOKEN_BLOCK) * (na + N_EXPERTS * (GROUP - 1)) // MOE_TILE) + N_EXPERTS
    n_tiles = -(-n_tiles // 16) * 16
    vec = lambda p: p.reshape(1, -1)
    l = 0

    x2d = x.reshape(n, d)
    u, q, fr, iv, og, gates = _inproj(x2d, vec(norm_mix_g[l]), w_in[l].astype(BF16), cw, kw)
    yc = _conv(u, conv_w[l], vec(conv_b[l]), vec(conv_ln_g[l]), vec(conv_ln_b[l]),
               conv_w_out[l].astype(BF16), gates, batch)
    x1 = _hgrn(q, fr, iv, og, gates, yc, x2d, hgrn_lb_logits[l:l + 2], vec(hgrn_onorm_g[l]),
               hgrn_w_out[l].astype(BF16), w_mix_out[l].astype(BF16), batch)
    k_bf, v_bf = _memkv(mem.reshape(-1, d), vec(norm_mem_g[l]), xa_w_k[l].astype(BF16), xa_w_v[l].astype(BF16), batch)
    pad = ROUTE_ROWS - N_GROUPS - N_EXPERTS
    w_route_t = jnp.pad(jnp.concatenate([router_group_w[l], router_expert_w[l]], axis=1).T, ((0, pad), (0, 0)))
    b_route = jnp.pad(jnp.concatenate([router_group_b[l], router_expert_b[l]]), (0, pad)).reshape(ROUTE_ROWS, 1)
    upper = (lax.broadcasted_iota(I32, (na, na), 0) < lax.broadcasted_iota(I32, (na, na), 1)).astype(BF16)
    x2, xb, cnt, lpos = _attn(x1, vec(norm_xa_g[l]), xa_w_q[l].astype(BF16), k_bf, v_bf, xa_w_o[l].astype(BF16),
                              vec(norm_ffn_g[l]), w_route_t, b_route, upper, batch)
    te, src, dst, nact = _plan(cnt.reshape(-1, 1), n_tiles)
    ys = _experts(te, src, nact, xb, moe_w_gate[l], moe_w_up[l], moe_w_down[l], n_tiles)
    out = _combine(dst, lpos, x2, vec(final_norm_g), ys)
    return out.reshape(batch, seq, d)
```

```python
import functools

import jax
import jax.numpy as jnp
from jax import lax
from jax.experimental import pallas as pl
from jax.experimental.pallas import tpu as pltpu

F32 = jnp.float32
BF16 = jnp.bfloat16
I32 = jnp.int32

EPS = 1e-6
CONV_K = 31
CONV_HALO = 32
CONV_ROWS = 64
HGRN_HEADS = 4
HGRN_DIM = 128
CHUNK = 64
SUB = 16
XA_HEADS = 4
N_GROUPS = 4
EXPERTS_PER_GROUP = 8
N_EXPERTS = N_GROUPS * EXPERTS_PER_GROUP
TOP_K = 2
ROUTE_ROWS = 40
TOKEN_BLOCK = 512
MOE_TILE = 256
GROUP = 16
LOCAL_ROWS = 1536
SORT_CHUNK = 256
RANK_SEGMENT = 128
W_LANES = 128
PLAN_CHUNKS = (512, 256, 128)
VMEM_LIMIT_BYTES = 48 * 1024 * 1024


def _rms(x, g):
    return x * lax.rsqrt(jnp.mean(x * x, axis=-1, keepdims=True) + EPS) * g


def _dot(a, b):
    return jnp.dot(a, b, preferred_element_type=F32)


def _dot_nt(a, b):
    return lax.dot_general(a, b, (((1,), (1,)), ((), ())), preferred_element_type=F32)


def _dot_tn(a, b):
    return lax.dot_general(a, b, (((0,), (0,)), ((), ())), preferred_element_type=F32)


def _split_bf16(x):
    hi = x.astype(BF16)
    lo = (x - hi.astype(F32)).astype(BF16)
    return hi, lo


def _params(*sem):
    return pltpu.CompilerParams(dimension_semantics=sem, vmem_limit_bytes=VMEM_LIMIT_BYTES)


def _inproj_kernel(x_ref, g_ref, w_ref, u_ref, q_ref, fr_ref, iv_ref, og_ref, gates_ref, *, cw, kw):
    hb = _rms(x_ref[...], g_ref[...]).astype(BF16)

    def proj(lo, width):
        return _dot(hb, w_ref[:, lo:lo + width])

    u_ref[...] = (proj(0, cw) * jax.nn.sigmoid(proj(cw, cw))).astype(BF16)
    base = 2 * cw
    q_ref[...] = jax.nn.silu(proj(base, kw)).astype(BF16)
    fr_ref[...] = proj(base + kw, kw)
    iv_ref[...] = proj(base + 2 * kw, kw).astype(BF16)
    og_ref[...] = jax.nn.silu(proj(base + 3 * kw, kw)).astype(BF16)
    base += 4 * kw
    for c in range(gates_ref.shape[1] // kw):
        gates_ref[:, c * kw:(c + 1) * kw] = jax.nn.sigmoid(proj(base + c * kw, kw)).astype(BF16)


def _inproj(x2d, g, w_bf, cw, kw):
    n, d = x2d.shape
    tb = TOKEN_BLOCK
    row = lambda i: (i, 0)
    fixed = lambda i: (0, 0)
    outs = [((n, cw), BF16), ((n, kw), BF16), ((n, kw), F32), ((n, kw), BF16), ((n, kw), BF16), ((n, 2 * d), BF16)]
    return pl.pallas_call(
        functools.partial(_inproj_kernel, cw=cw, kw=kw),
        grid=(n // tb,),
        in_specs=[pl.BlockSpec((tb, d), row), pl.BlockSpec((1, d), fixed), pl.BlockSpec(w_bf.shape, fixed)],
        out_specs=[pl.BlockSpec((tb, s[1]), row) for s, _ in outs],
        out_shape=[jax.ShapeDtypeStruct(s, t) for s, t in outs],
        compiler_params=_params("parallel"),
        name="inproj",
    )(x2d, g, w_bf)


def _conv_kernel(u_ref, cw_ref, cb_ref, lg_ref, lb_ref, wo_ref, gate_ref, y_ref, ext_ref, halo_ref, perm_ref, act_ref):
    tb, c = u_ref.shape
    nt = tb // 8
    slabs = c // 128
    hr = CONV_HALO * 8
    lanes = [slice(l * 128, (l + 1) * 128) for l in range(slabs)]

    @pl.when(pl.program_id(1) == 0)
    def _():
        halo_ref[...] = jnp.zeros_like(halo_ref)

    un = u_ref[...].astype(F32)
    per = nt // 8
    for j in range(nt):
        start = hr + (j % per) * 64 + j // per
        for l in range(slabs):
            ext_ref[l, pl.ds(start, 8, stride=8), :] = un[8 * j:8 * j + 8, lanes[l]]
    first = lax.broadcasted_iota(I32, (hr, 128), 0) % 8 == 0
    for l in range(slabs):
        cur = ext_ref[l, nt * 8:nt * 8 + hr, :]
        ext_ref[l, 0:hr, :] = jnp.where(first, pltpu.roll(halo_ref[l], hr - 7, axis=0), pltpu.roll(cur, 1, axis=0))
        halo_ref[l] = cur

    for r in range(tb // CONV_ROWS):
        accs = []
        for l in range(slabs):
            acc = jnp.broadcast_to(cb_ref[:, lanes[l]], (CONV_ROWS, 128))
            for dt in range(CONV_K):
                off = hr + r * CONV_ROWS - dt * 8
                acc = acc + cw_ref[CONV_K - 1 - dt:CONV_K - dt, lanes[l]] * ext_ref[l, off:off + CONV_ROWS, :]
            accs.append(acc)
        mu = functools.reduce(jnp.add, [jnp.sum(a, axis=-1, keepdims=True) for a in accs]) * (1.0 / c)
        cens = [a - mu for a in accs]
        var = functools.reduce(jnp.add, [jnp.sum(a * a, axis=-1, keepdims=True) for a in cens]) * (1.0 / c)
        inv = lax.rsqrt(var + EPS)
        for l in range(slabs):
            ln = cens[l] * inv * lg_ref[:, lanes[l]] + lb_ref[:, lanes[l]]
            perm_ref[l, r * CONV_ROWS:(r + 1) * CONV_ROWS, :] = jax.nn.silu(ln)
    for j in range(nt):
        start = (j % per) * 64 + j // per
        for l in range(slabs):
            act_ref[8 * j:8 * j + 8, lanes[l]] = perm_ref[l, pl.ds(start, 8, stride=8), :]
    y_ref[...] = (gate_ref[...].astype(F32) * _dot(act_ref[...].astype(BF16), wo_ref[...])).astype(BF16)


def _conv(u, conv_w, conv_b, ln_g, ln_b, w_out_bf, gates, batch):
    n, c = u.shape
    d = w_out_bf.shape[1]
    tb = TOKEN_BLOCK
    nsb = n // batch // tb
    row = lambda b, s: (b * nsb + s, 0)
    fixed = lambda b, s: (0, 0)
    return pl.pallas_call(
        _conv_kernel,
        grid=(batch, nsb),
        in_specs=[pl.BlockSpec((tb, c), row), pl.BlockSpec(conv_w.shape, fixed), pl.BlockSpec((1, c), fixed),
                  pl.BlockSpec((1, c), fixed), pl.BlockSpec((1, c), fixed), pl.BlockSpec((c, d), fixed),
                  pl.BlockSpec((tb, d), row)],
        out_specs=pl.BlockSpec((tb, d), row),
        out_shape=jax.ShapeDtypeStruct((n, d), BF16),
        scratch_shapes=[pltpu.VMEM((c // 128, tb + CONV_HALO * 8, 128), F32),
                        pltpu.VMEM((c // 128, CONV_HALO * 8, 128), F32),
                        pltpu.VMEM((c // 128, tb, 128), F32), pltpu.VMEM((tb, c), F32)],
        compiler_params=_params("arbitrary", "arbitrary"),
        name="conv",
    )(u, conv_w, conv_b, ln_g, ln_b, w_out_bf, gates)


def _hgrn_block(q_ref, fr_ref, iv_ref, lb, st_ref, tri):
    tb = q_ref.shape[0]
    chunks = [slice(c * CHUNK, (c + 1) * CHUNK) for c in range(tb // CHUNK)]
    heads = [slice(h * HGRN_DIM, (h + 1) * HGRN_DIM) for h in range(HGRN_HEADS)]

    qs, vs, kks, cums = [], [], [], []
    for rows in chunks:
        f = lb + (1.0 - lb) * jax.nn.sigmoid(fr_ref[rows, :])
        lf_hi, lf_lo = _split_bf16(jnp.log(f))
        cums.append(_dot(tri, lf_hi) + _dot(tri, lf_lo))
        kks.append(1.0 - f)
        qs.append(q_ref[rows, :].astype(F32))
        vs.append(iv_ref[rows, :])

    qes, kds, decays, blocks = [], [], [], []
    for q, kk, cum in zip(qs, kks, cums):
        last = cum[CHUNK - 1:CHUNK, :]
        qes.append((q * jnp.exp(cum)).astype(BF16))
        kds.append((kk * jnp.exp(last - cum)).astype(BF16))
        decays.append(jnp.exp(last))
        sub = []
        for i in range(CHUNK // SUB):
            rs, ne = i * SUB, (i + 1) * SUB
            ref = cum[rs + SUB // 2 - 1:rs + SUB // 2, :]
            qt = (q[rs:ne] * jnp.exp(cum[rs:ne] - ref)).astype(BF16)
            kt = (kk[0:ne] * jnp.exp(ref - cum[0:ne])).astype(BF16)
            sub.append((qt, kt))
        blocks.append(sub)

    updates = [[_dot_tn(v[:, hs], kd[:, hs]) for hs in heads] for v, kd in zip(vs, kds)]
    scores = [[[_dot_nt(qt[:, hs], kt[:, hs]) for qt, kt in sub] for hs in heads] for sub in blocks]

    states = [st_ref[h] for h in range(HGRN_HEADS)]
    inter = []
    for qe, decay, upd in zip(qes, decays, updates):
        inter.append([_dot_nt(qe[:, hs], st.astype(BF16)) for hs, st in zip(heads, states)])
        states = [st * decay[:, hs] + u for st, hs, u in zip(states, heads, upd)]
    for h in range(HGRN_HEADS):
        st_ref[h] = states[h]

    outs = []
    for v, sc, o_inter in zip(vs, scores, inter):
        per_head = []
        for h, hs in enumerate(heads):
            parts = []
            for i, a in enumerate(sc[h]):
                rs, ne = i * SUB, (i + 1) * SUB
                trow = lax.broadcasted_iota(I32, (SUB, ne), 0) + rs
                scol = lax.broadcasted_iota(I32, (SUB, ne), 1)
                a = jnp.where(scol <= trow, a, 0.0).astype(BF16)
                parts.append(_dot(a, v[0:ne, hs]))
            per_head.append(o_inter[h] + jnp.concatenate(parts, axis=0))
        outs.append(jnp.concatenate(per_head, axis=1))
    return outs


def _hgrn_kernel(q_ref, fr_ref, iv_ref, og_ref, gate_ref, yc_ref, x_ref, lbl_ref, on_ref, wo_ref, wm_ref,
                 x1_ref, st_ref, ob_ref):
    tb = q_ref.shape[0]

    @pl.when(pl.program_id(1) == 0)
    def _():
        st_ref[...] = jnp.zeros_like(st_ref)

    l0, l1 = lbl_ref[0:1, :], lbl_ref[1:2, :]
    m = jnp.maximum(l0, l1)
    e0, e1 = jnp.exp(l0 - m), jnp.exp(l1 - m)
    lb_all = e0 / (e0 + e1)
    trow = lax.broadcasted_iota(I32, (CHUNK, CHUNK), 0)
    tcol = lax.broadcasted_iota(I32, (CHUNK, CHUNK), 1)
    tri = jnp.where(tcol <= trow, 1.0, 0.0).astype(BF16)

    for c, o in enumerate(_hgrn_block(q_ref, fr_ref, iv_ref, lb_all, st_ref, tri)):
        rows = slice(c * CHUNK, (c + 1) * CHUNK)
        og = og_ref[rows, :].astype(F32)
        for h in range(HGRN_HEADS):
            hs = slice(h * HGRN_DIM, (h + 1) * HGRN_DIM)
            ob_ref[rows, hs] = (_rms(o[:, hs], on_ref[...]) * og[:, hs]).astype(BF16)
    y_rec = _dot(ob_ref[...], wo_ref[...])
    merged = yc_ref[...].astype(F32) + gate_ref[...].astype(F32) * y_rec
    x1_ref[...] = x_ref[...] + _dot(merged.astype(BF16), wm_ref[...])


def _hgrn(q, fr, iv, og, gates, yc, x2d, lb_logits, onorm_g, w_o_bf, w_mix_bf, batch):
    n, kw = q.shape
    d = x2d.shape[1]
    tb = TOKEN_BLOCK
    nsb = n // batch // tb
    row = lambda b, s: (b * nsb + s, 0)
    fixed = lambda b, s: (0, 0)
    return pl.pallas_call(
        _hgrn_kernel,
        grid=(batch, nsb),
        in_specs=[pl.BlockSpec((tb, kw), row), pl.BlockSpec((tb, kw), row), pl.BlockSpec((tb, kw), row),
                  pl.BlockSpec((tb, kw), row),
                  pl.BlockSpec((tb, d), lambda b, s: (b * nsb + s, 1)),
                  pl.BlockSpec((tb, d), row), pl.BlockSpec((tb, d), row),
                  pl.BlockSpec(lb_logits.shape, fixed), pl.BlockSpec((1, HGRN_DIM), fixed),
                  pl.BlockSpec(w_o_bf.shape, fixed), pl.BlockSpec(w_mix_bf.shape, fixed)],
        out_specs=pl.BlockSpec((tb, d), row),
        out_shape=jax.ShapeDtypeStruct((n, d), F32),
        scratch_shapes=[pltpu.VMEM((HGRN_HEADS, HGRN_DIM, HGRN_DIM), F32), pltpu.VMEM((tb, kw), BF16)],
        compiler_params=_params("arbitrary", "arbitrary"),
        name="hgrn",
    )(q, fr, iv, og, gates, yc, x2d, lb_logits, onorm_g, w_o_bf, w_mix_bf)


def _memkv_kernel(mem_ref, g_ref, wk_ref, wv_ref, k_ref, v_ref):
    mb = _rms(mem_ref[...], g_ref[...]).astype(BF16)
    k_ref[...] = _dot(mb, wk_ref[...]).astype(BF16)
    v_ref[...] = _dot(mb, wv_ref[...]).astype(BF16)


def _memkv(mem2d, g, wk_bf, wv_bf, batch):
    n, d = mem2d.shape
    m = n // batch
    row = lambda b: (b, 0)
    fixed = lambda b: (0, 0)
    return pl.pallas_call(
        _memkv_kernel,
        grid=(batch,),
        in_specs=[pl.BlockSpec((m, d), row), pl.BlockSpec((1, d), fixed), pl.BlockSpec((d, d), fixed),
                  pl.BlockSpec((d, d), fixed)],
        out_specs=[pl.BlockSpec((m, d), row)] * 2,
        out_shape=[jax.ShapeDtypeStruct((n, d), BF16)] * 2,
        compiler_params=_params("parallel"),
        name="memkv",
    )(mem2d, g, wk_bf, wv_bf)


def _route(lt):
    def row(r):
        return lt[r:r + 1, :]

    gl = [row(g) for g in range(N_GROUPS)]
    gmax = functools.reduce(jnp.maximum, gl)
    g_p = 1.0 / functools.reduce(jnp.add, [jnp.exp(l - gmax) for l in gl])
    gidx = jnp.full(gmax.shape, N_GROUPS - 1, I32)
    for g in range(N_GROUPS - 2, -1, -1):
        gidx = jnp.where(gl[g] == gmax, g, gidx)

    el = []
    for j in range(EXPERTS_PER_GROUP):
        v = row(N_GROUPS + (N_GROUPS - 1) * EXPERTS_PER_GROUP + j)
        for g in range(N_GROUPS - 2, -1, -1):
            v = jnp.where(gidx == g, row(N_GROUPS + g * EXPERTS_PER_GROUP + j), v)
        el.append(v)

    def argmax(vals):
        mx = functools.reduce(jnp.maximum, vals)
        idx = jnp.full(mx.shape, EXPERTS_PER_GROUP - 1, I32)
        for j in range(EXPERTS_PER_GROUP - 2, -1, -1):
            idx = jnp.where(vals[j] == mx, j, idx)
        return mx, idx

    m1, i1 = argmax(el)
    m2, i2 = argmax([jnp.where(i1 == j, -jnp.inf, el[j]) for j in range(EXPERTS_PER_GROUP)])
    r = jnp.exp(m2 - m1)
    w1 = g_p / (1.0 + r)
    w2 = g_p * r / (1.0 + r)
    base = gidx * EXPERTS_PER_GROUP
    return jnp.concatenate([base + i1, base + i2], axis=0), jnp.concatenate([w1, w2], axis=0)


def _excl_cumsum_rows(col):
    r = col.shape[0]
    lower = lax.broadcasted_iota(I32, (r, r), 1) < lax.broadcasted_iota(I32, (r, r), 0)
    lower_bf = jnp.where(lower, 1.0, 0.0).astype(BF16)
    hi, lo = _split_bf16(jnp.broadcast_to(col, (r, 128)))
    return (_dot(lower_bf, hi) + _dot(lower_bf, lo))[:, 0:1]


def _slot_masks(lpos, chunk, tb):
    slot = lax.broadcasted_iota(I32, (SORT_CHUNK, tb), 0) + chunk * SORT_CHUNK
    return slot == lpos[:, 0:tb], slot == lpos[:, tb:2 * tb]


def _earlier_same_expert(ones):
    seg = RANK_SEGMENT
    n_exp, n = ones.shape
    upper = lax.broadcasted_iota(I32, (seg, seg), 0) < lax.broadcasted_iota(I32, (seg, seg), 1)
    pieces = [ones[:, s * seg:(s + 1) * seg] for s in range(n // seg)]
    within = _dot(jnp.concatenate([p.astype(BF16) for p in pieces], axis=0), jnp.where(upper, 1.0, 0.0).astype(BF16))
    seen = jnp.zeros((n_exp, 1), F32)
    out = []
    for s, p in enumerate(pieces):
        out.append(within[s * n_exp:(s + 1) * n_exp, :] + seen)
        seen = seen + jnp.sum(p, axis=1, keepdims=True)
    return jnp.concatenate(out, axis=1), seen


def _local_sort(eid, wts, h2, xb_ref, cnt_ref, lpos_ref):
    tb, d = h2.shape
    na = TOP_K * tb
    e_all = jnp.concatenate([eid[k:k + 1] for k in range(TOP_K)], axis=1)
    onehot = lax.broadcasted_iota(I32, (N_EXPERTS, na), 0) == e_all
    earlier, cnt = _earlier_same_expert(jnp.where(onehot, 1.0, 0.0))
    cnt_pad = jnp.floor((cnt + (GROUP - 1)) * (1.0 / GROUP)) * GROUP
    start = _excl_cumsum_rows(cnt_pad)
    lpos = jnp.sum(jnp.where(onehot, start + earlier, 0.0), axis=0, keepdims=True).astype(I32)
    hb = h2.astype(BF16)
    n_chunks = LOCAL_ROWS // SORT_CHUNK

    def sort_chunk(c):
        m0, m1 = _slot_masks(lpos, c, tb)
        p = jnp.where(m0, 1.0, jnp.where(m1, 1.0, 0.0)).astype(BF16)
        pw = jnp.where(m0, wts[0:1], jnp.where(m1, wts[1:2], 0.0))
        rows = slice(c * SORT_CHUNK, (c + 1) * SORT_CHUNK)
        xb_ref[rows, 0:d] = _dot(p, hb).astype(BF16)
        w_row = jnp.broadcast_to(jnp.sum(pw, axis=1, keepdims=True), (SORT_CHUNK, W_LANES))
        w_hi = w_row.astype(BF16).astype(F32)
        low_half = lax.broadcasted_iota(I32, (SORT_CHUNK, W_LANES), 1) < W_LANES // 2
        xb_ref[rows, d:d + W_LANES] = jnp.where(low_half, w_hi, w_row - w_hi).astype(BF16)

    for c in range(n_chunks - 1):
        sort_chunk(c)
    last_used = jnp.sum(cnt_pad) > (n_chunks - 1) * SORT_CHUNK

    @pl.when(last_used)
    def _():
        sort_chunk(n_chunks - 1)

    @pl.when(jnp.logical_not(last_used))
    def _():
        xb_ref[(n_chunks - 1) * SORT_CHUNK:, :] = jnp.zeros((SORT_CHUNK, d + W_LANES), BF16)

    cnt_ref[...] = cnt_pad.astype(I32)
    lpos_ref[...] = lpos


def _attn_kernel(x1_ref, gxa_ref, wq_ref, k_ref, v_ref, wo_ref, gffn_ref, wr_ref, br_ref,
                 x2_ref, xb_ref, cnt_ref, lpos_ref):
    x1 = x1_ref[...]
    d = x1.shape[1]
    hd = d // XA_HEADS
    q = _dot(_rms(x1, gxa_ref[...]).astype(BF16), wq_ref[...]).astype(BF16)
    heads = []
    for h in range(XA_HEADS):
        hs = slice(h * hd, (h + 1) * hd)
        sc = _dot_nt(q[:, hs], k_ref[:, hs]) * (hd ** -0.5)
        p = jnp.exp(sc - jnp.max(sc, axis=-1, keepdims=True))
        p = p / jnp.sum(p, axis=-1, keepdims=True)
        heads.append(_dot(p.astype(BF16), v_ref[:, hs]).astype(BF16))
    x2 = x1 + _dot(jnp.concatenate(heads, axis=1), wo_ref[...])
    x2_ref[...] = x2
    h2 = _rms(x2, gffn_ref[...])
    h_hi, h_lo = _split_bf16(h2)
    w_hi, w_lo = _split_bf16(wr_ref[...])
    lt = _dot_nt(w_hi, h_hi) + (_dot_nt(w_hi, h_lo) + _dot_nt(w_lo, h_hi)) + br_ref[...]
    eid, wts = _route(lt)
    _local_sort(eid, wts, h2, xb_ref, cnt_ref, lpos_ref)


def _attn(x1, gxa, wq_bf, k_bf, v_bf, wo_bf, gffn, w_route_t, b_route, batch):
    n, d = x1.shape
    m = k_bf.shape[0] // batch
    tb = TOKEN_BLOCK
    nsb = n // batch // tb
    nb = n // tb
    row = lambda b, s: (b * nsb + s, 0)
    blk3 = lambda b, s: (b * nsb + s, 0, 0)
    fixed = lambda b, s: (0, 0)
    mem = lambda b, s: (b, 0)
    return pl.pallas_call(
        _attn_kernel,
        grid=(batch, nsb),
        in_specs=[pl.BlockSpec((tb, d), row), pl.BlockSpec((1, d), fixed), pl.BlockSpec((d, d), fixed),
                  pl.BlockSpec((m, d), mem), pl.BlockSpec((m, d), mem), pl.BlockSpec((d, d), fixed),
                  pl.BlockSpec((1, d), fixed), pl.BlockSpec((ROUTE_ROWS, d), fixed),
                  pl.BlockSpec((ROUTE_ROWS, 1), fixed)],
        out_specs=[pl.BlockSpec((tb, d), row), pl.BlockSpec((LOCAL_ROWS, d + W_LANES), row),
                   pl.BlockSpec((None, N_EXPERTS, 1), blk3), pl.BlockSpec((None, 1, TOP_K * tb), blk3)],
        out_shape=[jax.ShapeDtypeStruct((n, d), F32), jax.ShapeDtypeStruct((nb * LOCAL_ROWS, d + W_LANES), BF16),
                   jax.ShapeDtypeStruct((nb, N_EXPERTS, 1), I32), jax.ShapeDtypeStruct((nb, 1, TOP_K * tb), I32)],
        compiler_params=_params("parallel", "parallel"),
        name="attn",
    )(x1, gxa, wq_bf, k_bf, v_bf, wo_bf, gffn, w_route_t, b_route)


def _plan_kernel(cnt_ref, te_ref, src_ref, dst_ref, nact_ref):
    runs = cnt_ref.shape[0]
    lg = LOCAL_ROWS // GROUP
    tg = MOE_TILE // GROUP
    sh = N_EXPERTS.bit_length() - 1
    emask = N_EXPERTS - 1
    zero_group = lg - 1

    length = cnt_ref[...].astype(F32) * (1.0 / GROUP)
    len_bf = jnp.broadcast_to(length, (runs, 128)).astype(BF16)
    ri = lax.broadcasted_iota(I32, (runs, runs), 0)
    ci = lax.broadcasted_iota(I32, (runs, runs), 1)
    r_e, c_e = ri & emask, ci & emask
    r_b, c_b = lax.shift_right_logical(ri, sh), lax.shift_right_logical(ci, sh)
    same_expert_earlier = jnp.where(r_e == c_e, jnp.where(c_b < r_b, 1.0, 0.0), 0.0).astype(BF16)
    same_block_earlier = jnp.where(r_b == c_b, jnp.where(c_e < r_e, 1.0, 0.0), 0.0).astype(BF16)
    before = _dot(same_expert_earlier, len_bf)[:, 0:1]
    local = _dot(same_block_earlier, len_bf)[:, 0:1]

    of_expert = (lax.broadcasted_iota(I32, (N_EXPERTS, runs), 1) & emask) == lax.broadcasted_iota(
        I32, (N_EXPERTS, runs), 0)
    total = _dot(jnp.where(of_expert, 1.0, 0.0).astype(BF16), len_bf)[:, 0:1]
    tiles = jnp.floor((total + (tg - 1)) * (1.0 / tg))
    tile0 = _excl_cumsum_rows(tiles)
    to_run = (lax.broadcasted_iota(I32, (runs, N_EXPERTS), 0) & emask) == lax.broadcasted_iota(
        I32, (runs, N_EXPERTS), 1)
    to_run_bf = jnp.where(to_run, 1.0, 0.0).astype(BF16)
    base_hi, base_lo = _split_bf16(jnp.broadcast_to(tile0 * tg, (N_EXPERTS, 128)))
    g_start = (_dot(to_run_bf, base_hi) + _dot(to_run_bf, base_lo))[:, 0:1] + before
    block = lax.shift_right_logical(lax.broadcasted_iota(I32, (runs, 1), 0), sh).astype(F32)
    l_start = block * lg + local

    def cover(out_ref, start, offset, default):
        stop = start + length
        shift = offset - default
        n_out = out_ref.shape[1]
        chunk = next(c for c in PLAN_CHUNKS if n_out % c == 0)
        for c in range(n_out // chunk):
            j = (lax.broadcasted_iota(I32, (runs, chunk), 1) + c * chunk).astype(F32)
            hit = jnp.where(start <= j, jnp.where(j < stop, j + shift, 0.0), 0.0)
            out = jnp.sum(hit, axis=0, keepdims=True) + default
            out_ref[:, c * chunk:(c + 1) * chunk] = out.astype(I32)

    cover(src_ref, g_start, l_start - g_start, float(zero_group))
    cover(dst_ref, l_start, g_start - l_start, 0.0)
    t = lax.broadcasted_iota(I32, (N_EXPERTS, te_ref.shape[1]), 1).astype(F32)
    te_ref[...] = jnp.sum(jnp.where(t >= tile0 + tiles, 1.0, 0.0), axis=0, keepdims=True).astype(I32)
    nact_ref[...] = jnp.sum(jnp.broadcast_to(tiles, (N_EXPERTS, 128)), axis=0, keepdims=True).astype(I32)


def _plan(cnt, n_tiles):
    runs = cnt.shape[0]
    n_src = n_tiles * (MOE_TILE // GROUP)
    n_dst = runs // N_EXPERTS * (LOCAL_ROWS // GROUP)
    n_te = -(-n_tiles // 128) * 128
    te, src, dst, nact = pl.pallas_call(
        _plan_kernel,
        out_shape=[jax.ShapeDtypeStruct((1, n_te), I32), jax.ShapeDtypeStruct((1, n_src), I32),
                   jax.ShapeDtypeStruct((1, n_dst), I32), jax.ShapeDtypeStruct((1, 128), I32)],
        compiler_params=_params(),
        name="plan",
    )(cnt)
    return te.reshape(-1), src.reshape(-1), dst.reshape(-1), nact[0, :1]


def _group_copy(src_hbm, src_group, dst_buf, slot, index, sem):
    start = src_group * GROUP
    rows = pl.ds(start if isinstance(start, int) else pl.multiple_of(start, GROUP), GROUP)
    return pltpu.make_async_copy(src_hbm.at[rows], dst_buf.at[slot, pl.ds(index * GROUP, GROUP)], sem.at[slot])


def _experts_kernel(te_ref, src_ref, nact_ref, xb_ref, wg_hbm, wu_hbm, wd_hbm, ys_ref,
                    xbuf, sem, obuf, osem, wg_st, wu_st, wd_st, wsem, wg_bf, wu_bf, wd_bf):
    tg = MOE_TILE // GROUP
    tm, d = obuf.shape[1], obuf.shape[2]
    nact = nact_ref[0]
    n_tiles = ys_ref.shape[0] // tm - 2

    def start_gather(tile, slot):
        for i in range(tg):
            _group_copy(xb_ref, src_ref[tile * tg + i], xbuf, slot, i, sem).start()

    def wait_gather(slot):
        for i in range(tg):
            _group_copy(xb_ref, 0, xbuf, slot, i, sem).wait()

    def weight_copies(expert):
        pairs = ((wg_hbm, wg_st), (wu_hbm, wu_st), (wd_hbm, wd_st))
        return [pltpu.make_async_copy(w.at[expert], st, wsem.at[i]) for i, (w, st) in enumerate(pairs)]

    def out_copy(tile, slot):
        rows = pl.ds(pl.multiple_of(tile * tm, tm), tm)
        return pltpu.make_async_copy(obuf.at[slot], ys_ref.at[rows], osem.at[slot])

    obuf[...] = jnp.zeros_like(obuf)
    for s in range(2):
        out_copy(n_tiles + s, s).start()

    @pl.when(nact > 0)
    def _():
        start_gather(0, 0)
        for c in weight_copies(te_ref[0]):
            c.start()

    def tile_step(t, carry):
        e = te_ref[t]
        slot = lax.rem(t, 2)

        @pl.when(jnp.logical_or(t == 0, e != te_ref[jnp.maximum(t - 1, 0)]))
        def _():
            for c in weight_copies(e):
                c.wait()
            wg_bf[...] = wg_st[...].astype(BF16)
            wu_bf[...] = wu_st[...].astype(BF16)
            wd_bf[...] = wd_st[...].astype(BF16)
            nxt = lax.while_loop(lambda j: jnp.logical_and(j < nact, te_ref[jnp.minimum(j, n_tiles - 1)] == e),
                                 lambda j: j + 1, t + 1)

            @pl.when(nxt < nact)
            def _():
                for c in weight_copies(te_ref[jnp.minimum(nxt, n_tiles - 1)]):
                    c.start(priority=1)

        wait_gather(slot)
        out_copy(t, slot).wait()
        x = xbuf[slot, :, 0:d]
        hid = jax.nn.silu(_dot(x, wg_bf[...])) * _dot(x, wu_bf[...])
        start_gather(jnp.minimum(t + 1, nact - 1), 1 - slot)
        weight = (xbuf[slot, :, d:d + 1].astype(F32)
                  + xbuf[slot, :, d + W_LANES // 2:d + W_LANES // 2 + 1].astype(F32))
        obuf[slot] = (_dot(hid.astype(BF16), wd_bf[...]) * weight).astype(BF16)
        out_copy(t, slot).start()
        return carry

    lax.fori_loop(0, nact, tile_step, 0)

    @pl.when(nact > 0)
    def _():
        wait_gather(lax.rem(nact, 2))
    for s in range(2):
        out_copy(0, s).wait()
    obuf[0] = jnp.zeros((tm, d), BF16)

    def zero_start(t, carry):
        out_copy(t, 0).start()
        return carry

    def zero_wait(t, carry):
        out_copy(t, 0).wait()
        return carry

    lax.fori_loop(nact, n_tiles, zero_start, 0)
    lax.fori_loop(nact, n_tiles, zero_wait, 0)


def _experts(te, src, nact, xb, w_gate, w_up, w_down, n_tiles):
    dw = xb.shape[1]
    d = dw - W_LANES
    ff = w_gate.shape[2]
    tm = MOE_TILE
    hbm = pl.BlockSpec(memory_space=pl.ANY)
    return pl.pallas_call(
        _experts_kernel,
        grid_spec=pltpu.PrefetchScalarGridSpec(
            num_scalar_prefetch=3,
            grid=(1,),
            in_specs=[hbm, hbm, hbm, hbm],
            out_specs=hbm,
            scratch_shapes=[pltpu.VMEM((2, tm, dw), BF16), pltpu.SemaphoreType.DMA((2,)),
                            pltpu.VMEM((2, tm, d), BF16), pltpu.SemaphoreType.DMA((2,)),
                            pltpu.VMEM((d, ff), F32), pltpu.VMEM((d, ff), F32), pltpu.VMEM((ff, d), F32),
                            pltpu.SemaphoreType.DMA((3,)),
                            pltpu.VMEM((d, ff), BF16), pltpu.VMEM((d, ff), BF16), pltpu.VMEM((ff, d), BF16)],
        ),
        out_shape=jax.ShapeDtypeStruct(((n_tiles + 2) * tm, d), BF16),
        compiler_params=_params("arbitrary"),
        name="experts",
    )(te, src, nact, xb, w_gate, w_up, w_down)


def _combine_kernel(dst_ref, lpos_ref, x2_ref, g_ref, ys_ref, out_ref, ybuf, sem):
    b = pl.program_id(0)
    tb = x2_ref.shape[0]
    lg = LOCAL_ROWS // GROUP
    slot = lax.rem(b, 2)

    def start_gather(blk):
        for i in range(lg):
            _group_copy(ys_ref, dst_ref[blk * lg + i], ybuf, lax.rem(blk, 2), i, sem).start()

    @pl.when(b == 0)
    def _():
        start_gather(b)

    @pl.when(b + 1 < pl.num_programs(0))
    def _():
        start_gather(b + 1)

    for i in range(lg):
        _group_copy(ys_ref, 0, ybuf, slot, i, sem).wait()
    lpos = lpos_ref[...]
    y = jnp.zeros(x2_ref.shape, F32)
    for c in range(LOCAL_ROWS // SORT_CHUNK):
        m0, m1 = _slot_masks(lpos, c, tb)
        p = jnp.where(m0, 1.0, jnp.where(m1, 1.0, 0.0)).astype(BF16)
        y = y + _dot_tn(p, ybuf[slot, c * SORT_CHUNK:(c + 1) * SORT_CHUNK, :])
    out_ref[...] = _rms(x2_ref[...] + y, g_ref[...])


def _combine(dst, lpos, x2, g_final, ys):
    n, d = x2.shape
    tb = TOKEN_BLOCK
    return pl.pallas_call(
        _combine_kernel,
        grid_spec=pltpu.PrefetchScalarGridSpec(
            num_scalar_prefetch=1,
            grid=(n // tb,),
            in_specs=[pl.BlockSpec((None, 1, TOP_K * tb), lambda i, dst: (i, 0, 0)),
                      pl.BlockSpec((tb, d), lambda i, dst: (i, 0)), pl.BlockSpec((1, d), lambda i, dst: (0, 0)),
                      pl.BlockSpec(memory_space=pl.ANY)],
            out_specs=pl.BlockSpec((tb, d), lambda i, dst: (i, 0)),
            scratch_shapes=[pltpu.VMEM((2, LOCAL_ROWS, d), BF16), pltpu.SemaphoreType.DMA((2,))],
        ),
        out_shape=jax.ShapeDtypeStruct((n, d), F32),
        compiler_params=_params("arbitrary"),
        name="combine",
    )(dst, lpos, x2, g_final, ys)


def kernel(x, mem, norm_mix_g, w_in, conv_w, conv_b, conv_ln_g, conv_ln_b, conv_w_out, hgrn_lb_logits, hgrn_onorm_g, hgrn_w_out, w_mix_out, norm_xa_g, norm_mem_g, xa_w_q, xa_w_k, xa_w_v, xa_w_o, norm_ffn_g, router_group_w, router_group_b, router_expert_w, router_expert_b, moe_w_gate, moe_w_up, moe_w_down, final_norm_g):
    batch, seq, d = x.shape
    n = batch * seq
    assert w_in.shape[0] == 1, "the final RMSNorm is fused into the single layer's combine step"
    cw = conv_w.shape[2]
    kw = hgrn_w_out.shape[1]
    assert kw == HGRN_HEADS * HGRN_DIM and conv_w.shape[1] == CONV_K
    assert seq % TOKEN_BLOCK == 0 and TOKEN_BLOCK % CHUNK == 0 and TOKEN_BLOCK % CONV_ROWS == 0
    assert moe_w_gate.shape[1] == N_EXPERTS and router_group_w.shape[2] == N_GROUPS and TOP_K == 2
    na = TOP_K * TOKEN_BLOCK
    assert LOCAL_ROWS % SORT_CHUNK == 0 and LOCAL_ROWS >= na + N_EXPERTS * (GROUP - 1) + GROUP

    n_tiles = -(-(n // TOKEN_BLOCK) * (na + N_EXPERTS * (GROUP - 1)) // MOE_TILE) + N_EXPERTS
    n_tiles = -(-n_tiles // 16) * 16
    vec = lambda p: p.reshape(1, -1)
    l = 0

    x2d = x.reshape(n, d)
    u, q, fr, iv, og, gates = _inproj(x2d, vec(norm_mix_g[l]), w_in[l].astype(BF16), cw, kw)
    yc = _conv(u, conv_w[l], vec(conv_b[l]), vec(conv_ln_g[l]), vec(conv_ln_b[l]),
               conv_w_out[l].astype(BF16), gates, batch)
    x1 = _hgrn(q, fr, iv, og, gates, yc, x2d, hgrn_lb_logits[l:l + 2], vec(hgrn_onorm_g[l]),
               hgrn_w_out[l].astype(BF16), w_mix_out[l].astype(BF16), batch)
    k_bf, v_bf = _memkv(mem.reshape(-1, d), vec(norm_mem_g[l]), xa_w_k[l].astype(BF16), xa_w_v[l].astype(BF16), batch)
    pad = ROUTE_ROWS - N_GROUPS - N_EXPERTS
    w_route_t = jnp.pad(jnp.concatenate([router_group_w[l], router_expert_w[l]], axis=1).T, ((0, pad), (0, 0)))
    b_route = jnp.pad(jnp.concatenate([router_group_b[l], router_expert_b[l]]), (0, pad)).reshape(ROUTE_ROWS, 1)
    x2, xb, cnt, lpos = _attn(x1, vec(norm_xa_g[l]), xa_w_q[l].astype(BF16), k_bf, v_bf, xa_w_o[l].astype(BF16),
                              vec(norm_ffn_g[l]), w_route_t, b_route, batch)
    te, src, dst, nact = _plan(cnt.reshape(-1, 1), n_tiles)
    ys = _experts(te, src, nact, xb, moe_w_gate[l], moe_w_up[l], moe_w_down[l], n_tiles)
    out = _combine(dst, lpos, x2, vec(final_norm_g), ys)
    return out.reshape(batch, seq, d)
```

```python
import functools

import jax
import jax.numpy as jnp
from jax import lax
from jax.experimental import pallas as pl
from jax.experimental.pallas import tpu as pltpu

F32 = jnp.float32
BF16 = jnp.bfloat16
I32 = jnp.int32

EPS = 1e-6
CONV_K = 31
CONV_HALO = 32
CONV_ROWS = 64
HGRN_HEADS = 4
HGRN_DIM = 128
CHUNK = 64
SUB = 16
XA_HEADS = 4
N_GROUPS = 4
EXPERTS_PER_GROUP = 8
N_EXPERTS = N_GROUPS * EXPERTS_PER_GROUP
TOP_K = 2
ROUTE_ROWS = 40
TOKEN_BLOCK = 512
MOE_TILE = 256
GROUP = 16
LOCAL_ROWS = 1536
SORT_CHUNK = 256
RANK_SEGMENT = 128
GATHER_AHEAD = 2
W_LANES = 128
PLAN_CHUNKS = (512, 256, 128)
VMEM_LIMIT_BYTES = 48 * 1024 * 1024


def _rms(x, g):
    return x * lax.rsqrt(jnp.mean(x * x, axis=-1, keepdims=True) + EPS) * g


def _dot(a, b):
    return jnp.dot(a, b, preferred_element_type=F32)


def _dot_nt(a, b):
    return lax.dot_general(a, b, (((1,), (1,)), ((), ())), preferred_element_type=F32)


def _dot_tn(a, b):
    return lax.dot_general(a, b, (((0,), (0,)), ((), ())), preferred_element_type=F32)


def _split_bf16(x):
    hi = x.astype(BF16)
    lo = (x - hi.astype(F32)).astype(BF16)
    return hi, lo


def _params(*sem):
    return pltpu.CompilerParams(dimension_semantics=sem, vmem_limit_bytes=VMEM_LIMIT_BYTES)


def _inproj_kernel(x_ref, g_ref, w_ref, u_ref, q_ref, fr_ref, iv_ref, og_ref, gates_ref, *, cw, kw):
    hb = _rms(x_ref[...], g_ref[...]).astype(BF16)

    def proj(lo, width):
        return _dot(hb, w_ref[:, lo:lo + width])

    u_ref[...] = (proj(0, cw) * jax.nn.sigmoid(proj(cw, cw))).astype(BF16)
    base = 2 * cw
    q_ref[...] = jax.nn.silu(proj(base, kw)).astype(BF16)
    fr_ref[...] = proj(base + kw, kw)
    iv_ref[...] = proj(base + 2 * kw, kw).astype(BF16)
    og_ref[...] = jax.nn.silu(proj(base + 3 * kw, kw)).astype(BF16)
    base += 4 * kw
    for c in range(gates_ref.shape[1] // kw):
        gates_ref[:, c * kw:(c + 1) * kw] = jax.nn.sigmoid(proj(base + c * kw, kw)).astype(BF16)


def _inproj(x2d, g, w_bf, cw, kw):
    n, d = x2d.shape
    tb = TOKEN_BLOCK
    row = lambda i: (i, 0)
    fixed = lambda i: (0, 0)
    outs = [((n, cw), BF16), ((n, kw), BF16), ((n, kw), F32), ((n, kw), BF16), ((n, kw), BF16), ((n, 2 * d), BF16)]
    return pl.pallas_call(
        functools.partial(_inproj_kernel, cw=cw, kw=kw),
        grid=(n // tb,),
        in_specs=[pl.BlockSpec((tb, d), row), pl.BlockSpec((1, d), fixed), pl.BlockSpec(w_bf.shape, fixed)],
        out_specs=[pl.BlockSpec((tb, s[1]), row) for s, _ in outs],
        out_shape=[jax.ShapeDtypeStruct(s, t) for s, t in outs],
        compiler_params=_params("parallel"),
        name="inproj",
    )(x2d, g, w_bf)


def _conv_kernel(u_ref, cw_ref, cb_ref, lg_ref, lb_ref, wo_ref, gate_ref, y_ref, ext_ref, halo_ref, perm_ref, act_ref):
    tb, c = u_ref.shape
    nt = tb // 8
    slabs = c // 128
    hr = CONV_HALO * 8
    lanes = [slice(l * 128, (l + 1) * 128) for l in range(slabs)]

    @pl.when(pl.program_id(1) == 0)
    def _():
        halo_ref[...] = jnp.zeros_like(halo_ref)

    un = u_ref[...].astype(F32)
    per = nt // 8
    for j in range(nt):
        start = hr + (j % per) * 64 + j // per
        for l in range(slabs):
            ext_ref[l, pl.ds(start, 8, stride=8), :] = un[8 * j:8 * j + 8, lanes[l]]
    first = lax.broadcasted_iota(I32, (hr, 128), 0) % 8 == 0
    for l in range(slabs):
        cur = ext_ref[l, nt * 8:nt * 8 + hr, :]
        ext_ref[l, 0:hr, :] = jnp.where(first, pltpu.roll(halo_ref[l], hr - 7, axis=0), pltpu.roll(cur, 1, axis=0))
        halo_ref[l] = cur

    for r in range(tb // CONV_ROWS):
        accs = []
        for l in range(slabs):
            acc = jnp.broadcast_to(cb_ref[:, lanes[l]], (CONV_ROWS, 128))
            for dt in range(CONV_K):
                off = hr + r * CONV_ROWS - dt * 8
                acc = acc + cw_ref[CONV_K - 1 - dt:CONV_K - dt, lanes[l]] * ext_ref[l, off:off + CONV_ROWS, :]
            accs.append(acc)
        mu = functools.reduce(jnp.add, [jnp.sum(a, axis=-1, keepdims=True) for a in accs]) * (1.0 / c)
        cens = [a - mu for a in accs]
        var = functools.reduce(jnp.add, [jnp.sum(a * a, axis=-1, keepdims=True) for a in cens]) * (1.0 / c)
        inv = lax.rsqrt(var + EPS)
        for l in range(slabs):
            ln = cens[l] * inv * lg_ref[:, lanes[l]] + lb_ref[:, lanes[l]]
            perm_ref[l, r * CONV_ROWS:(r + 1) * CONV_ROWS, :] = jax.nn.silu(ln)
    for j in range(nt):
        start = (j % per) * 64 + j // per
        for l in range(slabs):
            act_ref[8 * j:8 * j + 8, lanes[l]] = perm_ref[l, pl.ds(start, 8, stride=8), :]
    y_ref[...] = (gate_ref[...].astype(F32) * _dot(act_ref[...].astype(BF16), wo_ref[...])).astype(BF16)


def _conv(u, conv_w, conv_b, ln_g, ln_b, w_out_bf, gates, batch):
    n, c = u.shape
    d = w_out_bf.shape[1]
    tb = TOKEN_BLOCK
    nsb = n // batch // tb
    row = lambda b, s: (b * nsb + s, 0)
    fixed = lambda b, s: (0, 0)
    return pl.pallas_call(
        _conv_kernel,
        grid=(batch, nsb),
        in_specs=[pl.BlockSpec((tb, c), row), pl.BlockSpec(conv_w.shape, fixed), pl.BlockSpec((1, c), fixed),
                  pl.BlockSpec((1, c), fixed), pl.BlockSpec((1, c), fixed), pl.BlockSpec((c, d), fixed),
                  pl.BlockSpec((tb, d), row)],
        out_specs=pl.BlockSpec((tb, d), row),
        out_shape=jax.ShapeDtypeStruct((n, d), BF16),
        scratch_shapes=[pltpu.VMEM((c // 128, tb + CONV_HALO * 8, 128), F32),
                        pltpu.VMEM((c // 128, CONV_HALO * 8, 128), F32),
                        pltpu.VMEM((c // 128, tb, 128), F32), pltpu.VMEM((tb, c), F32)],
        compiler_params=_params("arbitrary", "arbitrary"),
        name="conv",
    )(u, conv_w, conv_b, ln_g, ln_b, w_out_bf, gates)


def _hgrn_block(q_ref, fr_ref, iv_ref, lb, st_ref, tri):
    tb = q_ref.shape[0]
    chunks = [slice(c * CHUNK, (c + 1) * CHUNK) for c in range(tb // CHUNK)]
    heads = [slice(h * HGRN_DIM, (h + 1) * HGRN_DIM) for h in range(HGRN_HEADS)]

    qs, vs, kks, cums = [], [], [], []
    for rows in chunks:
        f = lb + (1.0 - lb) * jax.nn.sigmoid(fr_ref[rows, :])
        lf_hi, lf_lo = _split_bf16(jnp.log(f))
        cums.append(_dot(tri, lf_hi) + _dot(tri, lf_lo))
        kks.append(1.0 - f)
        qs.append(q_ref[rows, :].astype(F32))
        vs.append(iv_ref[rows, :])

    qes, kds, decays, blocks = [], [], [], []
    for q, kk, cum in zip(qs, kks, cums):
        last = cum[CHUNK - 1:CHUNK, :]
        qes.append((q * jnp.exp(cum)).astype(BF16))
        kds.append((kk * jnp.exp(last - cum)).astype(BF16))
        decays.append(jnp.exp(last))
        sub = []
        for i in range(CHUNK // SUB):
            rs, ne = i * SUB, (i + 1) * SUB
            ref = cum[rs + SUB // 2 - 1:rs + SUB // 2, :]
            qt = (q[rs:ne] * jnp.exp(cum[rs:ne] - ref)).astype(BF16)
            kt = (kk[0:ne] * jnp.exp(ref - cum[0:ne])).astype(BF16)
            sub.append((qt, kt))
        blocks.append(sub)

    updates = [[_dot_tn(v[:, hs], kd[:, hs]) for hs in heads] for v, kd in zip(vs, kds)]
    scores = [[[_dot_nt(qt[:, hs], kt[:, hs]) for qt, kt in sub] for hs in heads] for sub in blocks]

    states = [st_ref[h] for h in range(HGRN_HEADS)]
    inter = []
    for qe, decay, upd in zip(qes, decays, updates):
        inter.append([_dot_nt(qe[:, hs], st.astype(BF16)) for hs, st in zip(heads, states)])
        states = [st * decay[:, hs] + u for st, hs, u in zip(states, heads, upd)]
    for h in range(HGRN_HEADS):
        st_ref[h] = states[h]

    outs = []
    for v, sc, o_inter in zip(vs, scores, inter):
        per_head = []
        for h, hs in enumerate(heads):
            parts = []
            for i, a in enumerate(sc[h]):
                rs, ne = i * SUB, (i + 1) * SUB
                trow = lax.broadcasted_iota(I32, (SUB, ne), 0) + rs
                scol = lax.broadcasted_iota(I32, (SUB, ne), 1)
                a = jnp.where(scol <= trow, a, 0.0).astype(BF16)
                parts.append(_dot(a, v[0:ne, hs]))
            per_head.append(o_inter[h] + jnp.concatenate(parts, axis=0))
        outs.append(jnp.concatenate(per_head, axis=1))
    return outs


def _hgrn_kernel(q_ref, fr_ref, iv_ref, og_ref, gate_ref, yc_ref, x_ref, lbl_ref, on_ref, wo_ref, wm_ref,
                 x1_ref, st_ref, ob_ref):
    tb = q_ref.shape[0]

    @pl.when(pl.program_id(1) == 0)
    def _():
        st_ref[...] = jnp.zeros_like(st_ref)

    l0, l1 = lbl_ref[0:1, :], lbl_ref[1:2, :]
    m = jnp.maximum(l0, l1)
    e0, e1 = jnp.exp(l0 - m), jnp.exp(l1 - m)
    lb_all = e0 / (e0 + e1)
    trow = lax.broadcasted_iota(I32, (CHUNK, CHUNK), 0)
    tcol = lax.broadcasted_iota(I32, (CHUNK, CHUNK), 1)
    tri = jnp.where(tcol <= trow, 1.0, 0.0).astype(BF16)

    for c, o in enumerate(_hgrn_block(q_ref, fr_ref, iv_ref, lb_all, st_ref, tri)):
        rows = slice(c * CHUNK, (c + 1) * CHUNK)
        og = og_ref[rows, :].astype(F32)
        for h in range(HGRN_HEADS):
            hs = slice(h * HGRN_DIM, (h + 1) * HGRN_DIM)
            ob_ref[rows, hs] = (_rms(o[:, hs], on_ref[...]) * og[:, hs]).astype(BF16)
    y_rec = _dot(ob_ref[...], wo_ref[...])
    merged = yc_ref[...].astype(F32) + gate_ref[...].astype(F32) * y_rec
    x1_ref[...] = x_ref[...] + _dot(merged.astype(BF16), wm_ref[...])


def _hgrn(q, fr, iv, og, gates, yc, x2d, lb_logits, onorm_g, w_o_bf, w_mix_bf, batch):
    n, kw = q.shape
    d = x2d.shape[1]
    tb = TOKEN_BLOCK
    nsb = n // batch // tb
    row = lambda b, s: (b * nsb + s, 0)
    fixed = lambda b, s: (0, 0)
    return pl.pallas_call(
        _hgrn_kernel,
        grid=(batch, nsb),
        in_specs=[pl.BlockSpec((tb, kw), row), pl.BlockSpec((tb, kw), row), pl.BlockSpec((tb, kw), row),
                  pl.BlockSpec((tb, kw), row),
                  pl.BlockSpec((tb, d), lambda b, s: (b * nsb + s, 1)),
                  pl.BlockSpec((tb, d), row), pl.BlockSpec((tb, d), row),
                  pl.BlockSpec(lb_logits.shape, fixed), pl.BlockSpec((1, HGRN_DIM), fixed),
                  pl.BlockSpec(w_o_bf.shape, fixed), pl.BlockSpec(w_mix_bf.shape, fixed)],
        out_specs=pl.BlockSpec((tb, d), row),
        out_shape=jax.ShapeDtypeStruct((n, d), F32),
        scratch_shapes=[pltpu.VMEM((HGRN_HEADS, HGRN_DIM, HGRN_DIM), F32), pltpu.VMEM((tb, kw), BF16)],
        compiler_params=_params("arbitrary", "arbitrary"),
        name="hgrn",
    )(q, fr, iv, og, gates, yc, x2d, lb_logits, onorm_g, w_o_bf, w_mix_bf)


def _memkv_kernel(mem_ref, g_ref, wk_ref, wv_ref, k_ref, v_ref):
    mb = _rms(mem_ref[...], g_ref[...]).astype(BF16)
    k_ref[...] = _dot(mb, wk_ref[...]).astype(BF16)
    v_ref[...] = _dot(mb, wv_ref[...]).astype(BF16)


def _memkv(mem2d, g, wk_bf, wv_bf, batch):
    n, d = mem2d.shape
    m = n // batch
    row = lambda b: (b, 0)
    fixed = lambda b: (0, 0)
    return pl.pallas_call(
        _memkv_kernel,
        grid=(batch,),
        in_specs=[pl.BlockSpec((m, d), row), pl.BlockSpec((1, d), fixed), pl.BlockSpec((d, d), fixed),
                  pl.BlockSpec((d, d), fixed)],
        out_specs=[pl.BlockSpec((m, d), row)] * 2,
        out_shape=[jax.ShapeDtypeStruct((n, d), BF16)] * 2,
        compiler_params=_params("parallel"),
        name="memkv",
    )(mem2d, g, wk_bf, wv_bf)


def _route(lt):
    def row(r):
        return lt[r:r + 1, :]

    gl = [row(g) for g in range(N_GROUPS)]
    gmax = functools.reduce(jnp.maximum, gl)
    g_p = 1.0 / functools.reduce(jnp.add, [jnp.exp(l - gmax) for l in gl])
    gidx = jnp.full(gmax.shape, N_GROUPS - 1, I32)
    for g in range(N_GROUPS - 2, -1, -1):
        gidx = jnp.where(gl[g] == gmax, g, gidx)

    el = []
    for j in range(EXPERTS_PER_GROUP):
        v = row(N_GROUPS + (N_GROUPS - 1) * EXPERTS_PER_GROUP + j)
        for g in range(N_GROUPS - 2, -1, -1):
            v = jnp.where(gidx == g, row(N_GROUPS + g * EXPERTS_PER_GROUP + j), v)
        el.append(v)

    def argmax(vals):
        mx = functools.reduce(jnp.maximum, vals)
        idx = jnp.full(mx.shape, EXPERTS_PER_GROUP - 1, I32)
        for j in range(EXPERTS_PER_GROUP - 2, -1, -1):
            idx = jnp.where(vals[j] == mx, j, idx)
        return mx, idx

    m1, i1 = argmax(el)
    m2, i2 = argmax([jnp.where(i1 == j, -jnp.inf, el[j]) for j in range(EXPERTS_PER_GROUP)])
    r = jnp.exp(m2 - m1)
    w1 = g_p / (1.0 + r)
    w2 = g_p * r / (1.0 + r)
    base = gidx * EXPERTS_PER_GROUP
    return jnp.concatenate([base + i1, base + i2], axis=0), jnp.concatenate([w1, w2], axis=0)


def _excl_cumsum_rows(col):
    r = col.shape[0]
    lower = lax.broadcasted_iota(I32, (r, r), 1) < lax.broadcasted_iota(I32, (r, r), 0)
    lower_bf = jnp.where(lower, 1.0, 0.0).astype(BF16)
    hi, lo = _split_bf16(jnp.broadcast_to(col, (r, 128)))
    return (_dot(lower_bf, hi) + _dot(lower_bf, lo))[:, 0:1]


def _slot_masks(lpos, chunk, tb):
    slot = lax.broadcasted_iota(I32, (SORT_CHUNK, tb), 0) + chunk * SORT_CHUNK
    return slot == lpos[:, 0:tb], slot == lpos[:, tb:2 * tb]


def _earlier_same_expert(ones):
    seg = RANK_SEGMENT
    n_exp, n = ones.shape
    upper = lax.broadcasted_iota(I32, (seg, seg), 0) < lax.broadcasted_iota(I32, (seg, seg), 1)
    pieces = [ones[:, s * seg:(s + 1) * seg] for s in range(n // seg)]
    within = _dot(jnp.concatenate([p.astype(BF16) for p in pieces], axis=0), jnp.where(upper, 1.0, 0.0).astype(BF16))
    seen = jnp.zeros((n_exp, 1), F32)
    out = []
    for s, p in enumerate(pieces):
        out.append(within[s * n_exp:(s + 1) * n_exp, :] + seen)
        seen = seen + jnp.sum(p, axis=1, keepdims=True)
    return jnp.concatenate(out, axis=1), seen


def _local_sort(eid, wts, h2, xb_ref, cnt_ref, lpos_ref):
    tb, d = h2.shape
    na = TOP_K * tb
    e_all = jnp.concatenate([eid[k:k + 1] for k in range(TOP_K)], axis=1)
    onehot = lax.broadcasted_iota(I32, (N_EXPERTS, na), 0) == e_all
    earlier, cnt = _earlier_same_expert(jnp.where(onehot, 1.0, 0.0))
    cnt_pad = jnp.floor((cnt + (GROUP - 1)) * (1.0 / GROUP)) * GROUP
    start = _excl_cumsum_rows(cnt_pad)
    lpos = jnp.sum(jnp.where(onehot, start + earlier, 0.0), axis=0, keepdims=True).astype(I32)
    hb = h2.astype(BF16)
    n_chunks = LOCAL_ROWS // SORT_CHUNK

    def sort_chunk(c):
        m0, m1 = _slot_masks(lpos, c, tb)
        p = jnp.where(m0, 1.0, jnp.where(m1, 1.0, 0.0)).astype(BF16)
        pw = jnp.where(m0, wts[0:1], jnp.where(m1, wts[1:2], 0.0))
        rows = slice(c * SORT_CHUNK, (c + 1) * SORT_CHUNK)
        xb_ref[rows, 0:d] = _dot(p, hb).astype(BF16)
        w_row = jnp.broadcast_to(jnp.sum(pw, axis=1, keepdims=True), (SORT_CHUNK, W_LANES))
        w_hi = w_row.astype(BF16).astype(F32)
        low_half = lax.broadcasted_iota(I32, (SORT_CHUNK, W_LANES), 1) < W_LANES // 2
        xb_ref[rows, d:d + W_LANES] = jnp.where(low_half, w_hi, w_row - w_hi).astype(BF16)

    for c in range(n_chunks - 1):
        sort_chunk(c)
    last_used = jnp.sum(cnt_pad) > (n_chunks - 1) * SORT_CHUNK

    @pl.when(last_used)
    def _():
        sort_chunk(n_chunks - 1)

    @pl.when(jnp.logical_not(last_used))
    def _():
        xb_ref[(n_chunks - 1) * SORT_CHUNK:, :] = jnp.zeros((SORT_CHUNK, d + W_LANES), BF16)

    cnt_ref[...] = cnt_pad.astype(I32)
    lpos_ref[...] = lpos


def _attn_kernel(x1_ref, gxa_ref, wq_ref, k_ref, v_ref, wo_ref, gffn_ref, wr_ref, br_ref,
                 x2_ref, xb_ref, cnt_ref, lpos_ref):
    x1 = x1_ref[...]
    d = x1.shape[1]
    hd = d // XA_HEADS
    q = _dot(_rms(x1, gxa_ref[...]).astype(BF16), wq_ref[...]).astype(BF16)
    heads = []
    for h in range(XA_HEADS):
        hs = slice(h * hd, (h + 1) * hd)
        sc = _dot_nt(q[:, hs], k_ref[:, hs]) * (hd ** -0.5)
        p = jnp.exp(sc - jnp.max(sc, axis=-1, keepdims=True))
        p = p / jnp.sum(p, axis=-1, keepdims=True)
        heads.append(_dot(p.astype(BF16), v_ref[:, hs]).astype(BF16))
    x2 = x1 + _dot(jnp.concatenate(heads, axis=1), wo_ref[...])
    x2_ref[...] = x2
    h2 = _rms(x2, gffn_ref[...])
    h_hi, h_lo = _split_bf16(h2)
    w_hi, w_lo = _split_bf16(wr_ref[...])
    lt = _dot_nt(w_hi, h_hi) + (_dot_nt(w_hi, h_lo) + _dot_nt(w_lo, h_hi)) + br_ref[...]
    eid, wts = _route(lt)
    _local_sort(eid, wts, h2, xb_ref, cnt_ref, lpos_ref)


def _attn(x1, gxa, wq_bf, k_bf, v_bf, wo_bf, gffn, w_route_t, b_route, batch):
    n, d = x1.shape
    m = k_bf.shape[0] // batch
    tb = TOKEN_BLOCK
    nsb = n // batch // tb
    nb = n // tb
    row = lambda b, s: (b * nsb + s, 0)
    blk3 = lambda b, s: (b * nsb + s, 0, 0)
    fixed = lambda b, s: (0, 0)
    mem = lambda b, s: (b, 0)
    return pl.pallas_call(
        _attn_kernel,
        grid=(batch, nsb),
        in_specs=[pl.BlockSpec((tb, d), row), pl.BlockSpec((1, d), fixed), pl.BlockSpec((d, d), fixed),
                  pl.BlockSpec((m, d), mem), pl.BlockSpec((m, d), mem), pl.BlockSpec((d, d), fixed),
                  pl.BlockSpec((1, d), fixed), pl.BlockSpec((ROUTE_ROWS, d), fixed),
                  pl.BlockSpec((ROUTE_ROWS, 1), fixed)],
        out_specs=[pl.BlockSpec((tb, d), row), pl.BlockSpec((LOCAL_ROWS, d + W_LANES), row),
                   pl.BlockSpec((None, N_EXPERTS, 1), blk3), pl.BlockSpec((None, 1, TOP_K * tb), blk3)],
        out_shape=[jax.ShapeDtypeStruct((n, d), F32), jax.ShapeDtypeStruct((nb * LOCAL_ROWS, d + W_LANES), BF16),
                   jax.ShapeDtypeStruct((nb, N_EXPERTS, 1), I32), jax.ShapeDtypeStruct((nb, 1, TOP_K * tb), I32)],
        compiler_params=_params("parallel", "parallel"),
        name="attn",
    )(x1, gxa, wq_bf, k_bf, v_bf, wo_bf, gffn, w_route_t, b_route)


def _plan_kernel(cnt_ref, te_ref, src_ref, dst_ref, nact_ref):
    runs = cnt_ref.shape[0]
    lg = LOCAL_ROWS // GROUP
    tg = MOE_TILE // GROUP
    sh = N_EXPERTS.bit_length() - 1
    emask = N_EXPERTS - 1
    zero_group = lg - 1

    length = cnt_ref[...].astype(F32) * (1.0 / GROUP)
    len_bf = jnp.broadcast_to(length, (runs, 128)).astype(BF16)
    ri = lax.broadcasted_iota(I32, (runs, runs), 0)
    ci = lax.broadcasted_iota(I32, (runs, runs), 1)
    r_e, c_e = ri & emask, ci & emask
    r_b, c_b = lax.shift_right_logical(ri, sh), lax.shift_right_logical(ci, sh)
    same_expert_earlier = jnp.where(r_e == c_e, jnp.where(c_b < r_b, 1.0, 0.0), 0.0).astype(BF16)
    same_block_earlier = jnp.where(r_b == c_b, jnp.where(c_e < r_e, 1.0, 0.0), 0.0).astype(BF16)
    before = _dot(same_expert_earlier, len_bf)[:, 0:1]
    local = _dot(same_block_earlier, len_bf)[:, 0:1]

    of_expert = (lax.broadcasted_iota(I32, (N_EXPERTS, runs), 1) & emask) == lax.broadcasted_iota(
        I32, (N_EXPERTS, runs), 0)
    total = _dot(jnp.where(of_expert, 1.0, 0.0).astype(BF16), len_bf)[:, 0:1]
    tiles = jnp.floor((total + (tg - 1)) * (1.0 / tg))
    tile0 = _excl_cumsum_rows(tiles)
    to_run = (lax.broadcasted_iota(I32, (runs, N_EXPERTS), 0) & emask) == lax.broadcasted_iota(
        I32, (runs, N_EXPERTS), 1)
    to_run_bf = jnp.where(to_run, 1.0, 0.0).astype(BF16)
    base_hi, base_lo = _split_bf16(jnp.broadcast_to(tile0 * tg, (N_EXPERTS, 128)))
    g_start = (_dot(to_run_bf, base_hi) + _dot(to_run_bf, base_lo))[:, 0:1] + before
    block = lax.shift_right_logical(lax.broadcasted_iota(I32, (runs, 1), 0), sh).astype(F32)
    l_start = block * lg + local

    def cover(out_ref, start, offset, default):
        stop = start + length
        shift = offset - default
        n_out = out_ref.shape[1]
        chunk = next(c for c in PLAN_CHUNKS if n_out % c == 0)
        for c in range(n_out // chunk):
            j = (lax.broadcasted_iota(I32, (runs, chunk), 1) + c * chunk).astype(F32)
            hit = jnp.where(start <= j, jnp.where(j < stop, j + shift, 0.0), 0.0)
            out = jnp.sum(hit, axis=0, keepdims=True) + default
            out_ref[:, c * chunk:(c + 1) * chunk] = out.astype(I32)

    cover(src_ref, g_start, l_start - g_start, float(zero_group))
    cover(dst_ref, l_start, g_start - l_start, 0.0)
    t = lax.broadcasted_iota(I32, (N_EXPERTS, te_ref.shape[1]), 1).astype(F32)
    te_ref[...] = jnp.sum(jnp.where(t >= tile0 + tiles, 1.0, 0.0), axis=0, keepdims=True).astype(I32)
    nact_ref[...] = jnp.sum(jnp.broadcast_to(tiles, (N_EXPERTS, 128)), axis=0, keepdims=True).astype(I32)


def _plan(cnt, n_tiles):
    runs = cnt.shape[0]
    n_src = n_tiles * (MOE_TILE // GROUP)
    n_dst = runs // N_EXPERTS * (LOCAL_ROWS // GROUP)
    n_te = -(-n_tiles // 128) * 128
    te, src, dst, nact = pl.pallas_call(
        _plan_kernel,
        out_shape=[jax.ShapeDtypeStruct((1, n_te), I32), jax.ShapeDtypeStruct((1, n_src), I32),
                   jax.ShapeDtypeStruct((1, n_dst), I32), jax.ShapeDtypeStruct((1, 128), I32)],
        compiler_params=_params(),
        name="plan",
    )(cnt)
    return te.reshape(-1), src.reshape(-1), dst.reshape(-1), nact[0, :1]


def _group_copy(src_hbm, src_group, dst_buf, slot, index, sem):
    start = src_group * GROUP
    rows = pl.ds(start if isinstance(start, int) else pl.multiple_of(start, GROUP), GROUP)
    return pltpu.make_async_copy(src_hbm.at[rows], dst_buf.at[slot, pl.ds(index * GROUP, GROUP)], sem.at[slot])


def _experts_kernel(te_ref, src_ref, nact_ref, xb_ref, wg_hbm, wu_hbm, wd_hbm, ys_ref,
                    xbuf, sem, obuf, osem, wg_st, wu_st, wd_st, wsem, wg_bf, wu_bf, wd_bf):
    tg = MOE_TILE // GROUP
    tm, d = obuf.shape[1], obuf.shape[2]
    nact = nact_ref[0]
    n_tiles = ys_ref.shape[0] // tm - 2

    def start_gather(tile, slot):
        for i in range(tg):
            _group_copy(xb_ref, src_ref[tile * tg + i], xbuf, slot, i, sem).start()

    def wait_gather(slot):
        for i in range(tg):
            _group_copy(xb_ref, 0, xbuf, slot, i, sem).wait()

    def weight_copies(expert):
        pairs = ((wg_hbm, wg_st), (wu_hbm, wu_st), (wd_hbm, wd_st))
        return [pltpu.make_async_copy(w.at[expert], st, wsem.at[i]) for i, (w, st) in enumerate(pairs)]

    def out_copy(tile, slot):
        rows = pl.ds(pl.multiple_of(tile * tm, tm), tm)
        return pltpu.make_async_copy(obuf.at[slot], ys_ref.at[rows], osem.at[slot])

    obuf[...] = jnp.zeros_like(obuf)
    for s in range(2):
        out_copy(n_tiles + s, s).start()

    ring = GATHER_AHEAD + 1

    @pl.when(nact > 0)
    def _():
        for a in range(GATHER_AHEAD):
            start_gather(jnp.minimum(a, nact - 1), a)
        for c in weight_copies(te_ref[0]):
            c.start()

    def tile_step(t, carry):
        e = te_ref[t]
        slot = lax.rem(t, 2)
        xslot = lax.rem(t, ring)

        @pl.when(jnp.logical_or(t == 0, e != te_ref[jnp.maximum(t - 1, 0)]))
        def _():
            for c in weight_copies(e):
                c.wait()
            wg_bf[...] = wg_st[...].astype(BF16)
            wu_bf[...] = wu_st[...].astype(BF16)
            wd_bf[...] = wd_st[...].astype(BF16)
            nxt = lax.while_loop(lambda j: jnp.logical_and(j < nact, te_ref[jnp.minimum(j, n_tiles - 1)] == e),
                                 lambda j: j + 1, t + 1)

            @pl.when(nxt < nact)
            def _():
                for c in weight_copies(te_ref[jnp.minimum(nxt, n_tiles - 1)]):
                    c.start(priority=1)

        wait_gather(xslot)
        out_copy(t, slot).wait()
        x = xbuf[xslot, :, 0:d]
        hid = jax.nn.silu(_dot(x, wg_bf[...])) * _dot(x, wu_bf[...])
        start_gather(jnp.minimum(t + GATHER_AHEAD, nact - 1), lax.rem(t + GATHER_AHEAD, ring))
        weight = (xbuf[xslot, :, d:d + 1].astype(F32)
                  + xbuf[xslot, :, d + W_LANES // 2:d + W_LANES // 2 + 1].astype(F32))
        obuf[slot] = (_dot(hid.astype(BF16), wd_bf[...]) * weight).astype(BF16)
        out_copy(t, slot).start()
        return carry

    lax.fori_loop(0, nact, tile_step, 0)

    @pl.when(nact > 0)
    def _():
        for a in range(GATHER_AHEAD):
            wait_gather(lax.rem(nact + a, ring))
    for s in range(2):
        out_copy(0, s).wait()
    obuf[0] = jnp.zeros((tm, d), BF16)

    def zero_start(t, carry):
        out_copy(t, 0).start()
        return carry

    def zero_wait(t, carry):
        out_copy(t, 0).wait()
        return carry

    lax.fori_loop(nact, n_tiles, zero_start, 0)
    lax.fori_loop(nact, n_tiles, zero_wait, 0)


def _experts(te, src, nact, xb, w_gate, w_up, w_down, n_tiles):
    dw = xb.shape[1]
    d = dw - W_LANES
    ff = w_gate.shape[2]
    tm = MOE_TILE
    hbm = pl.BlockSpec(memory_space=pl.ANY)
    return pl.pallas_call(
        _experts_kernel,
        grid_spec=pltpu.PrefetchScalarGridSpec(
            num_scalar_prefetch=3,
            grid=(1,),
            in_specs=[hbm, hbm, hbm, hbm],
            out_specs=hbm,
            scratch_shapes=[pltpu.VMEM((GATHER_AHEAD + 1, tm, dw), BF16), pltpu.SemaphoreType.DMA((GATHER_AHEAD + 1,)),
                            pltpu.VMEM((2, tm, d), BF16), pltpu.SemaphoreType.DMA((2,)),
                            pltpu.VMEM((d, ff), F32), pltpu.VMEM((d, ff), F32), pltpu.VMEM((ff, d), F32),
                            pltpu.SemaphoreType.DMA((3,)),
                            pltpu.VMEM((d, ff), BF16), pltpu.VMEM((d, ff), BF16), pltpu.VMEM((ff, d), BF16)],
        ),
        out_shape=jax.ShapeDtypeStruct(((n_tiles + 2) * tm, d), BF16),
        compiler_params=_params("arbitrary"),
        name="experts",
    )(te, src, nact, xb, w_gate, w_up, w_down)


def _combine_kernel(dst_ref, lpos_ref, x2_ref, g_ref, ys_ref, out_ref, ybuf, sem):
    b = pl.program_id(0)
    tb = x2_ref.shape[0]
    lg = LOCAL_ROWS // GROUP
    slot = lax.rem(b, 2)

    def start_gather(blk):
        for i in range(lg):
            _group_copy(ys_ref, dst_ref[blk * lg + i], ybuf, lax.rem(blk, 2), i, sem).start()

    @pl.when(b == 0)
    def _():
        start_gather(b)

    @pl.when(b + 1 < pl.num_programs(0))
    def _():
        start_gather(b + 1)

    for i in range(lg):
        _group_copy(ys_ref, 0, ybuf, slot, i, sem).wait()
    lpos = lpos_ref[...]
    y = jnp.zeros(x2_ref.shape, F32)
    for c in range(LOCAL_ROWS // SORT_CHUNK):
        m0, m1 = _slot_masks(lpos, c, tb)
        p = jnp.where(m0, 1.0, jnp.where(m1, 1.0, 0.0)).astype(BF16)
        y = y + _dot_tn(p, ybuf[slot, c * SORT_CHUNK:(c + 1) * SORT_CHUNK, :])
    out_ref[...] = _rms(x2_ref[...] + y, g_ref[...])


def _combine(dst, lpos, x2, g_final, ys):
    n, d = x2.shape
    tb = TOKEN_BLOCK
    return pl.pallas_call(
        _combine_kernel,
        grid_spec=pltpu.PrefetchScalarGridSpec(
            num_scalar_prefetch=1,
            grid=(n // tb,),
            in_specs=[pl.BlockSpec((None, 1, TOP_K * tb), lambda i, dst: (i, 0, 0)),
                      pl.BlockSpec((tb, d), lambda i, dst: (i, 0)), pl.BlockSpec((1, d), lambda i, dst: (0, 0)),
                      pl.BlockSpec(memory_space=pl.ANY)],
            out_specs=pl.BlockSpec((tb, d), lambda i, dst: (i, 0)),
            scratch_shapes=[pltpu.VMEM((2, LOCAL_ROWS, d), BF16), pltpu.SemaphoreType.DMA((2,))],
        ),
        out_shape=jax.ShapeDtypeStruct((n, d), F32),
        compiler_params=_params("arbitrary"),
        name="combine",
    )(dst, lpos, x2, g_final, ys)


def kernel(x, mem, norm_mix_g, w_in, conv_w, conv_b, conv_ln_g, conv_ln_b, conv_w_out, hgrn_lb_logits, hgrn_onorm_g, hgrn_w_out, w_mix_out, norm_xa_g, norm_mem_g, xa_w_q, xa_w_k, xa_w_v, xa_w_o, norm_ffn_g, router_group_w, router_group_b, router_expert_w, router_expert_b, moe_w_gate, moe_w_up, moe_w_down, final_norm_g):
    batch, seq, d = x.shape
    n = batch * seq
    assert w_in.shape[0] == 1, "the final RMSNorm is fused into the single layer's combine step"
    cw = conv_w.shape[2]
    kw = hgrn_w_out.shape[1]
    assert kw == HGRN_HEADS * HGRN_DIM and conv_w.shape[1] == CONV_K
    assert seq % TOKEN_BLOCK == 0 and TOKEN_BLOCK % CHUNK == 0 and TOKEN_BLOCK % CONV_ROWS == 0
    assert moe_w_gate.shape[1] == N_EXPERTS and router_group_w.shape[2] == N_GROUPS and TOP_K == 2
    na = TOP_K * TOKEN_BLOCK
    assert LOCAL_ROWS % SORT_CHUNK == 0 and LOCAL_ROWS >= na + N_EXPERTS * (GROUP - 1) + GROUP

    n_tiles = -(-(n // TOKEN_BLOCK) * (na + N_EXPERTS * (GROUP - 1)) // MOE_TILE) + N_EXPERTS
    n_tiles = -(-n_tiles // 16) * 16
    vec = lambda p: p.reshape(1, -1)
    l = 0

    x2d = x.reshape(n, d)
    u, q, fr, iv, og, gates = _inproj(x2d, vec(norm_mix_g[l]), w_in[l].astype(BF16), cw, kw)
    yc = _conv(u, conv_w[l], vec(conv_b[l]), vec(conv_ln_g[l]), vec(conv_ln_b[l]),
               conv_w_out[l].astype(BF16), gates, batch)
    x1 = _hgrn(q, fr, iv, og, gates, yc, x2d, hgrn_lb_logits[l:l + 2], vec(hgrn_onorm_g[l]),
               hgrn_w_out[l].astype(BF16), w_mix_out[l].astype(BF16), batch)
    k_bf, v_bf = _memkv(mem.reshape(-1, d), vec(norm_mem_g[l]), xa_w_k[l].astype(BF16), xa_w_v[l].astype(BF16), batch)
    pad = ROUTE_ROWS - N_GROUPS - N_EXPERTS
    w_route_t = jnp.pad(jnp.concatenate([router_group_w[l], router_expert_w[l]], axis=1).T, ((0, pad), (0, 0)))
    b_route = jnp.pad(jnp.concatenate([router_group_b[l], router_expert_b[l]]), (0, pad)).reshape(ROUTE_ROWS, 1)
    x2, xb, cnt, lpos = _attn(x1, vec(norm_xa_g[l]), xa_w_q[l].astype(BF16), k_bf, v_bf, xa_w_o[l].astype(BF16),
                              vec(norm_ffn_g[l]), w_route_t, b_route, batch)
    te, src, dst, nact = _plan(cnt.reshape(-1, 1), n_tiles)
    ys = _experts(te, src, nact, xb, moe_w_gate[l], moe_w_up[l], moe_w_down[l], n_tiles)
    out = _combine(dst, lpos, x2, vec(final_norm_g), ys)
    return out.reshape(batch, seq, d)
```

```python
import functools

import jax
import jax.numpy as jnp
from jax import lax
from jax.experimental import pallas as pl
from jax.experimental.pallas import tpu as pltpu

F32 = jnp.float32
BF16 = jnp.bfloat16
I32 = jnp.int32

EPS = 1e-6
CONV_K = 31
CONV_HALO = 32
CONV_ROWS = 64
HGRN_HEADS = 4
HGRN_DIM = 128
CHUNK = 64
SUB = 16
XA_HEADS = 4
N_GROUPS = 4
EXPERTS_PER_GROUP = 8
N_EXPERTS = N_GROUPS * EXPERTS_PER_GROUP
TOP_K = 2
ROUTE_ROWS = 40
TOKEN_BLOCK = 512
MOE_TILE = 256
GROUP = 16
LOCAL_ROWS = 1536
SORT_CHUNK = 256
RANK_SEGMENT = 128
GATHER_AHEAD = 3
W_LANES = 128
PLAN_CHUNKS = (512, 256, 128)
VMEM_LIMIT_BYTES = 48 * 1024 * 1024


def _rms(x, g):
    return x * lax.rsqrt(jnp.mean(x * x, axis=-1, keepdims=True) + EPS) * g


def _dot(a, b):
    return jnp.dot(a, b, preferred_element_type=F32)


def _dot_nt(a, b):
    return lax.dot_general(a, b, (((1,), (1,)), ((), ())), preferred_element_type=F32)


def _dot_tn(a, b):
    return lax.dot_general(a, b, (((0,), (0,)), ((), ())), preferred_element_type=F32)


def _split_bf16(x):
    hi = x.astype(BF16)
    lo = (x - hi.astype(F32)).astype(BF16)
    return hi, lo


def _params(*sem):
    return pltpu.CompilerParams(dimension_semantics=sem, vmem_limit_bytes=VMEM_LIMIT_BYTES)


def _inproj_kernel(x_ref, g_ref, w_ref, u_ref, q_ref, fr_ref, iv_ref, og_ref, gates_ref, *, cw, kw):
    hb = _rms(x_ref[...], g_ref[...]).astype(BF16)

    def proj(lo, width):
        return _dot(hb, w_ref[:, lo:lo + width])

    u_ref[...] = (proj(0, cw) * jax.nn.sigmoid(proj(cw, cw))).astype(BF16)
    base = 2 * cw
    q_ref[...] = jax.nn.silu(proj(base, kw)).astype(BF16)
    fr_ref[...] = proj(base + kw, kw)
    iv_ref[...] = proj(base + 2 * kw, kw).astype(BF16)
    og_ref[...] = jax.nn.silu(proj(base + 3 * kw, kw)).astype(BF16)
    base += 4 * kw
    for c in range(gates_ref.shape[1] // kw):
        gates_ref[:, c * kw:(c + 1) * kw] = jax.nn.sigmoid(proj(base + c * kw, kw)).astype(BF16)


def _inproj(x2d, g, w_bf, cw, kw):
    n, d = x2d.shape
    tb = TOKEN_BLOCK
    row = lambda i: (i, 0)
    fixed = lambda i: (0, 0)
    outs = [((n, cw), BF16), ((n, kw), BF16), ((n, kw), F32), ((n, kw), BF16), ((n, kw), BF16), ((n, 2 * d), BF16)]
    return pl.pallas_call(
        functools.partial(_inproj_kernel, cw=cw, kw=kw),
        grid=(n // tb,),
        in_specs=[pl.BlockSpec((tb, d), row), pl.BlockSpec((1, d), fixed), pl.BlockSpec(w_bf.shape, fixed)],
        out_specs=[pl.BlockSpec((tb, s[1]), row) for s, _ in outs],
        out_shape=[jax.ShapeDtypeStruct(s, t) for s, t in outs],
        compiler_params=_params("parallel"),
        name="inproj",
    )(x2d, g, w_bf)


def _conv_kernel(u_ref, cw_ref, cb_ref, lg_ref, lb_ref, wo_ref, gate_ref, y_ref, ext_ref, halo_ref, perm_ref, act_ref):
    tb, c = u_ref.shape
    nt = tb // 8
    slabs = c // 128
    hr = CONV_HALO * 8
    lanes = [slice(l * 128, (l + 1) * 128) for l in range(slabs)]

    @pl.when(pl.program_id(1) == 0)
    def _():
        halo_ref[...] = jnp.zeros_like(halo_ref)

    un = u_ref[...].astype(F32)
    per = nt // 8
    for j in range(nt):
        start = hr + (j % per) * 64 + j // per
        for l in range(slabs):
            ext_ref[l, pl.ds(start, 8, stride=8), :] = un[8 * j:8 * j + 8, lanes[l]]
    first = lax.broadcasted_iota(I32, (hr, 128), 0) % 8 == 0
    for l in range(slabs):
        cur = ext_ref[l, nt * 8:nt * 8 + hr, :]
        ext_ref[l, 0:hr, :] = jnp.where(first, pltpu.roll(halo_ref[l], hr - 7, axis=0), pltpu.roll(cur, 1, axis=0))
        halo_ref[l] = cur

    for r in range(tb // CONV_ROWS):
        accs = []
        for l in range(slabs):
            acc = jnp.broadcast_to(cb_ref[:, lanes[l]], (CONV_ROWS, 128))
            for dt in range(CONV_K):
                off = hr + r * CONV_ROWS - dt * 8
                acc = acc + cw_ref[CONV_K - 1 - dt:CONV_K - dt, lanes[l]] * ext_ref[l, off:off + CONV_ROWS, :]
            accs.append(acc)
        mu = functools.reduce(jnp.add, [jnp.sum(a, axis=-1, keepdims=True) for a in accs]) * (1.0 / c)
        cens = [a - mu for a in accs]
        var = functools.reduce(jnp.add, [jnp.sum(a * a, axis=-1, keepdims=True) for a in cens]) * (1.0 / c)
        inv = lax.rsqrt(var + EPS)
        for l in range(slabs):
            ln = cens[l] * inv * lg_ref[:, lanes[l]] + lb_ref[:, lanes[l]]
            perm_ref[l, r * CONV_ROWS:(r + 1) * CONV_ROWS, :] = jax.nn.silu(ln)
    for j in range(nt):
        start = (j % per) * 64 + j // per
        for l in range(slabs):
            act_ref[8 * j:8 * j + 8, lanes[l]] = perm_ref[l, pl.ds(start, 8, stride=8), :]
    y_ref[...] = (gate_ref[...].astype(F32) * _dot(act_ref[...].astype(BF16), wo_ref[...])).astype(BF16)


def _conv(u, conv_w, conv_b, ln_g, ln_b, w_out_bf, gates, batch):
    n, c = u.shape
    d = w_out_bf.shape[1]
    tb = TOKEN_BLOCK
    nsb = n // batch // tb
    row = lambda b, s: (b * nsb + s, 0)
    fixed = lambda b, s: (0, 0)
    return pl.pallas_call(
        _conv_kernel,
        grid=(batch, nsb),
        in_specs=[pl.BlockSpec((tb, c), row), pl.BlockSpec(conv_w.shape, fixed), pl.BlockSpec((1, c), fixed),
                  pl.BlockSpec((1, c), fixed), pl.BlockSpec((1, c), fixed), pl.BlockSpec((c, d), fixed),
                  pl.BlockSpec((tb, d), row)],
        out_specs=pl.BlockSpec((tb, d), row),
        out_shape=jax.ShapeDtypeStruct((n, d), BF16),
        scratch_shapes=[pltpu.VMEM((c // 128, tb + CONV_HALO * 8, 128), F32),
                        pltpu.VMEM((c // 128, CONV_HALO * 8, 128), F32),
                        pltpu.VMEM((c // 128, tb, 128), F32), pltpu.VMEM((tb, c), F32)],
        compiler_params=_params("arbitrary", "arbitrary"),
        name="conv",
    )(u, conv_w, conv_b, ln_g, ln_b, w_out_bf, gates)


def _hgrn_block(q_ref, fr_ref, iv_ref, lb, st_ref, tri):
    tb = q_ref.shape[0]
    chunks = [slice(c * CHUNK, (c + 1) * CHUNK) for c in range(tb // CHUNK)]
    heads = [slice(h * HGRN_DIM, (h + 1) * HGRN_DIM) for h in range(HGRN_HEADS)]

    qs, vs, kks, cums = [], [], [], []
    for rows in chunks:
        f = lb + (1.0 - lb) * jax.nn.sigmoid(fr_ref[rows, :])
        lf_hi, lf_lo = _split_bf16(jnp.log(f))
        cums.append(_dot(tri, lf_hi) + _dot(tri, lf_lo))
        kks.append(1.0 - f)
        qs.append(q_ref[rows, :].astype(F32))
        vs.append(iv_ref[rows, :])

    qes, kds, decays, blocks = [], [], [], []
    for q, kk, cum in zip(qs, kks, cums):
        last = cum[CHUNK - 1:CHUNK, :]
        qes.append((q * jnp.exp(cum)).astype(BF16))
        kds.append((kk * jnp.exp(last - cum)).astype(BF16))
        decays.append(jnp.exp(last))
        sub = []
        for i in range(CHUNK // SUB):
            rs, ne = i * SUB, (i + 1) * SUB
            ref = cum[rs + SUB // 2 - 1:rs + SUB // 2, :]
            qt = (q[rs:ne] * jnp.exp(cum[rs:ne] - ref)).astype(BF16)
            kt = (kk[0:ne] * jnp.exp(ref - cum[0:ne])).astype(BF16)
            sub.append((qt, kt))
        blocks.append(sub)

    updates = [[_dot_tn(v[:, hs], kd[:, hs]) for hs in heads] for v, kd in zip(vs, kds)]
    scores = [[[_dot_nt(qt[:, hs], kt[:, hs]) for qt, kt in sub] for hs in heads] for sub in blocks]

    states = [st_ref[h] for h in range(HGRN_HEADS)]
    inter = []
    for qe, decay, upd in zip(qes, decays, updates):
        inter.append([_dot_nt(qe[:, hs], st.astype(BF16)) for hs, st in zip(heads, states)])
        states = [st * decay[:, hs] + u for st, hs, u in zip(states, heads, upd)]
    for h in range(HGRN_HEADS):
        st_ref[h] = states[h]

    outs = []
    for v, sc, o_inter in zip(vs, scores, inter):
        per_head = []
        for h, hs in enumerate(heads):
            parts = []
            for i, a in enumerate(sc[h]):
                rs, ne = i * SUB, (i + 1) * SUB
                trow = lax.broadcasted_iota(I32, (SUB, ne), 0) + rs
                scol = lax.broadcasted_iota(I32, (SUB, ne), 1)
                a = jnp.where(scol <= trow, a, 0.0).astype(BF16)
                parts.append(_dot(a, v[0:ne, hs]))
            per_head.append(o_inter[h] + jnp.concatenate(parts, axis=0))
        outs.append(jnp.concatenate(per_head, axis=1))
    return outs


def _hgrn_kernel(q_ref, fr_ref, iv_ref, og_ref, gate_ref, yc_ref, x_ref, lbl_ref, on_ref, wo_ref, wm_ref,
                 x1_ref, st_ref, ob_ref):
    tb = q_ref.shape[0]

    @pl.when(pl.program_id(1) == 0)
    def _():
        st_ref[...] = jnp.zeros_like(st_ref)

    l0, l1 = lbl_ref[0:1, :], lbl_ref[1:2, :]
    m = jnp.maximum(l0, l1)
    e0, e1 = jnp.exp(l0 - m), jnp.exp(l1 - m)
    lb_all = e0 / (e0 + e1)
    trow = lax.broadcasted_iota(I32, (CHUNK, CHUNK), 0)
    tcol = lax.broadcasted_iota(I32, (CHUNK, CHUNK), 1)
    tri = jnp.where(tcol <= trow, 1.0, 0.0).astype(BF16)

    for c, o in enumerate(_hgrn_block(q_ref, fr_ref, iv_ref, lb_all, st_ref, tri)):
        rows = slice(c * CHUNK, (c + 1) * CHUNK)
        og = og_ref[rows, :].astype(F32)
        for h in range(HGRN_HEADS):
            hs = slice(h * HGRN_DIM, (h + 1) * HGRN_DIM)
            ob_ref[rows, hs] = (_rms(o[:, hs], on_ref[...]) * og[:, hs]).astype(BF16)
    y_rec = _dot(ob_ref[...], wo_ref[...])
    merged = yc_ref[...].astype(F32) + gate_ref[...].astype(F32) * y_rec
    x1_ref[...] = x_ref[...] + _dot(merged.astype(BF16), wm_ref[...])


def _hgrn(q, fr, iv, og, gates, yc, x2d, lb_logits, onorm_g, w_o_bf, w_mix_bf, batch):
    n, kw = q.shape
    d = x2d.shape[1]
    tb = TOKEN_BLOCK
    nsb = n // batch // tb
    row = lambda b, s: (b * nsb + s, 0)
    fixed = lambda b, s: (0, 0)
    return pl.pallas_call(
        _hgrn_kernel,
        grid=(batch, nsb),
        in_specs=[pl.BlockSpec((tb, kw), row), pl.BlockSpec((tb, kw), row), pl.BlockSpec((tb, kw), row),
                  pl.BlockSpec((tb, kw), row),
                  pl.BlockSpec((tb, d), lambda b, s: (b * nsb + s, 1)),
                  pl.BlockSpec((tb, d), row), pl.BlockSpec((tb, d), row),
                  pl.BlockSpec(lb_logits.shape, fixed), pl.BlockSpec((1, HGRN_DIM), fixed),
                  pl.BlockSpec(w_o_bf.shape, fixed), pl.BlockSpec(w_mix_bf.shape, fixed)],
        out_specs=pl.BlockSpec((tb, d), row),
        out_shape=jax.ShapeDtypeStruct((n, d), F32),
        scratch_shapes=[pltpu.VMEM((HGRN_HEADS, HGRN_DIM, HGRN_DIM), F32), pltpu.VMEM((tb, kw), BF16)],
        compiler_params=_params("arbitrary", "arbitrary"),
        name="hgrn",
    )(q, fr, iv, og, gates, yc, x2d, lb_logits, onorm_g, w_o_bf, w_mix_bf)


def _memkv_kernel(mem_ref, g_ref, wk_ref, wv_ref, k_ref, v_ref):
    mb = _rms(mem_ref[...], g_ref[...]).astype(BF16)
    k_ref[...] = _dot(mb, wk_ref[...]).astype(BF16)
    v_ref[...] = _dot(mb, wv_ref[...]).astype(BF16)


def _memkv(mem2d, g, wk_bf, wv_bf, batch):
    n, d = mem2d.shape
    m = n // batch
    row = lambda b: (b, 0)
    fixed = lambda b: (0, 0)
    return pl.pallas_call(
        _memkv_kernel,
        grid=(batch,),
        in_specs=[pl.BlockSpec((m, d), row), pl.BlockSpec((1, d), fixed), pl.BlockSpec((d, d), fixed),
                  pl.BlockSpec((d, d), fixed)],
        out_specs=[pl.BlockSpec((m, d), row)] * 2,
        out_shape=[jax.ShapeDtypeStruct((n, d), BF16)] * 2,
        compiler_params=_params("parallel"),
        name="memkv",
    )(mem2d, g, wk_bf, wv_bf)


def _route(lt):
    def row(r):
        return lt[r:r + 1, :]

    gl = [row(g) for g in range(N_GROUPS)]
    gmax = functools.reduce(jnp.maximum, gl)
    g_p = 1.0 / functools.reduce(jnp.add, [jnp.exp(l - gmax) for l in gl])
    gidx = jnp.full(gmax.shape, N_GROUPS - 1, I32)
    for g in range(N_GROUPS - 2, -1, -1):
        gidx = jnp.where(gl[g] == gmax, g, gidx)

    el = []
    for j in range(EXPERTS_PER_GROUP):
        v = row(N_GROUPS + (N_GROUPS - 1) * EXPERTS_PER_GROUP + j)
        for g in range(N_GROUPS - 2, -1, -1):
            v = jnp.where(gidx == g, row(N_GROUPS + g * EXPERTS_PER_GROUP + j), v)
        el.append(v)

    def argmax(vals):
        mx = functools.reduce(jnp.maximum, vals)
        idx = jnp.full(mx.shape, EXPERTS_PER_GROUP - 1, I32)
        for j in range(EXPERTS_PER_GROUP - 2, -1, -1):
            idx = jnp.where(vals[j] == mx, j, idx)
        return mx, idx

    m1, i1 = argmax(el)
    m2, i2 = argmax([jnp.where(i1 == j, -jnp.inf, el[j]) for j in range(EXPERTS_PER_GROUP)])
    r = jnp.exp(m2 - m1)
    w1 = g_p / (1.0 + r)
    w2 = g_p * r / (1.0 + r)
    base = gidx * EXPERTS_PER_GROUP
    return jnp.concatenate([base + i1, base + i2], axis=0), jnp.concatenate([w1, w2], axis=0)


def _excl_cumsum_rows(col):
    r = col.shape[0]
    lower = lax.broadcasted_iota(I32, (r, r), 1) < lax.broadcasted_iota(I32, (r, r), 0)
    lower_bf = jnp.where(lower, 1.0, 0.0).astype(BF16)
    hi, lo = _split_bf16(jnp.broadcast_to(col, (r, 128)))
    return (_dot(lower_bf, hi) + _dot(lower_bf, lo))[:, 0:1]


def _slot_masks(lpos, chunk, tb):
    slot = lax.broadcasted_iota(I32, (SORT_CHUNK, tb), 0) + chunk * SORT_CHUNK
    return slot == lpos[:, 0:tb], slot == lpos[:, tb:2 * tb]


def _earlier_same_expert(ones):
    seg = RANK_SEGMENT
    n_exp, n = ones.shape
    upper = lax.broadcasted_iota(I32, (seg, seg), 0) < lax.broadcasted_iota(I32, (seg, seg), 1)
    pieces = [ones[:, s * seg:(s + 1) * seg] for s in range(n // seg)]
    within = _dot(jnp.concatenate([p.astype(BF16) for p in pieces], axis=0), jnp.where(upper, 1.0, 0.0).astype(BF16))
    seen = jnp.zeros((n_exp, 1), F32)
    out = []
    for s, p in enumerate(pieces):
        out.append(within[s * n_exp:(s + 1) * n_exp, :] + seen)
        seen = seen + jnp.sum(p, axis=1, keepdims=True)
    return jnp.concatenate(out, axis=1), seen


def _local_sort(eid, wts, h2, xb_ref, cnt_ref, lpos_ref):
    tb, d = h2.shape
    na = TOP_K * tb
    e_all = jnp.concatenate([eid[k:k + 1] for k in range(TOP_K)], axis=1)
    onehot = lax.broadcasted_iota(I32, (N_EXPERTS, na), 0) == e_all
    earlier, cnt = _earlier_same_expert(jnp.where(onehot, 1.0, 0.0))
    cnt_pad = jnp.floor((cnt + (GROUP - 1)) * (1.0 / GROUP)) * GROUP
    start = _excl_cumsum_rows(cnt_pad)
    lpos = jnp.sum(jnp.where(onehot, start + earlier, 0.0), axis=0, keepdims=True).astype(I32)
    hb = h2.astype(BF16)
    n_chunks = LOCAL_ROWS // SORT_CHUNK

    def sort_chunk(c):
        m0, m1 = _slot_masks(lpos, c, tb)
        p = jnp.where(m0, 1.0, jnp.where(m1, 1.0, 0.0)).astype(BF16)
        pw = jnp.where(m0, wts[0:1], jnp.where(m1, wts[1:2], 0.0))
        rows = slice(c * SORT_CHUNK, (c + 1) * SORT_CHUNK)
        xb_ref[rows, 0:d] = _dot(p, hb).astype(BF16)
        w_row = jnp.broadcast_to(jnp.sum(pw, axis=1, keepdims=True), (SORT_CHUNK, W_LANES))
        w_hi = w_row.astype(BF16).astype(F32)
        low_half = lax.broadcasted_iota(I32, (SORT_CHUNK, W_LANES), 1) < W_LANES // 2
        xb_ref[rows, d:d + W_LANES] = jnp.where(low_half, w_hi, w_row - w_hi).astype(BF16)

    for c in range(n_chunks - 1):
        sort_chunk(c)
    last_used = jnp.sum(cnt_pad) > (n_chunks - 1) * SORT_CHUNK

    @pl.when(last_used)
    def _():
        sort_chunk(n_chunks - 1)

    @pl.when(jnp.logical_not(last_used))
    def _():
        xb_ref[(n_chunks - 1) * SORT_CHUNK:, :] = jnp.zeros((SORT_CHUNK, d + W_LANES), BF16)

    cnt_ref[...] = cnt_pad.astype(I32)
    lpos_ref[...] = lpos


def _attn_kernel(x1_ref, gxa_ref, wq_ref, k_ref, v_ref, wo_ref, gffn_ref, wr_ref, br_ref,
                 x2_ref, xb_ref, cnt_ref, lpos_ref):
    x1 = x1_ref[...]
    d = x1.shape[1]
    hd = d // XA_HEADS
    q = _dot(_rms(x1, gxa_ref[...]).astype(BF16), wq_ref[...]).astype(BF16)
    heads = []
    for h in range(XA_HEADS):
        hs = slice(h * hd, (h + 1) * hd)
        sc = _dot_nt(q[:, hs], k_ref[:, hs]) * (hd ** -0.5)
        p = jnp.exp(sc - jnp.max(sc, axis=-1, keepdims=True))
        p = p / jnp.sum(p, axis=-1, keepdims=True)
        heads.append(_dot(p.astype(BF16), v_ref[:, hs]).astype(BF16))
    x2 = x1 + _dot(jnp.concatenate(heads, axis=1), wo_ref[...])
    x2_ref[...] = x2
    h2 = _rms(x2, gffn_ref[...])
    h_hi, h_lo = _split_bf16(h2)
    w_hi, w_lo = _split_bf16(wr_ref[...])
    lt = _dot_nt(w_hi, h_hi) + (_dot_nt(w_hi, h_lo) + _dot_nt(w_lo, h_hi)) + br_ref[...]
    eid, wts = _route(lt)
    _local_sort(eid, wts, h2, xb_ref, cnt_ref, lpos_ref)


def _attn(x1, gxa, wq_bf, k_bf, v_bf, wo_bf, gffn, w_route_t, b_route, batch):
    n, d = x1.shape
    m = k_bf.shape[0] // batch
    tb = TOKEN_BLOCK
    nsb = n // batch // tb
    nb = n // tb
    row = lambda b, s: (b * nsb + s, 0)
    blk3 = lambda b, s: (b * nsb + s, 0, 0)
    fixed = lambda b, s: (0, 0)
    mem = lambda b, s: (b, 0)
    return pl.pallas_call(
        _attn_kernel,
        grid=(batch, nsb),
        in_specs=[pl.BlockSpec((tb, d), row), pl.BlockSpec((1, d), fixed), pl.BlockSpec((d, d), fixed),
                  pl.BlockSpec((m, d), mem), pl.BlockSpec((m, d), mem), pl.BlockSpec((d, d), fixed),
                  pl.BlockSpec((1, d), fixed), pl.BlockSpec((ROUTE_ROWS, d), fixed),
                  pl.BlockSpec((ROUTE_ROWS, 1), fixed)],
        out_specs=[pl.BlockSpec((tb, d), row), pl.BlockSpec((LOCAL_ROWS, d + W_LANES), row),
                   pl.BlockSpec((None, N_EXPERTS, 1), blk3), pl.BlockSpec((None, 1, TOP_K * tb), blk3)],
        out_shape=[jax.ShapeDtypeStruct((n, d), F32), jax.ShapeDtypeStruct((nb * LOCAL_ROWS, d + W_LANES), BF16),
                   jax.ShapeDtypeStruct((nb, N_EXPERTS, 1), I32), jax.ShapeDtypeStruct((nb, 1, TOP_K * tb), I32)],
        compiler_params=_params("parallel", "parallel"),
        name="attn",
    )(x1, gxa, wq_bf, k_bf, v_bf, wo_bf, gffn, w_route_t, b_route)


def _plan_kernel(cnt_ref, te_ref, src_ref, dst_ref, nact_ref):
    runs = cnt_ref.shape[0]
    lg = LOCAL_ROWS // GROUP
    tg = MOE_TILE // GROUP
    sh = N_EXPERTS.bit_length() - 1
    emask = N_EXPERTS - 1
    zero_group = lg - 1

    length = cnt_ref[...].astype(F32) * (1.0 / GROUP)
    len_bf = jnp.broadcast_to(length, (runs, 128)).astype(BF16)
    ri = lax.broadcasted_iota(I32, (runs, runs), 0)
    ci = lax.broadcasted_iota(I32, (runs, runs), 1)
    r_e, c_e = ri & emask, ci & emask
    r_b, c_b = lax.shift_right_logical(ri, sh), lax.shift_right_logical(ci, sh)
    same_expert_earlier = jnp.where(r_e == c_e, jnp.where(c_b < r_b, 1.0, 0.0), 0.0).astype(BF16)
    same_block_earlier = jnp.where(r_b == c_b, jnp.where(c_e < r_e, 1.0, 0.0), 0.0).astype(BF16)
    before = _dot(same_expert_earlier, len_bf)[:, 0:1]
    local = _dot(same_block_earlier, len_bf)[:, 0:1]

    of_expert = (lax.broadcasted_iota(I32, (N_EXPERTS, runs), 1) & emask) == lax.broadcasted_iota(
        I32, (N_EXPERTS, runs), 0)
    total = _dot(jnp.where(of_expert, 1.0, 0.0).astype(BF16), len_bf)[:, 0:1]
    tiles = jnp.floor((total + (tg - 1)) * (1.0 / tg))
    tile0 = _excl_cumsum_rows(tiles)
    to_run = (lax.broadcasted_iota(I32, (runs, N_EXPERTS), 0) & emask) == lax.broadcasted_iota(
        I32, (runs, N_EXPERTS), 1)
    to_run_bf = jnp.where(to_run, 1.0, 0.0).astype(BF16)
    base_hi, base_lo = _split_bf16(jnp.broadcast_to(tile0 * tg, (N_EXPERTS, 128)))
    g_start = (_dot(to_run_bf, base_hi) + _dot(to_run_bf, base_lo))[:, 0:1] + before
    block = lax.shift_right_logical(lax.broadcasted_iota(I32, (runs, 1), 0), sh).astype(F32)
    l_start = block * lg + local

    def cover(out_ref, start, offset, default):
        stop = start + length
        shift = offset - default
        n_out = out_ref.shape[1]
        chunk = next(c for c in PLAN_CHUNKS if n_out % c == 0)
        for c in range(n_out // chunk):
            j = (lax.broadcasted_iota(I32, (runs, chunk), 1) + c * chunk).astype(F32)
            hit = jnp.where(start <= j, jnp.where(j < stop, j + shift, 0.0), 0.0)
            out = jnp.sum(hit, axis=0, keepdims=True) + default
            out_ref[:, c * chunk:(c + 1) * chunk] = out.astype(I32)

    cover(src_ref, g_start, l_start - g_start, float(zero_group))
    cover(dst_ref, l_start, g_start - l_start, 0.0)
    t = lax.broadcasted_iota(I32, (N_EXPERTS, te_ref.shape[1]), 1).astype(F32)
    te_ref[...] = jnp.sum(jnp.where(t >= tile0 + tiles, 1.0, 0.0), axis=0, keepdims=True).astype(I32)
    nact_ref[...] = jnp.sum(jnp.broadcast_to(tiles, (N_EXPERTS, 128)), axis=0, keepdims=True).astype(I32)


def _plan(cnt, n_tiles):
    runs = cnt.shape[0]
    n_src = n_tiles * (MOE_TILE // GROUP)
    n_dst = runs // N_EXPERTS * (LOCAL_ROWS // GROUP)
    n_te = -(-n_tiles // 128) * 128
    te, src, dst, nact = pl.pallas_call(
        _plan_kernel,
        out_shape=[jax.ShapeDtypeStruct((1, n_te), I32), jax.ShapeDtypeStruct((1, n_src), I32),
                   jax.ShapeDtypeStruct((1, n_dst), I32), jax.ShapeDtypeStruct((1, 128), I32)],
        compiler_params=_params(),
        name="plan",
    )(cnt)
    return te.reshape(-1), src.reshape(-1), dst.reshape(-1), nact[0, :1]


def _group_copy(src_hbm, src_group, dst_buf, slot, index, sem):
    start = src_group * GROUP
    rows = pl.ds(start if isinstance(start, int) else pl.multiple_of(start, GROUP), GROUP)
    return pltpu.make_async_copy(src_hbm.at[rows], dst_buf.at[slot, pl.ds(index * GROUP, GROUP)], sem.at[slot])


def _experts_kernel(te_ref, src_ref, nact_ref, xb_ref, wg_hbm, wu_hbm, wd_hbm, ys_ref,
                    xbuf, sem, obuf, osem, wg_st, wu_st, wd_st, wsem, wg_bf, wu_bf, wd_bf):
    tg = MOE_TILE // GROUP
    tm, d = obuf.shape[1], obuf.shape[2]
    nact = nact_ref[0]
    n_tiles = ys_ref.shape[0] // tm - 2

    def start_gather(tile, slot):
        for i in range(tg):
            _group_copy(xb_ref, src_ref[tile * tg + i], xbuf, slot, i, sem).start()

    def wait_gather(slot):
        for i in range(tg):
            _group_copy(xb_ref, 0, xbuf, slot, i, sem).wait()

    def weight_copies(expert):
        pairs = ((wg_hbm, wg_st), (wu_hbm, wu_st), (wd_hbm, wd_st))
        return [pltpu.make_async_copy(w.at[expert], st, wsem.at[i]) for i, (w, st) in enumerate(pairs)]

    def out_copy(tile, slot):
        rows = pl.ds(pl.multiple_of(tile * tm, tm), tm)
        return pltpu.make_async_copy(obuf.at[slot], ys_ref.at[rows], osem.at[slot])

    obuf[...] = jnp.zeros_like(obuf)
    for s in range(2):
        out_copy(n_tiles + s, s).start()

    ring = GATHER_AHEAD + 1

    @pl.when(nact > 0)
    def _():
        for a in range(GATHER_AHEAD):
            start_gather(jnp.minimum(a, nact - 1), a)
        for c in weight_copies(te_ref[0]):
            c.start()

    def tile_step(t, carry):
        e = te_ref[t]
        slot = lax.rem(t, 2)
        xslot = lax.rem(t, ring)

        @pl.when(jnp.logical_or(t == 0, e != te_ref[jnp.maximum(t - 1, 0)]))
        def _():
            for c in weight_copies(e):
                c.wait()
            wg_bf[...] = wg_st[...].astype(BF16)
            wu_bf[...] = wu_st[...].astype(BF16)
            wd_bf[...] = wd_st[...].astype(BF16)
            nxt = lax.while_loop(lambda j: jnp.logical_and(j < nact, te_ref[jnp.minimum(j, n_tiles - 1)] == e),
                                 lambda j: j + 1, t + 1)

            @pl.when(nxt < nact)
            def _():
                for c in weight_copies(te_ref[jnp.minimum(nxt, n_tiles - 1)]):
                    c.start(priority=1)

        wait_gather(xslot)
        out_copy(t, slot).wait()
        x = xbuf[xslot, :, 0:d]
        hid = jax.nn.silu(_dot(x, wg_bf[...])) * _dot(x, wu_bf[...])
        start_gather(jnp.minimum(t + GATHER_AHEAD, nact - 1), lax.rem(t + GATHER_AHEAD, ring))
        weight = (xbuf[xslot, :, d:d + 1].astype(F32)
                  + xbuf[xslot, :, d + W_LANES // 2:d + W_LANES // 2 + 1].astype(F32))
        obuf[slot] = (_dot(hid.astype(BF16), wd_bf[...]) * weight).astype(BF16)
        out_copy(t, slot).start()
        return carry

    lax.fori_loop(0, nact, tile_step, 0)

    @pl.when(nact > 0)
    def _():
        for a in range(GATHER_AHEAD):
            wait_gather(lax.rem(nact + a, ring))
    for s in range(2):
        out_copy(0, s).wait()
    obuf[0] = jnp.zeros((tm, d), BF16)

    def zero_start(t, carry):
        out_copy(t, 0).start()
        return carry

    def zero_wait(t, carry):
        out_copy(t, 0).wait()
        return carry

    lax.fori_loop(nact, n_tiles, zero_start, 0)
    lax.fori_loop(nact, n_tiles, zero_wait, 0)


def _experts(te, src, nact, xb, w_gate, w_up, w_down, n_tiles):
    dw = xb.shape[1]
    d = dw - W_LANES
    ff = w_gate.shape[2]
    tm = MOE_TILE
    hbm = pl.BlockSpec(memory_space=pl.ANY)
    return pl.pallas_call(
        _experts_kernel,
        grid_spec=pltpu.PrefetchScalarGridSpec(
            num_scalar_prefetch=3,
            grid=(1,),
            in_specs=[hbm, hbm, hbm, hbm],
            out_specs=hbm,
            scratch_shapes=[pltpu.VMEM((GATHER_AHEAD + 1, tm, dw), BF16), pltpu.SemaphoreType.DMA((GATHER_AHEAD + 1,)),
                            pltpu.VMEM((2, tm, d), BF16), pltpu.SemaphoreType.DMA((2,)),
                            pltpu.VMEM((d, ff), F32), pltpu.VMEM((d, ff), F32), pltpu.VMEM((ff, d), F32),
                            pltpu.SemaphoreType.DMA((3,)),
                            pltpu.VMEM((d, ff), BF16), pltpu.VMEM((d, ff), BF16), pltpu.VMEM((ff, d), BF16)],
        ),
        out_shape=jax.ShapeDtypeStruct(((n_tiles + 2) * tm, d), BF16),
        compiler_params=_params("arbitrary"),
        name="experts",
    )(te, src, nact, xb, w_gate, w_up, w_down)


def _combine_kernel(dst_ref, lpos_ref, x2_ref, g_ref, ys_ref, out_ref, ybuf, sem):
    b = pl.program_id(0)
    tb = x2_ref.shape[0]
    lg = LOCAL_ROWS // GROUP
    nb = dst_ref.shape[0] // lg
    ring = GATHER_AHEAD + 1
    slot = lax.rem(b, ring)

    def start_gather(blk):
        for i in range(lg):
            _group_copy(ys_ref, dst_ref[blk * lg + i], ybuf, lax.rem(blk, ring), i, sem).start()

    @pl.when(b == 0)
    def _():
        for a in range(min(GATHER_AHEAD, nb)):
            start_gather(a)

    @pl.when(b + GATHER_AHEAD < nb)
    def _():
        start_gather(b + GATHER_AHEAD)

    for i in range(lg):
        _group_copy(ys_ref, 0, ybuf, slot, i, sem).wait()
    lpos = lpos_ref[...]
    y = jnp.zeros(x2_ref.shape, F32)
    for c in range(LOCAL_ROWS // SORT_CHUNK):
        m0, m1 = _slot_masks(lpos, c, tb)
        p = jnp.where(m0, 1.0, jnp.where(m1, 1.0, 0.0)).astype(BF16)
        y = y + _dot_tn(p, ybuf[slot, c * SORT_CHUNK:(c + 1) * SORT_CHUNK, :])
    out_ref[...] = _rms(x2_ref[...] + y, g_ref[...])


def _combine(dst, lpos, x2, g_final, ys):
    n, d = x2.shape
    tb = TOKEN_BLOCK
    return pl.pallas_call(
        _combine_kernel,
        grid_spec=pltpu.PrefetchScalarGridSpec(
            num_scalar_prefetch=1,
            grid=(n // tb,),
            in_specs=[pl.BlockSpec((None, 1, TOP_K * tb), lambda i, dst: (i, 0, 0)),
                      pl.BlockSpec((tb, d), lambda i, dst: (i, 0)), pl.BlockSpec((1, d), lambda i, dst: (0, 0)),
                      pl.BlockSpec(memory_space=pl.ANY)],
            out_specs=pl.BlockSpec((tb, d), lambda i, dst: (i, 0)),
            scratch_shapes=[pltpu.VMEM((GATHER_AHEAD + 1, LOCAL_ROWS, d), BF16),
                            pltpu.SemaphoreType.DMA((GATHER_AHEAD + 1,))],
        ),
        out_shape=jax.ShapeDtypeStruct((n, d), F32),
        compiler_params=_params("arbitrary"),
        name="combine",
    )(dst, lpos, x2, g_final, ys)


def kernel(x, mem, norm_mix_g, w_in, conv_w, conv_b, conv_ln_g, conv_ln_b, conv_w_out, hgrn_lb_logits, hgrn_onorm_g, hgrn_w_out, w_mix_out, norm_xa_g, norm_mem_g, xa_w_q, xa_w_k, xa_w_v, xa_w_o, norm_ffn_g, router_group_w, router_group_b, router_expert_w, router_expert_b, moe_w_gate, moe_w_up, moe_w_down, final_norm_g):
    batch, seq, d = x.shape
    n = batch * seq
    assert w_in.shape[0] == 1, "the final RMSNorm is fused into the single layer's combine step"
    cw = conv_w.shape[2]
    kw = hgrn_w_out.shape[1]
    assert kw == HGRN_HEADS * HGRN_DIM and conv_w.shape[1] == CONV_K
    assert seq % TOKEN_BLOCK == 0 and TOKEN_BLOCK % CHUNK == 0 and TOKEN_BLOCK % CONV_ROWS == 0
    assert moe_w_gate.shape[1] == N_EXPERTS and router_group_w.shape[2] == N_GROUPS and TOP_K == 2
    na = TOP_K * TOKEN_BLOCK
    assert LOCAL_ROWS % SORT_CHUNK == 0 and LOCAL_ROWS >= na + N_EXPERTS * (GROUP - 1) + GROUP

    n_tiles = -(-(n // TOKEN_BLOCK) * (na + N_EXPERTS * (GROUP - 1)) // MOE_TILE) + N_EXPERTS
    n_tiles = -(-n_tiles // 16) * 16
    vec = lambda p: p.reshape(1, -1)
    l = 0

    x2d = x.reshape(n, d)
    u, q, fr, iv, og, gates = _inproj(x2d, vec(norm_mix_g[l]), w_in[l].astype(BF16), cw, kw)
    yc = _conv(u, conv_w[l], vec(conv_b[l]), vec(conv_ln_g[l]), vec(conv_ln_b[l]),
               conv_w_out[l].astype(BF16), gates, batch)
    x1 = _hgrn(q, fr, iv, og, gates, yc, x2d, hgrn_lb_logits[l:l + 2], vec(hgrn_onorm_g[l]),
               hgrn_w_out[l].astype(BF16), w_mix_out[l].astype(BF16), batch)
    k_bf, v_bf = _memkv(mem.reshape(-1, d), vec(norm_mem_g[l]), xa_w_k[l].astype(BF16), xa_w_v[l].astype(BF16), batch)
    pad = ROUTE_ROWS - N_GROUPS - N_EXPERTS
    w_route_t = jnp.pad(jnp.concatenate([router_group_w[l], router_expert_w[l]], axis=1).T, ((0, pad), (0, 0)))
    b_route = jnp.pad(jnp.concatenate([router_group_b[l], router_expert_b[l]]), (0, pad)).reshape(ROUTE_ROWS, 1)
    x2, xb, cnt, lpos = _attn(x1, vec(norm_xa_g[l]), xa_w_q[l].astype(BF16), k_bf, v_bf, xa_w_o[l].astype(BF16),
                              vec(norm_ffn_g[l]), w_route_t, b_route, batch)
    te, src, dst, nact = _plan(cnt.reshape(-1, 1), n_tiles)
    ys = _experts(te, src, nact, xb, moe_w_gate[l], moe_w_up[l], moe_w_down[l], n_tiles)
    out = _combine(dst, lpos, x2, vec(final_norm_g), ys)
    return out.reshape(batch, seq, d)
```

```python
import functools

import jax
import jax.numpy as jnp
from jax import lax
from jax.experimental import pallas as pl
from jax.experimental.pallas import tpu as pltpu

F32 = jnp.float32
BF16 = jnp.bfloat16
I32 = jnp.int32

EPS = 1e-6
CONV_K = 31
CONV_HALO = 32
CONV_ROWS = 64
HGRN_HEADS = 4
HGRN_DIM = 128
CHUNK = 64
SUB = 16
XA_HEADS = 4
N_GROUPS = 4
EXPERTS_PER_GROUP = 8
N_EXPERTS = N_GROUPS * EXPERTS_PER_GROUP
TOP_K = 2
ROUTE_ROWS = 40
TOKEN_BLOCK = 512
MOE_TILE = 256
GROUP = 16
LOCAL_ROWS = 1536
SORT_CHUNK = 256
RANK_SEGMENT = 128
GATHER_AHEAD = 3
W_LANES = 128
PLAN_CHUNKS = (512, 256, 128)
VMEM_LIMIT_BYTES = 48 * 1024 * 1024


def _rms(x, g):
    return x * lax.rsqrt(jnp.mean(x * x, axis=-1, keepdims=True) + EPS) * g


def _dot(a, b):
    return jnp.dot(a, b, preferred_element_type=F32)


def _dot_nt(a, b):
    return lax.dot_general(a, b, (((1,), (1,)), ((), ())), preferred_element_type=F32)


def _dot_tn(a, b):
    return lax.dot_general(a, b, (((0,), (0,)), ((), ())), preferred_element_type=F32)


def _split_bf16(x):
    hi = x.astype(BF16)
    lo = (x - hi.astype(F32)).astype(BF16)
    return hi, lo


def _params(*sem):
    return pltpu.CompilerParams(dimension_semantics=sem, vmem_limit_bytes=VMEM_LIMIT_BYTES)


def _conv_stages(un, first, cw_ref, cb_ref, lg_ref, lb_ref, ext_ref, halo_ref, perm_ref, act_ref):
    tb, c = un.shape
    nt = tb // 8
    slabs = c // 128
    hr = CONV_HALO * 8
    lanes = [slice(l * 128, (l + 1) * 128) for l in range(slabs)]

    @pl.when(first)
    def _():
        halo_ref[...] = jnp.zeros_like(halo_ref)

    per = nt // 8
    for j in range(nt):
        start = hr + (j % per) * 64 + j // per
        for l in range(slabs):
            ext_ref[l, pl.ds(start, 8, stride=8), :] = un[8 * j:8 * j + 8, lanes[l]]
    first_row = lax.broadcasted_iota(I32, (hr, 128), 0) % 8 == 0
    for l in range(slabs):
        cur = ext_ref[l, nt * 8:nt * 8 + hr, :]
        ext_ref[l, 0:hr, :] = jnp.where(first_row, pltpu.roll(halo_ref[l], hr - 7, axis=0),
                                        pltpu.roll(cur, 1, axis=0))
        halo_ref[l] = cur

    def row_tile(r):
        accs = []
        for l in range(slabs):
            acc = jnp.broadcast_to(cb_ref[:, lanes[l]], (CONV_ROWS, 128))
            for dt in range(CONV_K):
                off = hr + r * CONV_ROWS - dt * 8
                acc = acc + cw_ref[CONV_K - 1 - dt:CONV_K - dt, lanes[l]] * ext_ref[l, off:off + CONV_ROWS, :]
            accs.append(acc)
        mu = functools.reduce(jnp.add, [jnp.sum(a, axis=-1, keepdims=True) for a in accs]) * (1.0 / c)
        cens = [a - mu for a in accs]
        var = functools.reduce(jnp.add, [jnp.sum(a * a, axis=-1, keepdims=True) for a in cens]) * (1.0 / c)
        inv = lax.rsqrt(var + EPS)
        for l in range(slabs):
            ln = cens[l] * inv * lg_ref[:, lanes[l]] + lb_ref[:, lanes[l]]
            perm_ref[l, r * CONV_ROWS:(r + 1) * CONV_ROWS, :] = jax.nn.silu(ln)

    def finish():
        for j in range(nt):
            start = (j % per) * 64 + j // per
            for l in range(slabs):
                act_ref[8 * j:8 * j + 8, lanes[l]] = perm_ref[l, pl.ds(start, 8, stride=8), :]

    return [functools.partial(row_tile, r) for r in range(tb // CONV_ROWS)], finish


def _mixin_kernel(x_ref, g_ref, w_ref, cw_ref, cb_ref, lg_ref, lb_ref, wo_ref,
                  yc_ref, q_ref, fr_ref, iv_ref, og_ref, gate_ref, ext_ref, halo_ref, perm_ref, act_ref, *, cw, kw):
    d = yc_ref.shape[1]
    hb = _rms(x_ref[...], g_ref[...]).astype(BF16)

    def proj(lo, width):
        return _dot(hb, w_ref[:, lo:lo + width])

    conv_tiles, conv_finish = _conv_stages(proj(0, cw) * jax.nn.sigmoid(proj(cw, cw)), pl.program_id(1) == 0,
                                           cw_ref, cb_ref, lg_ref, lb_ref, ext_ref, halo_ref, perm_ref, act_ref)
    base = 2 * cw
    gbase = base + 4 * kw
    gate_c = []

    def chunk(i):
        if i == 0:
            q_ref[...] = jax.nn.silu(proj(base, kw)).astype(BF16)
        elif i == 1:
            fr_ref[...] = proj(base + kw, kw)
        elif i == 2:
            iv_ref[...] = proj(base + 2 * kw, kw).astype(BF16)
        elif i == 3:
            og_ref[...] = jax.nn.silu(proj(base + 3 * kw, kw)).astype(BF16)
        elif i < 4 + d // kw:
            c = i - 4
            gate_ref[:, c * kw:(c + 1) * kw] = jax.nn.sigmoid(proj(gbase + d + c * kw, kw)).astype(BF16)
        else:
            c = i - 4 - d // kw
            gate_c.append(jax.nn.sigmoid(proj(gbase + c * kw, kw)))

    n_chunks = 4 + 2 * (d // kw)
    for i in range(max(n_chunks, len(conv_tiles))):
        if i < n_chunks:
            chunk(i)
        if i < len(conv_tiles):
            conv_tiles[i]()
    conv_finish()
    y_conv = _dot(act_ref[...].astype(BF16), wo_ref[...])
    for c, gate in enumerate(gate_c):
        cols = slice(c * kw, (c + 1) * kw)
        yc_ref[:, cols] = (gate * y_conv[:, cols]).astype(BF16)


def _mixin(x2d, g, w_bf, conv_w, conv_b, ln_g, ln_b, w_out_bf, cw, kw, batch):
    n, d = x2d.shape
    tb = TOKEN_BLOCK
    nsb = n // batch // tb
    row = lambda b, s: (b * nsb + s, 0)
    fixed = lambda b, s: (0, 0)
    outs = [((n, d), BF16), ((n, kw), BF16), ((n, kw), F32), ((n, kw), BF16), ((n, kw), BF16), ((n, d), BF16)]
    return pl.pallas_call(
        functools.partial(_mixin_kernel, cw=cw, kw=kw),
        grid=(batch, nsb),
        in_specs=[pl.BlockSpec((tb, d), row), pl.BlockSpec((1, d), fixed), pl.BlockSpec(w_bf.shape, fixed),
                  pl.BlockSpec(conv_w.shape, fixed), pl.BlockSpec((1, cw), fixed), pl.BlockSpec((1, cw), fixed),
                  pl.BlockSpec((1, cw), fixed), pl.BlockSpec((cw, d), fixed)],
        out_specs=[pl.BlockSpec((tb, s[1]), row) for s, _ in outs],
        out_shape=[jax.ShapeDtypeStruct(s, t) for s, t in outs],
        scratch_shapes=[pltpu.VMEM((cw // 128, tb + CONV_HALO * 8, 128), F32),
                        pltpu.VMEM((cw // 128, CONV_HALO * 8, 128), F32),
                        pltpu.VMEM((cw // 128, tb, 128), F32), pltpu.VMEM((tb, cw), F32)],
        compiler_params=_params("arbitrary", "arbitrary"),
        name="mixin",
    )(x2d, g, w_bf, conv_w, conv_b, ln_g, ln_b, w_out_bf)


def _hgrn_block(q_ref, fr_ref, iv_ref, lb, st_ref, tri):
    tb = q_ref.shape[0]
    chunks = [slice(c * CHUNK, (c + 1) * CHUNK) for c in range(tb // CHUNK)]
    heads = [slice(h * HGRN_DIM, (h + 1) * HGRN_DIM) for h in range(HGRN_HEADS)]

    qs, vs, kks, cums = [], [], [], []
    for rows in chunks:
        f = lb + (1.0 - lb) * jax.nn.sigmoid(fr_ref[rows, :])
        lf_hi, lf_lo = _split_bf16(jnp.log(f))
        cums.append(_dot(tri, lf_hi) + _dot(tri, lf_lo))
        kks.append(1.0 - f)
        qs.append(q_ref[rows, :].astype(F32))
        vs.append(iv_ref[rows, :])

    qes, kds, decays, blocks = [], [], [], []
    for q, kk, cum in zip(qs, kks, cums):
        last = cum[CHUNK - 1:CHUNK, :]
        qes.append((q * jnp.exp(cum)).astype(BF16))
        kds.append((kk * jnp.exp(last - cum)).astype(BF16))
        decays.append(jnp.exp(last))
        sub = []
        for i in range(CHUNK // SUB):
            rs, ne = i * SUB, (i + 1) * SUB
            ref = cum[rs + SUB // 2 - 1:rs + SUB // 2, :]
            qt = (q[rs:ne] * jnp.exp(cum[rs:ne] - ref)).astype(BF16)
            kt = (kk[0:ne] * jnp.exp(ref - cum[0:ne])).astype(BF16)
            sub.append((qt, kt))
        blocks.append(sub)

    updates = [[_dot_tn(v[:, hs], kd[:, hs]) for hs in heads] for v, kd in zip(vs, kds)]
    scores = [[[_dot_nt(qt[:, hs], kt[:, hs]) for qt, kt in sub] for hs in heads] for sub in blocks]

    states = [st_ref[h] for h in range(HGRN_HEADS)]
    inter = []
    for qe, decay, upd in zip(qes, decays, updates):
        inter.append([_dot_nt(qe[:, hs], st.astype(BF16)) for hs, st in zip(heads, states)])
        states = [st * decay[:, hs] + u for st, hs, u in zip(states, heads, upd)]
    for h in range(HGRN_HEADS):
        st_ref[h] = states[h]

    outs = []
    for v, sc, o_inter in zip(vs, scores, inter):
        per_head = []
        for h, hs in enumerate(heads):
            parts = []
            for i, a in enumerate(sc[h]):
                rs, ne = i * SUB, (i + 1) * SUB
                trow = lax.broadcasted_iota(I32, (SUB, ne), 0) + rs
                scol = lax.broadcasted_iota(I32, (SUB, ne), 1)
                a = jnp.where(scol <= trow, a, 0.0).astype(BF16)
                parts.append(_dot(a, v[0:ne, hs]))
            per_head.append(o_inter[h] + jnp.concatenate(parts, axis=0))
        outs.append(jnp.concatenate(per_head, axis=1))
    return outs


def _hgrn_kernel(q_ref, fr_ref, iv_ref, og_ref, gate_ref, yc_ref, x_ref, lbl_ref, on_ref, wo_ref, wm_ref,
                 x1_ref, st_ref, ob_ref):
    tb = q_ref.shape[0]

    @pl.when(pl.program_id(1) == 0)
    def _():
        st_ref[...] = jnp.zeros_like(st_ref)

    l0, l1 = lbl_ref[0:1, :], lbl_ref[1:2, :]
    m = jnp.maximum(l0, l1)
    e0, e1 = jnp.exp(l0 - m), jnp.exp(l1 - m)
    lb_all = e0 / (e0 + e1)
    trow = lax.broadcasted_iota(I32, (CHUNK, CHUNK), 0)
    tcol = lax.broadcasted_iota(I32, (CHUNK, CHUNK), 1)
    tri = jnp.where(tcol <= trow, 1.0, 0.0).astype(BF16)

    for c, o in enumerate(_hgrn_block(q_ref, fr_ref, iv_ref, lb_all, st_ref, tri)):
        rows = slice(c * CHUNK, (c + 1) * CHUNK)
        og = og_ref[rows, :].astype(F32)
        for h in range(HGRN_HEADS):
            hs = slice(h * HGRN_DIM, (h + 1) * HGRN_DIM)
            ob_ref[rows, hs] = (_rms(o[:, hs], on_ref[...]) * og[:, hs]).astype(BF16)
    y_rec = _dot(ob_ref[...], wo_ref[...])
    merged = yc_ref[...].astype(F32) + gate_ref[...].astype(F32) * y_rec
    x1_ref[...] = x_ref[...] + _dot(merged.astype(BF16), wm_ref[...])


def _hgrn(q, fr, iv, og, gates, yc, x2d, lb_logits, onorm_g, w_o_bf, w_mix_bf, batch):
    n, kw = q.shape
    d = x2d.shape[1]
    tb = TOKEN_BLOCK
    nsb = n // batch // tb
    row = lambda b, s: (b * nsb + s, 0)
    fixed = lambda b, s: (0, 0)
    return pl.pallas_call(
        _hgrn_kernel,
        grid=(batch, nsb),
        in_specs=[pl.BlockSpec((tb, kw), row), pl.BlockSpec((tb, kw), row), pl.BlockSpec((tb, kw), row),
                  pl.BlockSpec((tb, kw), row),
                  pl.BlockSpec((tb, d), row),
                  pl.BlockSpec((tb, d), row), pl.BlockSpec((tb, d), row),
                  pl.BlockSpec(lb_logits.shape, fixed), pl.BlockSpec((1, HGRN_DIM), fixed),
                  pl.BlockSpec(w_o_bf.shape, fixed), pl.BlockSpec(w_mix_bf.shape, fixed)],
        out_specs=pl.BlockSpec((tb, d), row),
        out_shape=jax.ShapeDtypeStruct((n, d), F32),
        scratch_shapes=[pltpu.VMEM((HGRN_HEADS, HGRN_DIM, HGRN_DIM), F32), pltpu.VMEM((tb, kw), BF16)],
        compiler_params=_params("arbitrary", "arbitrary"),
        name="hgrn",
    )(q, fr, iv, og, gates, yc, x2d, lb_logits, onorm_g, w_o_bf, w_mix_bf)


def _memkv_kernel(mem_ref, g_ref, wk_ref, wv_ref, k_ref, v_ref):
    mb = _rms(mem_ref[...], g_ref[...]).astype(BF16)
    k_ref[...] = _dot(mb, wk_ref[...]).astype(BF16)
    v_ref[...] = _dot(mb, wv_ref[...]).astype(BF16)


def _memkv(mem2d, g, wk_bf, wv_bf, batch):
    n, d = mem2d.shape
    m = n // batch
    row = lambda b: (b, 0)
    fixed = lambda b: (0, 0)
    return pl.pallas_call(
        _memkv_kernel,
        grid=(batch,),
        in_specs=[pl.BlockSpec((m, d), row), pl.BlockSpec((1, d), fixed), pl.BlockSpec((d, d), fixed),
                  pl.BlockSpec((d, d), fixed)],
        out_specs=[pl.BlockSpec((m, d), row)] * 2,
        out_shape=[jax.ShapeDtypeStruct((n, d), BF16)] * 2,
        compiler_params=_params("parallel"),
        name="memkv",
    )(mem2d, g, wk_bf, wv_bf)


def _route(lt):
    def row(r):
        return lt[r:r + 1, :]

    gl = [row(g) for g in range(N_GROUPS)]
    gmax = functools.reduce(jnp.maximum, gl)
    g_p = 1.0 / functools.reduce(jnp.add, [jnp.exp(l - gmax) for l in gl])
    gidx = jnp.full(gmax.shape, N_GROUPS - 1, I32)
    for g in range(N_GROUPS - 2, -1, -1):
        gidx = jnp.where(gl[g] == gmax, g, gidx)

    el = []
    for j in range(EXPERTS_PER_GROUP):
        v = row(N_GROUPS + (N_GROUPS - 1) * EXPERTS_PER_GROUP + j)
        for g in range(N_GROUPS - 2, -1, -1):
            v = jnp.where(gidx == g, row(N_GROUPS + g * EXPERTS_PER_GROUP + j), v)
        el.append(v)

    def argmax(vals):
        mx = functools.reduce(jnp.maximum, vals)
        idx = jnp.full(mx.shape, EXPERTS_PER_GROUP - 1, I32)
        for j in range(EXPERTS_PER_GROUP - 2, -1, -1):
            idx = jnp.where(vals[j] == mx, j, idx)
        return mx, idx

    m1, i1 = argmax(el)
    m2, i2 = argmax([jnp.where(i1 == j, -jnp.inf, el[j]) for j in range(EXPERTS_PER_GROUP)])
    r = jnp.exp(m2 - m1)
    w1 = g_p / (1.0 + r)
    w2 = g_p * r / (1.0 + r)
    base = gidx * EXPERTS_PER_GROUP
    return jnp.concatenate([base + i1, base + i2], axis=0), jnp.concatenate([w1, w2], axis=0)


def _excl_cumsum_rows(col):
    r = col.shape[0]
    lower = lax.broadcasted_iota(I32, (r, r), 1) < lax.broadcasted_iota(I32, (r, r), 0)
    lower_bf = jnp.where(lower, 1.0, 0.0).astype(BF16)
    hi, lo = _split_bf16(jnp.broadcast_to(col, (r, 128)))
    return (_dot(lower_bf, hi) + _dot(lower_bf, lo))[:, 0:1]


def _slot_masks(lpos, chunk, tb):
    slot = lax.broadcasted_iota(I32, (SORT_CHUNK, tb), 0) + chunk * SORT_CHUNK
    return slot == lpos[:, 0:tb], slot == lpos[:, tb:2 * tb]


def _earlier_same_expert(ones):
    seg = RANK_SEGMENT
    n_exp, n = ones.shape
    upper = lax.broadcasted_iota(I32, (seg, seg), 0) < lax.broadcasted_iota(I32, (seg, seg), 1)
    pieces = [ones[:, s * seg:(s + 1) * seg] for s in range(n // seg)]
    within = _dot(jnp.concatenate([p.astype(BF16) for p in pieces], axis=0), jnp.where(upper, 1.0, 0.0).astype(BF16))
    seen = jnp.zeros((n_exp, 1), F32)
    out = []
    for s, p in enumerate(pieces):
        out.append(within[s * n_exp:(s + 1) * n_exp, :] + seen)
        seen = seen + jnp.sum(p, axis=1, keepdims=True)
    return jnp.concatenate(out, axis=1), seen


def _local_sort(eid, wts, h2, xb_ref, cnt_ref, lpos_ref):
    tb, d = h2.shape
    na = TOP_K * tb
    e_all = jnp.concatenate([eid[k:k + 1] for k in range(TOP_K)], axis=1)
    onehot = lax.broadcasted_iota(I32, (N_EXPERTS, na), 0) == e_all
    earlier, cnt = _earlier_same_expert(jnp.where(onehot, 1.0, 0.0))
    cnt_pad = jnp.floor((cnt + (GROUP - 1)) * (1.0 / GROUP)) * GROUP
    start = _excl_cumsum_rows(cnt_pad)
    lpos = jnp.sum(jnp.where(onehot, start + earlier, 0.0), axis=0, keepdims=True).astype(I32)
    hb = h2.astype(BF16)
    n_chunks = LOCAL_ROWS // SORT_CHUNK

    def sort_chunk(c):
        m0, m1 = _slot_masks(lpos, c, tb)
        p = jnp.where(m0, 1.0, jnp.where(m1, 1.0, 0.0)).astype(BF16)
        pw = jnp.where(m0, wts[0:1], jnp.where(m1, wts[1:2], 0.0))
        rows = slice(c * SORT_CHUNK, (c + 1) * SORT_CHUNK)
        xb_ref[rows, 0:d] = _dot(p, hb).astype(BF16)
        w_row = jnp.broadcast_to(jnp.sum(pw, axis=1, keepdims=True), (SORT_CHUNK, W_LANES))
        w_hi = w_row.astype(BF16).astype(F32)
        low_half = lax.broadcasted_iota(I32, (SORT_CHUNK, W_LANES), 1) < W_LANES // 2
        xb_ref[rows, d:d + W_LANES] = jnp.where(low_half, w_hi, w_row - w_hi).astype(BF16)

    for c in range(n_chunks - 1):
        sort_chunk(c)
    last_used = jnp.sum(cnt_pad) > (n_chunks - 1) * SORT_CHUNK

    @pl.when(last_used)
    def _():
        sort_chunk(n_chunks - 1)

    @pl.when(jnp.logical_not(last_used))
    def _():
        xb_ref[(n_chunks - 1) * SORT_CHUNK:, :] = jnp.zeros((SORT_CHUNK, d + W_LANES), BF16)

    cnt_ref[...] = cnt_pad.astype(I32)
    lpos_ref[...] = lpos


def _attn_kernel(x1_ref, gxa_ref, wq_ref, k_ref, v_ref, wo_ref, gffn_ref, wr_ref, br_ref,
                 x2_ref, xb_ref, cnt_ref, lpos_ref):
    x1 = x1_ref[...]
    d = x1.shape[1]
    hd = d // XA_HEADS
    q = _dot(_rms(x1, gxa_ref[...]).astype(BF16), wq_ref[...]).astype(BF16)
    heads = []
    for h in range(XA_HEADS):
        hs = slice(h * hd, (h + 1) * hd)
        sc = _dot_nt(q[:, hs], k_ref[:, hs]) * (hd ** -0.5)
        p = jnp.exp(sc - jnp.max(sc, axis=-1, keepdims=True))
        p = p / jnp.sum(p, axis=-1, keepdims=True)
        heads.append(_dot(p.astype(BF16), v_ref[:, hs]).astype(BF16))
    x2 = x1 + _dot(jnp.concatenate(heads, axis=1), wo_ref[...])
    x2_ref[...] = x2
    h2 = _rms(x2, gffn_ref[...])
    h_hi, h_lo = _split_bf16(h2)
    w_hi, w_lo = _split_bf16(wr_ref[...])
    lt = _dot_nt(w_hi, h_hi) + (_dot_nt(w_hi, h_lo) + _dot_nt(w_lo, h_hi)) + br_ref[...]
    eid, wts = _route(lt)
    _local_sort(eid, wts, h2, xb_ref, cnt_ref, lpos_ref)


def _attn(x1, gxa, wq_bf, k_bf, v_bf, wo_bf, gffn, w_route_t, b_route, batch):
    n, d = x1.shape
    m = k_bf.shape[0] // batch
    tb = TOKEN_BLOCK
    nsb = n // batch // tb
    nb = n // tb
    row = lambda b, s: (b * nsb + s, 0)
    blk3 = lambda b, s: (b * nsb + s, 0, 0)
    fixed = lambda b, s: (0, 0)
    mem = lambda b, s: (b, 0)
    return pl.pallas_call(
        _attn_kernel,
        grid=(batch, nsb),
        in_specs=[pl.BlockSpec((tb, d), row), pl.BlockSpec((1, d), fixed), pl.BlockSpec((d, d), fixed),
                  pl.BlockSpec((m, d), mem), pl.BlockSpec((m, d), mem), pl.BlockSpec((d, d), fixed),
                  pl.BlockSpec((1, d), fixed), pl.BlockSpec((ROUTE_ROWS, d), fixed),
                  pl.BlockSpec((ROUTE_ROWS, 1), fixed)],
        out_specs=[pl.BlockSpec((tb, d), row), pl.BlockSpec((LOCAL_ROWS, d + W_LANES), row),
                   pl.BlockSpec((None, N_EXPERTS, 1), blk3), pl.BlockSpec((None, 1, TOP_K * tb), blk3)],
        out_shape=[jax.ShapeDtypeStruct((n, d), F32), jax.ShapeDtypeStruct((nb * LOCAL_ROWS, d + W_LANES), BF16),
                   jax.ShapeDtypeStruct((nb, N_EXPERTS, 1), I32), jax.ShapeDtypeStruct((nb, 1, TOP_K * tb), I32)],
        compiler_params=_params("parallel", "parallel"),
        name="attn",
    )(x1, gxa, wq_bf, k_bf, v_bf, wo_bf, gffn, w_route_t, b_route)


def _plan_kernel(cnt_ref, te_ref, src_ref, dst_ref, nact_ref):
    runs = cnt_ref.shape[0]
    lg = LOCAL_ROWS // GROUP
    tg = MOE_TILE // GROUP
    sh = N_EXPERTS.bit_length() - 1
    emask = N_EXPERTS - 1
    zero_group = lg - 1

    length = cnt_ref[...].astype(F32) * (1.0 / GROUP)
    len_bf = jnp.broadcast_to(length, (runs, 128)).astype(BF16)
    ri = lax.broadcasted_iota(I32, (runs, runs), 0)
    ci = lax.broadcasted_iota(I32, (runs, runs), 1)
    r_e, c_e = ri & emask, ci & emask
    r_b, c_b = lax.shift_right_logical(ri, sh), lax.shift_right_logical(ci, sh)
    same_expert_earlier = jnp.where(r_e == c_e, jnp.where(c_b < r_b, 1.0, 0.0), 0.0).astype(BF16)
    same_block_earlier = jnp.where(r_b == c_b, jnp.where(c_e < r_e, 1.0, 0.0), 0.0).astype(BF16)
    before = _dot(same_expert_earlier, len_bf)[:, 0:1]
    local = _dot(same_block_earlier, len_bf)[:, 0:1]

    of_expert = (lax.broadcasted_iota(I32, (N_EXPERTS, runs), 1) & emask) == lax.broadcasted_iota(
        I32, (N_EXPERTS, runs), 0)
    total = _dot(jnp.where(of_expert, 1.0, 0.0).astype(BF16), len_bf)[:, 0:1]
    tiles = jnp.floor((total + (tg - 1)) * (1.0 / tg))
    tile0 = _excl_cumsum_rows(tiles)
    to_run = (lax.broadcasted_iota(I32, (runs, N_EXPERTS), 0) & emask) == lax.broadcasted_iota(
        I32, (runs, N_EXPERTS), 1)
    to_run_bf = jnp.where(to_run, 1.0, 0.0).astype(BF16)
    base_hi, base_lo = _split_bf16(jnp.broadcast_to(tile0 * tg, (N_EXPERTS, 128)))
    g_start = (_dot(to_run_bf, base_hi) + _dot(to_run_bf, base_lo))[:, 0:1] + before
    block = lax.shift_right_logical(lax.broadcasted_iota(I32, (runs, 1), 0), sh).astype(F32)
    l_start = block * lg + local

    def cover(out_ref, start, offset, default):
        stop = start + length
        shift = offset - default
        n_out = out_ref.shape[1]
        chunk = next(c for c in PLAN_CHUNKS if n_out % c == 0)
        for c in range(n_out // chunk):
            j = (lax.broadcasted_iota(I32, (runs, chunk), 1) + c * chunk).astype(F32)
            hit = jnp.where(start <= j, jnp.where(j < stop, j + shift, 0.0), 0.0)
            out = jnp.sum(hit, axis=0, keepdims=True) + default
            out_ref[:, c * chunk:(c + 1) * chunk] = out.astype(I32)

    cover(src_ref, g_start, l_start - g_start, float(zero_group))
    cover(dst_ref, l_start, g_start - l_start, 0.0)
    t = lax.broadcasted_iota(I32, (N_EXPERTS, te_ref.shape[1]), 1).astype(F32)
    te_ref[...] = jnp.sum(jnp.where(t >= tile0 + tiles, 1.0, 0.0), axis=0, keepdims=True).astype(I32)
    nact_ref[...] = jnp.sum(jnp.broadcast_to(tiles, (N_EXPERTS, 128)), axis=0, keepdims=True).astype(I32)


def _plan(cnt, n_tiles):
    runs = cnt.shape[0]
    n_src = n_tiles * (MOE_TILE // GROUP)
    n_dst = runs // N_EXPERTS * (LOCAL_ROWS // GROUP)
    n_te = -(-n_tiles // 128) * 128
    te, src, dst, nact = pl.pallas_call(
        _plan_kernel,
        out_shape=[jax.ShapeDtypeStruct((1, n_te), I32), jax.ShapeDtypeStruct((1, n_src), I32),
                   jax.ShapeDtypeStruct((1, n_dst), I32), jax.ShapeDtypeStruct((1, 128), I32)],
        compiler_params=_params(),
        name="plan",
    )(cnt)
    return te.reshape(-1), src.reshape(-1), dst.reshape(-1), nact[0, :1]


def _group_copy(src_hbm, src_group, dst_buf, slot, index, sem):
    start = src_group * GROUP
    rows = pl.ds(start if isinstance(start, int) else pl.multiple_of(start, GROUP), GROUP)
    return pltpu.make_async_copy(src_hbm.at[rows], dst_buf.at[slot, pl.ds(index * GROUP, GROUP)], sem.at[slot])


def _experts_kernel(te_ref, src_ref, nact_ref, xb_ref, wg_hbm, wu_hbm, wd_hbm, ys_ref,
                    xbuf, sem, obuf, osem, wg_st, wu_st, wd_st, wsem, wg_bf, wu_bf, wd_bf):
    tg = MOE_TILE // GROUP
    tm, d = obuf.shape[1], obuf.shape[2]
    nact = nact_ref[0]
    n_tiles = ys_ref.shape[0] // tm - 2

    def start_gather(tile, slot):
        for i in range(tg):
            _group_copy(xb_ref, src_ref[tile * tg + i], xbuf, slot, i, sem).start()

    def wait_gather(slot):
        for i in range(tg):
            _group_copy(xb_ref, 0, xbuf, slot, i, sem).wait()

    def weight_copies(expert):
        pairs = ((wg_hbm, wg_st), (wu_hbm, wu_st), (wd_hbm, wd_st))
        return [pltpu.make_async_copy(w.at[expert], st, wsem.at[i]) for i, (w, st) in enumerate(pairs)]

    def out_copy(tile, slot):
        rows = pl.ds(pl.multiple_of(tile * tm, tm), tm)
        return pltpu.make_async_copy(obuf.at[slot], ys_ref.at[rows], osem.at[slot])

    obuf[...] = jnp.zeros_like(obuf)
    for s in range(2):
        out_copy(n_tiles + s, s).start()

    ring = GATHER_AHEAD + 1

    @pl.when(nact > 0)
    def _():
        for a in range(GATHER_AHEAD):
            start_gather(jnp.minimum(a, nact - 1), a)
        for c in weight_copies(te_ref[0]):
            c.start()

    def tile_step(t, carry):
        e = te_ref[t]
        slot = lax.rem(t, 2)
        xslot = lax.rem(t, ring)

        @pl.when(jnp.logical_or(t == 0, e != te_ref[jnp.maximum(t - 1, 0)]))
        def _():
            for c in weight_copies(e):
                c.wait()
            wg_bf[...] = wg_st[...].astype(BF16)
            wu_bf[...] = wu_st[...].astype(BF16)
            wd_bf[...] = wd_st[...].astype(BF16)
            nxt = lax.while_loop(lambda j: jnp.logical_and(j < nact, te_ref[jnp.minimum(j, n_tiles - 1)] == e),
                                 lambda j: j + 1, t + 1)

            @pl.when(nxt < nact)
            def _():
                for c in weight_copies(te_ref[jnp.minimum(nxt, n_tiles - 1)]):
                    c.start(priority=1)

        wait_gather(xslot)
        out_copy(t, slot).wait()
        x = xbuf[xslot, :, 0:d]
        hid = jax.nn.silu(_dot(x, wg_bf[...])) * _dot(x, wu_bf[...])
        start_gather(jnp.minimum(t + GATHER_AHEAD, nact - 1), lax.rem(t + GATHER_AHEAD, ring))
        weight = (xbuf[xslot, :, d:d + 1].astype(F32)
                  + xbuf[xslot, :, d + W_LANES // 2:d + W_LANES // 2 + 1].astype(F32))
        obuf[slot] = (_dot(hid.astype(BF16), wd_bf[...]) * weight).astype(BF16)
        out_copy(t, slot).start()
        return carry

    lax.fori_loop(0, nact, tile_step, 0)

    @pl.when(nact > 0)
    def _():
        for a in range(GATHER_AHEAD):
            wait_gather(lax.rem(nact + a, ring))
    for s in range(2):
        out_copy(0, s).wait()
    obuf[0] = jnp.zeros((tm, d), BF16)

    def zero_start(t, carry):
        out_copy(t, 0).start()
        return carry

    def zero_wait(t, carry):
        out_copy(t, 0).wait()
        return carry

    lax.fori_loop(nact, n_tiles, zero_start, 0)
    lax.fori_loop(nact, n_tiles, zero_wait, 0)


def _experts(te, src, nact, xb, w_gate, w_up, w_down, n_tiles):
    dw = xb.shape[1]
    d = dw - W_LANES
    ff = w_gate.shape[2]
    tm = MOE_TILE
    hbm = pl.BlockSpec(memory_space=pl.ANY)
    return pl.pallas_call(
        _experts_kernel,
        grid_spec=pltpu.PrefetchScalarGridSpec(
            num_scalar_prefetch=3,
            grid=(1,),
            in_specs=[hbm, hbm, hbm, hbm],
            out_specs=hbm,
            scratch_shapes=[pltpu.VMEM((GATHER_AHEAD + 1, tm, dw), BF16), pltpu.SemaphoreType.DMA((GATHER_AHEAD + 1,)),
                            pltpu.VMEM((2, tm, d), BF16), pltpu.SemaphoreType.DMA((2,)),
                            pltpu.VMEM((d, ff), F32), pltpu.VMEM((d, ff), F32), pltpu.VMEM((ff, d), F32),
                            pltpu.SemaphoreType.DMA((3,)),
                            pltpu.VMEM((d, ff), BF16), pltpu.VMEM((d, ff), BF16), pltpu.VMEM((ff, d), BF16)],
        ),
        out_shape=jax.ShapeDtypeStruct(((n_tiles + 2) * tm, d), BF16),
        compiler_params=_params("arbitrary"),
        name="experts",
    )(te, src, nact, xb, w_gate, w_up, w_down)


def _combine_kernel(dst_ref, lpos_ref, x2_ref, g_ref, ys_ref, out_ref, ybuf, sem):
    b = pl.program_id(0)
    tb = x2_ref.shape[0]
    lg = LOCAL_ROWS // GROUP
    nb = dst_ref.shape[0] // lg
    ring = GATHER_AHEAD + 1
    slot = lax.rem(b, ring)

    def start_gather(blk):
        for i in range(lg):
            _group_copy(ys_ref, dst_ref[blk * lg + i], ybuf, lax.rem(blk, ring), i, sem).start()

    @pl.when(b == 0)
    def _():
        for a in range(min(GATHER_AHEAD, nb)):
            start_gather(a)

    @pl.when(b + GATHER_AHEAD < nb)
    def _():
        start_gather(b + GATHER_AHEAD)

    for i in range(lg):
        _group_copy(ys_ref, 0, ybuf, slot, i, sem).wait()
    lpos = lpos_ref[...]
    y = jnp.zeros(x2_ref.shape, F32)
    for c in range(LOCAL_ROWS // SORT_CHUNK):
        m0, m1 = _slot_masks(lpos, c, tb)
        p = jnp.where(m0, 1.0, jnp.where(m1, 1.0, 0.0)).astype(BF16)
        y = y + _dot_tn(p, ybuf[slot, c * SORT_CHUNK:(c + 1) * SORT_CHUNK, :])
    out_ref[...] = _rms(x2_ref[...] + y, g_ref[...])


def _combine(dst, lpos, x2, g_final, ys):
    n, d = x2.shape
    tb = TOKEN_BLOCK
    return pl.pallas_call(
        _combine_kernel,
        grid_spec=pltpu.PrefetchScalarGridSpec(
            num_scalar_prefetch=1,
            grid=(n // tb,),
            in_specs=[pl.BlockSpec((None, 1, TOP_K * tb), lambda i, dst: (i, 0, 0)),
                      pl.BlockSpec((tb, d), lambda i, dst: (i, 0)), pl.BlockSpec((1, d), lambda i, dst: (0, 0)),
                      pl.BlockSpec(memory_space=pl.ANY)],
            out_specs=pl.BlockSpec((tb, d), lambda i, dst: (i, 0)),
            scratch_shapes=[pltpu.VMEM((GATHER_AHEAD + 1, LOCAL_ROWS, d), BF16),
                            pltpu.SemaphoreType.DMA((GATHER_AHEAD + 1,))],
        ),
        out_shape=jax.ShapeDtypeStruct((n, d), F32),
        compiler_params=_params("arbitrary"),
        name="combine",
    )(dst, lpos, x2, g_final, ys)


def kernel(x, mem, norm_mix_g, w_in, conv_w, conv_b, conv_ln_g, conv_ln_b, conv_w_out, hgrn_lb_logits, hgrn_onorm_g, hgrn_w_out, w_mix_out, norm_xa_g, norm_mem_g, xa_w_q, xa_w_k, xa_w_v, xa_w_o, norm_ffn_g, router_group_w, router_group_b, router_expert_w, router_expert_b, moe_w_gate, moe_w_up, moe_w_down, final_norm_g):
    batch, seq, d = x.shape
    n = batch * seq
    assert w_in.shape[0] == 1, "the final RMSNorm is fused into the single layer's combine step"
    cw = conv_w.shape[2]
    kw = hgrn_w_out.shape[1]
    assert kw == HGRN_HEADS * HGRN_DIM and conv_w.shape[1] == CONV_K
    assert seq % TOKEN_BLOCK == 0 and TOKEN_BLOCK % CHUNK == 0 and TOKEN_BLOCK % CONV_ROWS == 0
    assert moe_w_gate.shape[1] == N_EXPERTS and router_group_w.shape[2] == N_GROUPS and TOP_K == 2
    na = TOP_K * TOKEN_BLOCK
    assert LOCAL_ROWS % SORT_CHUNK == 0 and LOCAL_ROWS >= na + N_EXPERTS * (GROUP - 1) + GROUP

    n_tiles = -(-(n // TOKEN_BLOCK) * (na + N_EXPERTS * (GROUP - 1)) // MOE_TILE) + N_EXPERTS
    n_tiles = -(-n_tiles // 16) * 16
    vec = lambda p: p.reshape(1, -1)
    l = 0

    x2d = x.reshape(n, d)
    yc, q, fr, iv, og, gates = _mixin(x2d, vec(norm_mix_g[l]), w_in[l].astype(BF16), conv_w[l], vec(conv_b[l]),
                                      vec(conv_ln_g[l]), vec(conv_ln_b[l]), conv_w_out[l].astype(BF16), cw, kw, batch)
    x1 = _hgrn(q, fr, iv, og, gates, yc, x2d, hgrn_lb_logits[l:l + 2], vec(hgrn_onorm_g[l]),
               hgrn_w_out[l].astype(BF16), w_mix_out[l].astype(BF16), batch)
    k_bf, v_bf = _memkv(mem.reshape(-1, d), vec(norm_mem_g[l]), xa_w_k[l].astype(BF16), xa_w_v[l].astype(BF16), batch)
    pad = ROUTE_ROWS - N_GROUPS - N_EXPERTS
    w_route_t = jnp.pad(jnp.concatenate([router_group_w[l], router_expert_w[l]], axis=1).T, ((0, pad), (0, 0)))
    b_route = jnp.pad(jnp.concatenate([router_group_b[l], router_expert_b[l]]), (0, pad)).reshape(ROUTE_ROWS, 1)
    x2, xb, cnt, lpos = _attn(x1, vec(norm_xa_g[l]), xa_w_q[l].astype(BF16), k_bf, v_bf, xa_w_o[l].astype(BF16),
                              vec(norm_ffn_g[l]), w_route_t, b_route, batch)
    te, src, dst, nact = _plan(cnt.reshape(-1, 1), n_tiles)
    ys = _experts(te, src, nact, xb, moe_w_gate[l], moe_w_up[l], moe_w_down[l], n_tiles)
    out = _combine(dst, lpos, x2, vec(final_norm_g), ys)
    return out.reshape(batch, seq, d)
```

```python
import functools

import jax
import jax.numpy as jnp
from jax import lax
from jax.experimental import pallas as pl
from jax.experimental.pallas import tpu as pltpu

F32 = jnp.float32
BF16 = jnp.bfloat16
I32 = jnp.int32

EPS = 1e-6
CONV_K = 31
CONV_HALO = 32
CONV_ROWS = 64
HGRN_HEADS = 4
HGRN_DIM = 128
CHUNK = 64
SUB = 16
XA_HEADS = 4
N_GROUPS = 4
EXPERTS_PER_GROUP = 8
N_EXPERTS = N_GROUPS * EXPERTS_PER_GROUP
TOP_K = 2
ROUTE_ROWS = 40
TOKEN_BLOCK = 512
MOE_TILE = 256
GROUP = 16
LOCAL_ROWS = 1536
SORT_CHUNK = 256
RANK_SEGMENT = 128
GATHER_AHEAD = 3
W_LANES = 128
PLAN_CHUNKS = (512, 256, 128)
VMEM_LIMIT_BYTES = 48 * 1024 * 1024


def _rms(x, g):
    return x * lax.rsqrt(jnp.mean(x * x, axis=-1, keepdims=True) + EPS) * g


def _dot(a, b):
    return jnp.dot(a, b, preferred_element_type=F32)


def _dot_nt(a, b):
    return lax.dot_general(a, b, (((1,), (1,)), ((), ())), preferred_element_type=F32)


def _dot_tn(a, b):
    return lax.dot_general(a, b, (((0,), (0,)), ((), ())), preferred_element_type=F32)


def _split_bf16(x):
    hi = x.astype(BF16)
    lo = (x - hi.astype(F32)).astype(BF16)
    return hi, lo


def _params(*sem):
    return pltpu.CompilerParams(dimension_semantics=sem, vmem_limit_bytes=VMEM_LIMIT_BYTES)


def _conv_stages(un, first, cw_ref, cb_ref, lg_ref, lb_ref, ext_ref, halo_ref, perm_ref, act_ref):
    tb, c = un.shape
    nt = tb // 8
    slabs = c // 128
    hr = CONV_HALO * 8
    lanes = [slice(l * 128, (l + 1) * 128) for l in range(slabs)]

    @pl.when(first)
    def _():
        halo_ref[...] = jnp.zeros_like(halo_ref)

    per = nt // 8
    for j in range(nt):
        start = hr + (j % per) * 64 + j // per
        for l in range(slabs):
            ext_ref[l, pl.ds(start, 8, stride=8), :] = un[8 * j:8 * j + 8, lanes[l]]
    first_row = lax.broadcasted_iota(I32, (hr, 128), 0) % 8 == 0
    for l in range(slabs):
        cur = ext_ref[l, nt * 8:nt * 8 + hr, :]
        ext_ref[l, 0:hr, :] = jnp.where(first_row, pltpu.roll(halo_ref[l], hr - 7, axis=0),
                                        pltpu.roll(cur, 1, axis=0))
        halo_ref[l] = cur

    def row_tile(r):
        accs = []
        for l in range(slabs):
            acc = jnp.broadcast_to(cb_ref[:, lanes[l]], (CONV_ROWS, 128))
            for dt in range(CONV_K):
                off = hr + r * CONV_ROWS - dt * 8
                acc = acc + cw_ref[CONV_K - 1 - dt:CONV_K - dt, lanes[l]] * ext_ref[l, off:off + CONV_ROWS, :]
            accs.append(acc)
        mu = functools.reduce(jnp.add, [jnp.sum(a, axis=-1, keepdims=True) for a in accs]) * (1.0 / c)
        cens = [a - mu for a in accs]
        var = functools.reduce(jnp.add, [jnp.sum(a * a, axis=-1, keepdims=True) for a in cens]) * (1.0 / c)
        inv = lax.rsqrt(var + EPS)
        for l in range(slabs):
            ln = cens[l] * inv * lg_ref[:, lanes[l]] + lb_ref[:, lanes[l]]
            perm_ref[l, r * CONV_ROWS:(r + 1) * CONV_ROWS, :] = jax.nn.silu(ln)

    def finish():
        for j in range(nt):
            start = (j % per) * 64 + j // per
            for l in range(slabs):
                act_ref[8 * j:8 * j + 8, lanes[l]] = perm_ref[l, pl.ds(start, 8, stride=8), :]

    return [functools.partial(row_tile, r) for r in range(tb // CONV_ROWS)], finish


def _mixin_kernel(x_ref, g_ref, w_hbm, cw_ref, cb_ref, lg_ref, lb_ref, wo_ref,
                  yc_ref, q_ref, fr_ref, iv_ref, og_ref, gate_ref,
                  w_ref, w_stage, w_sem, ext_ref, halo_ref, perm_ref, act_ref, *, cw, kw):
    d = yc_ref.shape[1]

    @pl.when(jnp.logical_and(pl.program_id(0) == 0, pl.program_id(1) == 0))
    def _():
        width = w_stage.shape[2]

        def fetch(c):
            return pltpu.make_async_copy(w_hbm.at[:, pl.ds(c * width, width)], w_stage.at[c % 2], w_sem.at[c % 2])

        n_fetch = w_ref.shape[1] // width
        fetch(0).start()
        for c in range(n_fetch):
            if c + 1 < n_fetch:
                fetch(c + 1).start()
            fetch(c).wait()
            w_ref[:, c * width:(c + 1) * width] = w_stage[c % 2].astype(BF16)

    hb = _rms(x_ref[...], g_ref[...]).astype(BF16)

    def proj(lo, width):
        return _dot(hb, w_ref[:, lo:lo + width])

    conv_tiles, conv_finish = _conv_stages(proj(0, cw) * jax.nn.sigmoid(proj(cw, cw)), pl.program_id(1) == 0,
                                           cw_ref, cb_ref, lg_ref, lb_ref, ext_ref, halo_ref, perm_ref, act_ref)
    def chunk(ref, col, lo, act):
        def run():
            ref[:, col * kw:(col + 1) * kw] = act(proj(lo, kw)).astype(ref.dtype)
        return run

    base = 2 * cw
    gbase = base + 4 * kw
    chunks = [chunk(q_ref, 0, base, jax.nn.silu), chunk(fr_ref, 0, base + kw, lambda v: v),
              chunk(iv_ref, 0, base + 2 * kw, lambda v: v), chunk(og_ref, 0, base + 3 * kw, jax.nn.silu)]
    for c in range(d // kw):
        chunks.append(chunk(gate_ref, c, gbase + d + c * kw, jax.nn.sigmoid))
        chunks.append(chunk(yc_ref, c, gbase + c * kw, jax.nn.sigmoid))
    for i in range(max(len(chunks), len(conv_tiles))):
        if i < len(chunks):
            chunks[i]()
        if i < len(conv_tiles):
            conv_tiles[i]()
    conv_finish()
    y_conv = _dot(act_ref[...].astype(BF16), wo_ref[...])
    yc_ref[...] = (yc_ref[...].astype(F32) * y_conv).astype(BF16)


def _mixin(x2d, g, w_in, conv_w, conv_b, ln_g, ln_b, w_out_bf, cw, kw, batch):
    n, d = x2d.shape
    tb = TOKEN_BLOCK
    nsb = n // batch // tb
    row = lambda b, s: (b * nsb + s, 0)
    fixed = lambda b, s: (0, 0)
    outs = [((n, d), BF16), ((n, kw), BF16), ((n, kw), F32), ((n, kw), BF16), ((n, kw), BF16), ((n, d), BF16)]
    return pl.pallas_call(
        functools.partial(_mixin_kernel, cw=cw, kw=kw),
        grid=(batch, nsb),
        in_specs=[pl.BlockSpec((tb, d), row), pl.BlockSpec((1, d), fixed), pl.BlockSpec(memory_space=pl.ANY),
                  pl.BlockSpec(conv_w.shape, fixed), pl.BlockSpec((1, cw), fixed), pl.BlockSpec((1, cw), fixed),
                  pl.BlockSpec((1, cw), fixed), pl.BlockSpec((cw, d), fixed)],
        out_specs=[pl.BlockSpec((tb, s[1]), row) for s, _ in outs],
        out_shape=[jax.ShapeDtypeStruct(s, t) for s, t in outs],
        scratch_shapes=[pltpu.VMEM(w_in.shape, BF16), pltpu.VMEM((2, d, kw), F32), pltpu.SemaphoreType.DMA((2,)),
                        pltpu.VMEM((cw // 128, tb + CONV_HALO * 8, 128), F32),
                        pltpu.VMEM((cw // 128, CONV_HALO * 8, 128), F32),
                        pltpu.VMEM((cw // 128, tb, 128), F32), pltpu.VMEM((tb, cw), F32)],
        compiler_params=_params("arbitrary", "arbitrary"),
        name="mixin",
    )(x2d, g, w_in, conv_w, conv_b, ln_g, ln_b, w_out_bf)


def _hgrn_block(q_ref, fr_ref, iv_ref, lb, st_ref, tri):
    tb = q_ref.shape[0]
    chunks = [slice(c * CHUNK, (c + 1) * CHUNK) for c in range(tb // CHUNK)]
    heads = [slice(h * HGRN_DIM, (h + 1) * HGRN_DIM) for h in range(HGRN_HEADS)]

    qs, vs, kks, cums = [], [], [], []
    for rows in chunks:
        f = lb + (1.0 - lb) * jax.nn.sigmoid(fr_ref[rows, :])
        lf_hi, lf_lo = _split_bf16(jnp.log(f))
        cums.append(_dot(tri, jnp.concatenate([lf_hi, lf_lo], axis=0)))
        kks.append(1.0 - f)
        qs.append(q_ref[rows, :].astype(F32))
        vs.append(iv_ref[rows, :])

    qes, kds, decays, blocks = [], [], [], []
    for q, kk, cum in zip(qs, kks, cums):
        last = cum[CHUNK - 1:CHUNK, :]
        qes.append((q * jnp.exp(cum)).astype(BF16))
        kds.append((kk * jnp.exp(last - cum)).astype(BF16))
        decays.append(jnp.exp(last))
        sub = []
        for i in range(CHUNK // SUB):
            rs, ne = i * SUB, (i + 1) * SUB
            ref = cum[rs + SUB // 2 - 1:rs + SUB // 2, :]
            qt = (q[rs:ne] * jnp.exp(cum[rs:ne] - ref)).astype(BF16)
            kt = (kk[0:ne] * jnp.exp(ref - cum[0:ne])).astype(BF16)
            sub.append((qt, kt))
        blocks.append(sub)

    updates = [[_dot_tn(v[:, hs], kd[:, hs]) for hs in heads] for v, kd in zip(vs, kds)]
    scores = [[[_dot_nt(qt[:, hs], kt[:, hs]) for qt, kt in sub] for hs in heads] for sub in blocks]

    states = [st_ref[h] for h in range(HGRN_HEADS)]
    inter = []
    for qe, decay, upd in zip(qes, decays, updates):
        inter.append([_dot_nt(qe[:, hs], st.astype(BF16)) for hs, st in zip(heads, states)])
        states = [st * decay[:, hs] + u for st, hs, u in zip(states, heads, upd)]
    for h in range(HGRN_HEADS):
        st_ref[h] = states[h]

    outs = []
    for v, sc, o_inter in zip(vs, scores, inter):
        per_head = []
        for h, hs in enumerate(heads):
            parts = []
            for i, a in enumerate(sc[h]):
                rs, ne = i * SUB, (i + 1) * SUB
                trow = lax.broadcasted_iota(I32, (SUB, ne), 0) + rs
                scol = lax.broadcasted_iota(I32, (SUB, ne), 1)
                a = jnp.where(scol <= trow, a, 0.0).astype(BF16)
                parts.append(_dot(a, v[0:ne, hs]))
            per_head.append(o_inter[h] + jnp.concatenate(parts, axis=0))
        outs.append(jnp.concatenate(per_head, axis=1))
    return outs


def _hgrn_kernel(q_ref, fr_ref, iv_ref, og_ref, gate_ref, yc_ref, x_ref, lbl_ref, on_ref, wo_ref, wm_ref,
                 x1_ref, st_ref, ob_ref):
    tb = q_ref.shape[0]

    @pl.when(pl.program_id(1) == 0)
    def _():
        st_ref[...] = jnp.zeros_like(st_ref)

    l0, l1 = lbl_ref[0:1, :], lbl_ref[1:2, :]
    m = jnp.maximum(l0, l1)
    e0, e1 = jnp.exp(l0 - m), jnp.exp(l1 - m)
    lb_all = e0 / (e0 + e1)
    trow = lax.broadcasted_iota(I32, (CHUNK, 2 * CHUNK), 0)
    tcol = lax.broadcasted_iota(I32, (CHUNK, 2 * CHUNK), 1) % CHUNK
    tri = jnp.where(tcol <= trow, 1.0, 0.0).astype(BF16)

    for c, o in enumerate(_hgrn_block(q_ref, fr_ref, iv_ref, lb_all, st_ref, tri)):
        rows = slice(c * CHUNK, (c + 1) * CHUNK)
        og = og_ref[rows, :].astype(F32)
        for h in range(HGRN_HEADS):
            hs = slice(h * HGRN_DIM, (h + 1) * HGRN_DIM)
            ob_ref[rows, hs] = (_rms(o[:, hs], on_ref[...]) * og[:, hs]).astype(BF16)
    y_rec = _dot(ob_ref[...], wo_ref[...])
    merged = yc_ref[...].astype(F32) + gate_ref[...].astype(F32) * y_rec
    x1_ref[...] = x_ref[...] + _dot(merged.astype(BF16), wm_ref[...])


def _hgrn(q, fr, iv, og, gates, yc, x2d, lb_logits, onorm_g, w_o_bf, w_mix_bf, batch):
    n, kw = q.shape
    d = x2d.shape[1]
    tb = TOKEN_BLOCK
    nsb = n // batch // tb
    row = lambda b, s: (b * nsb + s, 0)
    fixed = lambda b, s: (0, 0)
    return pl.pallas_call(
        _hgrn_kernel,
        grid=(batch, nsb),
        in_specs=[pl.BlockSpec((tb, kw), row), pl.BlockSpec((tb, kw), row), pl.BlockSpec((tb, kw), row),
                  pl.BlockSpec((tb, kw), row),
                  pl.BlockSpec((tb, d), row),
                  pl.BlockSpec((tb, d), row), pl.BlockSpec((tb, d), row),
                  pl.BlockSpec(lb_logits.shape, fixed), pl.BlockSpec((1, HGRN_DIM), fixed),
                  pl.BlockSpec(w_o_bf.shape, fixed), pl.BlockSpec(w_mix_bf.shape, fixed)],
        out_specs=pl.BlockSpec((tb, d), row),
        out_shape=jax.ShapeDtypeStruct((n, d), F32),
        scratch_shapes=[pltpu.VMEM((HGRN_HEADS, HGRN_DIM, HGRN_DIM), F32), pltpu.VMEM((tb, kw), BF16)],
        compiler_params=_params("arbitrary", "arbitrary"),
        name="hgrn",
    )(q, fr, iv, og, gates, yc, x2d, lb_logits, onorm_g, w_o_bf, w_mix_bf)


def _memkv_kernel(mem_ref, g_ref, wk_ref, wv_ref, k_ref, v_ref):
    mb = _rms(mem_ref[...], g_ref[...]).astype(BF16)
    k_ref[...] = _dot(mb, wk_ref[...]).astype(BF16)
    v_ref[...] = _dot(mb, wv_ref[...]).astype(BF16)


def _memkv(mem2d, g, wk_bf, wv_bf, batch):
    n, d = mem2d.shape
    m = n // batch
    row = lambda b: (b, 0)
    fixed = lambda b: (0, 0)
    return pl.pallas_call(
        _memkv_kernel,
        grid=(batch,),
        in_specs=[pl.BlockSpec((m, d), row), pl.BlockSpec((1, d), fixed), pl.BlockSpec((d, d), fixed),
                  pl.BlockSpec((d, d), fixed)],
        out_specs=[pl.BlockSpec((m, d), row)] * 2,
        out_shape=[jax.ShapeDtypeStruct((n, d), BF16)] * 2,
        compiler_params=_params("parallel"),
        name="memkv",
    )(mem2d, g, wk_bf, wv_bf)


def _route(lt):
    def row(r):
        return lt[r:r + 1, :]

    gl = [row(g) for g in range(N_GROUPS)]
    gmax = functools.reduce(jnp.maximum, gl)
    g_p = 1.0 / functools.reduce(jnp.add, [jnp.exp(l - gmax) for l in gl])
    gidx = jnp.full(gmax.shape, N_GROUPS - 1, I32)
    for g in range(N_GROUPS - 2, -1, -1):
        gidx = jnp.where(gl[g] == gmax, g, gidx)

    el = []
    for j in range(EXPERTS_PER_GROUP):
        v = row(N_GROUPS + (N_GROUPS - 1) * EXPERTS_PER_GROUP + j)
        for g in range(N_GROUPS - 2, -1, -1):
            v = jnp.where(gidx == g, row(N_GROUPS + g * EXPERTS_PER_GROUP + j), v)
        el.append(v)

    def argmax(vals):
        mx = functools.reduce(jnp.maximum, vals)
        idx = jnp.full(mx.shape, EXPERTS_PER_GROUP - 1, I32)
        for j in range(EXPERTS_PER_GROUP - 2, -1, -1):
            idx = jnp.where(vals[j] == mx, j, idx)
        return mx, idx

    m1, i1 = argmax(el)
    m2, i2 = argmax([jnp.where(i1 == j, -jnp.inf, el[j]) for j in range(EXPERTS_PER_GROUP)])
    r = jnp.exp(m2 - m1)
    w1 = g_p / (1.0 + r)
    w2 = g_p * r / (1.0 + r)
    base = gidx * EXPERTS_PER_GROUP
    return jnp.concatenate([base + i1, base + i2], axis=0), jnp.concatenate([w1, w2], axis=0)


def _excl_cumsum_rows(col):
    r = col.shape[0]
    lower = lax.broadcasted_iota(I32, (r, r), 1) < lax.broadcasted_iota(I32, (r, r), 0)
    lower_bf = jnp.where(lower, 1.0, 0.0).astype(BF16)
    hi, lo = _split_bf16(jnp.broadcast_to(col, (r, 128)))
    return (_dot(lower_bf, hi) + _dot(lower_bf, lo))[:, 0:1]


def _slot_masks(lpos, chunk, tb):
    slot = lax.broadcasted_iota(I32, (SORT_CHUNK, tb), 0) + chunk * SORT_CHUNK
    return slot == lpos[:, 0:tb], slot == lpos[:, tb:2 * tb]


def _earlier_same_expert(ones):
    seg = RANK_SEGMENT
    n_exp, n = ones.shape
    upper = lax.broadcasted_iota(I32, (seg, seg), 0) < lax.broadcasted_iota(I32, (seg, seg), 1)
    pieces = [ones[:, s * seg:(s + 1) * seg] for s in range(n // seg)]
    within = _dot(jnp.concatenate([p.astype(BF16) for p in pieces], axis=0), jnp.where(upper, 1.0, 0.0).astype(BF16))
    seen = jnp.zeros((n_exp, 1), F32)
    out = []
    for s, p in enumerate(pieces):
        out.append(within[s * n_exp:(s + 1) * n_exp, :] + seen)
        seen = seen + jnp.sum(p, axis=1, keepdims=True)
    return jnp.concatenate(out, axis=1), seen


def _local_sort(eid, wts, h2, xb_ref, cnt_ref, lpos_ref):
    tb, d = h2.shape
    na = TOP_K * tb
    e_all = jnp.concatenate([eid[k:k + 1] for k in range(TOP_K)], axis=1)
    onehot = lax.broadcasted_iota(I32, (N_EXPERTS, na), 0) == e_all
    earlier, cnt = _earlier_same_expert(jnp.where(onehot, 1.0, 0.0))
    cnt_pad = jnp.floor((cnt + (GROUP - 1)) * (1.0 / GROUP)) * GROUP
    start = _excl_cumsum_rows(cnt_pad)
    lpos = jnp.sum(jnp.where(onehot, start + earlier, 0.0), axis=0, keepdims=True).astype(I32)
    hb = h2.astype(BF16)
    n_chunks = LOCAL_ROWS // SORT_CHUNK

    def sort_chunk(c):
        m0, m1 = _slot_masks(lpos, c, tb)
        p = jnp.where(m0, 1.0, jnp.where(m1, 1.0, 0.0)).astype(BF16)
        pw = jnp.where(m0, wts[0:1], jnp.where(m1, wts[1:2], 0.0))
        rows = slice(c * SORT_CHUNK, (c + 1) * SORT_CHUNK)
        xb_ref[rows, 0:d] = _dot(p, hb).astype(BF16)
        w_row = jnp.broadcast_to(jnp.sum(pw, axis=1, keepdims=True), (SORT_CHUNK, W_LANES))
        w_hi = w_row.astype(BF16).astype(F32)
        low_half = lax.broadcasted_iota(I32, (SORT_CHUNK, W_LANES), 1) < W_LANES // 2
        xb_ref[rows, d:d + W_LANES] = jnp.where(low_half, w_hi, w_row - w_hi).astype(BF16)

    for c in range(n_chunks - 1):
        sort_chunk(c)
    last_used = jnp.sum(cnt_pad) > (n_chunks - 1) * SORT_CHUNK

    @pl.when(last_used)
    def _():
        sort_chunk(n_chunks - 1)

    @pl.when(jnp.logical_not(last_used))
    def _():
        xb_ref[(n_chunks - 1) * SORT_CHUNK:, :] = jnp.zeros((SORT_CHUNK, d + W_LANES), BF16)

    cnt_ref[...] = cnt_pad.astype(I32)
    lpos_ref[...] = lpos


def _attn_kernel(x1_ref, gxa_ref, wq_ref, k_ref, v_ref, wo_ref, gffn_ref, wr_ref, br_ref,
                 x2_ref, xb_ref, cnt_ref, lpos_ref):
    x1 = x1_ref[...]
    d = x1.shape[1]
    hd = d // XA_HEADS
    q = _dot(_rms(x1, gxa_ref[...]).astype(BF16), wq_ref[...]).astype(BF16)
    heads = []
    for h in range(XA_HEADS):
        hs = slice(h * hd, (h + 1) * hd)
        sc = _dot_nt(q[:, hs], k_ref[:, hs]) * (hd ** -0.5)
        p = jnp.exp(sc - jnp.max(sc, axis=-1, keepdims=True))
        p = p / jnp.sum(p, axis=-1, keepdims=True)
        heads.append(_dot(p.astype(BF16), v_ref[:, hs]).astype(BF16))
    x2 = x1 + _dot(jnp.concatenate(heads, axis=1), wo_ref[...])
    x2_ref[...] = x2
    h2 = _rms(x2, gffn_ref[...])
    h_hi, h_lo = _split_bf16(h2)
    w_hi, w_lo = _split_bf16(wr_ref[...])
    lt = _dot_nt(w_hi, h_hi) + (_dot_nt(w_hi, h_lo) + _dot_nt(w_lo, h_hi)) + br_ref[...]
    eid, wts = _route(lt)
    _local_sort(eid, wts, h2, xb_ref, cnt_ref, lpos_ref)


def _attn(x1, gxa, wq_bf, k_bf, v_bf, wo_bf, gffn, w_route_t, b_route, batch):
    n, d = x1.shape
    m = k_bf.shape[0] // batch
    tb = TOKEN_BLOCK
    nsb = n // batch // tb
    nb = n // tb
    row = lambda b, s: (b * nsb + s, 0)
    blk3 = lambda b, s: (b * nsb + s, 0, 0)
    fixed = lambda b, s: (0, 0)
    mem = lambda b, s: (b, 0)
    return pl.pallas_call(
        _attn_kernel,
        grid=(batch, nsb),
        in_specs=[pl.BlockSpec((tb, d), row), pl.BlockSpec((1, d), fixed), pl.BlockSpec((d, d), fixed),
                  pl.BlockSpec((m, d), mem), pl.BlockSpec((m, d), mem), pl.BlockSpec((d, d), fixed),
                  pl.BlockSpec((1, d), fixed), pl.BlockSpec((ROUTE_ROWS, d), fixed),
                  pl.BlockSpec((ROUTE_ROWS, 1), fixed)],
        out_specs=[pl.BlockSpec((tb, d), row), pl.BlockSpec((LOCAL_ROWS, d + W_LANES), row),
                   pl.BlockSpec((None, N_EXPERTS, 1), blk3), pl.BlockSpec((None, 1, TOP_K * tb), blk3)],
        out_shape=[jax.ShapeDtypeStruct((n, d), F32), jax.ShapeDtypeStruct((nb * LOCAL_ROWS, d + W_LANES), BF16),
                   jax.ShapeDtypeStruct((nb, N_EXPERTS, 1), I32), jax.ShapeDtypeStruct((nb, 1, TOP_K * tb), I32)],
        compiler_params=_params("parallel", "parallel"),
        name="attn",
    )(x1, gxa, wq_bf, k_bf, v_bf, wo_bf, gffn, w_route_t, b_route)


def _plan_kernel(cnt_ref, te_ref, src_ref, dst_ref, nact_ref):
    runs = cnt_ref.shape[0]
    lg = LOCAL_ROWS // GROUP
    tg = MOE_TILE // GROUP
    sh = N_EXPERTS.bit_length() - 1
    emask = N_EXPERTS - 1
    zero_group = lg - 1

    length = cnt_ref[...].astype(F32) * (1.0 / GROUP)
    len_bf = jnp.broadcast_to(length, (runs, 128)).astype(BF16)
    ri = lax.broadcasted_iota(I32, (runs, runs), 0)
    ci = lax.broadcasted_iota(I32, (runs, runs), 1)
    r_e, c_e = ri & emask, ci & emask
    r_b, c_b = lax.shift_right_logical(ri, sh), lax.shift_right_logical(ci, sh)
    same_expert_earlier = jnp.where(r_e == c_e, jnp.where(c_b < r_b, 1.0, 0.0), 0.0).astype(BF16)
    same_block_earlier = jnp.where(r_b == c_b, jnp.where(c_e < r_e, 1.0, 0.0), 0.0).astype(BF16)
    before = _dot(same_expert_earlier, len_bf)[:, 0:1]
    local = _dot(same_block_earlier, len_bf)[:, 0:1]

    of_expert = (lax.broadcasted_iota(I32, (N_EXPERTS, runs), 1) & emask) == lax.broadcasted_iota(
        I32, (N_EXPERTS, runs), 0)
    total = _dot(jnp.where(of_expert, 1.0, 0.0).astype(BF16), len_bf)[:, 0:1]
    tiles = jnp.floor((total + (tg - 1)) * (1.0 / tg))
    tile0 = _excl_cumsum_rows(tiles)
    to_run = (lax.broadcasted_iota(I32, (runs, N_EXPERTS), 0) & emask) == lax.broadcasted_iota(
        I32, (runs, N_EXPERTS), 1)
    to_run_bf = jnp.where(to_run, 1.0, 0.0).astype(BF16)
    base_hi, base_lo = _split_bf16(jnp.broadcast_to(tile0 * tg, (N_EXPERTS, 128)))
    g_start = (_dot(to_run_bf, base_hi) + _dot(to_run_bf, base_lo))[:, 0:1] + before
    block = lax.shift_right_logical(lax.broadcasted_iota(I32, (runs, 1), 0), sh).astype(F32)
    l_start = block * lg + local

    def cover(out_ref, start, offset, default):
        stop = start + length
        shift = offset - default
        n_out = out_ref.shape[1]
        chunk = next(c for c in PLAN_CHUNKS if n_out % c == 0)
        for c in range(n_out // chunk):
            j = (lax.broadcasted_iota(I32, (runs, chunk), 1) + c * chunk).astype(F32)
            hit = jnp.where(start <= j, jnp.where(j < stop, j + shift, 0.0), 0.0)
            out = jnp.sum(hit, axis=0, keepdims=True) + default
            out_ref[:, c * chunk:(c + 1) * chunk] = out.astype(I32)

    cover(src_ref, g_start, l_start - g_start, float(zero_group))
    cover(dst_ref, l_start, g_start - l_start, 0.0)
    t = lax.broadcasted_iota(I32, (N_EXPERTS, te_ref.shape[1]), 1).astype(F32)
    te_ref[...] = jnp.sum(jnp.where(t >= tile0 + tiles, 1.0, 0.0), axis=0, keepdims=True).astype(I32)
    nact_ref[...] = jnp.sum(jnp.broadcast_to(tiles, (N_EXPERTS, 128)), axis=0, keepdims=True).astype(I32)


def _plan(cnt, n_tiles):
    runs = cnt.shape[0]
    n_src = n_tiles * (MOE_TILE // GROUP)
    n_dst = runs // N_EXPERTS * (LOCAL_ROWS // GROUP)
    n_te = -(-n_tiles // 128) * 128
    te, src, dst, nact = pl.pallas_call(
        _plan_kernel,
        out_shape=[jax.ShapeDtypeStruct((1, n_te), I32), jax.ShapeDtypeStruct((1, n_src), I32),
                   jax.ShapeDtypeStruct((1, n_dst), I32), jax.ShapeDtypeStruct((1, 128), I32)],
        compiler_params=_params(),
        name="plan",
    )(cnt)
    return te.reshape(-1), src.reshape(-1), dst.reshape(-1), nact[0, :1]


def _group_copy(src_hbm, src_group, dst_buf, slot, index, sem):
    start = src_group * GROUP
    rows = pl.ds(start if isinstance(start, int) else pl.multiple_of(start, GROUP), GROUP)
    return pltpu.make_async_copy(src_hbm.at[rows], dst_buf.at[slot, pl.ds(index * GROUP, GROUP)], sem.at[slot])


def _experts_kernel(te_ref, src_ref, nact_ref, xb_ref, wg_hbm, wu_hbm, wd_hbm, ys_ref,
                    xbuf, sem, obuf, osem, wg_st, wu_st, wd_st, wsem, wg_bf, wu_bf, wd_bf):
    tg = MOE_TILE // GROUP
    tm, d = obuf.shape[1], obuf.shape[2]
    nact = nact_ref[0]
    n_tiles = ys_ref.shape[0] // tm - 2

    def start_gather(tile, slot):
        for i in range(tg):
            _group_copy(xb_ref, src_ref[tile * tg + i], xbuf, slot, i, sem).start()

    def wait_gather(slot):
        for i in range(tg):
            _group_copy(xb_ref, 0, xbuf, slot, i, sem).wait()

    def weight_copies(expert):
        pairs = ((wg_hbm, wg_st), (wu_hbm, wu_st), (wd_hbm, wd_st))
        return [pltpu.make_async_copy(w.at[expert], st, wsem.at[i]) for i, (w, st) in enumerate(pairs)]

    def out_copy(tile, slot):
        rows = pl.ds(pl.multiple_of(tile * tm, tm), tm)
        return pltpu.make_async_copy(obuf.at[slot], ys_ref.at[rows], osem.at[slot])

    obuf[...] = jnp.zeros_like(obuf)
    for s in range(2):
        out_copy(n_tiles + s, s).start()

    ring = GATHER_AHEAD + 1

    @pl.when(nact > 0)
    def _():
        for a in range(GATHER_AHEAD):
            start_gather(jnp.minimum(a, nact - 1), a)
        for c in weight_copies(te_ref[0]):
            c.start()

    def tile_step(t, carry):
        e = te_ref[t]
        slot = lax.rem(t, 2)
        xslot = lax.rem(t, ring)

        @pl.when(jnp.logical_or(t == 0, e != te_ref[jnp.maximum(t - 1, 0)]))
        def _():
            for c in weight_copies(e):
                c.wait()
            wg_bf[...] = wg_st[...].astype(BF16)
            wu_bf[...] = wu_st[...].astype(BF16)
            wd_bf[...] = wd_st[...].astype(BF16)
            nxt = lax.while_loop(lambda j: jnp.logical_and(j < nact, te_ref[jnp.minimum(j, n_tiles - 1)] == e),
                                 lambda j: j + 1, t + 1)

            @pl.when(nxt < nact)
            def _():
                for c in weight_copies(te_ref[jnp.minimum(nxt, n_tiles - 1)]):
                    c.start(priority=1)

        wait_gather(xslot)
        out_copy(t, slot).wait()
        x = xbuf[xslot, :, 0:d]
        hid = jax.nn.silu(_dot(x, wg_bf[...])) * _dot(x, wu_bf[...])
        start_gather(jnp.minimum(t + GATHER_AHEAD, nact - 1), lax.rem(t + GATHER_AHEAD, ring))
        weight = (xbuf[xslot, :, d:d + 1].astype(F32)
                  + xbuf[xslot, :, d + W_LANES // 2:d + W_LANES // 2 + 1].astype(F32))
        obuf[slot] = (_dot(hid.astype(BF16), wd_bf[...]) * weight).astype(BF16)
        out_copy(t, slot).start()
        return carry

    lax.fori_loop(0, nact, tile_step, 0)

    @pl.when(nact > 0)
    def _():
        for a in range(GATHER_AHEAD):
            wait_gather(lax.rem(nact + a, ring))
    for s in range(2):
        out_copy(0, s).wait()
    obuf[0] = jnp.zeros((tm, d), BF16)

    def zero_start(t, carry):
        out_copy(t, 0).start()
        return carry

    def zero_wait(t, carry):
        out_copy(t, 0).wait()
        return carry

    lax.fori_loop(nact, n_tiles, zero_start, 0)
    lax.fori_loop(nact, n_tiles, zero_wait, 0)


def _experts(te, src, nact, xb, w_gate, w_up, w_down, n_tiles):
    dw = xb.shape[1]
    d = dw - W_LANES
    ff = w_gate.shape[2]
    tm = MOE_TILE
    hbm = pl.BlockSpec(memory_space=pl.ANY)
    return pl.pallas_call(
        _experts_kernel,
        grid_spec=pltpu.PrefetchScalarGridSpec(
            num_scalar_prefetch=3,
            grid=(1,),
            in_specs=[hbm, hbm, hbm, hbm],
            out_specs=hbm,
            scratch_shapes=[pltpu.VMEM((GATHER_AHEAD + 1, tm, dw), BF16), pltpu.SemaphoreType.DMA((GATHER_AHEAD + 1,)),
                            pltpu.VMEM((2, tm, d), BF16), pltpu.SemaphoreType.DMA((2,)),
                            pltpu.VMEM((d, ff), F32), pltpu.VMEM((d, ff), F32), pltpu.VMEM((ff, d), F32),
                            pltpu.SemaphoreType.DMA((3,)),
                            pltpu.VMEM((d, ff), BF16), pltpu.VMEM((d, ff), BF16), pltpu.VMEM((ff, d), BF16)],
        ),
        out_shape=jax.ShapeDtypeStruct(((n_tiles + 2) * tm, d), BF16),
        compiler_params=_params("arbitrary"),
        name="experts",
    )(te, src, nact, xb, w_gate, w_up, w_down)


def _combine_kernel(dst_ref, lpos_ref, x2_ref, g_ref, ys_ref, out_ref, ybuf, sem):
    b = pl.program_id(0)
    tb = x2_ref.shape[0]
    lg = LOCAL_ROWS // GROUP
    nb = dst_ref.shape[0] // lg
    ring = GATHER_AHEAD + 1
    slot = lax.rem(b, ring)

    def start_gather(blk):
        for i in range(lg):
            _group_copy(ys_ref, dst_ref[blk * lg + i], ybuf, lax.rem(blk, ring), i, sem).start()

    @pl.when(b == 0)
    def _():
        for a in range(min(GATHER_AHEAD, nb)):
            start_gather(a)

    @pl.when(b + GATHER_AHEAD < nb)
    def _():
        start_gather(b + GATHER_AHEAD)

    for i in range(lg):
        _group_copy(ys_ref, 0, ybuf, slot, i, sem).wait()
    lpos = lpos_ref[...]
    y = jnp.zeros(x2_ref.shape, F32)
    for c in range(LOCAL_ROWS // SORT_CHUNK):
        m0, m1 = _slot_masks(lpos, c, tb)
        p = jnp.where(m0, 1.0, jnp.where(m1, 1.0, 0.0)).astype(BF16)
        y = y + _dot_tn(p, ybuf[slot, c * SORT_CHUNK:(c + 1) * SORT_CHUNK, :])
    out_ref[...] = _rms(x2_ref[...] + y, g_ref[...])


def _combine(dst, lpos, x2, g_final, ys):
    n, d = x2.shape
    tb = TOKEN_BLOCK
    return pl.pallas_call(
        _combine_kernel,
        grid_spec=pltpu.PrefetchScalarGridSpec(
            num_scalar_prefetch=1,
            grid=(n // tb,),
            in_specs=[pl.BlockSpec((None, 1, TOP_K * tb), lambda i, dst: (i, 0, 0)),
                      pl.BlockSpec((tb, d), lambda i, dst: (i, 0)), pl.BlockSpec((1, d), lambda i, dst: (0, 0)),
                      pl.BlockSpec(memory_space=pl.ANY)],
            out_specs=pl.BlockSpec((tb, d), lambda i, dst: (i, 0)),
            scratch_shapes=[pltpu.VMEM((GATHER_AHEAD + 1, LOCAL_ROWS, d), BF16),
                            pltpu.SemaphoreType.DMA((GATHER_AHEAD + 1,))],
        ),
        out_shape=jax.ShapeDtypeStruct((n, d), F32),
        compiler_params=_params("arbitrary"),
        name="combine",
    )(dst, lpos, x2, g_final, ys)


def kernel(x, mem, norm_mix_g, w_in, conv_w, conv_b, conv_ln_g, conv_ln_b, conv_w_out, hgrn_lb_logits, hgrn_onorm_g, hgrn_w_out, w_mix_out, norm_xa_g, norm_mem_g, xa_w_q, xa_w_k, xa_w_v, xa_w_o, norm_ffn_g, router_group_w, router_group_b, router_expert_w, router_expert_b, moe_w_gate, moe_w_up, moe_w_down, final_norm_g):
    batch, seq, d = x.shape
    n = batch * seq
    assert w_in.shape[0] == 1, "the final RMSNorm is fused into the single layer's combine step"
    cw = conv_w.shape[2]
    kw = hgrn_w_out.shape[1]
    assert kw == HGRN_HEADS * HGRN_DIM and conv_w.shape[1] == CONV_K
    assert seq % TOKEN_BLOCK == 0 and TOKEN_BLOCK % CHUNK == 0 and TOKEN_BLOCK % CONV_ROWS == 0
    assert moe_w_gate.shape[1] == N_EXPERTS and router_group_w.shape[2] == N_GROUPS and TOP_K == 2
    na = TOP_K * TOKEN_BLOCK
    assert LOCAL_ROWS % SORT_CHUNK == 0 and LOCAL_ROWS >= na + N_EXPERTS * (GROUP - 1) + GROUP

    n_tiles = -(-(n // TOKEN_BLOCK) * (na + N_EXPERTS * (GROUP - 1)) // MOE_TILE) + N_EXPERTS
    n_tiles = -(-n_tiles // 16) * 16
    vec = lambda p: p.reshape(1, -1)
    l = 0

    x2d = x.reshape(n, d)
    yc, q, fr, iv, og, gates = _mixin(x2d, vec(norm_mix_g[l]), w_in[l], conv_w[l], vec(conv_b[l]),
                                      vec(conv_ln_g[l]), vec(conv_ln_b[l]), conv_w_out[l].astype(BF16), cw, kw, batch)
    x1 = _hgrn(q, fr, iv, og, gates, yc, x2d, hgrn_lb_logits[l:l + 2], vec(hgrn_onorm_g[l]),
               hgrn_w_out[l].astype(BF16), w_mix_out[l].astype(BF16), batch)
    k_bf, v_bf = _memkv(mem.reshape(-1, d), vec(norm_mem_g[l]), xa_w_k[l].astype(BF16), xa_w_v[l].astype(BF16), batch)
    pad = ROUTE_ROWS - N_GROUPS - N_EXPERTS
    w_route_t = jnp.pad(jnp.concatenate([router_group_w[l], router_expert_w[l]], axis=1).T, ((0, pad), (0, 0)))
    b_route = jnp.pad(jnp.concatenate([router_group_b[l], router_expert_b[l]]), (0, pad)).reshape(ROUTE_ROWS, 1)
    x2, xb, cnt, lpos = _attn(x1, vec(norm_xa_g[l]), xa_w_q[l].astype(BF16), k_bf, v_bf, xa_w_o[l].astype(BF16),
                              vec(norm_ffn_g[l]), w_route_t, b_route, batch)
    te, src, dst, nact = _plan(cnt.reshape(-1, 1), n_tiles)
    ys = _experts(te, src, nact, xb, moe_w_gate[l], moe_w_up[l], moe_w_down[l], n_tiles)
    out = _combine(dst, lpos, x2, vec(final_norm_g), ys)
    return out.reshape(batch, seq, d)
```

```python
import functools

import jax
import jax.numpy as jnp
from jax import lax
from jax.experimental import pallas as pl
from jax.experimental.pallas import tpu as pltpu

F32 = jnp.float32
BF16 = jnp.bfloat16
I32 = jnp.int32

EPS = 1e-6
CONV_K = 31
CONV_HALO = 32
CONV_ROWS = 64
HGRN_HEADS = 4
HGRN_DIM = 128
CHUNK = 64
SUB = 16
XA_HEADS = 4
N_GROUPS = 4
EXPERTS_PER_GROUP = 8
N_EXPERTS = N_GROUPS * EXPERTS_PER_GROUP
TOP_K = 2
ROUTE_ROWS = 40
TOKEN_BLOCK = 512
MOE_TILE = 256
GROUP = 16
LOCAL_ROWS = 1536
SORT_CHUNK = 256
RANK_SEGMENT = 128
GATHER_AHEAD = 3
W_LANES = 128
PLAN_CHUNKS = (512, 256, 128)
VMEM_LIMIT_BYTES = 48 * 1024 * 1024


def _rms(x, g):
    return x * lax.rsqrt(jnp.mean(x * x, axis=-1, keepdims=True) + EPS) * g


def _dot(a, b):
    return jnp.dot(a, b, preferred_element_type=F32)


def _dot_nt(a, b):
    return lax.dot_general(a, b, (((1,), (1,)), ((), ())), preferred_element_type=F32)


def _dot_tn(a, b):
    return lax.dot_general(a, b, (((0,), (0,)), ((), ())), preferred_element_type=F32)


def _split_bf16(x):
    hi = x.astype(BF16)
    lo = (x - hi.astype(F32)).astype(BF16)
    return hi, lo


def _params(*sem):
    return pltpu.CompilerParams(dimension_semantics=sem, vmem_limit_bytes=VMEM_LIMIT_BYTES)


def _conv_stages(un, first, cw_ref, cb_ref, lg_ref, lb_ref, ext_ref, halo_ref, perm_ref, act_ref):
    tb, c = un.shape
    nt = tb // 8
    slabs = c // 128
    hr = CONV_HALO * 8
    lanes = [slice(l * 128, (l + 1) * 128) for l in range(slabs)]

    @pl.when(first)
    def _():
        halo_ref[...] = jnp.zeros_like(halo_ref)

    per = nt // 8
    for j in range(nt):
        start = hr + (j % per) * 64 + j // per
        for l in range(slabs):
            ext_ref[l, pl.ds(start, 8, stride=8), :] = un[8 * j:8 * j + 8, lanes[l]]
    first_row = lax.broadcasted_iota(I32, (hr, 128), 0) % 8 == 0
    for l in range(slabs):
        cur = ext_ref[l, nt * 8:nt * 8 + hr, :]
        ext_ref[l, 0:hr, :] = jnp.where(first_row, pltpu.roll(halo_ref[l], hr - 7, axis=0),
                                        pltpu.roll(cur, 1, axis=0))
        halo_ref[l] = cur

    def row_tile(r):
        accs = []
        for l in range(slabs):
            acc = jnp.broadcast_to(cb_ref[:, lanes[l]], (CONV_ROWS, 128))
            for dt in range(CONV_K):
                off = hr + r * CONV_ROWS - dt * 8
                acc = acc + cw_ref[CONV_K - 1 - dt:CONV_K - dt, lanes[l]] * ext_ref[l, off:off + CONV_ROWS, :]
            accs.append(acc)
        mu = functools.reduce(jnp.add, [jnp.sum(a, axis=-1, keepdims=True) for a in accs]) * (1.0 / c)
        cens = [a - mu for a in accs]
        var = functools.reduce(jnp.add, [jnp.sum(a * a, axis=-1, keepdims=True) for a in cens]) * (1.0 / c)
        inv = lax.rsqrt(var + EPS)
        for l in range(slabs):
            ln = cens[l] * inv * lg_ref[:, lanes[l]] + lb_ref[:, lanes[l]]
            perm_ref[l, r * CONV_ROWS:(r + 1) * CONV_ROWS, :] = jax.nn.silu(ln)

    def finish():
        for j in range(nt):
            start = (j % per) * 64 + j // per
            for l in range(slabs):
                act_ref[8 * j:8 * j + 8, lanes[l]] = perm_ref[l, pl.ds(start, 8, stride=8), :]

    return [functools.partial(row_tile, r) for r in range(tb // CONV_ROWS)], finish


def _mixin_kernel(x_ref, g_ref, w_hbm, cw_ref, cb_ref, lg_ref, lb_ref, wo_ref,
                  yc_ref, q_ref, fr_ref, iv_ref, og_ref, gate_ref,
                  w_ref, w_stage, w_sem, ext_ref, halo_ref, perm_ref, act_ref, *, cw, kw):
    d = yc_ref.shape[1]

    @pl.when(jnp.logical_and(pl.program_id(0) == 0, pl.program_id(1) == 0))
    def _():
        width = w_stage.shape[2]

        def fetch(c):
            return pltpu.make_async_copy(w_hbm.at[:, pl.ds(c * width, width)], w_stage.at[c % 2], w_sem.at[c % 2])

        n_fetch = w_ref.shape[1] // width
        fetch(0).start()
        for c in range(n_fetch):
            if c + 1 < n_fetch:
                fetch(c + 1).start()
            fetch(c).wait()
            w_ref[:, c * width:(c + 1) * width] = w_stage[c % 2].astype(BF16)

    hb = _rms(x_ref[...], g_ref[...]).astype(BF16)

    def proj(lo, width):
        return _dot(hb, w_ref[:, lo:lo + width])

    conv_tiles, conv_finish = _conv_stages(proj(0, cw) * jax.nn.sigmoid(proj(cw, cw)), pl.program_id(1) == 0,
                                           cw_ref, cb_ref, lg_ref, lb_ref, ext_ref, halo_ref, perm_ref, act_ref)
    def chunk(ref, col, lo, act):
        def run():
            ref[:, col * kw:(col + 1) * kw] = act(proj(lo, kw)).astype(ref.dtype)
        return run

    base = 2 * cw
    gbase = base + 4 * kw
    chunks = [chunk(q_ref, 0, base, jax.nn.silu), chunk(fr_ref, 0, base + kw, lambda v: v),
              chunk(iv_ref, 0, base + 2 * kw, lambda v: v), chunk(og_ref, 0, base + 3 * kw, jax.nn.silu)]
    for c in range(d // kw):
        chunks.append(chunk(gate_ref, c, gbase + d + c * kw, jax.nn.sigmoid))
        chunks.append(chunk(yc_ref, c, gbase + c * kw, jax.nn.sigmoid))
    for i in range(max(len(chunks), len(conv_tiles))):
        if i < len(chunks):
            chunks[i]()
        if i < len(conv_tiles):
            conv_tiles[i]()
    conv_finish()
    y_conv = _dot(act_ref[...].astype(BF16), wo_ref[...])
    yc_ref[...] = (yc_ref[...].astype(F32) * y_conv).astype(BF16)


def _hgrn_block(q_ref, fr_ref, iv_ref, lb, st_ref, tri):
    tb = q_ref.shape[0]
    chunks = [slice(c * CHUNK, (c + 1) * CHUNK) for c in range(tb // CHUNK)]
    heads = [slice(h * HGRN_DIM, (h + 1) * HGRN_DIM) for h in range(HGRN_HEADS)]

    qs, vs, kks, cums = [], [], [], []
    for rows in chunks:
        f = lb + (1.0 - lb) * jax.nn.sigmoid(fr_ref[rows, :])
        lf_hi, lf_lo = _split_bf16(jnp.log(f))
        cums.append(_dot(tri, jnp.concatenate([lf_hi, lf_lo], axis=0)))
        kks.append(1.0 - f)
        qs.append(q_ref[rows, :].astype(F32))
        vs.append(iv_ref[rows, :])

    qes, kds, decays, blocks = [], [], [], []
    for q, kk, cum in zip(qs, kks, cums):
        last = cum[CHUNK - 1:CHUNK, :]
        qes.append((q * jnp.exp(cum)).astype(BF16))
        kds.append((kk * jnp.exp(last - cum)).astype(BF16))
        decays.append(jnp.exp(last))
        sub = []
        for i in range(CHUNK // SUB):
            rs, ne = i * SUB, (i + 1) * SUB
            ref = cum[rs + SUB // 2 - 1:rs + SUB // 2, :]
            qt = (q[rs:ne] * jnp.exp(cum[rs:ne] - ref)).astype(BF16)
            kt = (kk[0:ne] * jnp.exp(ref - cum[0:ne])).astype(BF16)
            sub.append((qt, kt))
        blocks.append(sub)

    updates = [[_dot_tn(v[:, hs], kd[:, hs]) for hs in heads] for v, kd in zip(vs, kds)]
    scores = [[[_dot_nt(qt[:, hs], kt[:, hs]) for qt, kt in sub] for hs in heads] for sub in blocks]

    states = [st_ref[h] for h in range(HGRN_HEADS)]
    inter = []
    for qe, decay, upd in zip(qes, decays, updates):
        inter.append([_dot_nt(qe[:, hs], st.astype(BF16)) for hs, st in zip(heads, states)])
        states = [st * decay[:, hs] + u for st, hs, u in zip(states, heads, upd)]
    for h in range(HGRN_HEADS):
        st_ref[h] = states[h]

    outs = []
    for v, sc, o_inter in zip(vs, scores, inter):
        per_head = []
        for h, hs in enumerate(heads):
            parts = []
            for i, a in enumerate(sc[h]):
                rs, ne = i * SUB, (i + 1) * SUB
                trow = lax.broadcasted_iota(I32, (SUB, ne), 0) + rs
                scol = lax.broadcasted_iota(I32, (SUB, ne), 1)
                a = jnp.where(scol <= trow, a, 0.0).astype(BF16)
                parts.append(_dot(a, v[0:ne, hs]))
            per_head.append(o_inter[h] + jnp.concatenate(parts, axis=0))
        outs.append(jnp.concatenate(per_head, axis=1))
    return outs


def _hgrn_kernel(q_ref, fr_ref, iv_ref, og_ref, gate_ref, yc_ref, x_ref, lbl_ref, on_ref, wo_ref, wm_ref,
                 x1_ref, st_ref, ob_ref):
    tb = q_ref.shape[0]

    @pl.when(pl.program_id(1) == 0)
    def _():
        st_ref[...] = jnp.zeros_like(st_ref)

    l0, l1 = lbl_ref[0:1, :], lbl_ref[1:2, :]
    m = jnp.maximum(l0, l1)
    e0, e1 = jnp.exp(l0 - m), jnp.exp(l1 - m)
    lb_all = e0 / (e0 + e1)
    trow = lax.broadcasted_iota(I32, (CHUNK, 2 * CHUNK), 0)
    tcol = lax.broadcasted_iota(I32, (CHUNK, 2 * CHUNK), 1) % CHUNK
    tri = jnp.where(tcol <= trow, 1.0, 0.0).astype(BF16)

    for c, o in enumerate(_hgrn_block(q_ref, fr_ref, iv_ref, lb_all, st_ref, tri)):
        rows = slice(c * CHUNK, (c + 1) * CHUNK)
        og = og_ref[rows, :].astype(F32)
        for h in range(HGRN_HEADS):
            hs = slice(h * HGRN_DIM, (h + 1) * HGRN_DIM)
            ob_ref[rows, hs] = (_rms(o[:, hs], on_ref[...]) * og[:, hs]).astype(BF16)
    y_rec = _dot(ob_ref[...], wo_ref[...])
    merged = yc_ref[...].astype(F32) + gate_ref[...].astype(F32) * y_rec
    x1_ref[...] = x_ref[...] + _dot(merged.astype(BF16), wm_ref[...])


def _mixer_kernel(x_ref, g_ref, w_hbm, cw_ref, cb_ref, lg_ref, lb_ref, wco_ref, lbl_ref, on_ref, wro_ref, wm_ref,
                  x1_ref, yc_s, q_s, fr_s, iv_s, og_s, gate_s, w_ref, w_stage, w_sem, ext_ref, halo_ref, perm_ref,
                  act_ref, st_ref, ob_ref, *, cw, kw):
    _mixin_kernel(x_ref, g_ref, w_hbm, cw_ref, cb_ref, lg_ref, lb_ref, wco_ref, yc_s, q_s, fr_s, iv_s, og_s, gate_s,
                  w_ref, w_stage, w_sem, ext_ref, halo_ref, perm_ref, act_ref, cw=cw, kw=kw)
    _hgrn_kernel(q_s, fr_s, iv_s, og_s, gate_s, yc_s, x_ref, lbl_ref, on_ref, wro_ref, wm_ref, x1_ref, st_ref, ob_ref)


def _mixer(x2d, g, w_in, conv_w, conv_b, ln_g, ln_b, w_conv_out_bf, lb_logits, onorm_g, w_o_bf, w_mix_bf, cw, kw,
           batch):
    n, d = x2d.shape
    tb = TOKEN_BLOCK
    nsb = n // batch // tb
    row = lambda b, s: (b * nsb + s, 0)
    fixed = lambda b, s: (0, 0)
    return pl.pallas_call(
        functools.partial(_mixer_kernel, cw=cw, kw=kw),
        grid=(batch, nsb),
        in_specs=[pl.BlockSpec((tb, d), row), pl.BlockSpec((1, d), fixed), pl.BlockSpec(memory_space=pl.ANY),
                  pl.BlockSpec(conv_w.shape, fixed), pl.BlockSpec((1, cw), fixed), pl.BlockSpec((1, cw), fixed),
                  pl.BlockSpec((1, cw), fixed), pl.BlockSpec((cw, d), fixed),
                  pl.BlockSpec(lb_logits.shape, fixed), pl.BlockSpec((1, HGRN_DIM), fixed),
                  pl.BlockSpec(w_o_bf.shape, fixed), pl.BlockSpec(w_mix_bf.shape, fixed)],
        out_specs=pl.BlockSpec((tb, d), row),
        out_shape=jax.ShapeDtypeStruct((n, d), F32),
        scratch_shapes=[pltpu.VMEM((tb, d), BF16), pltpu.VMEM((tb, kw), BF16), pltpu.VMEM((tb, kw), F32),
                        pltpu.VMEM((tb, kw), BF16), pltpu.VMEM((tb, kw), BF16), pltpu.VMEM((tb, d), BF16),
                        pltpu.VMEM(w_in.shape, BF16), pltpu.VMEM((2, d, kw), F32), pltpu.SemaphoreType.DMA((2,)),
                        pltpu.VMEM((cw // 128, tb + CONV_HALO * 8, 128), F32),
                        pltpu.VMEM((cw // 128, CONV_HALO * 8, 128), F32),
                        pltpu.VMEM((cw // 128, tb, 128), F32), pltpu.VMEM((tb, cw), F32),
                        pltpu.VMEM((HGRN_HEADS, HGRN_DIM, HGRN_DIM), F32), pltpu.VMEM((tb, kw), BF16)],
        compiler_params=_params("arbitrary", "arbitrary"),
        name="mixer",
    )(x2d, g, w_in, conv_w, conv_b, ln_g, ln_b, w_conv_out_bf, lb_logits, onorm_g, w_o_bf, w_mix_bf)


def _memkv_kernel(mem_ref, g_ref, wk_ref, wv_ref, k_ref, v_ref):
    mb = _rms(mem_ref[...], g_ref[...]).astype(BF16)
    k_ref[...] = _dot(mb, wk_ref[...]).astype(BF16)
    v_ref[...] = _dot(mb, wv_ref[...]).astype(BF16)


def _memkv(mem2d, g, wk_bf, wv_bf, batch):
    n, d = mem2d.shape
    m = n // batch
    row = lambda b: (b, 0)
    fixed = lambda b: (0, 0)
    return pl.pallas_call(
        _memkv_kernel,
        grid=(batch,),
        in_specs=[pl.BlockSpec((m, d), row), pl.BlockSpec((1, d), fixed), pl.BlockSpec((d, d), fixed),
                  pl.BlockSpec((d, d), fixed)],
        out_specs=[pl.BlockSpec((m, d), row)] * 2,
        out_shape=[jax.ShapeDtypeStruct((n, d), BF16)] * 2,
        compiler_params=_params("parallel"),
        name="memkv",
    )(mem2d, g, wk_bf, wv_bf)


def _route(lt):
    def row(r):
        return lt[r:r + 1, :]

    gl = [row(g) for g in range(N_GROUPS)]
    gmax = functools.reduce(jnp.maximum, gl)
    g_p = 1.0 / functools.reduce(jnp.add, [jnp.exp(l - gmax) for l in gl])
    gidx = jnp.full(gmax.shape, N_GROUPS - 1, I32)
    for g in range(N_GROUPS - 2, -1, -1):
        gidx = jnp.where(gl[g] == gmax, g, gidx)

    el = []
    for j in range(EXPERTS_PER_GROUP):
        v = row(N_GROUPS + (N_GROUPS - 1) * EXPERTS_PER_GROUP + j)
        for g in range(N_GROUPS - 2, -1, -1):
            v = jnp.where(gidx == g, row(N_GROUPS + g * EXPERTS_PER_GROUP + j), v)
        el.append(v)

    def argmax(vals):
        mx = functools.reduce(jnp.maximum, vals)
        idx = jnp.full(mx.shape, EXPERTS_PER_GROUP - 1, I32)
        for j in range(EXPERTS_PER_GROUP - 2, -1, -1):
            idx = jnp.where(vals[j] == mx, j, idx)
        return mx, idx

    m1, i1 = argmax(el)
    m2, i2 = argmax([jnp.where(i1 == j, -jnp.inf, el[j]) for j in range(EXPERTS_PER_GROUP)])
    r = jnp.exp(m2 - m1)
    w1 = g_p / (1.0 + r)
    w2 = g_p * r / (1.0 + r)
    base = gidx * EXPERTS_PER_GROUP
    return jnp.concatenate([base + i1, base + i2], axis=0), jnp.concatenate([w1, w2], axis=0)


def _excl_cumsum_rows(col):
    r = col.shape[0]
    lower = lax.broadcasted_iota(I32, (r, r), 1) < lax.broadcasted_iota(I32, (r, r), 0)
    lower_bf = jnp.where(lower, 1.0, 0.0).astype(BF16)
    hi, lo = _split_bf16(jnp.broadcast_to(col, (r, 128)))
    return (_dot(lower_bf, hi) + _dot(lower_bf, lo))[:, 0:1]


def _slot_masks(lpos, chunk, tb):
    slot = lax.broadcasted_iota(I32, (SORT_CHUNK, tb), 0) + chunk * SORT_CHUNK
    return slot == lpos[:, 0:tb], slot == lpos[:, tb:2 * tb]


def _earlier_same_expert(ones):
    seg = RANK_SEGMENT
    n_exp, n = ones.shape
    upper = lax.broadcasted_iota(I32, (seg, seg), 0) < lax.broadcasted_iota(I32, (seg, seg), 1)
    pieces = [ones[:, s * seg:(s + 1) * seg] for s in range(n // seg)]
    within = _dot(jnp.concatenate([p.astype(BF16) for p in pieces], axis=0), jnp.where(upper, 1.0, 0.0).astype(BF16))
    seen = jnp.zeros((n_exp, 1), F32)
    out = []
    for s, p in enumerate(pieces):
        out.append(within[s * n_exp:(s + 1) * n_exp, :] + seen)
        seen = seen + jnp.sum(p, axis=1, keepdims=True)
    return jnp.concatenate(out, axis=1), seen


def _local_sort(eid, wts, h2, xb_ref, cnt_ref, lpos_ref):
    tb, d = h2.shape
    na = TOP_K * tb
    e_all = jnp.concatenate([eid[k:k + 1] for k in range(TOP_K)], axis=1)
    onehot = lax.broadcasted_iota(I32, (N_EXPERTS, na), 0) == e_all
    earlier, cnt = _earlier_same_expert(jnp.where(onehot, 1.0, 0.0))
    cnt_pad = jnp.floor((cnt + (GROUP - 1)) * (1.0 / GROUP)) * GROUP
    start = _excl_cumsum_rows(cnt_pad)
    lpos = jnp.sum(jnp.where(onehot, start + earlier, 0.0), axis=0, keepdims=True).astype(I32)
    hb = h2.astype(BF16)
    n_chunks = LOCAL_ROWS // SORT_CHUNK

    def sort_chunk(c):
        m0, m1 = _slot_masks(lpos, c, tb)
        p = jnp.where(m0, 1.0, jnp.where(m1, 1.0, 0.0)).astype(BF16)
        pw = jnp.where(m0, wts[0:1], jnp.where(m1, wts[1:2], 0.0))
        rows = slice(c * SORT_CHUNK, (c + 1) * SORT_CHUNK)
        xb_ref[rows, 0:d] = _dot(p, hb).astype(BF16)
        w_row = jnp.broadcast_to(jnp.sum(pw, axis=1, keepdims=True), (SORT_CHUNK, W_LANES))
        w_hi = w_row.astype(BF16).astype(F32)
        low_half = lax.broadcasted_iota(I32, (SORT_CHUNK, W_LANES), 1) < W_LANES // 2
        xb_ref[rows, d:d + W_LANES] = jnp.where(low_half, w_hi, w_row - w_hi).astype(BF16)

    for c in range(n_chunks - 1):
        sort_chunk(c)
    last_used = jnp.sum(cnt_pad) > (n_chunks - 1) * SORT_CHUNK

    @pl.when(last_used)
    def _():
        sort_chunk(n_chunks - 1)

    @pl.when(jnp.logical_not(last_used))
    def _():
        xb_ref[(n_chunks - 1) * SORT_CHUNK:, :] = jnp.zeros((SORT_CHUNK, d + W_LANES), BF16)

    cnt_ref[...] = cnt_pad.astype(I32)
    lpos_ref[...] = lpos


def _attn_kernel(x1_ref, gxa_ref, wq_ref, k_ref, v_ref, wo_ref, gffn_ref, wr_ref, br_ref,
                 x2_ref, xb_ref, cnt_ref, lpos_ref):
    x1 = x1_ref[...]
    d = x1.shape[1]
    hd = d // XA_HEADS
    q = _dot(_rms(x1, gxa_ref[...]).astype(BF16), wq_ref[...]).astype(BF16)
    heads = []
    for h in range(XA_HEADS):
        hs = slice(h * hd, (h + 1) * hd)
        sc = _dot_nt(q[:, hs], k_ref[:, hs]) * (hd ** -0.5)
        p = jnp.exp(sc - jnp.max(sc, axis=-1, keepdims=True))
        p = p / jnp.sum(p, axis=-1, keepdims=True)
        heads.append(_dot(p.astype(BF16), v_ref[:, hs]).astype(BF16))
    x2 = x1 + _dot(jnp.concatenate(heads, axis=1), wo_ref[...])
    x2_ref[...] = x2
    h2 = _rms(x2, gffn_ref[...])
    h_hi, h_lo = _split_bf16(h2)
    w_hi, w_lo = _split_bf16(wr_ref[...])
    lt = _dot_nt(w_hi, h_hi) + (_dot_nt(w_hi, h_lo) + _dot_nt(w_lo, h_hi)) + br_ref[...]
    eid, wts = _route(lt)
    _local_sort(eid, wts, h2, xb_ref, cnt_ref, lpos_ref)


def _attn(x1, gxa, wq_bf, k_bf, v_bf, wo_bf, gffn, w_route_t, b_route, batch):
    n, d = x1.shape
    m = k_bf.shape[0] // batch
    tb = TOKEN_BLOCK
    nsb = n // batch // tb
    nb = n // tb
    row = lambda b, s: (b * nsb + s, 0)
    blk3 = lambda b, s: (b * nsb + s, 0, 0)
    fixed = lambda b, s: (0, 0)
    mem = lambda b, s: (b, 0)
    return pl.pallas_call(
        _attn_kernel,
        grid=(batch, nsb),
        in_specs=[pl.BlockSpec((tb, d), row), pl.BlockSpec((1, d), fixed), pl.BlockSpec((d, d), fixed),
                  pl.BlockSpec((m, d), mem), pl.BlockSpec((m, d), mem), pl.BlockSpec((d, d), fixed),
                  pl.BlockSpec((1, d), fixed), pl.BlockSpec((ROUTE_ROWS, d), fixed),
                  pl.BlockSpec((ROUTE_ROWS, 1), fixed)],
        out_specs=[pl.BlockSpec((tb, d), row), pl.BlockSpec((LOCAL_ROWS, d + W_LANES), row),
                   pl.BlockSpec((None, N_EXPERTS, 1), blk3), pl.BlockSpec((None, 1, TOP_K * tb), blk3)],
        out_shape=[jax.ShapeDtypeStruct((n, d), F32), jax.ShapeDtypeStruct((nb * LOCAL_ROWS, d + W_LANES), BF16),
                   jax.ShapeDtypeStruct((nb, N_EXPERTS, 1), I32), jax.ShapeDtypeStruct((nb, 1, TOP_K * tb), I32)],
        compiler_params=_params("parallel", "parallel"),
        name="attn",
    )(x1, gxa, wq_bf, k_bf, v_bf, wo_bf, gffn, w_route_t, b_route)


def _plan_kernel(cnt_ref, te_ref, src_ref, dst_ref, nact_ref):
    runs = cnt_ref.shape[0]
    lg = LOCAL_ROWS // GROUP
    tg = MOE_TILE // GROUP
    sh = N_EXPERTS.bit_length() - 1
    emask = N_EXPERTS - 1
    zero_group = lg - 1

    length = cnt_ref[...].astype(F32) * (1.0 / GROUP)
    len_bf = jnp.broadcast_to(length, (runs, 128)).astype(BF16)
    ri = lax.broadcasted_iota(I32, (runs, runs), 0)
    ci = lax.broadcasted_iota(I32, (runs, runs), 1)
    r_e, c_e = ri & emask, ci & emask
    r_b, c_b = lax.shift_right_logical(ri, sh), lax.shift_right_logical(ci, sh)
    same_expert_earlier = jnp.where(r_e == c_e, jnp.where(c_b < r_b, 1.0, 0.0), 0.0).astype(BF16)
    same_block_earlier = jnp.where(r_b == c_b, jnp.where(c_e < r_e, 1.0, 0.0), 0.0).astype(BF16)
    before = _dot(same_expert_earlier, len_bf)[:, 0:1]
    local = _dot(same_block_earlier, len_bf)[:, 0:1]

    of_expert = (lax.broadcasted_iota(I32, (N_EXPERTS, runs), 1) & emask) == lax.broadcasted_iota(
        I32, (N_EXPERTS, runs), 0)
    total = _dot(jnp.where(of_expert, 1.0, 0.0).astype(BF16), len_bf)[:, 0:1]
    tiles = jnp.floor((total + (tg - 1)) * (1.0 / tg))
    tile0 = _excl_cumsum_rows(tiles)
    to_run = (lax.broadcasted_iota(I32, (runs, N_EXPERTS), 0) & emask) == lax.broadcasted_iota(
        I32, (runs, N_EXPERTS), 1)
    to_run_bf = jnp.where(to_run, 1.0, 0.0).astype(BF16)
    base_hi, base_lo = _split_bf16(jnp.broadcast_to(tile0 * tg, (N_EXPERTS, 128)))
    g_start = (_dot(to_run_bf, base_hi) + _dot(to_run_bf, base_lo))[:, 0:1] + before
    block = lax.shift_right_logical(lax.broadcasted_iota(I32, (runs, 1), 0), sh).astype(F32)
    l_start = block * lg + local

    def cover(out_ref, start, offset, default):
        stop = start + length
        shift = offset - default
        n_out = out_ref.shape[1]
        chunk = next(c for c in PLAN_CHUNKS if n_out % c == 0)
        for c in range(n_out // chunk):
            j = (lax.broadcasted_iota(I32, (runs, chunk), 1) + c * chunk).astype(F32)
            hit = jnp.where(start <= j, jnp.where(j < stop, j + shift, 0.0), 0.0)
            out = jnp.sum(hit, axis=0, keepdims=True) + default
            out_ref[:, c * chunk:(c + 1) * chunk] = out.astype(I32)

    cover(src_ref, g_start, l_start - g_start, float(zero_group))
    cover(dst_ref, l_start, g_start - l_start, 0.0)
    t = lax.broadcasted_iota(I32, (N_EXPERTS, te_ref.shape[1]), 1).astype(F32)
    te_ref[...] = jnp.sum(jnp.where(t >= tile0 + tiles, 1.0, 0.0), axis=0, keepdims=True).astype(I32)
    nact_ref[...] = jnp.sum(jnp.broadcast_to(tiles, (N_EXPERTS, 128)), axis=0, keepdims=True).astype(I32)


def _plan(cnt, n_tiles):
    runs = cnt.shape[0]
    n_src = n_tiles * (MOE_TILE // GROUP)
    n_dst = runs // N_EXPERTS * (LOCAL_ROWS // GROUP)
    n_te = -(-n_tiles // 128) * 128
    te, src, dst, nact = pl.pallas_call(
        _plan_kernel,
        out_shape=[jax.ShapeDtypeStruct((1, n_te), I32), jax.ShapeDtypeStruct((1, n_src), I32),
                   jax.ShapeDtypeStruct((1, n_dst), I32), jax.ShapeDtypeStruct((1, 128), I32)],
        compiler_params=_params(),
        name="plan",
    )(cnt)
    return te.reshape(-1), src.reshape(-1), dst.reshape(-1), nact[0, :1]


def _group_copy(src_hbm, src_group, dst_buf, slot, index, sem):
    start = src_group * GROUP
    rows = pl.ds(start if isinstance(start, int) else pl.multiple_of(start, GROUP), GROUP)
    return pltpu.make_async_copy(src_hbm.at[rows], dst_buf.at[slot, pl.ds(index * GROUP, GROUP)], sem.at[slot])


def _experts_kernel(te_ref, src_ref, nact_ref, xb_ref, wg_hbm, wu_hbm, wd_hbm, ys_ref,
                    xbuf, sem, obuf, osem, wg_st, wu_st, wd_st, wsem, wg_bf, wu_bf, wd_bf):
    tg = MOE_TILE // GROUP
    tm, d = obuf.shape[1], obuf.shape[2]
    nact = nact_ref[0]
    n_tiles = ys_ref.shape[0] // tm - 2

    def start_gather(tile, slot):
        for i in range(tg):
            _group_copy(xb_ref, src_ref[tile * tg + i], xbuf, slot, i, sem).start()

    def wait_gather(slot):
        for i in range(tg):
            _group_copy(xb_ref, 0, xbuf, slot, i, sem).wait()

    def weight_copies(expert):
        pairs = ((wg_hbm, wg_st), (wu_hbm, wu_st), (wd_hbm, wd_st))
        return [pltpu.make_async_copy(w.at[expert], st, wsem.at[i]) for i, (w, st) in enumerate(pairs)]

    def out_copy(tile, slot):
        rows = pl.ds(pl.multiple_of(tile * tm, tm), tm)
        return pltpu.make_async_copy(obuf.at[slot], ys_ref.at[rows], osem.at[slot])

    obuf[...] = jnp.zeros_like(obuf)
    for s in range(2):
        out_copy(n_tiles + s, s).start()

    ring = GATHER_AHEAD + 1

    @pl.when(nact > 0)
    def _():
        for a in range(GATHER_AHEAD):
            start_gather(jnp.minimum(a, nact - 1), a)
        for c in weight_copies(te_ref[0]):
            c.start()

    def tile_step(t, carry):
        e = te_ref[t]
        slot = lax.rem(t, 2)
        xslot = lax.rem(t, ring)

        @pl.when(jnp.logical_or(t == 0, e != te_ref[jnp.maximum(t - 1, 0)]))
        def _():
            for c in weight_copies(e):
                c.wait()
            wg_bf[...] = wg_st[...].astype(BF16)
            wu_bf[...] = wu_st[...].astype(BF16)
            wd_bf[...] = wd_st[...].astype(BF16)
            nxt = lax.while_loop(lambda j: jnp.logical_and(j < nact, te_ref[jnp.minimum(j, n_tiles - 1)] == e),
                                 lambda j: j + 1, t + 1)

            @pl.when(nxt < nact)
            def _():
                for c in weight_copies(te_ref[jnp.minimum(nxt, n_tiles - 1)]):
                    c.start(priority=1)

        wait_gather(xslot)
        out_copy(t, slot).wait()
        x = xbuf[xslot, :, 0:d]
        hid = jax.nn.silu(_dot(x, wg_bf[...])) * _dot(x, wu_bf[...])
        start_gather(jnp.minimum(t + GATHER_AHEAD, nact - 1), lax.rem(t + GATHER_AHEAD, ring))
        weight = (xbuf[xslot, :, d:d + 1].astype(F32)
                  + xbuf[xslot, :, d + W_LANES // 2:d + W_LANES // 2 + 1].astype(F32))
        obuf[slot] = (_dot(hid.astype(BF16), wd_bf[...]) * weight).astype(BF16)
        out_copy(t, slot).start()
        return carry

    lax.fori_loop(0, nact, tile_step, 0)

    @pl.when(nact > 0)
    def _():
        for a in range(GATHER_AHEAD):
            wait_gather(lax.rem(nact + a, ring))
    for s in range(2):
        out_copy(0, s).wait()
    obuf[0] = jnp.zeros((tm, d), BF16)

    def zero_start(t, carry):
        out_copy(t, 0).start()
        return carry

    def zero_wait(t, carry):
        out_copy(t, 0).wait()
        return carry

    lax.fori_loop(nact, n_tiles, zero_start, 0)
    lax.fori_loop(nact, n_tiles, zero_wait, 0)


def _experts(te, src, nact, xb, w_gate, w_up, w_down, n_tiles):
    dw = xb.shape[1]
    d = dw - W_LANES
    ff = w_gate.shape[2]
    tm = MOE_TILE
    hbm = pl.BlockSpec(memory_space=pl.ANY)
    return pl.pallas_call(
        _experts_kernel,
        grid_spec=pltpu.PrefetchScalarGridSpec(
            num_scalar_prefetch=3,
            grid=(1,),
            in_specs=[hbm, hbm, hbm, hbm],
            out_specs=hbm,
            scratch_shapes=[pltpu.VMEM((GATHER_AHEAD + 1, tm, dw), BF16), pltpu.SemaphoreType.DMA((GATHER_AHEAD + 1,)),
                            pltpu.VMEM((2, tm, d), BF16), pltpu.SemaphoreType.DMA((2,)),
                            pltpu.VMEM((d, ff), F32), pltpu.VMEM((d, ff), F32), pltpu.VMEM((ff, d), F32),
                            pltpu.SemaphoreType.DMA((3,)),
                            pltpu.VMEM((d, ff), BF16), pltpu.VMEM((d, ff), BF16), pltpu.VMEM((ff, d), BF16)],
        ),
        out_shape=jax.ShapeDtypeStruct(((n_tiles + 2) * tm, d), BF16),
        compiler_params=_params("arbitrary"),
        name="experts",
    )(te, src, nact, xb, w_gate, w_up, w_down)


def _combine_kernel(dst_ref, lpos_ref, x2_ref, g_ref, ys_ref, out_ref, ybuf, sem):
    b = pl.program_id(0)
    tb = x2_ref.shape[0]
    lg = LOCAL_ROWS // GROUP
    nb = dst_ref.shape[0] // lg
    ring = GATHER_AHEAD + 1
    slot = lax.rem(b, ring)

    def start_gather(blk):
        for i in range(lg):
            _group_copy(ys_ref, dst_ref[blk * lg + i], ybuf, lax.rem(blk, ring), i, sem).start()

    @pl.when(b == 0)
    def _():
        for a in range(min(GATHER_AHEAD, nb)):
            start_gather(a)

    @pl.when(b + GATHER_AHEAD < nb)
    def _():
        start_gather(b + GATHER_AHEAD)

    for i in range(lg):
        _group_copy(ys_ref, 0, ybuf, slot, i, sem).wait()
    lpos = lpos_ref[...]
    y = jnp.zeros(x2_ref.shape, F32)
    for c in range(LOCAL_ROWS // SORT_CHUNK):
        m0, m1 = _slot_masks(lpos, c, tb)
        p = jnp.where(m0, 1.0, jnp.where(m1, 1.0, 0.0)).astype(BF16)
        y = y + _dot_tn(p, ybuf[slot, c * SORT_CHUNK:(c + 1) * SORT_CHUNK, :])
    out_ref[...] = _rms(x2_ref[...] + y, g_ref[...])


def _combine(dst, lpos, x2, g_final, ys):
    n, d = x2.shape
    tb = TOKEN_BLOCK
    return pl.pallas_call(
        _combine_kernel,
        grid_spec=pltpu.PrefetchScalarGridSpec(
            num_scalar_prefetch=1,
            grid=(n // tb,),
            in_specs=[pl.BlockSpec((None, 1, TOP_K * tb), lambda i, dst: (i, 0, 0)),
                      pl.BlockSpec((tb, d), lambda i, dst: (i, 0)), pl.BlockSpec((1, d), lambda i, dst: (0, 0)),
                      pl.BlockSpec(memory_space=pl.ANY)],
            out_specs=pl.BlockSpec((tb, d), lambda i, dst: (i, 0)),
            scratch_shapes=[pltpu.VMEM((GATHER_AHEAD + 1, LOCAL_ROWS, d), BF16),
                            pltpu.SemaphoreType.DMA((GATHER_AHEAD + 1,))],
        ),
        out_shape=jax.ShapeDtypeStruct((n, d), F32),
        compiler_params=_params("arbitrary"),
        name="combine",
    )(dst, lpos, x2, g_final, ys)


def kernel(x, mem, norm_mix_g, w_in, conv_w, conv_b, conv_ln_g, conv_ln_b, conv_w_out, hgrn_lb_logits, hgrn_onorm_g, hgrn_w_out, w_mix_out, norm_xa_g, norm_mem_g, xa_w_q, xa_w_k, xa_w_v, xa_w_o, norm_ffn_g, router_group_w, router_group_b, router_expert_w, router_expert_b, moe_w_gate, moe_w_up, moe_w_down, final_norm_g):
    batch, seq, d = x.shape
    n = batch * seq
    assert w_in.shape[0] == 1, "the final RMSNorm is fused into the single layer's combine step"
    cw = conv_w.shape[2]
    kw = hgrn_w_out.shape[1]
    assert kw == HGRN_HEADS * HGRN_DIM and conv_w.shape[1] == CONV_K
    assert seq % TOKEN_BLOCK == 0 and TOKEN_BLOCK % CHUNK == 0 and TOKEN_BLOCK % CONV_ROWS == 0
    assert moe_w_gate.shape[1] == N_EXPERTS and router_group_w.shape[2] == N_GROUPS and TOP_K == 2
    na = TOP_K * TOKEN_BLOCK
    assert LOCAL_ROWS % SORT_CHUNK == 0 and LOCAL_ROWS >= na + N_EXPERTS * (GROUP - 1) + GROUP

    n_tiles = -(-(n // TOKEN_BLOCK) * (na + N_EXPERTS * (GROUP - 1)) // MOE_TILE) + N_EXPERTS
    n_tiles = -(-n_tiles // 16) * 16
    vec = lambda p: p.reshape(1, -1)
    l = 0

    x2d = x.reshape(n, d)
    x1 = _mixer(x2d, vec(norm_mix_g[l]), w_in[l], conv_w[l], vec(conv_b[l]), vec(conv_ln_g[l]), vec(conv_ln_b[l]),
                conv_w_out[l].astype(BF16), hgrn_lb_logits[l:l + 2], vec(hgrn_onorm_g[l]),
                hgrn_w_out[l].astype(BF16), w_mix_out[l].astype(BF16), cw, kw, batch)
    k_bf, v_bf = _memkv(mem.reshape(-1, d), vec(norm_mem_g[l]), xa_w_k[l].astype(BF16), xa_w_v[l].astype(BF16), batch)
    pad = ROUTE_ROWS - N_GROUPS - N_EXPERTS
    w_route_t = jnp.pad(jnp.concatenate([router_group_w[l], router_expert_w[l]], axis=1).T, ((0, pad), (0, 0)))
    b_route = jnp.pad(jnp.concatenate([router_group_b[l], router_expert_b[l]]), (0, pad)).reshape(ROUTE_ROWS, 1)
    x2, xb, cnt, lpos = _attn(x1, vec(norm_xa_g[l]), xa_w_q[l].astype(BF16), k_bf, v_bf, xa_w_o[l].astype(BF16),
                              vec(norm_ffn_g[l]), w_route_t, b_route, batch)
    te, src, dst, nact = _plan(cnt.reshape(-1, 1), n_tiles)
    ys = _experts(te, src, nact, xb, moe_w_gate[l], moe_w_up[l], moe_w_down[l], n_tiles)
    out = _combine(dst, lpos, x2, vec(final_norm_g), ys)
    return out.reshape(batch, seq, d)
```

```python
import functools

import jax
import jax.numpy as jnp
from jax import lax
from jax.experimental import pallas as pl
from jax.experimental.pallas import tpu as pltpu

F32 = jnp.float32
BF16 = jnp.bfloat16
I32 = jnp.int32

EPS = 1e-6
CONV_K = 31
CONV_HALO = 32
CONV_ROWS = 64
HGRN_HEADS = 4
HGRN_DIM = 128
CHUNK = 64
SUB = 16
XA_HEADS = 4
N_GROUPS = 4
EXPERTS_PER_GROUP = 8
N_EXPERTS = N_GROUPS * EXPERTS_PER_GROUP
TOP_K = 2
ROUTE_ROWS = 40
TOKEN_BLOCK = 512
MOE_TILE = 256
GROUP = 16
LOCAL_ROWS = 1536
SORT_CHUNK = 256
RANK_SEGMENT = 128
GATHER_AHEAD = 4
W_LANES = 128
PLAN_CHUNKS = (512, 256, 128)
VMEM_LIMIT_BYTES = 48 * 1024 * 1024


def _rms(x, g):
    return x * lax.rsqrt(jnp.mean(x * x, axis=-1, keepdims=True) + EPS) * g


def _dot(a, b):
    return jnp.dot(a, b, preferred_element_type=F32)


def _dot_nt(a, b):
    return lax.dot_general(a, b, (((1,), (1,)), ((), ())), preferred_element_type=F32)


def _dot_tn(a, b):
    return lax.dot_general(a, b, (((0,), (0,)), ((), ())), preferred_element_type=F32)


def _split_bf16(x):
    hi = x.astype(BF16)
    lo = (x - hi.astype(F32)).astype(BF16)
    return hi, lo


def _params(*sem):
    return pltpu.CompilerParams(dimension_semantics=sem, vmem_limit_bytes=VMEM_LIMIT_BYTES)


def _conv_stages(un, first, cw_ref, cb_ref, lg_ref, lb_ref, ext_ref, halo_ref, perm_ref, act_ref):
    tb, c = un.shape
    nt = tb // 8
    slabs = c // 128
    hr = CONV_HALO * 8
    lanes = [slice(l * 128, (l + 1) * 128) for l in range(slabs)]

    @pl.when(first)
    def _():
        halo_ref[...] = jnp.zeros_like(halo_ref)

    per = nt // 8
    for j in range(nt):
        start = hr + (j % per) * 64 + j // per
        for l in range(slabs):
            ext_ref[l, pl.ds(start, 8, stride=8), :] = un[8 * j:8 * j + 8, lanes[l]]
    first_row = lax.broadcasted_iota(I32, (hr, 128), 0) % 8 == 0
    for l in range(slabs):
        cur = ext_ref[l, nt * 8:nt * 8 + hr, :]
        ext_ref[l, 0:hr, :] = jnp.where(first_row, pltpu.roll(halo_ref[l], hr - 7, axis=0),
                                        pltpu.roll(cur, 1, axis=0))
        halo_ref[l] = cur

    def row_tile(r):
        accs = []
        for l in range(slabs):
            acc = jnp.broadcast_to(cb_ref[:, lanes[l]], (CONV_ROWS, 128))
            for dt in range(CONV_K):
                off = hr + r * CONV_ROWS - dt * 8
                acc = acc + cw_ref[CONV_K - 1 - dt:CONV_K - dt, lanes[l]] * ext_ref[l, off:off + CONV_ROWS, :]
            accs.append(acc)
        mu = functools.reduce(jnp.add, [jnp.sum(a, axis=-1, keepdims=True) for a in accs]) * (1.0 / c)
        cens = [a - mu for a in accs]
        var = functools.reduce(jnp.add, [jnp.sum(a * a, axis=-1, keepdims=True) for a in cens]) * (1.0 / c)
        inv = lax.rsqrt(var + EPS)
        for l in range(slabs):
            ln = cens[l] * inv * lg_ref[:, lanes[l]] + lb_ref[:, lanes[l]]
            perm_ref[l, r * CONV_ROWS:(r + 1) * CONV_ROWS, :] = jax.nn.silu(ln)

    def finish():
        for j in range(nt):
            start = (j % per) * 64 + j // per
            for l in range(slabs):
                act_ref[8 * j:8 * j + 8, lanes[l]] = perm_ref[l, pl.ds(start, 8, stride=8), :]

    return [functools.partial(row_tile, r) for r in range(tb // CONV_ROWS)], finish


def _mixin_kernel(x_ref, g_ref, w_hbm, cw_ref, cb_ref, lg_ref, lb_ref, wo_ref,
                  yc_ref, q_ref, fr_ref, iv_ref, og_ref, gate_ref,
                  w_ref, w_stage, w_sem, ext_ref, halo_ref, perm_ref, act_ref, *, cw, kw):
    d = yc_ref.shape[1]

    @pl.when(jnp.logical_and(pl.program_id(0) == 0, pl.program_id(1) == 0))
    def _():
        width = w_stage.shape[2]

        def fetch(c):
            return pltpu.make_async_copy(w_hbm.at[:, pl.ds(c * width, width)], w_stage.at[c % 2], w_sem.at[c % 2])

        n_fetch = w_ref.shape[1] // width
        fetch(0).start()
        for c in range(n_fetch):
            if c + 1 < n_fetch:
                fetch(c + 1).start()
            fetch(c).wait()
            w_ref[:, c * width:(c + 1) * width] = w_stage[c % 2].astype(BF16)

    hb = _rms(x_ref[...], g_ref[...]).astype(BF16)

    def proj(lo, width):
        return _dot(hb, w_ref[:, lo:lo + width])

    conv_tiles, conv_finish = _conv_stages(proj(0, cw) * jax.nn.sigmoid(proj(cw, cw)), pl.program_id(1) == 0,
                                           cw_ref, cb_ref, lg_ref, lb_ref, ext_ref, halo_ref, perm_ref, act_ref)
    def chunk(ref, col, lo, act):
        def run():
            ref[:, col * kw:(col + 1) * kw] = act(proj(lo, kw)).astype(ref.dtype)
        return run

    base = 2 * cw
    gbase = base + 4 * kw
    chunks = [chunk(q_ref, 0, base, jax.nn.silu), chunk(fr_ref, 0, base + kw, lambda v: v),
              chunk(iv_ref, 0, base + 2 * kw, lambda v: v), chunk(og_ref, 0, base + 3 * kw, jax.nn.silu)]
    for c in range(d // kw):
        chunks.append(chunk(gate_ref, c, gbase + d + c * kw, jax.nn.sigmoid))
        chunks.append(chunk(yc_ref, c, gbase + c * kw, jax.nn.sigmoid))
    for i in range(max(len(chunks), len(conv_tiles))):
        if i < len(chunks):
            chunks[i]()
        if i < len(conv_tiles):
            conv_tiles[i]()
    conv_finish()
    y_conv = _dot(act_ref[...].astype(BF16), wo_ref[...])
    yc_ref[...] = (yc_ref[...].astype(F32) * y_conv).astype(BF16)


def _hgrn_block(q_ref, fr_ref, iv_ref, lb, st_ref, tri):
    tb = q_ref.shape[0]
    chunks = [slice(c * CHUNK, (c + 1) * CHUNK) for c in range(tb // CHUNK)]
    heads = [slice(h * HGRN_DIM, (h + 1) * HGRN_DIM) for h in range(HGRN_HEADS)]

    qs, vs, kks, cums = [], [], [], []
    for rows in chunks:
        f = lb + (1.0 - lb) * jax.nn.sigmoid(fr_ref[rows, :])
        lf_hi, lf_lo = _split_bf16(jnp.log(f))
        cums.append(_dot(tri, jnp.concatenate([lf_hi, lf_lo], axis=0)))
        kks.append(1.0 - f)
        qs.append(q_ref[rows, :].astype(F32))
        vs.append(iv_ref[rows, :])

    qes, kds, decays, blocks = [], [], [], []
    for q, kk, cum in zip(qs, kks, cums):
        last = cum[CHUNK - 1:CHUNK, :]
        qes.append((q * jnp.exp(cum)).astype(BF16))
        kds.append((kk * jnp.exp(last - cum)).astype(BF16))
        decays.append(jnp.exp(last))
        sub = []
        for i in range(CHUNK // SUB):
            rs, ne = i * SUB, (i + 1) * SUB
            ref = cum[rs + SUB // 2 - 1:rs + SUB // 2, :]
            qt = (q[rs:ne] * jnp.exp(cum[rs:ne] - ref)).astype(BF16)
            kt = (kk[0:ne] * jnp.exp(ref - cum[0:ne])).astype(BF16)
            sub.append((qt, kt))
        blocks.append(sub)

    updates = [[_dot_tn(v[:, hs], kd[:, hs]) for hs in heads] for v, kd in zip(vs, kds)]
    scores = [[[_dot_nt(qt[:, hs], kt[:, hs]) for qt, kt in sub] for hs in heads] for sub in blocks]

    states = [st_ref[h] for h in range(HGRN_HEADS)]
    inter = []
    for qe, decay, upd in zip(qes, decays, updates):
        inter.append([_dot_nt(qe[:, hs], st.astype(BF16)) for hs, st in zip(heads, states)])
        states = [st * decay[:, hs] + u for st, hs, u in zip(states, heads, upd)]
    for h in range(HGRN_HEADS):
        st_ref[h] = states[h]

    outs = []
    for v, sc, o_inter in zip(vs, scores, inter):
        per_head = []
        for h, hs in enumerate(heads):
            parts = []
            for i, a in enumerate(sc[h]):
                rs, ne = i * SUB, (i + 1) * SUB
                trow = lax.broadcasted_iota(I32, (SUB, ne), 0) + rs
                scol = lax.broadcasted_iota(I32, (SUB, ne), 1)
                a = jnp.where(scol <= trow, a, 0.0).astype(BF16)
                parts.append(_dot(a, v[0:ne, hs]))
            per_head.append(o_inter[h] + jnp.concatenate(parts, axis=0))
        outs.append(jnp.concatenate(per_head, axis=1))
    return outs


def _hgrn_kernel(q_ref, fr_ref, iv_ref, og_ref, gate_ref, yc_ref, x_ref, lbl_ref, on_ref, wo_ref, wm_ref,
                 x1_ref, st_ref, ob_ref):
    tb = q_ref.shape[0]

    @pl.when(pl.program_id(1) == 0)
    def _():
        st_ref[...] = jnp.zeros_like(st_ref)

    l0, l1 = lbl_ref[0:1, :], lbl_ref[1:2, :]
    m = jnp.maximum(l0, l1)
    e0, e1 = jnp.exp(l0 - m), jnp.exp(l1 - m)
    lb_all = e0 / (e0 + e1)
    trow = lax.broadcasted_iota(I32, (CHUNK, 2 * CHUNK), 0)
    tcol = lax.broadcasted_iota(I32, (CHUNK, 2 * CHUNK), 1) % CHUNK
    tri = jnp.where(tcol <= trow, 1.0, 0.0).astype(BF16)

    for c, o in enumerate(_hgrn_block(q_ref, fr_ref, iv_ref, lb_all, st_ref, tri)):
        rows = slice(c * CHUNK, (c + 1) * CHUNK)
        og = og_ref[rows, :].astype(F32)
        for h in range(HGRN_HEADS):
            hs = slice(h * HGRN_DIM, (h + 1) * HGRN_DIM)
            ob_ref[rows, hs] = (_rms(o[:, hs], on_ref[...]) * og[:, hs]).astype(BF16)
    y_rec = _dot(ob_ref[...], wo_ref[...])
    merged = yc_ref[...].astype(F32) + gate_ref[...].astype(F32) * y_rec
    x1_ref[...] = x_ref[...] + _dot(merged.astype(BF16), wm_ref[...])


def _mixer_kernel(x_ref, g_ref, w_hbm, cw_ref, cb_ref, lg_ref, lb_ref, wco_ref, lbl_ref, on_ref, wro_ref, wm_ref,
                  x1_ref, yc_s, q_s, fr_s, iv_s, og_s, gate_s, w_ref, w_stage, w_sem, ext_ref, halo_ref, perm_ref,
                  act_ref, st_ref, ob_ref, *, cw, kw):
    _mixin_kernel(x_ref, g_ref, w_hbm, cw_ref, cb_ref, lg_ref, lb_ref, wco_ref, yc_s, q_s, fr_s, iv_s, og_s, gate_s,
                  w_ref, w_stage, w_sem, ext_ref, halo_ref, perm_ref, act_ref, cw=cw, kw=kw)
    _hgrn_kernel(q_s, fr_s, iv_s, og_s, gate_s, yc_s, x_ref, lbl_ref, on_ref, wro_ref, wm_ref, x1_ref, st_ref, ob_ref)


def _mixer(x2d, g, w_in, conv_w, conv_b, ln_g, ln_b, w_conv_out_bf, lb_logits, onorm_g, w_o_bf, w_mix_bf, cw, kw,
           batch):
    n, d = x2d.shape
    tb = TOKEN_BLOCK
    nsb = n // batch // tb
    row = lambda b, s: (b * nsb + s, 0)
    fixed = lambda b, s: (0, 0)
    return pl.pallas_call(
        functools.partial(_mixer_kernel, cw=cw, kw=kw),
        grid=(batch, nsb),
        in_specs=[pl.BlockSpec((tb, d), row), pl.BlockSpec((1, d), fixed), pl.BlockSpec(memory_space=pl.ANY),
                  pl.BlockSpec(conv_w.shape, fixed), pl.BlockSpec((1, cw), fixed), pl.BlockSpec((1, cw), fixed),
                  pl.BlockSpec((1, cw), fixed), pl.BlockSpec((cw, d), fixed),
                  pl.BlockSpec(lb_logits.shape, fixed), pl.BlockSpec((1, HGRN_DIM), fixed),
                  pl.BlockSpec(w_o_bf.shape, fixed), pl.BlockSpec(w_mix_bf.shape, fixed)],
        out_specs=pl.BlockSpec((tb, d), row),
        out_shape=jax.ShapeDtypeStruct((n, d), F32),
        scratch_shapes=[pltpu.VMEM((tb, d), BF16), pltpu.VMEM((tb, kw), BF16), pltpu.VMEM((tb, kw), F32),
                        pltpu.VMEM((tb, kw), BF16), pltpu.VMEM((tb, kw), BF16), pltpu.VMEM((tb, d), BF16),
                        pltpu.VMEM(w_in.shape, BF16), pltpu.VMEM((2, d, kw), F32), pltpu.SemaphoreType.DMA((2,)),
                        pltpu.VMEM((cw // 128, tb + CONV_HALO * 8, 128), F32),
                        pltpu.VMEM((cw // 128, CONV_HALO * 8, 128), F32),
                        pltpu.VMEM((cw // 128, tb, 128), F32), pltpu.VMEM((tb, cw), F32),
                        pltpu.VMEM((HGRN_HEADS, HGRN_DIM, HGRN_DIM), F32), pltpu.VMEM((tb, kw), BF16)],
        compiler_params=_params("arbitrary", "arbitrary"),
        name="mixer",
    )(x2d, g, w_in, conv_w, conv_b, ln_g, ln_b, w_conv_out_bf, lb_logits, onorm_g, w_o_bf, w_mix_bf)


def _memkv_kernel(mem_ref, g_ref, wk_ref, wv_ref, k_ref, v_ref):
    mb = _rms(mem_ref[...], g_ref[...]).astype(BF16)
    k_ref[...] = _dot(mb, wk_ref[...]).astype(BF16)
    v_ref[...] = _dot(mb, wv_ref[...]).astype(BF16)


def _memkv(mem2d, g, wk_bf, wv_bf, batch):
    n, d = mem2d.shape
    m = n // batch
    row = lambda b: (b, 0)
    fixed = lambda b: (0, 0)
    return pl.pallas_call(
        _memkv_kernel,
        grid=(batch,),
        in_specs=[pl.BlockSpec((m, d), row), pl.BlockSpec((1, d), fixed), pl.BlockSpec((d, d), fixed),
                  pl.BlockSpec((d, d), fixed)],
        out_specs=[pl.BlockSpec((m, d), row)] * 2,
        out_shape=[jax.ShapeDtypeStruct((n, d), BF16)] * 2,
        compiler_params=_params("parallel"),
        name="memkv",
    )(mem2d, g, wk_bf, wv_bf)


def _route(lt):
    def row(r):
        return lt[r:r + 1, :]

    gl = [row(g) for g in range(N_GROUPS)]
    gmax = functools.reduce(jnp.maximum, gl)
    g_p = 1.0 / functools.reduce(jnp.add, [jnp.exp(l - gmax) for l in gl])
    gidx = jnp.full(gmax.shape, N_GROUPS - 1, I32)
    for g in range(N_GROUPS - 2, -1, -1):
        gidx = jnp.where(gl[g] == gmax, g, gidx)

    el = []
    for j in range(EXPERTS_PER_GROUP):
        v = row(N_GROUPS + (N_GROUPS - 1) * EXPERTS_PER_GROUP + j)
        for g in range(N_GROUPS - 2, -1, -1):
            v = jnp.where(gidx == g, row(N_GROUPS + g * EXPERTS_PER_GROUP + j), v)
        el.append(v)

    def argmax(vals):
        mx = functools.reduce(jnp.maximum, vals)
        idx = jnp.full(mx.shape, EXPERTS_PER_GROUP - 1, I32)
        for j in range(EXPERTS_PER_GROUP - 2, -1, -1):
            idx = jnp.where(vals[j] == mx, j, idx)
        return mx, idx

    m1, i1 = argmax(el)
    m2, i2 = argmax([jnp.where(i1 == j, -jnp.inf, el[j]) for j in range(EXPERTS_PER_GROUP)])
    r = jnp.exp(m2 - m1)
    w1 = g_p / (1.0 + r)
    w2 = g_p * r / (1.0 + r)
    base = gidx * EXPERTS_PER_GROUP
    return jnp.concatenate([base + i1, base + i2], axis=0), jnp.concatenate([w1, w2], axis=0)


def _excl_cumsum_rows(col):
    r = col.shape[0]
    lower = lax.broadcasted_iota(I32, (r, r), 1) < lax.broadcasted_iota(I32, (r, r), 0)
    lower_bf = jnp.where(lower, 1.0, 0.0).astype(BF16)
    hi, lo = _split_bf16(jnp.broadcast_to(col, (r, 128)))
    return (_dot(lower_bf, hi) + _dot(lower_bf, lo))[:, 0:1]


def _slot_masks(lpos, chunk, tb):
    slot = lax.broadcasted_iota(I32, (SORT_CHUNK, tb), 0) + chunk * SORT_CHUNK
    return slot == lpos[:, 0:tb], slot == lpos[:, tb:2 * tb]


def _earlier_same_expert(ones):
    seg = RANK_SEGMENT
    n_exp, n = ones.shape
    upper = lax.broadcasted_iota(I32, (seg, seg), 0) < lax.broadcasted_iota(I32, (seg, seg), 1)
    pieces = [ones[:, s * seg:(s + 1) * seg] for s in range(n // seg)]
    within = _dot(jnp.concatenate([p.astype(BF16) for p in pieces], axis=0), jnp.where(upper, 1.0, 0.0).astype(BF16))
    seen = jnp.zeros((n_exp, 1), F32)
    out = []
    for s, p in enumerate(pieces):
        out.append(within[s * n_exp:(s + 1) * n_exp, :] + seen)
        seen = seen + jnp.sum(p, axis=1, keepdims=True)
    return jnp.concatenate(out, axis=1), seen


def _local_sort(eid, wts, h2, xb_ref, cnt_ref, lpos_ref):
    tb, d = h2.shape
    na = TOP_K * tb
    e_all = jnp.concatenate([eid[k:k + 1] for k in range(TOP_K)], axis=1)
    onehot = lax.broadcasted_iota(I32, (N_EXPERTS, na), 0) == e_all
    earlier, cnt = _earlier_same_expert(jnp.where(onehot, 1.0, 0.0))
    cnt_pad = jnp.floor((cnt + (GROUP - 1)) * (1.0 / GROUP)) * GROUP
    start = _excl_cumsum_rows(cnt_pad)
    lpos = jnp.sum(jnp.where(onehot, start + earlier, 0.0), axis=0, keepdims=True).astype(I32)
    hb = h2.astype(BF16)
    n_chunks = LOCAL_ROWS // SORT_CHUNK

    def sort_chunk(c):
        m0, m1 = _slot_masks(lpos, c, tb)
        p = jnp.where(m0, 1.0, jnp.where(m1, 1.0, 0.0)).astype(BF16)
        pw = jnp.where(m0, wts[0:1], jnp.where(m1, wts[1:2], 0.0))
        rows = slice(c * SORT_CHUNK, (c + 1) * SORT_CHUNK)
        xb_ref[rows, 0:d] = _dot(p, hb).astype(BF16)
        w_row = jnp.broadcast_to(jnp.sum(pw, axis=1, keepdims=True), (SORT_CHUNK, W_LANES))
        w_hi = w_row.astype(BF16).astype(F32)
        low_half = lax.broadcasted_iota(I32, (SORT_CHUNK, W_LANES), 1) < W_LANES // 2
        xb_ref[rows, d:d + W_LANES] = jnp.where(low_half, w_hi, w_row - w_hi).astype(BF16)

    for c in range(n_chunks - 1):
        sort_chunk(c)
    last_used = jnp.sum(cnt_pad) > (n_chunks - 1) * SORT_CHUNK

    @pl.when(last_used)
    def _():
        sort_chunk(n_chunks - 1)

    @pl.when(jnp.logical_not(last_used))
    def _():
        xb_ref[(n_chunks - 1) * SORT_CHUNK:, :] = jnp.zeros((SORT_CHUNK, d + W_LANES), BF16)

    cnt_ref[...] = cnt_pad.astype(I32)
    lpos_ref[...] = lpos


def _attn_kernel(x1_ref, gxa_ref, wq_ref, k_ref, v_ref, wo_ref, gffn_ref, wr_ref, br_ref,
                 x2_ref, xb_ref, cnt_ref, lpos_ref):
    x1 = x1_ref[...]
    d = x1.shape[1]
    hd = d // XA_HEADS
    q = _dot(_rms(x1, gxa_ref[...]).astype(BF16), wq_ref[...]).astype(BF16)
    heads = []
    for h in range(XA_HEADS):
        hs = slice(h * hd, (h + 1) * hd)
        sc = _dot_nt(q[:, hs], k_ref[:, hs]) * (hd ** -0.5)
        p = jnp.exp(sc - jnp.max(sc, axis=-1, keepdims=True))
        p = p / jnp.sum(p, axis=-1, keepdims=True)
        heads.append(_dot(p.astype(BF16), v_ref[:, hs]).astype(BF16))
    x2 = x1 + _dot(jnp.concatenate(heads, axis=1), wo_ref[...])
    x2_ref[...] = x2
    h2 = _rms(x2, gffn_ref[...])
    h_hi, h_lo = _split_bf16(h2)
    w_hi, w_lo = _split_bf16(wr_ref[...])
    lt = _dot_nt(w_hi, h_hi) + (_dot_nt(w_hi, h_lo) + _dot_nt(w_lo, h_hi)) + br_ref[...]
    eid, wts = _route(lt)
    _local_sort(eid, wts, h2, xb_ref, cnt_ref, lpos_ref)


def _attn(x1, gxa, wq_bf, k_bf, v_bf, wo_bf, gffn, w_route_t, b_route, batch):
    n, d = x1.shape
    m = k_bf.shape[0] // batch
    tb = TOKEN_BLOCK
    nsb = n // batch // tb
    nb = n // tb
    row = lambda b, s: (b * nsb + s, 0)
    blk3 = lambda b, s: (b * nsb + s, 0, 0)
    fixed = lambda b, s: (0, 0)
    mem = lambda b, s: (b, 0)
    return pl.pallas_call(
        _attn_kernel,
        grid=(batch, nsb),
        in_specs=[pl.BlockSpec((tb, d), row), pl.BlockSpec((1, d), fixed), pl.BlockSpec((d, d), fixed),
                  pl.BlockSpec((m, d), mem), pl.BlockSpec((m, d), mem), pl.BlockSpec((d, d), fixed),
                  pl.BlockSpec((1, d), fixed), pl.BlockSpec((ROUTE_ROWS, d), fixed),
                  pl.BlockSpec((ROUTE_ROWS, 1), fixed)],
        out_specs=[pl.BlockSpec((tb, d), row), pl.BlockSpec((LOCAL_ROWS, d + W_LANES), row),
                   pl.BlockSpec((None, N_EXPERTS, 1), blk3), pl.BlockSpec((None, 1, TOP_K * tb), blk3)],
        out_shape=[jax.ShapeDtypeStruct((n, d), F32), jax.ShapeDtypeStruct((nb * LOCAL_ROWS, d + W_LANES), BF16),
                   jax.ShapeDtypeStruct((nb, N_EXPERTS, 1), I32), jax.ShapeDtypeStruct((nb, 1, TOP_K * tb), I32)],
        compiler_params=_params("parallel", "parallel"),
        name="attn",
    )(x1, gxa, wq_bf, k_bf, v_bf, wo_bf, gffn, w_route_t, b_route)


def _plan_kernel(cnt_ref, te_ref, src_ref, dst_ref, nact_ref):
    runs = cnt_ref.shape[0]
    lg = LOCAL_ROWS // GROUP
    tg = MOE_TILE // GROUP
    sh = N_EXPERTS.bit_length() - 1
    emask = N_EXPERTS - 1
    zero_group = lg - 1

    length = cnt_ref[...].astype(F32) * (1.0 / GROUP)
    len_bf = jnp.broadcast_to(length, (runs, 128)).astype(BF16)
    ri = lax.broadcasted_iota(I32, (runs, runs), 0)
    ci = lax.broadcasted_iota(I32, (runs, runs), 1)
    r_e, c_e = ri & emask, ci & emask
    r_b, c_b = lax.shift_right_logical(ri, sh), lax.shift_right_logical(ci, sh)
    same_expert_earlier = jnp.where(r_e == c_e, jnp.where(c_b < r_b, 1.0, 0.0), 0.0).astype(BF16)
    same_block_earlier = jnp.where(r_b == c_b, jnp.where(c_e < r_e, 1.0, 0.0), 0.0).astype(BF16)
    before = _dot(same_expert_earlier, len_bf)[:, 0:1]
    local = _dot(same_block_earlier, len_bf)[:, 0:1]

    of_expert = (lax.broadcasted_iota(I32, (N_EXPERTS, runs), 1) & emask) == lax.broadcasted_iota(
        I32, (N_EXPERTS, runs), 0)
    total = _dot(jnp.where(of_expert, 1.0, 0.0).astype(BF16), len_bf)[:, 0:1]
    tiles = jnp.floor((total + (tg - 1)) * (1.0 / tg))
    tile0 = _excl_cumsum_rows(tiles)
    to_run = (lax.broadcasted_iota(I32, (runs, N_EXPERTS), 0) & emask) == lax.broadcasted_iota(
        I32, (runs, N_EXPERTS), 1)
    to_run_bf = jnp.where(to_run, 1.0, 0.0).astype(BF16)
    base_hi, base_lo = _split_bf16(jnp.broadcast_to(tile0 * tg, (N_EXPERTS, 128)))
    g_start = (_dot(to_run_bf, base_hi) + _dot(to_run_bf, base_lo))[:, 0:1] + before
    block = lax.shift_right_logical(lax.broadcasted_iota(I32, (runs, 1), 0), sh).astype(F32)
    l_start = block * lg + local

    def cover(out_ref, start, offset, default):
        stop = start + length
        shift = offset - default
        n_out = out_ref.shape[1]
        chunk = next(c for c in PLAN_CHUNKS if n_out % c == 0)
        for c in range(n_out // chunk):
            j = (lax.broadcasted_iota(I32, (runs, chunk), 1) + c * chunk).astype(F32)
            hit = jnp.where(start <= j, jnp.where(j < stop, j + shift, 0.0), 0.0)
            out = jnp.sum(hit, axis=0, keepdims=True) + default
            out_ref[:, c * chunk:(c + 1) * chunk] = out.astype(I32)

    cover(src_ref, g_start, l_start - g_start, float(zero_group))
    cover(dst_ref, l_start, g_start - l_start, 0.0)
    t = lax.broadcasted_iota(I32, (N_EXPERTS, te_ref.shape[1]), 1).astype(F32)
    te_ref[...] = jnp.sum(jnp.where(t >= tile0 + tiles, 1.0, 0.0), axis=0, keepdims=True).astype(I32)
    nact_ref[...] = jnp.sum(jnp.broadcast_to(tiles, (N_EXPERTS, 128)), axis=0, keepdims=True).astype(I32)


def _plan(cnt, n_tiles):
    runs = cnt.shape[0]
    n_src = n_tiles * (MOE_TILE // GROUP)
    n_dst = runs // N_EXPERTS * (LOCAL_ROWS // GROUP)
    n_te = -(-n_tiles // 128) * 128
    te, src, dst, nact = pl.pallas_call(
        _plan_kernel,
        out_shape=[jax.ShapeDtypeStruct((1, n_te), I32), jax.ShapeDtypeStruct((1, n_src), I32),
                   jax.ShapeDtypeStruct((1, n_dst), I32), jax.ShapeDtypeStruct((1, 128), I32)],
        compiler_params=_params(),
        name="plan",
    )(cnt)
    return te.reshape(-1), src.reshape(-1), dst.reshape(-1), nact[0, :1]


def _group_copy(src_hbm, src_group, dst_buf, slot, index, sem):
    start = src_group * GROUP
    rows = pl.ds(start if isinstance(start, int) else pl.multiple_of(start, GROUP), GROUP)
    return pltpu.make_async_copy(src_hbm.at[rows], dst_buf.at[slot, pl.ds(index * GROUP, GROUP)], sem.at[slot])


def _experts_kernel(te_ref, src_ref, nact_ref, xb_ref, wg_hbm, wu_hbm, wd_hbm, ys_ref,
                    xbuf, sem, obuf, osem, wg_st, wu_st, wd_st, wsem, wg_bf, wu_bf, wd_bf):
    tg = MOE_TILE // GROUP
    tm, d = obuf.shape[1], obuf.shape[2]
    nact = nact_ref[0]
    n_tiles = ys_ref.shape[0] // tm - 2

    def start_gather(tile, slot):
        for i in range(tg):
            _group_copy(xb_ref, src_ref[tile * tg + i], xbuf, slot, i, sem).start()

    def wait_gather(slot):
        for i in range(tg):
            _group_copy(xb_ref, 0, xbuf, slot, i, sem).wait()

    def weight_copies(expert):
        pairs = ((wg_hbm, wg_st), (wu_hbm, wu_st), (wd_hbm, wd_st))
        return [pltpu.make_async_copy(w.at[expert], st, wsem.at[i]) for i, (w, st) in enumerate(pairs)]

    def out_copy(tile, slot):
        rows = pl.ds(pl.multiple_of(tile * tm, tm), tm)
        return pltpu.make_async_copy(obuf.at[slot], ys_ref.at[rows], osem.at[slot])

    obuf[...] = jnp.zeros_like(obuf)
    for s in range(2):
        out_copy(n_tiles + s, s).start()

    ring = GATHER_AHEAD + 1

    @pl.when(nact > 0)
    def _():
        for a in range(GATHER_AHEAD):
            start_gather(jnp.minimum(a, nact - 1), a)
        for c in weight_copies(te_ref[0]):
            c.start()

    def tile_step(t, carry):
        e = te_ref[t]
        slot = lax.rem(t, 2)
        xslot = lax.rem(t, ring)

        @pl.when(jnp.logical_or(t == 0, e != te_ref[jnp.maximum(t - 1, 0)]))
        def _():
            for c in weight_copies(e):
                c.wait()
            wg_bf[...] = wg_st[...].astype(BF16)
            wu_bf[...] = wu_st[...].astype(BF16)
            wd_bf[...] = wd_st[...].astype(BF16)
            nxt = lax.while_loop(lambda j: jnp.logical_and(j < nact, te_ref[jnp.minimum(j, n_tiles - 1)] == e),
                                 lambda j: j + 1, t + 1)

            @pl.when(nxt < nact)
            def _():
                for c in weight_copies(te_ref[jnp.minimum(nxt, n_tiles - 1)]):
                    c.start(priority=1)

        wait_gather(xslot)
        out_copy(t, slot).wait()
        x = xbuf[xslot, :, 0:d]
        hid = jax.nn.silu(_dot(x, wg_bf[...])) * _dot(x, wu_bf[...])
        start_gather(jnp.minimum(t + GATHER_AHEAD, nact - 1), lax.rem(t + GATHER_AHEAD, ring))
        weight = (xbuf[xslot, :, d:d + 1].astype(F32)
                  + xbuf[xslot, :, d + W_LANES // 2:d + W_LANES // 2 + 1].astype(F32))
        obuf[slot] = (_dot(hid.astype(BF16), wd_bf[...]) * weight).astype(BF16)
        out_copy(t, slot).start()
        return carry

    lax.fori_loop(0, nact, tile_step, 0)

    @pl.when(nact > 0)
    def _():
        for a in range(GATHER_AHEAD):
            wait_gather(lax.rem(nact + a, ring))
    for s in range(2):
        out_copy(0, s).wait()
    obuf[0] = jnp.zeros((tm, d), BF16)

    def zero_start(t, carry):
        out_copy(t, 0).start()
        return carry

    def zero_wait(t, carry):
        out_copy(t, 0).wait()
        return carry

    lax.fori_loop(nact, n_tiles, zero_start, 0)
    lax.fori_loop(nact, n_tiles, zero_wait, 0)


def _experts(te, src, nact, xb, w_gate, w_up, w_down, n_tiles):
    dw = xb.shape[1]
    d = dw - W_LANES
    ff = w_gate.shape[2]
    tm = MOE_TILE
    hbm = pl.BlockSpec(memory_space=pl.ANY)
    return pl.pallas_call(
        _experts_kernel,
        grid_spec=pltpu.PrefetchScalarGridSpec(
            num_scalar_prefetch=3,
            grid=(1,),
            in_specs=[hbm, hbm, hbm, hbm],
            out_specs=hbm,
            scratch_shapes=[pltpu.VMEM((GATHER_AHEAD + 1, tm, dw), BF16), pltpu.SemaphoreType.DMA((GATHER_AHEAD + 1,)),
                            pltpu.VMEM((2, tm, d), BF16), pltpu.SemaphoreType.DMA((2,)),
                            pltpu.VMEM((d, ff), F32), pltpu.VMEM((d, ff), F32), pltpu.VMEM((ff, d), F32),
                            pltpu.SemaphoreType.DMA((3,)),
                            pltpu.VMEM((d, ff), BF16), pltpu.VMEM((d, ff), BF16), pltpu.VMEM((ff, d), BF16)],
        ),
        out_shape=jax.ShapeDtypeStruct(((n_tiles + 2) * tm, d), BF16),
        compiler_params=_params("arbitrary"),
        name="experts",
    )(te, src, nact, xb, w_gate, w_up, w_down)


def _combine_kernel(dst_ref, lpos_ref, x2_ref, g_ref, ys_ref, out_ref, ybuf, sem):
    b = pl.program_id(0)
    tb = x2_ref.shape[0]
    lg = LOCAL_ROWS // GROUP
    nb = dst_ref.shape[0] // lg
    ring = GATHER_AHEAD + 1
    slot = lax.rem(b, ring)

    def start_gather(blk):
        for i in range(lg):
            _group_copy(ys_ref, dst_ref[blk * lg + i], ybuf, lax.rem(blk, ring), i, sem).start()

    @pl.when(b == 0)
    def _():
        for a in range(min(GATHER_AHEAD, nb)):
            start_gather(a)

    @pl.when(b + GATHER_AHEAD < nb)
    def _():
        start_gather(b + GATHER_AHEAD)

    for i in range(lg):
        _group_copy(ys_ref, 0, ybuf, slot, i, sem).wait()
    lpos = lpos_ref[...]
    y = jnp.zeros(x2_ref.shape, F32)
    for c in range(LOCAL_ROWS // SORT_CHUNK):
        m0, m1 = _slot_masks(lpos, c, tb)
        p = jnp.where(m0, 1.0, jnp.where(m1, 1.0, 0.0)).astype(BF16)
        y = y + _dot_tn(p, ybuf[slot, c * SORT_CHUNK:(c + 1) * SORT_CHUNK, :])
    out_ref[...] = _rms(x2_ref[...] + y, g_ref[...])


def _combine(dst, lpos, x2, g_final, ys):
    n, d = x2.shape
    tb = TOKEN_BLOCK
    return pl.pallas_call(
        _combine_kernel,
        grid_spec=pltpu.PrefetchScalarGridSpec(
            num_scalar_prefetch=1,
            grid=(n // tb,),
            in_specs=[pl.BlockSpec((None, 1, TOP_K * tb), lambda i, dst: (i, 0, 0)),
                      pl.BlockSpec((tb, d), lambda i, dst: (i, 0)), pl.BlockSpec((1, d), lambda i, dst: (0, 0)),
                      pl.BlockSpec(memory_space=pl.ANY)],
            out_specs=pl.BlockSpec((tb, d), lambda i, dst: (i, 0)),
            scratch_shapes=[pltpu.VMEM((GATHER_AHEAD + 1, LOCAL_ROWS, d), BF16),
                            pltpu.SemaphoreType.DMA((GATHER_AHEAD + 1,))],
        ),
        out_shape=jax.ShapeDtypeStruct((n, d), F32),
        compiler_params=_params("arbitrary"),
        name="combine",
    )(dst, lpos, x2, g_final, ys)


def kernel(x, mem, norm_mix_g, w_in, conv_w, conv_b, conv_ln_g, conv_ln_b, conv_w_out, hgrn_lb_logits, hgrn_onorm_g, hgrn_w_out, w_mix_out, norm_xa_g, norm_mem_g, xa_w_q, xa_w_k, xa_w_v, xa_w_o, norm_ffn_g, router_group_w, router_group_b, router_expert_w, router_expert_b, moe_w_gate, moe_w_up, moe_w_down, final_norm_g):
    batch, seq, d = x.shape
    n = batch * seq
    assert w_in.shape[0] == 1, "the final RMSNorm is fused into the single layer's combine step"
    cw = conv_w.shape[2]
    kw = hgrn_w_out.shape[1]
    assert kw == HGRN_HEADS * HGRN_DIM and conv_w.shape[1] == CONV_K
    assert seq % TOKEN_BLOCK == 0 and TOKEN_BLOCK % CHUNK == 0 and TOKEN_BLOCK % CONV_ROWS == 0
    assert moe_w_gate.shape[1] == N_EXPERTS and router_group_w.shape[2] == N_GROUPS and TOP_K == 2
    na = TOP_K * TOKEN_BLOCK
    assert LOCAL_ROWS % SORT_CHUNK == 0 and LOCAL_ROWS >= na + N_EXPERTS * (GROUP - 1) + GROUP

    n_tiles = -(-(n // TOKEN_BLOCK) * (na + N_EXPERTS * (GROUP - 1)) // MOE_TILE) + N_EXPERTS
    n_tiles = -(-n_tiles // 16) * 16
    vec = lambda p: p.reshape(1, -1)
    l = 0

    x2d = x.reshape(n, d)
    x1 = _mixer(x2d, vec(norm_mix_g[l]), w_in[l], conv_w[l], vec(conv_b[l]), vec(conv_ln_g[l]), vec(conv_ln_b[l]),
                conv_w_out[l].astype(BF16), hgrn_lb_logits[l:l + 2], vec(hgrn_onorm_g[l]),
                hgrn_w_out[l].astype(BF16), w_mix_out[l].astype(BF16), cw, kw, batch)
    k_bf, v_bf = _memkv(mem.reshape(-1, d), vec(norm_mem_g[l]), xa_w_k[l].astype(BF16), xa_w_v[l].astype(BF16), batch)
    pad = ROUTE_ROWS - N_GROUPS - N_EXPERTS
    w_route_t = jnp.pad(jnp.concatenate([router_group_w[l], router_expert_w[l]], axis=1).T, ((0, pad), (0, 0)))
    b_route = jnp.pad(jnp.concatenate([router_group_b[l], router_expert_b[l]]), (0, pad)).reshape(ROUTE_ROWS, 1)
    x2, xb, cnt, lpos = _attn(x1, vec(norm_xa_g[l]), xa_w_q[l].astype(BF16), k_bf, v_bf, xa_w_o[l].astype(BF16),
                              vec(norm_ffn_g[l]), w_route_t, b_route, batch)
    te, src, dst, nact = _plan(cnt.reshape(-1, 1), n_tiles)
    ys = _experts(te, src, nact, xb, moe_w_gate[l], moe_w_up[l], moe_w_down[l], n_tiles)
    out = _combine(dst, lpos, x2, vec(final_norm_g), ys)
    return out.reshape(batch, seq, d)
```

```python
import functools

import jax
import jax.numpy as jnp
from jax import lax
from jax.experimental import pallas as pl
from jax.experimental.pallas import tpu as pltpu

F32 = jnp.float32
BF16 = jnp.bfloat16
I32 = jnp.int32

EPS = 1e-6
CONV_K = 31
CONV_HALO = 32
CONV_ROWS = 64
HGRN_HEADS = 4
HGRN_DIM = 128
CHUNK = 64
SUB = 16
XA_HEADS = 4
N_GROUPS = 4
EXPERTS_PER_GROUP = 8
N_EXPERTS = N_GROUPS * EXPERTS_PER_GROUP
TOP_K = 2
ROUTE_ROWS = 40
TOKEN_BLOCK = 512
MOE_TILE = 256
GROUP = 16
LOCAL_ROWS = 1536
SORT_CHUNK = 256
RANK_SEGMENT = 128
GATHER_AHEAD = 3
W_LANES = 128
PLAN_CHUNKS = (512, 256, 128)
VMEM_LIMIT_BYTES = 48 * 1024 * 1024


def _rms(x, g):
    return x * lax.rsqrt(jnp.mean(x * x, axis=-1, keepdims=True) + EPS) * g


def _dot(a, b):
    return jnp.dot(a, b, preferred_element_type=F32)


def _dot_nt(a, b):
    return lax.dot_general(a, b, (((1,), (1,)), ((), ())), preferred_element_type=F32)


def _dot_tn(a, b):
    return lax.dot_general(a, b, (((0,), (0,)), ((), ())), preferred_element_type=F32)


def _split_bf16(x):
    hi = x.astype(BF16)
    lo = (x - hi.astype(F32)).astype(BF16)
    return hi, lo


def _params(*sem):
    return pltpu.CompilerParams(dimension_semantics=sem, vmem_limit_bytes=VMEM_LIMIT_BYTES)


def _conv_stages(un, first, cw_ref, cb_ref, lg_ref, lb_ref, ext_ref, halo_ref, perm_ref, act_ref):
    tb, c = un.shape
    nt = tb // 8
    slabs = c // 128
    hr = CONV_HALO * 8
    lanes = [slice(l * 128, (l + 1) * 128) for l in range(slabs)]

    @pl.when(first)
    def _():
        halo_ref[...] = jnp.zeros_like(halo_ref)

    per = nt // 8
    for j in range(nt):
        start = hr + (j % per) * 64 + j // per
        for l in range(slabs):
            ext_ref[l, pl.ds(start, 8, stride=8), :] = un[8 * j:8 * j + 8, lanes[l]]
    first_row = lax.broadcasted_iota(I32, (hr, 128), 0) % 8 == 0
    for l in range(slabs):
        cur = ext_ref[l, nt * 8:nt * 8 + hr, :]
        ext_ref[l, 0:hr, :] = jnp.where(first_row, pltpu.roll(halo_ref[l], hr - 7, axis=0),
                                        pltpu.roll(cur, 1, axis=0))
        halo_ref[l] = cur

    def row_tile(r):
        accs = []
        for l in range(slabs):
            acc = jnp.broadcast_to(cb_ref[:, lanes[l]], (CONV_ROWS, 128))
            for dt in range(CONV_K):
                off = hr + r * CONV_ROWS - dt * 8
                acc = acc + cw_ref[CONV_K - 1 - dt:CONV_K - dt, lanes[l]] * ext_ref[l, off:off + CONV_ROWS, :]
            accs.append(acc)
        mu = functools.reduce(jnp.add, [jnp.sum(a, axis=-1, keepdims=True) for a in accs]) * (1.0 / c)
        cens = [a - mu for a in accs]
        var = functools.reduce(jnp.add, [jnp.sum(a * a, axis=-1, keepdims=True) for a in cens]) * (1.0 / c)
        inv = lax.rsqrt(var + EPS)
        for l in range(slabs):
            ln = cens[l] * inv * lg_ref[:, lanes[l]] + lb_ref[:, lanes[l]]
            perm_ref[l, r * CONV_ROWS:(r + 1) * CONV_ROWS, :] = jax.nn.silu(ln)

    def finish():
        for j in range(nt):
            start = (j % per) * 64 + j // per
            for l in range(slabs):
                act_ref[8 * j:8 * j + 8, lanes[l]] = perm_ref[l, pl.ds(start, 8, stride=8), :]

    return [functools.partial(row_tile, r) for r in range(tb // CONV_ROWS)], finish


def _mixin_kernel(x_ref, g_ref, w_hbm, cw_ref, cb_ref, lg_ref, lb_ref, wo_ref,
                  yc_ref, q_ref, fr_ref, iv_ref, og_ref, gate_ref,
                  w_ref, w_stage, w_sem, ext_ref, halo_ref, perm_ref, act_ref, *, cw, kw):
    d = yc_ref.shape[1]

    @pl.when(jnp.logical_and(pl.program_id(0) == 0, pl.program_id(1) == 0))
    def _():
        width = w_stage.shape[2]

        def fetch(c):
            return pltpu.make_async_copy(w_hbm.at[:, pl.ds(c * width, width)], w_stage.at[c % 2], w_sem.at[c % 2])

        n_fetch = w_ref.shape[1] // width
        fetch(0).start()
        for c in range(n_fetch):
            if c + 1 < n_fetch:
                fetch(c + 1).start()
            fetch(c).wait()
            w_ref[:, c * width:(c + 1) * width] = w_stage[c % 2].astype(BF16)

    hb = _rms(x_ref[...], g_ref[...]).astype(BF16)

    def proj(lo, width):
        return _dot(hb, w_ref[:, lo:lo + width])

    conv_tiles, conv_finish = _conv_stages(proj(0, cw) * jax.nn.sigmoid(proj(cw, cw)), pl.program_id(1) == 0,
                                           cw_ref, cb_ref, lg_ref, lb_ref, ext_ref, halo_ref, perm_ref, act_ref)
    def chunk(ref, col, lo, act):
        def run():
            ref[:, col * kw:(col + 1) * kw] = act(proj(lo, kw)).astype(ref.dtype)
        return run

    base = 2 * cw
    gbase = base + 4 * kw
    chunks = [chunk(q_ref, 0, base, jax.nn.silu), chunk(fr_ref, 0, base + kw, lambda v: v),
              chunk(iv_ref, 0, base + 2 * kw, lambda v: v), chunk(og_ref, 0, base + 3 * kw, jax.nn.silu)]
    for c in range(d // kw):
        chunks.append(chunk(gate_ref, c, gbase + d + c * kw, jax.nn.sigmoid))
        chunks.append(chunk(yc_ref, c, gbase + c * kw, jax.nn.sigmoid))
    for i in range(max(len(chunks), len(conv_tiles))):
        if i < len(chunks):
            chunks[i]()
        if i < len(conv_tiles):
            conv_tiles[i]()
    conv_finish()
    y_conv = _dot(act_ref[...].astype(BF16), wo_ref[...])
    yc_ref[...] = (yc_ref[...].astype(F32) * y_conv).astype(BF16)


def _hgrn_block(q_ref, fr_ref, iv_ref, lb, st_ref, tri):
    tb = q_ref.shape[0]
    chunks = [slice(c * CHUNK, (c + 1) * CHUNK) for c in range(tb // CHUNK)]
    heads = [slice(h * HGRN_DIM, (h + 1) * HGRN_DIM) for h in range(HGRN_HEADS)]

    qs, vs, kks, cums = [], [], [], []
    for rows in chunks:
        f = lb + (1.0 - lb) * jax.nn.sigmoid(fr_ref[rows, :])
        lf_hi, lf_lo = _split_bf16(jnp.log(f))
        cums.append(_dot(tri, jnp.concatenate([lf_hi, lf_lo], axis=0)))
        kks.append(1.0 - f)
        qs.append(q_ref[rows, :].astype(F32))
        vs.append(iv_ref[rows, :])

    qes, kds, decays, blocks = [], [], [], []
    for q, kk, cum in zip(qs, kks, cums):
        last = cum[CHUNK - 1:CHUNK, :]
        qes.append((q * jnp.exp(cum)).astype(BF16))
        kds.append((kk * jnp.exp(last - cum)).astype(BF16))
        decays.append(jnp.exp(last))
        sub = []
        for i in range(CHUNK // SUB):
            rs, ne = i * SUB, (i + 1) * SUB
            ref = cum[rs + SUB // 2 - 1:rs + SUB // 2, :]
            qt = (q[rs:ne] * jnp.exp(cum[rs:ne] - ref)).astype(BF16)
            kt = (kk[0:ne] * jnp.exp(ref - cum[0:ne])).astype(BF16)
            sub.append((qt, kt))
        blocks.append(sub)

    updates = [[_dot_tn(v[:, hs], kd[:, hs]) for hs in heads] for v, kd in zip(vs, kds)]
    scores = [[[_dot_nt(qt[:, hs], kt[:, hs]) for qt, kt in sub] for hs in heads] for sub in blocks]

    states = [st_ref[h] for h in range(HGRN_HEADS)]
    inter = []
    for qe, decay, upd in zip(qes, decays, updates):
        inter.append([_dot_nt(qe[:, hs], st.astype(BF16)) for hs, st in zip(heads, states)])
        states = [st * decay[:, hs] + u for st, hs, u in zip(states, heads, upd)]
    for h in range(HGRN_HEADS):
        st_ref[h] = states[h]

    outs = []
    for v, sc, o_inter in zip(vs, scores, inter):
        per_head = []
        for h, hs in enumerate(heads):
            parts = []
            for i, a in enumerate(sc[h]):
                rs, ne = i * SUB, (i + 1) * SUB
                trow = lax.broadcasted_iota(I32, (SUB, ne), 0) + rs
                scol = lax.broadcasted_iota(I32, (SUB, ne), 1)
                a = jnp.where(scol <= trow, a, 0.0).astype(BF16)
                parts.append(_dot(a, v[0:ne, hs]))
            per_head.append(o_inter[h] + jnp.concatenate(parts, axis=0))
        outs.append(jnp.concatenate(per_head, axis=1))
    return outs


def _hgrn_kernel(q_ref, fr_ref, iv_ref, og_ref, gate_ref, yc_ref, x_ref, lbl_ref, on_ref, wo_ref, wm_ref,
                 x1_ref, st_ref, ob_ref):
    tb = q_ref.shape[0]

    @pl.when(pl.program_id(1) == 0)
    def _():
        st_ref[...] = jnp.zeros_like(st_ref)

    l0, l1 = lbl_ref[0:1, :], lbl_ref[1:2, :]
    m = jnp.maximum(l0, l1)
    e0, e1 = jnp.exp(l0 - m), jnp.exp(l1 - m)
    lb_all = e0 / (e0 + e1)
    trow = lax.broadcasted_iota(I32, (CHUNK, 2 * CHUNK), 0)
    tcol = lax.broadcasted_iota(I32, (CHUNK, 2 * CHUNK), 1) % CHUNK
    tri = jnp.where(tcol <= trow, 1.0, 0.0).astype(BF16)

    for c, o in enumerate(_hgrn_block(q_ref, fr_ref, iv_ref, lb_all, st_ref, tri)):
        rows = slice(c * CHUNK, (c + 1) * CHUNK)
        og = og_ref[rows, :].astype(F32)
        for h in range(HGRN_HEADS):
            hs = slice(h * HGRN_DIM, (h + 1) * HGRN_DIM)
            ob_ref[rows, hs] = (_rms(o[:, hs], on_ref[...]) * og[:, hs]).astype(BF16)
    y_rec = _dot(ob_ref[...], wo_ref[...])
    merged = yc_ref[...].astype(F32) + gate_ref[...].astype(F32) * y_rec
    x1_ref[...] = x_ref[...] + _dot(merged.astype(BF16), wm_ref[...])


def _mixer_kernel(x_ref, g_ref, w_hbm, cw_ref, cb_ref, lg_ref, lb_ref, wco_ref, lbl_ref, on_ref, wro_ref, wm_ref,
                  x1_ref, yc_s, q_s, fr_s, iv_s, og_s, gate_s, w_ref, w_stage, w_sem, ext_ref, halo_ref, perm_ref,
                  act_ref, st_ref, ob_ref, *, cw, kw):
    _mixin_kernel(x_ref, g_ref, w_hbm, cw_ref, cb_ref, lg_ref, lb_ref, wco_ref, yc_s, q_s, fr_s, iv_s, og_s, gate_s,
                  w_ref, w_stage, w_sem, ext_ref, halo_ref, perm_ref, act_ref, cw=cw, kw=kw)
    _hgrn_kernel(q_s, fr_s, iv_s, og_s, gate_s, yc_s, x_ref, lbl_ref, on_ref, wro_ref, wm_ref, x1_ref, st_ref, ob_ref)


def _mixer(x2d, g, w_in, conv_w, conv_b, ln_g, ln_b, w_conv_out_bf, lb_logits, onorm_g, w_o_bf, w_mix_bf, cw, kw,
           batch):
    n, d = x2d.shape
    tb = TOKEN_BLOCK
    nsb = n // batch // tb
    row = lambda b, s: (b * nsb + s, 0)
    fixed = lambda b, s: (0, 0)
    return pl.pallas_call(
        functools.partial(_mixer_kernel, cw=cw, kw=kw),
        grid=(batch, nsb),
        in_specs=[pl.BlockSpec((tb, d), row), pl.BlockSpec((1, d), fixed), pl.BlockSpec(memory_space=pl.ANY),
                  pl.BlockSpec(conv_w.shape, fixed), pl.BlockSpec((1, cw), fixed), pl.BlockSpec((1, cw), fixed),
                  pl.BlockSpec((1, cw), fixed), pl.BlockSpec((cw, d), fixed),
                  pl.BlockSpec(lb_logits.shape, fixed), pl.BlockSpec((1, HGRN_DIM), fixed),
                  pl.BlockSpec(w_o_bf.shape, fixed), pl.BlockSpec(w_mix_bf.shape, fixed)],
        out_specs=pl.BlockSpec((tb, d), row),
        out_shape=jax.ShapeDtypeStruct((n, d), F32),
        scratch_shapes=[pltpu.VMEM((tb, d), BF16), pltpu.VMEM((tb, kw), BF16), pltpu.VMEM((tb, kw), F32),
                        pltpu.VMEM((tb, kw), BF16), pltpu.VMEM((tb, kw), BF16), pltpu.VMEM((tb, d), BF16),
                        pltpu.VMEM(w_in.shape, BF16), pltpu.VMEM((2, d, kw), F32), pltpu.SemaphoreType.DMA((2,)),
                        pltpu.VMEM((cw // 128, tb + CONV_HALO * 8, 128), F32),
                        pltpu.VMEM((cw // 128, CONV_HALO * 8, 128), F32),
                        pltpu.VMEM((cw // 128, tb, 128), F32), pltpu.VMEM((tb, cw), F32),
                        pltpu.VMEM((HGRN_HEADS, HGRN_DIM, HGRN_DIM), F32), pltpu.VMEM((tb, kw), BF16)],
        compiler_params=_params("arbitrary", "arbitrary"),
        name="mixer",
    )(x2d, g, w_in, conv_w, conv_b, ln_g, ln_b, w_conv_out_bf, lb_logits, onorm_g, w_o_bf, w_mix_bf)


def _memkv_kernel(mem_ref, g_ref, wk_ref, wv_ref, k_ref, v_ref):
    mb = _rms(mem_ref[...], g_ref[...]).astype(BF16)
    k_ref[...] = _dot(mb, wk_ref[...]).astype(BF16)
    v_ref[...] = _dot(mb, wv_ref[...]).astype(BF16)


def _memkv(mem2d, g, wk_bf, wv_bf, batch):
    n, d = mem2d.shape
    m = n // batch
    row = lambda b: (b, 0)
    fixed = lambda b: (0, 0)
    return pl.pallas_call(
        _memkv_kernel,
        grid=(batch,),
        in_specs=[pl.BlockSpec((m, d), row), pl.BlockSpec((1, d), fixed), pl.BlockSpec((d, d), fixed),
                  pl.BlockSpec((d, d), fixed)],
        out_specs=[pl.BlockSpec((m, d), row)] * 2,
        out_shape=[jax.ShapeDtypeStruct((n, d), BF16)] * 2,
        compiler_params=_params("parallel"),
        name="memkv",
    )(mem2d, g, wk_bf, wv_bf)


def _route(lt):
    def row(r):
        return lt[r:r + 1, :]

    gl = [row(g) for g in range(N_GROUPS)]
    gmax = functools.reduce(jnp.maximum, gl)
    g_p = 1.0 / functools.reduce(jnp.add, [jnp.exp(l - gmax) for l in gl])
    gidx = jnp.full(gmax.shape, N_GROUPS - 1, I32)
    for g in range(N_GROUPS - 2, -1, -1):
        gidx = jnp.where(gl[g] == gmax, g, gidx)

    el = []
    for j in range(EXPERTS_PER_GROUP):
        v = row(N_GROUPS + (N_GROUPS - 1) * EXPERTS_PER_GROUP + j)
        for g in range(N_GROUPS - 2, -1, -1):
            v = jnp.where(gidx == g, row(N_GROUPS + g * EXPERTS_PER_GROUP + j), v)
        el.append(v)

    def argmax(vals):
        mx = functools.reduce(jnp.maximum, vals)
        idx = jnp.full(mx.shape, EXPERTS_PER_GROUP - 1, I32)
        for j in range(EXPERTS_PER_GROUP - 2, -1, -1):
            idx = jnp.where(vals[j] == mx, j, idx)
        return mx, idx

    m1, i1 = argmax(el)
    m2, i2 = argmax([jnp.where(i1 == j, -jnp.inf, el[j]) for j in range(EXPERTS_PER_GROUP)])
    r = jnp.exp(m2 - m1)
    w1 = g_p / (1.0 + r)
    w2 = g_p * r / (1.0 + r)
    base = gidx * EXPERTS_PER_GROUP
    return jnp.concatenate([base + i1, base + i2], axis=0), jnp.concatenate([w1, w2], axis=0)


def _excl_cumsum_rows(col):
    r = col.shape[0]
    lower = lax.broadcasted_iota(I32, (r, r), 1) < lax.broadcasted_iota(I32, (r, r), 0)
    lower_bf = jnp.where(lower, 1.0, 0.0).astype(BF16)
    hi, lo = _split_bf16(jnp.broadcast_to(col, (r, 128)))
    return (_dot(lower_bf, hi) + _dot(lower_bf, lo))[:, 0:1]


def _slot_masks(lpos, chunk, tb):
    slot = lax.broadcasted_iota(I32, (SORT_CHUNK, tb), 0) + chunk * SORT_CHUNK
    return slot == lpos[:, 0:tb], slot == lpos[:, tb:2 * tb]


def _earlier_same_expert(ones):
    seg = RANK_SEGMENT
    n_exp, n = ones.shape
    upper = lax.broadcasted_iota(I32, (seg, seg), 0) < lax.broadcasted_iota(I32, (seg, seg), 1)
    pieces = [ones[:, s * seg:(s + 1) * seg] for s in range(n // seg)]
    within = _dot(jnp.concatenate([p.astype(BF16) for p in pieces], axis=0), jnp.where(upper, 1.0, 0.0).astype(BF16))
    seen = jnp.zeros((n_exp, 1), F32)
    out = []
    for s, p in enumerate(pieces):
        out.append(within[s * n_exp:(s + 1) * n_exp, :] + seen)
        seen = seen + jnp.sum(p, axis=1, keepdims=True)
    return jnp.concatenate(out, axis=1), seen


def _local_sort(eid, wts, h2, xb_ref, cnt_ref, lpos_ref):
    tb, d = h2.shape
    na = TOP_K * tb
    e_all = jnp.concatenate([eid[k:k + 1] for k in range(TOP_K)], axis=1)
    onehot = lax.broadcasted_iota(I32, (N_EXPERTS, na), 0) == e_all
    earlier, cnt = _earlier_same_expert(jnp.where(onehot, 1.0, 0.0))
    cnt_pad = jnp.floor((cnt + (GROUP - 1)) * (1.0 / GROUP)) * GROUP
    start = _excl_cumsum_rows(cnt_pad)
    lpos = jnp.sum(jnp.where(onehot, start + earlier, 0.0), axis=0, keepdims=True).astype(I32)
    hb = h2.astype(BF16)
    n_chunks = LOCAL_ROWS // SORT_CHUNK

    def sort_chunk(c):
        m0, m1 = _slot_masks(lpos, c, tb)
        p = jnp.where(m0, 1.0, jnp.where(m1, 1.0, 0.0)).astype(BF16)
        pw = jnp.where(m0, wts[0:1], jnp.where(m1, wts[1:2], 0.0))
        rows = slice(c * SORT_CHUNK, (c + 1) * SORT_CHUNK)
        xb_ref[rows, 0:d] = _dot(p, hb).astype(BF16)
        w_row = jnp.broadcast_to(jnp.sum(pw, axis=1, keepdims=True), (SORT_CHUNK, W_LANES))
        w_hi = w_row.astype(BF16).astype(F32)
        low_half = lax.broadcasted_iota(I32, (SORT_CHUNK, W_LANES), 1) < W_LANES // 2
        xb_ref[rows, d:d + W_LANES] = jnp.where(low_half, w_hi, w_row - w_hi).astype(BF16)

    for c in range(n_chunks - 1):
        sort_chunk(c)
    last_used = jnp.sum(cnt_pad) > (n_chunks - 1) * SORT_CHUNK

    @pl.when(last_used)
    def _():
        sort_chunk(n_chunks - 1)

    @pl.when(jnp.logical_not(last_used))
    def _():
        xb_ref[(n_chunks - 1) * SORT_CHUNK:, :] = jnp.zeros((SORT_CHUNK, d + W_LANES), BF16)

    cnt_ref[...] = cnt_pad.astype(I32)
    lpos_ref[...] = lpos


def _attn_kernel(x1_ref, gxa_ref, wq_ref, k_ref, v_ref, wo_ref, gffn_ref, wr_ref, br_ref,
                 x2_ref, xb_ref, cnt_ref, lpos_ref):
    x1 = x1_ref[...]
    d = x1.shape[1]
    hd = d // XA_HEADS
    q = _dot(_rms(x1, gxa_ref[...]).astype(BF16), wq_ref[...]).astype(BF16)
    heads = []
    for h in range(XA_HEADS):
        hs = slice(h * hd, (h + 1) * hd)
        sc = _dot_nt(q[:, hs], k_ref[:, hs]) * (hd ** -0.5)
        p = jnp.exp(sc - jnp.max(sc, axis=-1, keepdims=True))
        p = p / jnp.sum(p, axis=-1, keepdims=True)
        heads.append(_dot(p.astype(BF16), v_ref[:, hs]).astype(BF16))
    x2 = x1 + _dot(jnp.concatenate(heads, axis=1), wo_ref[...])
    x2_ref[...] = x2
    h2 = _rms(x2, gffn_ref[...])
    h_hi, h_lo = _split_bf16(h2)
    w_hi, w_lo = _split_bf16(wr_ref[...])
    lt = _dot_nt(w_hi, h_hi) + (_dot_nt(w_hi, h_lo) + _dot_nt(w_lo, h_hi)) + br_ref[...]
    eid, wts = _route(lt)
    _local_sort(eid, wts, h2, xb_ref, cnt_ref, lpos_ref)


def _attn(x1, gxa, wq_bf, k_bf, v_bf, wo_bf, gffn, w_route_t, b_route, batch):
    n, d = x1.shape
    m = k_bf.shape[0] // batch
    tb = TOKEN_BLOCK
    nsb = n // batch // tb
    nb = n // tb
    row = lambda b, s: (b * nsb + s, 0)
    blk3 = lambda b, s: (b * nsb + s, 0, 0)
    fixed = lambda b, s: (0, 0)
    mem = lambda b, s: (b, 0)
    return pl.pallas_call(
        _attn_kernel,
        grid=(batch, nsb),
        in_specs=[pl.BlockSpec((tb, d), row), pl.BlockSpec((1, d), fixed), pl.BlockSpec((d, d), fixed),
                  pl.BlockSpec((m, d), mem), pl.BlockSpec((m, d), mem), pl.BlockSpec((d, d), fixed),
                  pl.BlockSpec((1, d), fixed), pl.BlockSpec((ROUTE_ROWS, d), fixed),
                  pl.BlockSpec((ROUTE_ROWS, 1), fixed)],
        out_specs=[pl.BlockSpec((tb, d), row), pl.BlockSpec((LOCAL_ROWS, d + W_LANES), row),
                   pl.BlockSpec((None, N_EXPERTS, 1), blk3), pl.BlockSpec((None, 1, TOP_K * tb), blk3)],
        out_shape=[jax.ShapeDtypeStruct((n, d), F32), jax.ShapeDtypeStruct((nb * LOCAL_ROWS, d + W_LANES), BF16),
                   jax.ShapeDtypeStruct((nb, N_EXPERTS, 1), I32), jax.ShapeDtypeStruct((nb, 1, TOP_K * tb), I32)],
        compiler_params=_params("parallel", "parallel"),
        name="attn",
    )(x1, gxa, wq_bf, k_bf, v_bf, wo_bf, gffn, w_route_t, b_route)


def _plan_kernel(cnt_ref, te_ref, src_ref, dst_ref, nact_ref):
    runs = cnt_ref.shape[0]
    lg = LOCAL_ROWS // GROUP
    tg = MOE_TILE // GROUP
    sh = N_EXPERTS.bit_length() - 1
    emask = N_EXPERTS - 1
    zero_group = lg - 1

    length = cnt_ref[...].astype(F32) * (1.0 / GROUP)
    len_bf = jnp.broadcast_to(length, (runs, 128)).astype(BF16)
    ri = lax.broadcasted_iota(I32, (runs, runs), 0)
    ci = lax.broadcasted_iota(I32, (runs, runs), 1)
    r_e, c_e = ri & emask, ci & emask
    r_b, c_b = lax.shift_right_logical(ri, sh), lax.shift_right_logical(ci, sh)
    same_expert_earlier = jnp.where(r_e == c_e, jnp.where(c_b < r_b, 1.0, 0.0), 0.0).astype(BF16)
    same_block_earlier = jnp.where(r_b == c_b, jnp.where(c_e < r_e, 1.0, 0.0), 0.0).astype(BF16)
    before = _dot(same_expert_earlier, len_bf)[:, 0:1]
    local = _dot(same_block_earlier, len_bf)[:, 0:1]

    of_expert = (lax.broadcasted_iota(I32, (N_EXPERTS, runs), 1) & emask) == lax.broadcasted_iota(
        I32, (N_EXPERTS, runs), 0)
    total = _dot(jnp.where(of_expert, 1.0, 0.0).astype(BF16), len_bf)[:, 0:1]
    tiles = jnp.floor((total + (tg - 1)) * (1.0 / tg))
    tile0 = _excl_cumsum_rows(tiles)
    to_run = (lax.broadcasted_iota(I32, (runs, N_EXPERTS), 0) & emask) == lax.broadcasted_iota(
        I32, (runs, N_EXPERTS), 1)
    to_run_bf = jnp.where(to_run, 1.0, 0.0).astype(BF16)
    base_hi, base_lo = _split_bf16(jnp.broadcast_to(tile0 * tg, (N_EXPERTS, 128)))
    g_start = (_dot(to_run_bf, base_hi) + _dot(to_run_bf, base_lo))[:, 0:1] + before
    block = lax.shift_right_logical(lax.broadcasted_iota(I32, (runs, 1), 0), sh).astype(F32)
    l_start = block * lg + local

    def cover(out_ref, start, offset, default):
        stop = start + length
        shift = offset - default
        n_out = out_ref.shape[1]
        chunk = next(c for c in PLAN_CHUNKS if n_out % c == 0)
        for c in range(n_out // chunk):
            j = (lax.broadcasted_iota(I32, (runs, chunk), 1) + c * chunk).astype(F32)
            hit = jnp.where(start <= j, jnp.where(j < stop, j + shift, 0.0), 0.0)
            out = jnp.sum(hit, axis=0, keepdims=True) + default
            out_ref[:, c * chunk:(c + 1) * chunk] = out.astype(I32)

    cover(src_ref, g_start, l_start - g_start, float(zero_group))
    cover(dst_ref, l_start, g_start - l_start, 0.0)
    t = lax.broadcasted_iota(I32, (N_EXPERTS, te_ref.shape[1]), 1).astype(F32)
    te_ref[...] = jnp.sum(jnp.where(t >= tile0 + tiles, 1.0, 0.0), axis=0, keepdims=True).astype(I32)
    nact_ref[...] = jnp.sum(jnp.broadcast_to(tiles, (N_EXPERTS, 128)), axis=0, keepdims=True).astype(I32)


def _plan(cnt, n_tiles):
    runs = cnt.shape[0]
    n_src = n_tiles * (MOE_TILE // GROUP)
    n_dst = runs // N_EXPERTS * (LOCAL_ROWS // GROUP)
    n_te = -(-n_tiles // 128) * 128
    te, src, dst, nact = pl.pallas_call(
        _plan_kernel,
        out_shape=[jax.ShapeDtypeStruct((1, n_te), I32), jax.ShapeDtypeStruct((1, n_src), I32),
                   jax.ShapeDtypeStruct((1, n_dst), I32), jax.ShapeDtypeStruct((1, 128), I32)],
        compiler_params=_params(),
        name="plan",
    )(cnt)
    return te.reshape(-1), src.reshape(-1), dst.reshape(-1), nact[0, :1]


def _group_copy(src_hbm, src_group, dst_buf, slot, index, sem):
    start = src_group * GROUP
    rows = pl.ds(start if isinstance(start, int) else pl.multiple_of(start, GROUP), GROUP)
    return pltpu.make_async_copy(src_hbm.at[rows], dst_buf.at[slot, pl.ds(index * GROUP, GROUP)], sem.at[slot])


def _experts_kernel(te_ref, src_ref, nact_ref, xb_ref, wg_hbm, wu_hbm, wd_hbm, ys_ref,
                    xbuf, sem, obuf, osem, zbuf, zsem, wg_st, wu_st, wd_st, wsem, wg_bf, wu_bf, wd_bf):
    tg = MOE_TILE // GROUP
    tm, d = obuf.shape[1], obuf.shape[2]
    nact = nact_ref[0]
    n_tiles = ys_ref.shape[0] // tm - 2

    def start_gather(tile, slot):
        for i in range(tg):
            _group_copy(xb_ref, src_ref[tile * tg + i], xbuf, slot, i, sem).start()

    def wait_gather(slot):
        for i in range(tg):
            _group_copy(xb_ref, 0, xbuf, slot, i, sem).wait()

    def weight_copies(expert):
        pairs = ((wg_hbm, wg_st), (wu_hbm, wu_st), (wd_hbm, wd_st))
        return [pltpu.make_async_copy(w.at[expert], st, wsem.at[i]) for i, (w, st) in enumerate(pairs)]

    def out_copy(tile, slot):
        rows = pl.ds(pl.multiple_of(tile * tm, tm), tm)
        return pltpu.make_async_copy(obuf.at[slot], ys_ref.at[rows], osem.at[slot])

    def zero_copy(tile):
        rows = pl.ds(pl.multiple_of(tile * tm, tm), tm)
        return pltpu.make_async_copy(zbuf, ys_ref.at[rows], zsem)

    def zero_start(t, carry):
        zero_copy(t).start()
        return carry

    def zero_wait(t, carry):
        zero_copy(t).wait()
        return carry

    zbuf[...] = jnp.zeros_like(zbuf)
    obuf[...] = jnp.zeros_like(obuf)
    for s in range(2):
        out_copy(n_tiles + s, s).start()
    lax.fori_loop(nact, n_tiles, zero_start, 0)

    ring = GATHER_AHEAD + 1

    @pl.when(nact > 0)
    def _():
        for a in range(GATHER_AHEAD):
            start_gather(jnp.minimum(a, nact - 1), a)
        for c in weight_copies(te_ref[0]):
            c.start()

    def tile_step(t, carry):
        e = te_ref[t]
        slot = lax.rem(t, 2)
        xslot = lax.rem(t, ring)

        @pl.when(jnp.logical_or(t == 0, e != te_ref[jnp.maximum(t - 1, 0)]))
        def _():
            for c in weight_copies(e):
                c.wait()
            wg_bf[...] = wg_st[...].astype(BF16)
            wu_bf[...] = wu_st[...].astype(BF16)
            wd_bf[...] = wd_st[...].astype(BF16)
            nxt = lax.while_loop(lambda j: jnp.logical_and(j < nact, te_ref[jnp.minimum(j, n_tiles - 1)] == e),
                                 lambda j: j + 1, t + 1)

            @pl.when(nxt < nact)
            def _():
                for c in weight_copies(te_ref[jnp.minimum(nxt, n_tiles - 1)]):
                    c.start(priority=1)

        wait_gather(xslot)
        out_copy(t, slot).wait()
        x = xbuf[xslot, :, 0:d]
        hid = jax.nn.silu(_dot(x, wg_bf[...])) * _dot(x, wu_bf[...])
        start_gather(jnp.minimum(t + GATHER_AHEAD, nact - 1), lax.rem(t + GATHER_AHEAD, ring))
        weight = (xbuf[xslot, :, d:d + 1].astype(F32)
                  + xbuf[xslot, :, d + W_LANES // 2:d + W_LANES // 2 + 1].astype(F32))
        obuf[slot] = (_dot(hid.astype(BF16), wd_bf[...]) * weight).astype(BF16)
        out_copy(t, slot).start()
        return carry

    lax.fori_loop(0, nact, tile_step, 0)

    @pl.when(nact > 0)
    def _():
        for a in range(GATHER_AHEAD):
            wait_gather(lax.rem(nact + a, ring))
    for s in range(2):
        out_copy(0, s).wait()
    lax.fori_loop(nact, n_tiles, zero_wait, 0)


def _experts(te, src, nact, xb, w_gate, w_up, w_down, n_tiles):
    dw = xb.shape[1]
    d = dw - W_LANES
    ff = w_gate.shape[2]
    tm = MOE_TILE
    hbm = pl.BlockSpec(memory_space=pl.ANY)
    return pl.pallas_call(
        _experts_kernel,
        grid_spec=pltpu.PrefetchScalarGridSpec(
            num_scalar_prefetch=3,
            grid=(1,),
            in_specs=[hbm, hbm, hbm, hbm],
            out_specs=hbm,
            scratch_shapes=[pltpu.VMEM((GATHER_AHEAD + 1, tm, dw), BF16), pltpu.SemaphoreType.DMA((GATHER_AHEAD + 1,)),
                            pltpu.VMEM((2, tm, d), BF16), pltpu.SemaphoreType.DMA((2,)),
                            pltpu.VMEM((tm, d), BF16), pltpu.SemaphoreType.DMA(()),
                            pltpu.VMEM((d, ff), F32), pltpu.VMEM((d, ff), F32), pltpu.VMEM((ff, d), F32),
                            pltpu.SemaphoreType.DMA((3,)),
                            pltpu.VMEM((d, ff), BF16), pltpu.VMEM((d, ff), BF16), pltpu.VMEM((ff, d), BF16)],
        ),
        out_shape=jax.ShapeDtypeStruct(((n_tiles + 2) * tm, d), BF16),
        compiler_params=_params("arbitrary"),
        name="experts",
    )(te, src, nact, xb, w_gate, w_up, w_down)


def _combine_kernel(dst_ref, lpos_ref, x2_ref, g_ref, ys_ref, out_ref, ybuf, sem):
    b = pl.program_id(0)
    tb = x2_ref.shape[0]
    lg = LOCAL_ROWS // GROUP
    nb = dst_ref.shape[0] // lg
    ring = GATHER_AHEAD + 1
    slot = lax.rem(b, ring)

    def start_gather(blk):
        for i in range(lg):
            _group_copy(ys_ref, dst_ref[blk * lg + i], ybuf, lax.rem(blk, ring), i, sem).start()

    @pl.when(b == 0)
    def _():
        for a in range(min(GATHER_AHEAD, nb)):
            start_gather(a)

    @pl.when(b + GATHER_AHEAD < nb)
    def _():
        start_gather(b + GATHER_AHEAD)

    for i in range(lg):
        _group_copy(ys_ref, 0, ybuf, slot, i, sem).wait()
    lpos = lpos_ref[...]
    y = jnp.zeros(x2_ref.shape, F32)
    for c in range(LOCAL_ROWS // SORT_CHUNK):
        m0, m1 = _slot_masks(lpos, c, tb)
        p = jnp.where(m0, 1.0, jnp.where(m1, 1.0, 0.0)).astype(BF16)
        y = y + _dot_tn(p, ybuf[slot, c * SORT_CHUNK:(c + 1) * SORT_CHUNK, :])
    out_ref[...] = _rms(x2_ref[...] + y, g_ref[...])


def _combine(dst, lpos, x2, g_final, ys):
    n, d = x2.shape
    tb = TOKEN_BLOCK
    return pl.pallas_call(
        _combine_kernel,
        grid_spec=pltpu.PrefetchScalarGridSpec(
            num_scalar_prefetch=1,
            grid=(n // tb,),
            in_specs=[pl.BlockSpec((None, 1, TOP_K * tb), lambda i, dst: (i, 0, 0)),
                      pl.BlockSpec((tb, d), lambda i, dst: (i, 0)), pl.BlockSpec((1, d), lambda i, dst: (0, 0)),
                      pl.BlockSpec(memory_space=pl.ANY)],
            out_specs=pl.BlockSpec((tb, d), lambda i, dst: (i, 0)),
            scratch_shapes=[pltpu.VMEM((GATHER_AHEAD + 1, LOCAL_ROWS, d), BF16),
                            pltpu.SemaphoreType.DMA((GATHER_AHEAD + 1,))],
        ),
        out_shape=jax.ShapeDtypeStruct((n, d), F32),
        compiler_params=_params("arbitrary"),
        name="combine",
    )(dst, lpos, x2, g_final, ys)


def kernel(x, mem, norm_mix_g, w_in, conv_w, conv_b, conv_ln_g, conv_ln_b, conv_w_out, hgrn_lb_logits, hgrn_onorm_g, hgrn_w_out, w_mix_out, norm_xa_g, norm_mem_g, xa_w_q, xa_w_k, xa_w_v, xa_w_o, norm_ffn_g, router_group_w, router_group_b, router_expert_w, router_expert_b, moe_w_gate, moe_w_up, moe_w_down, final_norm_g):
    batch, seq, d = x.shape
    n = batch * seq
    assert w_in.shape[0] == 1, "the final RMSNorm is fused into the single layer's combine step"
    cw = conv_w.shape[2]
    kw = hgrn_w_out.shape[1]
    assert kw == HGRN_HEADS * HGRN_DIM and conv_w.shape[1] == CONV_K
    assert seq % TOKEN_BLOCK == 0 and TOKEN_BLOCK % CHUNK == 0 and TOKEN_BLOCK % CONV_ROWS == 0
    assert moe_w_gate.shape[1] == N_EXPERTS and router_group_w.shape[2] == N_GROUPS and TOP_K == 2
    na = TOP_K * TOKEN_BLOCK
    assert LOCAL_ROWS % SORT_CHUNK == 0 and LOCAL_ROWS >= na + N_EXPERTS * (GROUP - 1) + GROUP

    n_tiles = -(-(n // TOKEN_BLOCK) * (na + N_EXPERTS * (GROUP - 1)) // MOE_TILE) + N_EXPERTS
    n_tiles = -(-n_tiles // 16) * 16
    vec = lambda p: p.reshape(1, -1)
    l = 0

    x2d = x.reshape(n, d)
    x1 = _mixer(x2d, vec(norm_mix_g[l]), w_in[l], conv_w[l], vec(conv_b[l]), vec(conv_ln_g[l]), vec(conv_ln_b[l]),
                conv_w_out[l].astype(BF16), hgrn_lb_logits[l:l + 2], vec(hgrn_onorm_g[l]),
                hgrn_w_out[l].astype(BF16), w_mix_out[l].astype(BF16), cw, kw, batch)
    k_bf, v_bf = _memkv(mem.reshape(-1, d), vec(norm_mem_g[l]), xa_w_k[l].astype(BF16), xa_w_v[l].astype(BF16), batch)
    pad = ROUTE_ROWS - N_GROUPS - N_EXPERTS
    w_route_t = jnp.pad(jnp.concatenate([router_group_w[l], router_expert_w[l]], axis=1).T, ((0, pad), (0, 0)))
    b_route = jnp.pad(jnp.concatenate([router_group_b[l], router_expert_b[l]]), (0, pad)).reshape(ROUTE_ROWS, 1)
    x2, xb, cnt, lpos = _attn(x1, vec(norm_xa_g[l]), xa_w_q[l].astype(BF16), k_bf, v_bf, xa_w_o[l].astype(BF16),
                              vec(norm_ffn_g[l]), w_route_t, b_route, batch)
    te, src, dst, nact = _plan(cnt.reshape(-1, 1), n_tiles)
    ys = _experts(te, src, nact, xb, moe_w_gate[l], moe_w_up[l], moe_w_down[l], n_tiles)
    out = _combine(dst, lpos, x2, vec(final_norm_g), ys)
    return out.reshape(batch, seq, d)
```

```python
import functools

import jax
import jax.numpy as jnp
from jax import lax
from jax.experimental import pallas as pl
from jax.experimental.pallas import tpu as pltpu

F32 = jnp.float32
BF16 = jnp.bfloat16
I32 = jnp.int32

EPS = 1e-6
CONV_K = 31
CONV_HALO = 32
CONV_ROWS = 64
HGRN_HEADS = 4
HGRN_DIM = 128
CHUNK = 64
SUB = 16
XA_HEADS = 4
N_GROUPS = 4
EXPERTS_PER_GROUP = 8
N_EXPERTS = N_GROUPS * EXPERTS_PER_GROUP
TOP_K = 2
ROUTE_ROWS = 40
TOKEN_BLOCK = 512
MOE_TILE = 256
GROUP = 16
LOCAL_ROWS = 1536
SORT_CHUNK = 256
RANK_SEGMENT = 128
GATHER_AHEAD = 3
OUT_RING = 4
W_LANES = 128
PLAN_CHUNKS = (512, 256, 128)
VMEM_LIMIT_BYTES = 48 * 1024 * 1024


def _rms(x, g):
    return x * lax.rsqrt(jnp.mean(x * x, axis=-1, keepdims=True) + EPS) * g


def _dot(a, b):
    return jnp.dot(a, b, preferred_element_type=F32)


def _dot_nt(a, b):
    return lax.dot_general(a, b, (((1,), (1,)), ((), ())), preferred_element_type=F32)


def _dot_tn(a, b):
    return lax.dot_general(a, b, (((0,), (0,)), ((), ())), preferred_element_type=F32)


def _split_bf16(x):
    hi = x.astype(BF16)
    lo = (x - hi.astype(F32)).astype(BF16)
    return hi, lo


def _params(*sem):
    return pltpu.CompilerParams(dimension_semantics=sem, vmem_limit_bytes=VMEM_LIMIT_BYTES)


def _conv_stages(un, first, cw_ref, cb_ref, lg_ref, lb_ref, ext_ref, halo_ref, perm_ref, act_ref):
    tb, c = un.shape
    nt = tb // 8
    slabs = c // 128
    hr = CONV_HALO * 8
    lanes = [slice(l * 128, (l + 1) * 128) for l in range(slabs)]

    @pl.when(first)
    def _():
        halo_ref[...] = jnp.zeros_like(halo_ref)

    per = nt // 8
    for j in range(nt):
        start = hr + (j % per) * 64 + j // per
        for l in range(slabs):
            ext_ref[l, pl.ds(start, 8, stride=8), :] = un[8 * j:8 * j + 8, lanes[l]]
    first_row = lax.broadcasted_iota(I32, (hr, 128), 0) % 8 == 0
    for l in range(slabs):
        cur = ext_ref[l, nt * 8:nt * 8 + hr, :]
        ext_ref[l, 0:hr, :] = jnp.where(first_row, pltpu.roll(halo_ref[l], hr - 7, axis=0),
                                        pltpu.roll(cur, 1, axis=0))
        halo_ref[l] = cur

    def row_tile(r):
        accs = []
        for l in range(slabs):
            acc = jnp.broadcast_to(cb_ref[:, lanes[l]], (CONV_ROWS, 128))
            for dt in range(CONV_K):
                off = hr + r * CONV_ROWS - dt * 8
                acc = acc + cw_ref[CONV_K - 1 - dt:CONV_K - dt, lanes[l]] * ext_ref[l, off:off + CONV_ROWS, :]
            accs.append(acc)
        mu = functools.reduce(jnp.add, [jnp.sum(a, axis=-1, keepdims=True) for a in accs]) * (1.0 / c)
        cens = [a - mu for a in accs]
        var = functools.reduce(jnp.add, [jnp.sum(a * a, axis=-1, keepdims=True) for a in cens]) * (1.0 / c)
        inv = lax.rsqrt(var + EPS)
        for l in range(slabs):
            ln = cens[l] * inv * lg_ref[:, lanes[l]] + lb_ref[:, lanes[l]]
            perm_ref[l, r * CONV_ROWS:(r + 1) * CONV_ROWS, :] = jax.nn.silu(ln)

    def finish():
        for j in range(nt):
            start = (j % per) * 64 + j // per
            for l in range(slabs):
                act_ref[8 * j:8 * j + 8, lanes[l]] = perm_ref[l, pl.ds(start, 8, stride=8), :]

    return [functools.partial(row_tile, r) for r in range(tb // CONV_ROWS)], finish


def _mixin_kernel(x_ref, g_ref, w_hbm, cw_ref, cb_ref, lg_ref, lb_ref, wo_ref,
                  yc_ref, q_ref, fr_ref, iv_ref, og_ref, gate_ref,
                  w_ref, w_stage, w_sem, ext_ref, halo_ref, perm_ref, act_ref, *, cw, kw):
    d = yc_ref.shape[1]

    @pl.when(jnp.logical_and(pl.program_id(0) == 0, pl.program_id(1) == 0))
    def _():
        width = w_stage.shape[2]

        def fetch(c):
            return pltpu.make_async_copy(w_hbm.at[:, pl.ds(c * width, width)], w_stage.at[c % 2], w_sem.at[c % 2])

        n_fetch = w_ref.shape[1] // width
        fetch(0).start()
        for c in range(n_fetch):
            if c + 1 < n_fetch:
                fetch(c + 1).start()
            fetch(c).wait()
            w_ref[:, c * width:(c + 1) * width] = w_stage[c % 2].astype(BF16)

    hb = _rms(x_ref[...], g_ref[...]).astype(BF16)

    def proj(lo, width):
        return _dot(hb, w_ref[:, lo:lo + width])

    conv_tiles, conv_finish = _conv_stages(proj(0, cw) * jax.nn.sigmoid(proj(cw, cw)), pl.program_id(1) == 0,
                                           cw_ref, cb_ref, lg_ref, lb_ref, ext_ref, halo_ref, perm_ref, act_ref)
    def chunk(ref, col, lo, act):
        def run():
            ref[:, col * kw:(col + 1) * kw] = act(proj(lo, kw)).astype(ref.dtype)
        return run

    base = 2 * cw
    gbase = base + 4 * kw
    chunks = [chunk(q_ref, 0, base, jax.nn.silu), chunk(fr_ref, 0, base + kw, lambda v: v),
              chunk(iv_ref, 0, base + 2 * kw, lambda v: v), chunk(og_ref, 0, base + 3 * kw, jax.nn.silu)]
    for c in range(d // kw):
        chunks.append(chunk(gate_ref, c, gbase + d + c * kw, jax.nn.sigmoid))
        chunks.append(chunk(yc_ref, c, gbase + c * kw, jax.nn.sigmoid))
    for i in range(max(len(chunks), len(conv_tiles))):
        if i < len(chunks):
            chunks[i]()
        if i < len(conv_tiles):
            conv_tiles[i]()
    conv_finish()
    y_conv = _dot(act_ref[...].astype(BF16), wo_ref[...])
    yc_ref[...] = (yc_ref[...].astype(F32) * y_conv).astype(BF16)


def _hgrn_block(q_ref, fr_ref, iv_ref, lb, st_ref, tri):
    tb = q_ref.shape[0]
    chunks = [slice(c * CHUNK, (c + 1) * CHUNK) for c in range(tb // CHUNK)]
    heads = [slice(h * HGRN_DIM, (h + 1) * HGRN_DIM) for h in range(HGRN_HEADS)]

    qs, vs, kks, cums = [], [], [], []
    for rows in chunks:
        f = lb + (1.0 - lb) * jax.nn.sigmoid(fr_ref[rows, :])
        lf_hi, lf_lo = _split_bf16(jnp.log(f))
        cums.append(_dot(tri, jnp.concatenate([lf_hi, lf_lo], axis=0)))
        kks.append(1.0 - f)
        qs.append(q_ref[rows, :].astype(F32))
        vs.append(iv_ref[rows, :])

    qes, kds, decays, blocks = [], [], [], []
    for q, kk, cum in zip(qs, kks, cums):
        last = cum[CHUNK - 1:CHUNK, :]
        qes.append((q * jnp.exp(cum)).astype(BF16))
        kds.append((kk * jnp.exp(last - cum)).astype(BF16))
        decays.append(jnp.exp(last))
        sub = []
        for i in range(CHUNK // SUB):
            rs, ne = i * SUB, (i + 1) * SUB
            ref = cum[rs + SUB // 2 - 1:rs + SUB // 2, :]
            qt = (q[rs:ne] * jnp.exp(cum[rs:ne] - ref)).astype(BF16)
            kt = (kk[0:ne] * jnp.exp(ref - cum[0:ne])).astype(BF16)
            sub.append((qt, kt))
        blocks.append(sub)

    updates = [[_dot_tn(v[:, hs], kd[:, hs]) for hs in heads] for v, kd in zip(vs, kds)]
    scores = [[[_dot_nt(qt[:, hs], kt[:, hs]) for qt, kt in sub] for hs in heads] for sub in blocks]

    states = [st_ref[h] for h in range(HGRN_HEADS)]
    inter = []
    for qe, decay, upd in zip(qes, decays, updates):
        inter.append([_dot_nt(qe[:, hs], st.astype(BF16)) for hs, st in zip(heads, states)])
        states = [st * decay[:, hs] + u for st, hs, u in zip(states, heads, upd)]
    for h in range(HGRN_HEADS):
        st_ref[h] = states[h]

    outs = []
    for v, sc, o_inter in zip(vs, scores, inter):
        per_head = []
        for h, hs in enumerate(heads):
            parts = []
            for i, a in enumerate(sc[h]):
                rs, ne = i * SUB, (i + 1) * SUB
                trow = lax.broadcasted_iota(I32, (SUB, ne), 0) + rs
                scol = lax.broadcasted_iota(I32, (SUB, ne), 1)
                a = jnp.where(scol <= trow, a, 0.0).astype(BF16)
                parts.append(_dot(a, v[0:ne, hs]))
            per_head.append(o_inter[h] + jnp.concatenate(parts, axis=0))
        outs.append(jnp.concatenate(per_head, axis=1))
    return outs


def _hgrn_kernel(q_ref, fr_ref, iv_ref, og_ref, gate_ref, yc_ref, x_ref, lbl_ref, on_ref, wo_ref, wm_ref,
                 x1_ref, st_ref, ob_ref):
    tb = q_ref.shape[0]

    @pl.when(pl.program_id(1) == 0)
    def _():
        st_ref[...] = jnp.zeros_like(st_ref)

    l0, l1 = lbl_ref[0:1, :], lbl_ref[1:2, :]
    m = jnp.maximum(l0, l1)
    e0, e1 = jnp.exp(l0 - m), jnp.exp(l1 - m)
    lb_all = e0 / (e0 + e1)
    trow = lax.broadcasted_iota(I32, (CHUNK, 2 * CHUNK), 0)
    tcol = lax.broadcasted_iota(I32, (CHUNK, 2 * CHUNK), 1) % CHUNK
    tri = jnp.where(tcol <= trow, 1.0, 0.0).astype(BF16)

    for c, o in enumerate(_hgrn_block(q_ref, fr_ref, iv_ref, lb_all, st_ref, tri)):
        rows = slice(c * CHUNK, (c + 1) * CHUNK)
        og = og_ref[rows, :].astype(F32)
        for h in range(HGRN_HEADS):
            hs = slice(h * HGRN_DIM, (h + 1) * HGRN_DIM)
            ob_ref[rows, hs] = (_rms(o[:, hs], on_ref[...]) * og[:, hs]).astype(BF16)
    y_rec = _dot(ob_ref[...], wo_ref[...])
    merged = yc_ref[...].astype(F32) + gate_ref[...].astype(F32) * y_rec
    x1_ref[...] = x_ref[...] + _dot(merged.astype(BF16), wm_ref[...])


def _mixer_kernel(x_ref, g_ref, w_hbm, cw_ref, cb_ref, lg_ref, lb_ref, wco_ref, lbl_ref, on_ref, wro_ref, wm_ref,
                  x1_ref, yc_s, q_s, fr_s, iv_s, og_s, gate_s, w_ref, w_stage, w_sem, ext_ref, halo_ref, perm_ref,
                  act_ref, st_ref, ob_ref, *, cw, kw):
    _mixin_kernel(x_ref, g_ref, w_hbm, cw_ref, cb_ref, lg_ref, lb_ref, wco_ref, yc_s, q_s, fr_s, iv_s, og_s, gate_s,
                  w_ref, w_stage, w_sem, ext_ref, halo_ref, perm_ref, act_ref, cw=cw, kw=kw)
    _hgrn_kernel(q_s, fr_s, iv_s, og_s, gate_s, yc_s, x_ref, lbl_ref, on_ref, wro_ref, wm_ref, x1_ref, st_ref, ob_ref)


def _mixer(x2d, g, w_in, conv_w, conv_b, ln_g, ln_b, w_conv_out_bf, lb_logits, onorm_g, w_o_bf, w_mix_bf, cw, kw,
           batch):
    n, d = x2d.shape
    tb = TOKEN_BLOCK
    nsb = n // batch // tb
    row = lambda b, s: (b * nsb + s, 0)
    fixed = lambda b, s: (0, 0)
    return pl.pallas_call(
        functools.partial(_mixer_kernel, cw=cw, kw=kw),
        grid=(batch, nsb),
        in_specs=[pl.BlockSpec((tb, d), row), pl.BlockSpec((1, d), fixed), pl.BlockSpec(memory_space=pl.ANY),
                  pl.BlockSpec(conv_w.shape, fixed), pl.BlockSpec((1, cw), fixed), pl.BlockSpec((1, cw), fixed),
                  pl.BlockSpec((1, cw), fixed), pl.BlockSpec((cw, d), fixed),
                  pl.BlockSpec(lb_logits.shape, fixed), pl.BlockSpec((1, HGRN_DIM), fixed),
                  pl.BlockSpec(w_o_bf.shape, fixed), pl.BlockSpec(w_mix_bf.shape, fixed)],
        out_specs=pl.BlockSpec((tb, d), row),
        out_shape=jax.ShapeDtypeStruct((n, d), F32),
        scratch_shapes=[pltpu.VMEM((tb, d), BF16), pltpu.VMEM((tb, kw), BF16), pltpu.VMEM((tb, kw), F32),
                        pltpu.VMEM((tb, kw), BF16), pltpu.VMEM((tb, kw), BF16), pltpu.VMEM((tb, d), BF16),
                        pltpu.VMEM(w_in.shape, BF16), pltpu.VMEM((2, d, kw), F32), pltpu.SemaphoreType.DMA((2,)),
                        pltpu.VMEM((cw // 128, tb + CONV_HALO * 8, 128), F32),
                        pltpu.VMEM((cw // 128, CONV_HALO * 8, 128), F32),
                        pltpu.VMEM((cw // 128, tb, 128), F32), pltpu.VMEM((tb, cw), F32),
                        pltpu.VMEM((HGRN_HEADS, HGRN_DIM, HGRN_DIM), F32), pltpu.VMEM((tb, kw), BF16)],
        compiler_params=_params("arbitrary", "arbitrary"),
        name="mixer",
    )(x2d, g, w_in, conv_w, conv_b, ln_g, ln_b, w_conv_out_bf, lb_logits, onorm_g, w_o_bf, w_mix_bf)


def _memkv_kernel(mem_ref, g_ref, wk_ref, wv_ref, k_ref, v_ref):
    mb = _rms(mem_ref[...], g_ref[...]).astype(BF16)
    k_ref[...] = _dot(mb, wk_ref[...]).astype(BF16)
    v_ref[...] = _dot(mb, wv_ref[...]).astype(BF16)


def _memkv(mem2d, g, wk_bf, wv_bf, batch):
    n, d = mem2d.shape
    m = n // batch
    row = lambda b: (b, 0)
    fixed = lambda b: (0, 0)
    return pl.pallas_call(
        _memkv_kernel,
        grid=(batch,),
        in_specs=[pl.BlockSpec((m, d), row), pl.BlockSpec((1, d), fixed), pl.BlockSpec((d, d), fixed),
                  pl.BlockSpec((d, d), fixed)],
        out_specs=[pl.BlockSpec((m, d), row)] * 2,
        out_shape=[jax.ShapeDtypeStruct((n, d), BF16)] * 2,
        compiler_params=_params("parallel"),
        name="memkv",
    )(mem2d, g, wk_bf, wv_bf)


def _route(lt):
    def row(r):
        return lt[r:r + 1, :]

    gl = [row(g) for g in range(N_GROUPS)]
    gmax = functools.reduce(jnp.maximum, gl)
    g_p = 1.0 / functools.reduce(jnp.add, [jnp.exp(l - gmax) for l in gl])
    gidx = jnp.full(gmax.shape, N_GROUPS - 1, I32)
    for g in range(N_GROUPS - 2, -1, -1):
        gidx = jnp.where(gl[g] == gmax, g, gidx)

    el = []
    for j in range(EXPERTS_PER_GROUP):
        v = row(N_GROUPS + (N_GROUPS - 1) * EXPERTS_PER_GROUP + j)
        for g in range(N_GROUPS - 2, -1, -1):
            v = jnp.where(gidx == g, row(N_GROUPS + g * EXPERTS_PER_GROUP + j), v)
        el.append(v)

    def argmax(vals):
        mx = functools.reduce(jnp.maximum, vals)
        idx = jnp.full(mx.shape, EXPERTS_PER_GROUP - 1, I32)
        for j in range(EXPERTS_PER_GROUP - 2, -1, -1):
            idx = jnp.where(vals[j] == mx, j, idx)
        return mx, idx

    m1, i1 = argmax(el)
    m2, i2 = argmax([jnp.where(i1 == j, -jnp.inf, el[j]) for j in range(EXPERTS_PER_GROUP)])
    r = jnp.exp(m2 - m1)
    w1 = g_p / (1.0 + r)
    w2 = g_p * r / (1.0 + r)
    base = gidx * EXPERTS_PER_GROUP
    return jnp.concatenate([base + i1, base + i2], axis=0), jnp.concatenate([w1, w2], axis=0)


def _excl_cumsum_rows(col):
    r = col.shape[0]
    lower = lax.broadcasted_iota(I32, (r, r), 1) < lax.broadcasted_iota(I32, (r, r), 0)
    lower_bf = jnp.where(lower, 1.0, 0.0).astype(BF16)
    hi, lo = _split_bf16(jnp.broadcast_to(col, (r, 128)))
    return (_dot(lower_bf, hi) + _dot(lower_bf, lo))[:, 0:1]


def _slot_masks(lpos, chunk, tb):
    slot = lax.broadcasted_iota(I32, (SORT_CHUNK, tb), 0) + chunk * SORT_CHUNK
    return slot == lpos[:, 0:tb], slot == lpos[:, tb:2 * tb]


def _earlier_same_expert(ones):
    seg = RANK_SEGMENT
    n_exp, n = ones.shape
    upper = lax.broadcasted_iota(I32, (seg, seg), 0) < lax.broadcasted_iota(I32, (seg, seg), 1)
    pieces = [ones[:, s * seg:(s + 1) * seg] for s in range(n // seg)]
    within = _dot(jnp.concatenate([p.astype(BF16) for p in pieces], axis=0), jnp.where(upper, 1.0, 0.0).astype(BF16))
    seen = jnp.zeros((n_exp, 1), F32)
    out = []
    for s, p in enumerate(pieces):
        out.append(within[s * n_exp:(s + 1) * n_exp, :] + seen)
        seen = seen + jnp.sum(p, axis=1, keepdims=True)
    return jnp.concatenate(out, axis=1), seen


def _local_sort(eid, wts, h2, xb_ref, cnt_ref, lpos_ref):
    tb, d = h2.shape
    na = TOP_K * tb
    e_all = jnp.concatenate([eid[k:k + 1] for k in range(TOP_K)], axis=1)
    onehot = lax.broadcasted_iota(I32, (N_EXPERTS, na), 0) == e_all
    earlier, cnt = _earlier_same_expert(jnp.where(onehot, 1.0, 0.0))
    cnt_pad = jnp.floor((cnt + (GROUP - 1)) * (1.0 / GROUP)) * GROUP
    start = _excl_cumsum_rows(cnt_pad)
    lpos = jnp.sum(jnp.where(onehot, start + earlier, 0.0), axis=0, keepdims=True).astype(I32)
    hb = h2.astype(BF16)
    n_chunks = LOCAL_ROWS // SORT_CHUNK

    def sort_chunk(c):
        m0, m1 = _slot_masks(lpos, c, tb)
        p = jnp.where(m0, 1.0, jnp.where(m1, 1.0, 0.0)).astype(BF16)
        pw = jnp.where(m0, wts[0:1], jnp.where(m1, wts[1:2], 0.0))
        rows = slice(c * SORT_CHUNK, (c + 1) * SORT_CHUNK)
        xb_ref[rows, 0:d] = _dot(p, hb).astype(BF16)
        w_row = jnp.broadcast_to(jnp.sum(pw, axis=1, keepdims=True), (SORT_CHUNK, W_LANES))
        w_hi = w_row.astype(BF16).astype(F32)
        low_half = lax.broadcasted_iota(I32, (SORT_CHUNK, W_LANES), 1) < W_LANES // 2
        xb_ref[rows, d:d + W_LANES] = jnp.where(low_half, w_hi, w_row - w_hi).astype(BF16)

    for c in range(n_chunks - 1):
        sort_chunk(c)
    last_used = jnp.sum(cnt_pad) > (n_chunks - 1) * SORT_CHUNK

    @pl.when(last_used)
    def _():
        sort_chunk(n_chunks - 1)

    @pl.when(jnp.logical_not(last_used))
    def _():
        xb_ref[(n_chunks - 1) * SORT_CHUNK:, :] = jnp.zeros((SORT_CHUNK, d + W_LANES), BF16)

    cnt_ref[...] = cnt_pad.astype(I32)
    lpos_ref[...] = lpos


def _attn_kernel(x1_ref, gxa_ref, wq_ref, k_ref, v_ref, wo_ref, gffn_ref, wr_ref, br_ref,
                 x2_ref, xb_ref, cnt_ref, lpos_ref):
    x1 = x1_ref[...]
    d = x1.shape[1]
    hd = d // XA_HEADS
    q = _dot(_rms(x1, gxa_ref[...]).astype(BF16), wq_ref[...]).astype(BF16)
    heads = []
    for h in range(XA_HEADS):
        hs = slice(h * hd, (h + 1) * hd)
        sc = _dot_nt(q[:, hs], k_ref[:, hs]) * (hd ** -0.5)
        p = jnp.exp(sc - jnp.max(sc, axis=-1, keepdims=True))
        p = p / jnp.sum(p, axis=-1, keepdims=True)
        heads.append(_dot(p.astype(BF16), v_ref[:, hs]).astype(BF16))
    x2 = x1 + _dot(jnp.concatenate(heads, axis=1), wo_ref[...])
    x2_ref[...] = x2
    h2 = _rms(x2, gffn_ref[...])
    h_hi, h_lo = _split_bf16(h2)
    w_hi, w_lo = _split_bf16(wr_ref[...])
    lt = _dot_nt(w_hi, h_hi) + (_dot_nt(w_hi, h_lo) + _dot_nt(w_lo, h_hi)) + br_ref[...]
    eid, wts = _route(lt)
    _local_sort(eid, wts, h2, xb_ref, cnt_ref, lpos_ref)


def _attn(x1, gxa, wq_bf, k_bf, v_bf, wo_bf, gffn, w_route_t, b_route, batch):
    n, d = x1.shape
    m = k_bf.shape[0] // batch
    tb = TOKEN_BLOCK
    nsb = n // batch // tb
    nb = n // tb
    row = lambda b, s: (b * nsb + s, 0)
    blk3 = lambda b, s: (b * nsb + s, 0, 0)
    fixed = lambda b, s: (0, 0)
    mem = lambda b, s: (b, 0)
    return pl.pallas_call(
        _attn_kernel,
        grid=(batch, nsb),
        in_specs=[pl.BlockSpec((tb, d), row), pl.BlockSpec((1, d), fixed), pl.BlockSpec((d, d), fixed),
                  pl.BlockSpec((m, d), mem), pl.BlockSpec((m, d), mem), pl.BlockSpec((d, d), fixed),
                  pl.BlockSpec((1, d), fixed), pl.BlockSpec((ROUTE_ROWS, d), fixed),
                  pl.BlockSpec((ROUTE_ROWS, 1), fixed)],
        out_specs=[pl.BlockSpec((tb, d), row), pl.BlockSpec((LOCAL_ROWS, d + W_LANES), row),
                   pl.BlockSpec((None, N_EXPERTS, 1), blk3), pl.BlockSpec((None, 1, TOP_K * tb), blk3)],
        out_shape=[jax.ShapeDtypeStruct((n, d), F32), jax.ShapeDtypeStruct((nb * LOCAL_ROWS, d + W_LANES), BF16),
                   jax.ShapeDtypeStruct((nb, N_EXPERTS, 1), I32), jax.ShapeDtypeStruct((nb, 1, TOP_K * tb), I32)],
        compiler_params=_params("parallel", "parallel"),
        name="attn",
    )(x1, gxa, wq_bf, k_bf, v_bf, wo_bf, gffn, w_route_t, b_route)


def _plan_kernel(cnt_ref, te_ref, src_ref, dst_ref, nact_ref):
    runs = cnt_ref.shape[0]
    lg = LOCAL_ROWS // GROUP
    tg = MOE_TILE // GROUP
    sh = N_EXPERTS.bit_length() - 1
    emask = N_EXPERTS - 1
    zero_group = lg - 1

    length = cnt_ref[...].astype(F32) * (1.0 / GROUP)
    len_bf = jnp.broadcast_to(length, (runs, 128)).astype(BF16)
    ri = lax.broadcasted_iota(I32, (runs, runs), 0)
    ci = lax.broadcasted_iota(I32, (runs, runs), 1)
    r_e, c_e = ri & emask, ci & emask
    r_b, c_b = lax.shift_right_logical(ri, sh), lax.shift_right_logical(ci, sh)
    same_expert_earlier = jnp.where(r_e == c_e, jnp.where(c_b < r_b, 1.0, 0.0), 0.0).astype(BF16)
    same_block_earlier = jnp.where(r_b == c_b, jnp.where(c_e < r_e, 1.0, 0.0), 0.0).astype(BF16)
    before = _dot(same_expert_earlier, len_bf)[:, 0:1]
    local = _dot(same_block_earlier, len_bf)[:, 0:1]

    of_expert = (lax.broadcasted_iota(I32, (N_EXPERTS, runs), 1) & emask) == lax.broadcasted_iota(
        I32, (N_EXPERTS, runs), 0)
    total = _dot(jnp.where(of_expert, 1.0, 0.0).astype(BF16), len_bf)[:, 0:1]
    tiles = jnp.floor((total + (tg - 1)) * (1.0 / tg))
    tile0 = _excl_cumsum_rows(tiles)
    to_run = (lax.broadcasted_iota(I32, (runs, N_EXPERTS), 0) & emask) == lax.broadcasted_iota(
        I32, (runs, N_EXPERTS), 1)
    to_run_bf = jnp.where(to_run, 1.0, 0.0).astype(BF16)
    base_hi, base_lo = _split_bf16(jnp.broadcast_to(tile0 * tg, (N_EXPERTS, 128)))
    g_start = (_dot(to_run_bf, base_hi) + _dot(to_run_bf, base_lo))[:, 0:1] + before
    block = lax.shift_right_logical(lax.broadcasted_iota(I32, (runs, 1), 0), sh).astype(F32)
    l_start = block * lg + local

    def cover(out_ref, start, offset, default):
        stop = start + length
        shift = offset - default
        n_out = out_ref.shape[1]
        chunk = next(c for c in PLAN_CHUNKS if n_out % c == 0)
        for c in range(n_out // chunk):
            j = (lax.broadcasted_iota(I32, (runs, chunk), 1) + c * chunk).astype(F32)
            hit = jnp.where(start <= j, jnp.where(j < stop, j + shift, 0.0), 0.0)
            out = jnp.sum(hit, axis=0, keepdims=True) + default
            out_ref[:, c * chunk:(c + 1) * chunk] = out.astype(I32)

    cover(src_ref, g_start, l_start - g_start, float(zero_group))
    cover(dst_ref, l_start, g_start - l_start, 0.0)
    t = lax.broadcasted_iota(I32, (N_EXPERTS, te_ref.shape[1]), 1).astype(F32)
    te_ref[...] = jnp.sum(jnp.where(t >= tile0 + tiles, 1.0, 0.0), axis=0, keepdims=True).astype(I32)
    nact_ref[...] = jnp.sum(jnp.broadcast_to(tiles, (N_EXPERTS, 128)), axis=0, keepdims=True).astype(I32)


def _plan(cnt, n_tiles):
    runs = cnt.shape[0]
    n_src = n_tiles * (MOE_TILE // GROUP)
    n_dst = runs // N_EXPERTS * (LOCAL_ROWS // GROUP)
    n_te = -(-n_tiles // 128) * 128
    te, src, dst, nact = pl.pallas_call(
        _plan_kernel,
        out_shape=[jax.ShapeDtypeStruct((1, n_te), I32), jax.ShapeDtypeStruct((1, n_src), I32),
                   jax.ShapeDtypeStruct((1, n_dst), I32), jax.ShapeDtypeStruct((1, 128), I32)],
        compiler_params=_params(),
        name="plan",
    )(cnt)
    return te.reshape(-1), src.reshape(-1), dst.reshape(-1), nact[0, :1]


def _group_copy(src_hbm, src_group, dst_buf, slot, index, sem):
    start = src_group * GROUP
    rows = pl.ds(start if isinstance(start, int) else pl.multiple_of(start, GROUP), GROUP)
    return pltpu.make_async_copy(src_hbm.at[rows], dst_buf.at[slot, pl.ds(index * GROUP, GROUP)], sem.at[slot])


def _experts_kernel(te_ref, src_ref, nact_ref, xb_ref, wg_hbm, wu_hbm, wd_hbm, ys_ref,
                    xbuf, sem, obuf, osem, zbuf, zsem, wg_st, wu_st, wd_st, wsem, wg_bf, wu_bf, wd_bf):
    tg = MOE_TILE // GROUP
    xring, oring = xbuf.shape[0], obuf.shape[0]
    tm, d = obuf.shape[1], obuf.shape[2]
    nact = nact_ref[0]
    n_tiles = ys_ref.shape[0] // tm - oring

    def start_gather(position):
        tile = jnp.minimum(position, nact - 1)
        slot = lax.rem(position, xring)
        for i in range(tg):
            _group_copy(xb_ref, src_ref[tile * tg + i], xbuf, slot, i, sem).start()

    def wait_gather(position):
        slot = lax.rem(position, xring)
        for i in range(tg):
            _group_copy(xb_ref, 0, xbuf, slot, i, sem).wait()

    def weight_copies(expert):
        pairs = ((wg_hbm, wg_st), (wu_hbm, wu_st), (wd_hbm, wd_st))
        return [pltpu.make_async_copy(w.at[expert], st, wsem.at[i]) for i, (w, st) in enumerate(pairs)]

    def out_copy(tile, slot):
        rows = pl.ds(pl.multiple_of(tile * tm, tm), tm)
        return pltpu.make_async_copy(obuf.at[slot], ys_ref.at[rows], osem.at[slot])

    def zero_copy(tile):
        rows = pl.ds(pl.multiple_of(tile * tm, tm), tm)
        return pltpu.make_async_copy(zbuf, ys_ref.at[rows], zsem)

    def zero_start(t, carry):
        zero_copy(t).start()
        return carry

    def zero_wait(t, carry):
        zero_copy(t).wait()
        return carry

    zbuf[...] = jnp.zeros_like(zbuf)
    obuf[...] = jnp.zeros_like(obuf)
    for s in range(oring):
        out_copy(n_tiles + s, s).start()
    lax.fori_loop(nact, n_tiles, zero_start, 0)

    @pl.when(nact > 0)
    def _():
        for a in range(GATHER_AHEAD):
            start_gather(jnp.int32(a))
        for c in weight_copies(te_ref[0]):
            c.start()

    def compute(t, count):
        for k in range(count):
            wait_gather(t + k)
            out_copy(t, lax.rem(t + k, oring)).wait()
        xs = [xbuf[lax.rem(t + k, xring)] for k in range(count)]
        x = jnp.concatenate([v[:, 0:d] for v in xs], axis=0)
        hid = jax.nn.silu(_dot(x, wg_bf[...])) * _dot(x, wu_bf[...])
        for k in range(count):
            start_gather(t + GATHER_AHEAD + k)
        weight = jnp.concatenate([v[:, d:d + 1].astype(F32) + v[:, d + W_LANES // 2:d + W_LANES // 2 + 1].astype(F32)
                                  for v in xs], axis=0)
        y = (_dot(hid.astype(BF16), wd_bf[...]) * weight).astype(BF16)
        for k in range(count):
            slot = lax.rem(t + k, oring)
            obuf[slot] = y[k * tm:(k + 1) * tm]
            out_copy(t + k, slot).start()

    def step(t):
        e = te_ref[t]

        @pl.when(jnp.logical_or(t == 0, e != te_ref[jnp.maximum(t - 1, 0)]))
        def _():
            for c in weight_copies(e):
                c.wait()
            wg_bf[...] = wg_st[...].astype(BF16)
            wu_bf[...] = wu_st[...].astype(BF16)
            wd_bf[...] = wd_st[...].astype(BF16)
            nxt = lax.while_loop(lambda j: jnp.logical_and(j < nact, te_ref[jnp.minimum(j, n_tiles - 1)] == e),
                                 lambda j: j + 1, t + 1)

            @pl.when(nxt < nact)
            def _():
                for c in weight_copies(te_ref[jnp.minimum(nxt, n_tiles - 1)]):
                    c.start(priority=1)

        pair = jnp.logical_and(t + 1 < nact, te_ref[jnp.minimum(t + 1, n_tiles - 1)] == e)

        @pl.when(pair)
        def _():
            compute(t, 2)

        @pl.when(jnp.logical_not(pair))
        def _():
            compute(t, 1)

        return t + jnp.where(pair, 2, 1)

    lax.while_loop(lambda t: t < nact, step, jnp.int32(0))

    @pl.when(nact > 0)
    def _():
        for a in range(GATHER_AHEAD):
            wait_gather(nact + a)
    for s in range(oring):
        out_copy(0, s).wait()
    lax.fori_loop(nact, n_tiles, zero_wait, 0)


def _experts(te, src, nact, xb, w_gate, w_up, w_down, n_tiles):
    dw = xb.shape[1]
    d = dw - W_LANES
    ff = w_gate.shape[2]
    tm = MOE_TILE
    hbm = pl.BlockSpec(memory_space=pl.ANY)
    return pl.pallas_call(
        _experts_kernel,
        grid_spec=pltpu.PrefetchScalarGridSpec(
            num_scalar_prefetch=3,
            grid=(1,),
            in_specs=[hbm, hbm, hbm, hbm],
            out_specs=hbm,
            scratch_shapes=[pltpu.VMEM((GATHER_AHEAD + 2, tm, dw), BF16), pltpu.SemaphoreType.DMA((GATHER_AHEAD + 2,)),
                            pltpu.VMEM((OUT_RING, tm, d), BF16), pltpu.SemaphoreType.DMA((OUT_RING,)),
                            pltpu.VMEM((tm, d), BF16), pltpu.SemaphoreType.DMA(()),
                            pltpu.VMEM((d, ff), F32), pltpu.VMEM((d, ff), F32), pltpu.VMEM((ff, d), F32),
                            pltpu.SemaphoreType.DMA((3,)),
                            pltpu.VMEM((d, ff), BF16), pltpu.VMEM((d, ff), BF16), pltpu.VMEM((ff, d), BF16)],
        ),
        out_shape=jax.ShapeDtypeStruct(((n_tiles + OUT_RING) * tm, d), BF16),
        compiler_params=_params("arbitrary"),
        name="experts",
    )(te, src, nact, xb, w_gate, w_up, w_down)


def _combine_kernel(dst_ref, lpos_ref, x2_ref, g_ref, ys_ref, out_ref, ybuf, sem):
    b = pl.program_id(0)
    tb = x2_ref.shape[0]
    lg = LOCAL_ROWS // GROUP
    nb = dst_ref.shape[0] // lg
    ring = GATHER_AHEAD + 1
    slot = lax.rem(b, ring)

    def start_gather(blk):
        for i in range(lg):
            _group_copy(ys_ref, dst_ref[blk * lg + i], ybuf, lax.rem(blk, ring), i, sem).start()

    @pl.when(b == 0)
    def _():
        for a in range(min(GATHER_AHEAD, nb)):
            start_gather(a)

    @pl.when(b + GATHER_AHEAD < nb)
    def _():
        start_gather(b + GATHER_AHEAD)

    for i in range(lg):
        _group_copy(ys_ref, 0, ybuf, slot, i, sem).wait()
    lpos = lpos_ref[...]
    y = jnp.zeros(x2_ref.shape, F32)
    for c in range(LOCAL_ROWS // SORT_CHUNK):
        m0, m1 = _slot_masks(lpos, c, tb)
        p = jnp.where(m0, 1.0, jnp.where(m1, 1.0, 0.0)).astype(BF16)
        y = y + _dot_tn(p, ybuf[slot, c * SORT_CHUNK:(c + 1) * SORT_CHUNK, :])
    out_ref[...] = _rms(x2_ref[...] + y, g_ref[...])


def _combine(dst, lpos, x2, g_final, ys):
    n, d = x2.shape
    tb = TOKEN_BLOCK
    return pl.pallas_call(
        _combine_kernel,
        grid_spec=pltpu.PrefetchScalarGridSpec(
            num_scalar_prefetch=1,
            grid=(n // tb,),
            in_specs=[pl.BlockSpec((None, 1, TOP_K * tb), lambda i, dst: (i, 0, 0)),
                      pl.BlockSpec((tb, d), lambda i, dst: (i, 0)), pl.BlockSpec((1, d), lambda i, dst: (0, 0)),
                      pl.BlockSpec(memory_space=pl.ANY)],
            out_specs=pl.BlockSpec((tb, d), lambda i, dst: (i, 0)),
            scratch_shapes=[pltpu.VMEM((GATHER_AHEAD + 1, LOCAL_ROWS, d), BF16),
                            pltpu.SemaphoreType.DMA((GATHER_AHEAD + 1,))],
        ),
        out_shape=jax.ShapeDtypeStruct((n, d), F32),
        compiler_params=_params("arbitrary"),
        name="combine",
    )(dst, lpos, x2, g_final, ys)


def kernel(x, mem, norm_mix_g, w_in, conv_w, conv_b, conv_ln_g, conv_ln_b, conv_w_out, hgrn_lb_logits, hgrn_onorm_g, hgrn_w_out, w_mix_out, norm_xa_g, norm_mem_g, xa_w_q, xa_w_k, xa_w_v, xa_w_o, norm_ffn_g, router_group_w, router_group_b, router_expert_w, router_expert_b, moe_w_gate, moe_w_up, moe_w_down, final_norm_g):
    batch, seq, d = x.shape
    n = batch * seq
    assert w_in.shape[0] == 1, "the final RMSNorm is fused into the single layer's combine step"
    cw = conv_w.shape[2]
    kw = hgrn_w_out.shape[1]
    assert kw == HGRN_HEADS * HGRN_DIM and conv_w.shape[1] == CONV_K
    assert seq % TOKEN_BLOCK == 0 and TOKEN_BLOCK % CHUNK == 0 and TOKEN_BLOCK % CONV_ROWS == 0
    assert moe_w_gate.shape[1] == N_EXPERTS and router_group_w.shape[2] == N_GROUPS and TOP_K == 2
    na = TOP_K * TOKEN_BLOCK
    assert LOCAL_ROWS % SORT_CHUNK == 0 and LOCAL_ROWS >= na + N_EXPERTS * (GROUP - 1) + GROUP

    n_tiles = -(-(n // TOKEN_BLOCK) * (na + N_EXPERTS * (GROUP - 1)) // MOE_TILE) + N_EXPERTS
    n_tiles = -(-n_tiles // 16) * 16
    vec = lambda p: p.reshape(1, -1)
    l = 0

    x2d = x.reshape(n, d)
    x1 = _mixer(x2d, vec(norm_mix_g[l]), w_in[l], conv_w[l], vec(conv_b[l]), vec(conv_ln_g[l]), vec(conv_ln_b[l]),
                conv_w_out[l].astype(BF16), hgrn_lb_logits[l:l + 2], vec(hgrn_onorm_g[l]),
                hgrn_w_out[l].astype(BF16), w_mix_out[l].astype(BF16), cw, kw, batch)
    k_bf, v_bf = _memkv(mem.reshape(-1, d), vec(norm_mem_g[l]), xa_w_k[l].astype(BF16), xa_w_v[l].astype(BF16), batch)
    pad = ROUTE_ROWS - N_GROUPS - N_EXPERTS
    w_route_t = jnp.pad(jnp.concatenate([router_group_w[l], router_expert_w[l]], axis=1).T, ((0, pad), (0, 0)))
    b_route = jnp.pad(jnp.concatenate([router_group_b[l], router_expert_b[l]]), (0, pad)).reshape(ROUTE_ROWS, 1)
    x2, xb, cnt, lpos = _attn(x1, vec(norm_xa_g[l]), xa_w_q[l].astype(BF16), k_bf, v_bf, xa_w_o[l].astype(BF16),
                              vec(norm_ffn_g[l]), w_route_t, b_route, batch)
    te, src, dst, nact = _plan(cnt.reshape(-1, 1), n_tiles)
    ys = _experts(te, src, nact, xb, moe_w_gate[l], moe_w_up[l], moe_w_down[l], n_tiles)
    out = _combine(dst, lpos, x2, vec(final_norm_g), ys)
    return out.reshape(batch, seq, d)
```

```python
import functools

import jax
import jax.numpy as jnp
from jax import lax
from jax.experimental import pallas as pl
from jax.experimental.pallas import tpu as pltpu

F32 = jnp.float32
BF16 = jnp.bfloat16
I32 = jnp.int32

EPS = 1e-6
CONV_K = 31
CONV_HALO = 32
CONV_ROWS = 64
HGRN_HEADS = 4
HGRN_DIM = 128
CHUNK = 64
SUB = 16
XA_HEADS = 4
N_GROUPS = 4
EXPERTS_PER_GROUP = 8
N_EXPERTS = N_GROUPS * EXPERTS_PER_GROUP
TOP_K = 2
ROUTE_ROWS = 40
TOKEN_BLOCK = 512
MOE_TILE = 256
GROUP = 16
LOCAL_ROWS = 1536
SORT_CHUNK = 256
RANK_SEGMENT = 128
GATHER_AHEAD = 3
W_LANES = 128
PLAN_CHUNKS = (512, 256, 128)
VMEM_LIMIT_BYTES = 48 * 1024 * 1024


def _rms(x, g):
    return x * lax.rsqrt(jnp.mean(x * x, axis=-1, keepdims=True) + EPS) * g


def _dot(a, b):
    return jnp.dot(a, b, preferred_element_type=F32)


def _dot_nt(a, b):
    return lax.dot_general(a, b, (((1,), (1,)), ((), ())), preferred_element_type=F32)


def _dot_tn(a, b):
    return lax.dot_general(a, b, (((0,), (0,)), ((), ())), preferred_element_type=F32)


def _split_bf16(x):
    hi = x.astype(BF16)
    lo = (x - hi.astype(F32)).astype(BF16)
    return hi, lo


def _params(*sem):
    return pltpu.CompilerParams(dimension_semantics=sem, vmem_limit_bytes=VMEM_LIMIT_BYTES)


def _conv_stages(un, first, cw_ref, cb_ref, lg_ref, lb_ref, ext_ref, halo_ref, perm_ref, act_ref):
    tb, c = un.shape
    nt = tb // 8
    slabs = c // 128
    hr = CONV_HALO * 8
    lanes = [slice(l * 128, (l + 1) * 128) for l in range(slabs)]

    @pl.when(first)
    def _():
        halo_ref[...] = jnp.zeros_like(halo_ref)

    per = nt // 8
    for j in range(nt):
        start = hr + (j % per) * 64 + j // per
        for l in range(slabs):
            ext_ref[l, pl.ds(start, 8, stride=8), :] = un[8 * j:8 * j + 8, lanes[l]]
    first_row = lax.broadcasted_iota(I32, (hr, 128), 0) % 8 == 0
    for l in range(slabs):
        cur = ext_ref[l, nt * 8:nt * 8 + hr, :]
        ext_ref[l, 0:hr, :] = jnp.where(first_row, pltpu.roll(halo_ref[l], hr - 7, axis=0),
                                        pltpu.roll(cur, 1, axis=0))
        halo_ref[l] = cur

    def row_tile(r):
        accs = []
        for l in range(slabs):
            acc = jnp.broadcast_to(cb_ref[:, lanes[l]], (CONV_ROWS, 128))
            for dt in range(CONV_K):
                off = hr + r * CONV_ROWS - dt * 8
                acc = acc + cw_ref[CONV_K - 1 - dt:CONV_K - dt, lanes[l]] * ext_ref[l, off:off + CONV_ROWS, :]
            accs.append(acc)
        mu = functools.reduce(jnp.add, [jnp.sum(a, axis=-1, keepdims=True) for a in accs]) * (1.0 / c)
        cens = [a - mu for a in accs]
        var = functools.reduce(jnp.add, [jnp.sum(a * a, axis=-1, keepdims=True) for a in cens]) * (1.0 / c)
        inv = lax.rsqrt(var + EPS)
        for l in range(slabs):
            ln = cens[l] * inv * lg_ref[:, lanes[l]] + lb_ref[:, lanes[l]]
            perm_ref[l, r * CONV_ROWS:(r + 1) * CONV_ROWS, :] = jax.nn.silu(ln)

    def finish():
        for j in range(nt):
            start = (j % per) * 64 + j // per
            for l in range(slabs):
                act_ref[8 * j:8 * j + 8, lanes[l]] = perm_ref[l, pl.ds(start, 8, stride=8), :]

    return [functools.partial(row_tile, r) for r in range(tb // CONV_ROWS)], finish


def _mixin_kernel(x_ref, g_ref, w_hbm, cw_ref, cb_ref, lg_ref, lb_ref, wo_ref,
                  yc_ref, q_ref, fr_ref, iv_ref, og_ref, gate_ref,
                  w_ref, w_stage, w_sem, ext_ref, halo_ref, perm_ref, act_ref, *, cw, kw):
    d = yc_ref.shape[1]

    @pl.when(jnp.logical_and(pl.program_id(0) == 0, pl.program_id(1) == 0))
    def _():
        width = w_stage.shape[2]

        def fetch(c):
            return pltpu.make_async_copy(w_hbm.at[:, pl.ds(c * width, width)], w_stage.at[c % 2], w_sem.at[c % 2])

        n_fetch = w_ref.shape[1] // width
        fetch(0).start()
        for c in range(n_fetch):
            if c + 1 < n_fetch:
                fetch(c + 1).start()
            fetch(c).wait()
            w_ref[:, c * width:(c + 1) * width] = w_stage[c % 2].astype(BF16)

    hb = _rms(x_ref[...], g_ref[...]).astype(BF16)

    def proj(lo, width):
        return _dot(hb, w_ref[:, lo:lo + width])

    conv_tiles, conv_finish = _conv_stages(proj(0, cw) * jax.nn.sigmoid(proj(cw, cw)), pl.program_id(1) == 0,
                                           cw_ref, cb_ref, lg_ref, lb_ref, ext_ref, halo_ref, perm_ref, act_ref)

    def chunk(ref, col, lo, act):
        def run():
            ref[:, col * kw:(col + 1) * kw] = act(proj(lo, kw)).astype(ref.dtype)
        return run

    base = 2 * cw
    gbase = base + 4 * kw
    chunks = [chunk(q_ref, 0, base, jax.nn.silu), chunk(fr_ref, 0, base + kw, lambda v: v),
              chunk(iv_ref, 0, base + 2 * kw, lambda v: v), chunk(og_ref, 0, base + 3 * kw, jax.nn.silu)]
    for c in range(d // kw):
        chunks.append(chunk(gate_ref, c, gbase + d + c * kw, jax.nn.sigmoid))
        chunks.append(chunk(yc_ref, c, gbase + c * kw, jax.nn.sigmoid))
    for i in range(max(len(chunks), len(conv_tiles))):
        if i < len(chunks):
            chunks[i]()
        if i < len(conv_tiles):
            conv_tiles[i]()
    conv_finish()
    y_conv = _dot(act_ref[...].astype(BF16), wo_ref[...])
    yc_ref[...] = (yc_ref[...].astype(F32) * y_conv).astype(BF16)


def _hgrn_block(q_ref, fr_ref, iv_ref, lb, st_ref, tri):
    tb = q_ref.shape[0]
    chunks = [slice(c * CHUNK, (c + 1) * CHUNK) for c in range(tb // CHUNK)]
    heads = [slice(h * HGRN_DIM, (h + 1) * HGRN_DIM) for h in range(HGRN_HEADS)]

    qs, vs, kks, cums = [], [], [], []
    for rows in chunks:
        f = lb + (1.0 - lb) * jax.nn.sigmoid(fr_ref[rows, :])
        lf_hi, lf_lo = _split_bf16(jnp.log(f))
        cums.append(_dot(tri, jnp.concatenate([lf_hi, lf_lo], axis=0)))
        kks.append(1.0 - f)
        qs.append(q_ref[rows, :].astype(F32))
        vs.append(iv_ref[rows, :])

    qes, kds, decays, blocks = [], [], [], []
    for q, kk, cum in zip(qs, kks, cums):
        last = cum[CHUNK - 1:CHUNK, :]
        qes.append((q * jnp.exp(cum)).astype(BF16))
        kds.append((kk * jnp.exp(last - cum)).astype(BF16))
        decays.append(jnp.exp(last))
        sub = []
        for i in range(CHUNK // SUB):
            rs, ne = i * SUB, (i + 1) * SUB
            ref = cum[rs + SUB // 2 - 1:rs + SUB // 2, :]
            qt = (q[rs:ne] * jnp.exp(cum[rs:ne] - ref)).astype(BF16)
            kt = (kk[0:ne] * jnp.exp(ref - cum[0:ne])).astype(BF16)
            sub.append((qt, kt))
        blocks.append(sub)

    updates = [[_dot_tn(v[:, hs], kd[:, hs]) for hs in heads] for v, kd in zip(vs, kds)]
    scores = [[[_dot_nt(qt[:, hs], kt[:, hs]) for qt, kt in sub] for hs in heads] for sub in blocks]

    states = [st_ref[h] for h in range(HGRN_HEADS)]
    inter = []
    for qe, decay, upd in zip(qes, decays, updates):
        inter.append([_dot_nt(qe[:, hs], st.astype(BF16)) for hs, st in zip(heads, states)])
        states = [st * decay[:, hs] + u for st, hs, u in zip(states, heads, upd)]
    for h in range(HGRN_HEADS):
        st_ref[h] = states[h]

    outs = []
    for v, sc, o_inter in zip(vs, scores, inter):
        per_head = []
        for h, hs in enumerate(heads):
            parts = []
            for i, a in enumerate(sc[h]):
                rs, ne = i * SUB, (i + 1) * SUB
                trow = lax.broadcasted_iota(I32, (SUB, ne), 0) + rs
                scol = lax.broadcasted_iota(I32, (SUB, ne), 1)
                a = jnp.where(scol <= trow, a, 0.0).astype(BF16)
                parts.append(_dot(a, v[0:ne, hs]))
            per_head.append(o_inter[h] + jnp.concatenate(parts, axis=0))
        outs.append(jnp.concatenate(per_head, axis=1))
    return outs


def _hgrn_kernel(q_ref, fr_ref, iv_ref, og_ref, gate_ref, yc_ref, x_ref, lbl_ref, on_ref, wo_ref, wm_ref,
                 x1_ref, st_ref, ob_ref):
    tb = q_ref.shape[0]

    @pl.when(pl.program_id(1) == 0)
    def _():
        st_ref[...] = jnp.zeros_like(st_ref)

    l0, l1 = lbl_ref[0:1, :], lbl_ref[1:2, :]
    m = jnp.maximum(l0, l1)
    e0, e1 = jnp.exp(l0 - m), jnp.exp(l1 - m)
    lb_all = e0 / (e0 + e1)
    trow = lax.broadcasted_iota(I32, (CHUNK, 2 * CHUNK), 0)
    tcol = lax.broadcasted_iota(I32, (CHUNK, 2 * CHUNK), 1) % CHUNK
    tri = jnp.where(tcol <= trow, 1.0, 0.0).astype(BF16)

    for c, o in enumerate(_hgrn_block(q_ref, fr_ref, iv_ref, lb_all, st_ref, tri)):
        rows = slice(c * CHUNK, (c + 1) * CHUNK)
        og = og_ref[rows, :].astype(F32)
        for h in range(HGRN_HEADS):
            hs = slice(h * HGRN_DIM, (h + 1) * HGRN_DIM)
            ob_ref[rows, hs] = (_rms(o[:, hs], on_ref[...]) * og[:, hs]).astype(BF16)
    y_rec = _dot(ob_ref[...], wo_ref[...])
    merged = yc_ref[...].astype(F32) + gate_ref[...].astype(F32) * y_rec
    x1_ref[...] = x_ref[...] + _dot(merged.astype(BF16), wm_ref[...])


def _mixer_kernel(x_ref, g_ref, w_hbm, cw_ref, cb_ref, lg_ref, lb_ref, wco_ref, lbl_ref, on_ref, wro_ref, wm_ref,
                  x1_ref, yc_s, q_s, fr_s, iv_s, og_s, gate_s, w_ref, w_stage, w_sem, ext_ref, halo_ref, perm_ref,
                  act_ref, st_ref, ob_ref, *, cw, kw):
    _mixin_kernel(x_ref, g_ref, w_hbm, cw_ref, cb_ref, lg_ref, lb_ref, wco_ref, yc_s, q_s, fr_s, iv_s, og_s, gate_s,
                  w_ref, w_stage, w_sem, ext_ref, halo_ref, perm_ref, act_ref, cw=cw, kw=kw)
    _hgrn_kernel(q_s, fr_s, iv_s, og_s, gate_s, yc_s, x_ref, lbl_ref, on_ref, wro_ref, wm_ref, x1_ref, st_ref, ob_ref)


def _mixer(x2d, g, w_in, conv_w, conv_b, ln_g, ln_b, w_conv_out_bf, lb_logits, onorm_g, w_o_bf, w_mix_bf, cw, kw,
           batch):
    n, d = x2d.shape
    tb = TOKEN_BLOCK
    nsb = n // batch // tb
    row = lambda b, s: (b * nsb + s, 0)
    fixed = lambda b, s: (0, 0)
    return pl.pallas_call(
        functools.partial(_mixer_kernel, cw=cw, kw=kw),
        grid=(batch, nsb),
        in_specs=[pl.BlockSpec((tb, d), row), pl.BlockSpec((1, d), fixed), pl.BlockSpec(memory_space=pl.ANY),
                  pl.BlockSpec(conv_w.shape, fixed), pl.BlockSpec((1, cw), fixed), pl.BlockSpec((1, cw), fixed),
                  pl.BlockSpec((1, cw), fixed), pl.BlockSpec((cw, d), fixed),
                  pl.BlockSpec(lb_logits.shape, fixed), pl.BlockSpec((1, HGRN_DIM), fixed),
                  pl.BlockSpec(w_o_bf.shape, fixed), pl.BlockSpec(w_mix_bf.shape, fixed)],
        out_specs=pl.BlockSpec((tb, d), row),
        out_shape=jax.ShapeDtypeStruct((n, d), F32),
        scratch_shapes=[pltpu.VMEM((tb, d), BF16), pltpu.VMEM((tb, kw), BF16), pltpu.VMEM((tb, kw), F32),
                        pltpu.VMEM((tb, kw), BF16), pltpu.VMEM((tb, kw), BF16), pltpu.VMEM((tb, d), BF16),
                        pltpu.VMEM(w_in.shape, BF16), pltpu.VMEM((2, d, kw), F32), pltpu.SemaphoreType.DMA((2,)),
                        pltpu.VMEM((cw // 128, tb + CONV_HALO * 8, 128), F32),
                        pltpu.VMEM((cw // 128, CONV_HALO * 8, 128), F32),
                        pltpu.VMEM((cw // 128, tb, 128), F32), pltpu.VMEM((tb, cw), F32),
                        pltpu.VMEM((HGRN_HEADS, HGRN_DIM, HGRN_DIM), F32), pltpu.VMEM((tb, kw), BF16)],
        compiler_params=_params("arbitrary", "arbitrary"),
        name="mixer",
    )(x2d, g, w_in, conv_w, conv_b, ln_g, ln_b, w_conv_out_bf, lb_logits, onorm_g, w_o_bf, w_mix_bf)


def _memkv_kernel(mem_ref, g_ref, wk_ref, wv_ref, k_ref, v_ref):
    mb = _rms(mem_ref[...], g_ref[...]).astype(BF16)
    k_ref[...] = _dot(mb, wk_ref[...]).astype(BF16)
    v_ref[...] = _dot(mb, wv_ref[...]).astype(BF16)


def _memkv(mem2d, g, wk_bf, wv_bf, batch):
    n, d = mem2d.shape
    m = n // batch
    row = lambda b: (b, 0)
    fixed = lambda b: (0, 0)
    return pl.pallas_call(
        _memkv_kernel,
        grid=(batch,),
        in_specs=[pl.BlockSpec((m, d), row), pl.BlockSpec((1, d), fixed), pl.BlockSpec((d, d), fixed),
                  pl.BlockSpec((d, d), fixed)],
        out_specs=[pl.BlockSpec((m, d), row)] * 2,
        out_shape=[jax.ShapeDtypeStruct((n, d), BF16)] * 2,
        compiler_params=_params("parallel"),
        name="memkv",
    )(mem2d, g, wk_bf, wv_bf)


def _route(lt):
    def row(r):
        return lt[r:r + 1, :]

    gl = [row(g) for g in range(N_GROUPS)]
    gmax = functools.reduce(jnp.maximum, gl)
    g_p = 1.0 / functools.reduce(jnp.add, [jnp.exp(l - gmax) for l in gl])
    gidx = jnp.full(gmax.shape, N_GROUPS - 1, I32)
    for g in range(N_GROUPS - 2, -1, -1):
        gidx = jnp.where(gl[g] == gmax, g, gidx)

    el = []
    for j in range(EXPERTS_PER_GROUP):
        v = row(N_GROUPS + (N_GROUPS - 1) * EXPERTS_PER_GROUP + j)
        for g in range(N_GROUPS - 2, -1, -1):
            v = jnp.where(gidx == g, row(N_GROUPS + g * EXPERTS_PER_GROUP + j), v)
        el.append(v)

    def argmax(vals):
        mx = functools.reduce(jnp.maximum, vals)
        idx = jnp.full(mx.shape, EXPERTS_PER_GROUP - 1, I32)
        for j in range(EXPERTS_PER_GROUP - 2, -1, -1):
            idx = jnp.where(vals[j] == mx, j, idx)
        return mx, idx

    m1, i1 = argmax(el)
    m2, i2 = argmax([jnp.where(i1 == j, -jnp.inf, el[j]) for j in range(EXPERTS_PER_GROUP)])
    r = jnp.exp(m2 - m1)
    w1 = g_p / (1.0 + r)
    w2 = g_p * r / (1.0 + r)
    base = gidx * EXPERTS_PER_GROUP
    return jnp.concatenate([base + i1, base + i2], axis=0), jnp.concatenate([w1, w2], axis=0)


def _excl_cumsum_rows(col):
    r = col.shape[0]
    lower = lax.broadcasted_iota(I32, (r, r), 1) < lax.broadcasted_iota(I32, (r, r), 0)
    lower_bf = jnp.where(lower, 1.0, 0.0).astype(BF16)
    hi, lo = _split_bf16(jnp.broadcast_to(col, (r, 128)))
    return (_dot(lower_bf, hi) + _dot(lower_bf, lo))[:, 0:1]


def _slot_masks(lpos, chunk, tb):
    slot = lax.broadcasted_iota(I32, (SORT_CHUNK, tb), 0) + chunk * SORT_CHUNK
    return slot == lpos[:, 0:tb], slot == lpos[:, tb:2 * tb]


def _earlier_same_expert(ones):
    seg = RANK_SEGMENT
    n_exp, n = ones.shape
    upper = lax.broadcasted_iota(I32, (seg, seg), 0) < lax.broadcasted_iota(I32, (seg, seg), 1)
    pieces = [ones[:, s * seg:(s + 1) * seg] for s in range(n // seg)]
    within = _dot(jnp.concatenate([p.astype(BF16) for p in pieces], axis=0), jnp.where(upper, 1.0, 0.0).astype(BF16))
    seen = jnp.zeros((n_exp, 1), F32)
    out = []
    for s, p in enumerate(pieces):
        out.append(within[s * n_exp:(s + 1) * n_exp, :] + seen)
        seen = seen + jnp.sum(p, axis=1, keepdims=True)
    return jnp.concatenate(out, axis=1), seen


def _local_sort(eid, wts, h2, xb_ref, cnt_ref, lpos_ref):
    tb, d = h2.shape
    na = TOP_K * tb
    e_all = jnp.concatenate([eid[k:k + 1] for k in range(TOP_K)], axis=1)
    onehot = lax.broadcasted_iota(I32, (N_EXPERTS, na), 0) == e_all
    earlier, cnt = _earlier_same_expert(jnp.where(onehot, 1.0, 0.0))
    cnt_pad = jnp.floor((cnt + (GROUP - 1)) * (1.0 / GROUP)) * GROUP
    start = _excl_cumsum_rows(cnt_pad)
    lpos = jnp.sum(jnp.where(onehot, start + earlier, 0.0), axis=0, keepdims=True).astype(I32)
    hb = h2.astype(BF16)
    n_chunks = LOCAL_ROWS // SORT_CHUNK

    def sort_chunk(c):
        m0, m1 = _slot_masks(lpos, c, tb)
        p = jnp.where(m0, 1.0, jnp.where(m1, 1.0, 0.0)).astype(BF16)
        pw = jnp.where(m0, wts[0:1], jnp.where(m1, wts[1:2], 0.0))
        rows = slice(c * SORT_CHUNK, (c + 1) * SORT_CHUNK)
        xb_ref[rows, 0:d] = _dot(p, hb).astype(BF16)
        w_row = jnp.broadcast_to(jnp.sum(pw, axis=1, keepdims=True), (SORT_CHUNK, W_LANES))
        w_hi = w_row.astype(BF16).astype(F32)
        low_half = lax.broadcasted_iota(I32, (SORT_CHUNK, W_LANES), 1) < W_LANES // 2
        xb_ref[rows, d:d + W_LANES] = jnp.where(low_half, w_hi, w_row - w_hi).astype(BF16)

    for c in range(n_chunks - 1):
        sort_chunk(c)
    last_used = jnp.sum(cnt_pad) > (n_chunks - 1) * SORT_CHUNK

    @pl.when(last_used)
    def _():
        sort_chunk(n_chunks - 1)

    @pl.when(jnp.logical_not(last_used))
    def _():
        xb_ref[(n_chunks - 1) * SORT_CHUNK:, :] = jnp.zeros((SORT_CHUNK, d + W_LANES), BF16)

    cnt_ref[...] = cnt_pad.astype(I32)
    lpos_ref[...] = lpos


def _attn_kernel(x1_ref, gxa_ref, wq_ref, k_ref, v_ref, wo_ref, gffn_ref, wr_ref, br_ref,
                 x2_ref, xb_ref, cnt_ref, lpos_ref):
    x1 = x1_ref[...]
    d = x1.shape[1]
    hd = d // XA_HEADS
    q = _dot(_rms(x1, gxa_ref[...]).astype(BF16), wq_ref[...]).astype(BF16)
    heads = []
    for h in range(XA_HEADS):
        hs = slice(h * hd, (h + 1) * hd)
        sc = _dot_nt(q[:, hs], k_ref[:, hs]) * (hd ** -0.5)
        p = jnp.exp(sc - jnp.max(sc, axis=-1, keepdims=True))
        p = p / jnp.sum(p, axis=-1, keepdims=True)
        heads.append(_dot(p.astype(BF16), v_ref[:, hs]).astype(BF16))
    x2 = x1 + _dot(jnp.concatenate(heads, axis=1), wo_ref[...])
    x2_ref[...] = x2
    h2 = _rms(x2, gffn_ref[...])
    h_hi, h_lo = _split_bf16(h2)
    w_hi, w_lo = _split_bf16(wr_ref[...])
    lt = _dot_nt(w_hi, h_hi) + (_dot_nt(w_hi, h_lo) + _dot_nt(w_lo, h_hi)) + br_ref[...]
    eid, wts = _route(lt)
    _local_sort(eid, wts, h2, xb_ref, cnt_ref, lpos_ref)


def _attn(x1, gxa, wq_bf, k_bf, v_bf, wo_bf, gffn, w_route_t, b_route, batch):
    n, d = x1.shape
    m = k_bf.shape[0] // batch
    tb = TOKEN_BLOCK
    nsb = n // batch // tb
    nb = n // tb
    row = lambda b, s: (b * nsb + s, 0)
    blk3 = lambda b, s: (b * nsb + s, 0, 0)
    fixed = lambda b, s: (0, 0)
    mem = lambda b, s: (b, 0)
    return pl.pallas_call(
        _attn_kernel,
        grid=(batch, nsb),
        in_specs=[pl.BlockSpec((tb, d), row), pl.BlockSpec((1, d), fixed), pl.BlockSpec((d, d), fixed),
                  pl.BlockSpec((m, d), mem), pl.BlockSpec((m, d), mem), pl.BlockSpec((d, d), fixed),
                  pl.BlockSpec((1, d), fixed), pl.BlockSpec((ROUTE_ROWS, d), fixed),
                  pl.BlockSpec((ROUTE_ROWS, 1), fixed)],
        out_specs=[pl.BlockSpec((tb, d), row), pl.BlockSpec((LOCAL_ROWS, d + W_LANES), row),
                   pl.BlockSpec((None, N_EXPERTS, 1), blk3), pl.BlockSpec((None, 1, TOP_K * tb), blk3)],
        out_shape=[jax.ShapeDtypeStruct((n, d), F32), jax.ShapeDtypeStruct((nb * LOCAL_ROWS, d + W_LANES), BF16),
                   jax.ShapeDtypeStruct((nb, N_EXPERTS, 1), I32), jax.ShapeDtypeStruct((nb, 1, TOP_K * tb), I32)],
        compiler_params=_params("parallel", "parallel"),
        name="attn",
    )(x1, gxa, wq_bf, k_bf, v_bf, wo_bf, gffn, w_route_t, b_route)


def _plan_kernel(cnt_ref, te_ref, src_ref, dst_ref, nact_ref):
    runs = cnt_ref.shape[0]
    lg = LOCAL_ROWS // GROUP
    tg = MOE_TILE // GROUP
    sh = N_EXPERTS.bit_length() - 1
    emask = N_EXPERTS - 1
    zero_group = lg - 1

    length = cnt_ref[...].astype(F32) * (1.0 / GROUP)
    len_bf = jnp.broadcast_to(length, (runs, 128)).astype(BF16)
    ri = lax.broadcasted_iota(I32, (runs, runs), 0)
    ci = lax.broadcasted_iota(I32, (runs, runs), 1)
    r_e, c_e = ri & emask, ci & emask
    r_b, c_b = lax.shift_right_logical(ri, sh), lax.shift_right_logical(ci, sh)
    same_expert_earlier = jnp.where(r_e == c_e, jnp.where(c_b < r_b, 1.0, 0.0), 0.0).astype(BF16)
    same_block_earlier = jnp.where(r_b == c_b, jnp.where(c_e < r_e, 1.0, 0.0), 0.0).astype(BF16)
    before = _dot(same_expert_earlier, len_bf)[:, 0:1]
    local = _dot(same_block_earlier, len_bf)[:, 0:1]

    of_expert = (lax.broadcasted_iota(I32, (N_EXPERTS, runs), 1) & emask) == lax.broadcasted_iota(
        I32, (N_EXPERTS, runs), 0)
    total = _dot(jnp.where(of_expert, 1.0, 0.0).astype(BF16), len_bf)[:, 0:1]
    tiles = jnp.floor((total + (tg - 1)) * (1.0 / tg))
    tile0 = _excl_cumsum_rows(tiles)
    to_run = (lax.broadcasted_iota(I32, (runs, N_EXPERTS), 0) & emask) == lax.broadcasted_iota(
        I32, (runs, N_EXPERTS), 1)
    to_run_bf = jnp.where(to_run, 1.0, 0.0).astype(BF16)
    base_hi, base_lo = _split_bf16(jnp.broadcast_to(tile0 * tg, (N_EXPERTS, 128)))
    g_start = (_dot(to_run_bf, base_hi) + _dot(to_run_bf, base_lo))[:, 0:1] + before
    block = lax.shift_right_logical(lax.broadcasted_iota(I32, (runs, 1), 0), sh).astype(F32)
    l_start = block * lg + local

    def cover(out_ref, start, offset, default):
        stop = start + length
        shift = offset - default
        n_out = out_ref.shape[1]
        chunk = next(c for c in PLAN_CHUNKS if n_out % c == 0)
        for c in range(n_out // chunk):
            j = (lax.broadcasted_iota(I32, (runs, chunk), 1) + c * chunk).astype(F32)
            hit = jnp.where(start <= j, jnp.where(j < stop, j + shift, 0.0), 0.0)
            out = jnp.sum(hit, axis=0, keepdims=True) + default
            out_ref[:, c * chunk:(c + 1) * chunk] = out.astype(I32)

    cover(src_ref, g_start, l_start - g_start, float(zero_group))
    cover(dst_ref, l_start, g_start - l_start, 0.0)
    t = lax.broadcasted_iota(I32, (N_EXPERTS, te_ref.shape[1]), 1).astype(F32)
    te_ref[...] = jnp.sum(jnp.where(t >= tile0 + tiles, 1.0, 0.0), axis=0, keepdims=True).astype(I32)
    nact_ref[...] = jnp.sum(jnp.broadcast_to(tiles, (N_EXPERTS, 128)), axis=0, keepdims=True).astype(I32)


def _plan(cnt, n_tiles):
    runs = cnt.shape[0]
    n_src = n_tiles * (MOE_TILE // GROUP)
    n_dst = runs // N_EXPERTS * (LOCAL_ROWS // GROUP)
    n_te = -(-n_tiles // 128) * 128
    te, src, dst, nact = pl.pallas_call(
        _plan_kernel,
        out_shape=[jax.ShapeDtypeStruct((1, n_te), I32), jax.ShapeDtypeStruct((1, n_src), I32),
                   jax.ShapeDtypeStruct((1, n_dst), I32), jax.ShapeDtypeStruct((1, 128), I32)],
        compiler_params=_params(),
        name="plan",
    )(cnt)
    return te.reshape(-1), src.reshape(-1), dst.reshape(-1), nact[0, :1]


def _group_copy(src_hbm, src_group, dst_buf, slot, index, sem):
    start = src_group * GROUP
    rows = pl.ds(start if isinstance(start, int) else pl.multiple_of(start, GROUP), GROUP)
    return pltpu.make_async_copy(src_hbm.at[rows], dst_buf.at[slot, pl.ds(index * GROUP, GROUP)], sem.at[slot])


def _experts_kernel(te_ref, src_ref, nact_ref, xb_ref, wg_hbm, wu_hbm, wd_hbm, ys_ref,
                    xbuf, sem, obuf, osem, zbuf, zsem, wg_st, wu_st, wd_st, wsem, wg_bf, wu_bf, wd_bf):
    tg = MOE_TILE // GROUP
    tm, d = obuf.shape[1], obuf.shape[2]
    nact = nact_ref[0]
    n_tiles = ys_ref.shape[0] // tm - 2

    def start_gather(tile, slot):
        for i in range(tg):
            _group_copy(xb_ref, src_ref[tile * tg + i], xbuf, slot, i, sem).start()

    def wait_gather(slot):
        for i in range(tg):
            _group_copy(xb_ref, 0, xbuf, slot, i, sem).wait()

    def weight_copies(expert):
        pairs = ((wg_hbm, wg_st), (wu_hbm, wu_st), (wd_hbm, wd_st))
        return [pltpu.make_async_copy(w.at[expert], st, wsem.at[i]) for i, (w, st) in enumerate(pairs)]

    def out_copy(tile, slot):
        rows = pl.ds(pl.multiple_of(tile * tm, tm), tm)
        return pltpu.make_async_copy(obuf.at[slot], ys_ref.at[rows], osem.at[slot])

    def zero_copy(tile):
        rows = pl.ds(pl.multiple_of(tile * tm, tm), tm)
        return pltpu.make_async_copy(zbuf, ys_ref.at[rows], zsem)

    def zero_start(t, carry):
        zero_copy(t).start()
        return carry

    def zero_wait(t, carry):
        zero_copy(t).wait()
        return carry

    zbuf[...] = jnp.zeros_like(zbuf)
    obuf[...] = jnp.zeros_like(obuf)
    for s in range(2):
        out_copy(n_tiles + s, s).start()
    lax.fori_loop(nact, n_tiles, zero_start, 0)

    ring = GATHER_AHEAD + 1

    @pl.when(nact > 0)
    def _():
        for a in range(GATHER_AHEAD):
            start_gather(jnp.minimum(a, nact - 1), a)
        for c in weight_copies(te_ref[0]):
            c.start()

    def tile_step(t, carry):
        e = te_ref[t]
        slot = lax.rem(t, 2)
        xslot = lax.rem(t, ring)

        @pl.when(jnp.logical_or(t == 0, e != te_ref[jnp.maximum(t - 1, 0)]))
        def _():
            for c in weight_copies(e):
                c.wait()
            wg_bf[...] = wg_st[...].astype(BF16)
            wu_bf[...] = wu_st[...].astype(BF16)
            wd_bf[...] = wd_st[...].astype(BF16)
            nxt = lax.while_loop(lambda j: jnp.logical_and(j < nact, te_ref[jnp.minimum(j, n_tiles - 1)] == e),
                                 lambda j: j + 1, t + 1)

            @pl.when(nxt < nact)
            def _():
                for c in weight_copies(te_ref[jnp.minimum(nxt, n_tiles - 1)]):
                    c.start()

        wait_gather(xslot)
        out_copy(t, slot).wait()
        x = xbuf[xslot, :, 0:d]
        hid = jax.nn.silu(_dot(x, wg_bf[...])) * _dot(x, wu_bf[...])
        start_gather(jnp.minimum(t + GATHER_AHEAD, nact - 1), lax.rem(t + GATHER_AHEAD, ring))
        weight = (xbuf[xslot, :, d:d + 1].astype(F32)
                  + xbuf[xslot, :, d + W_LANES // 2:d + W_LANES // 2 + 1].astype(F32))
        obuf[slot] = (_dot(hid.astype(BF16), wd_bf[...]) * weight).astype(BF16)
        out_copy(t, slot).start()
        return carry

    lax.fori_loop(0, nact, tile_step, 0)

    @pl.when(nact > 0)
    def _():
        for a in range(GATHER_AHEAD):
            wait_gather(lax.rem(nact + a, ring))
    for s in range(2):
        out_copy(0, s).wait()
    lax.fori_loop(nact, n_tiles, zero_wait, 0)


def _experts(te, src, nact, xb, w_gate, w_up, w_down, n_tiles):
    dw = xb.shape[1]
    d = dw - W_LANES
    ff = w_gate.shape[2]
    tm = MOE_TILE
    hbm = pl.BlockSpec(memory_space=pl.ANY)
    return pl.pallas_call(
        _experts_kernel,
        grid_spec=pltpu.PrefetchScalarGridSpec(
            num_scalar_prefetch=3,
            grid=(1,),
            in_specs=[hbm, hbm, hbm, hbm],
            out_specs=hbm,
            scratch_shapes=[pltpu.VMEM((GATHER_AHEAD + 1, tm, dw), BF16), pltpu.SemaphoreType.DMA((GATHER_AHEAD + 1,)),
                            pltpu.VMEM((2, tm, d), BF16), pltpu.SemaphoreType.DMA((2,)),
                            pltpu.VMEM((tm, d), BF16), pltpu.SemaphoreType.DMA(()),
                            pltpu.VMEM((d, ff), F32), pltpu.VMEM((d, ff), F32), pltpu.VMEM((ff, d), F32),
                            pltpu.SemaphoreType.DMA((3,)),
                            pltpu.VMEM((d, ff), BF16), pltpu.VMEM((d, ff), BF16), pltpu.VMEM((ff, d), BF16)],
        ),
        out_shape=jax.ShapeDtypeStruct(((n_tiles + 2) * tm, d), BF16),
        compiler_params=_params("arbitrary"),
        name="experts",
    )(te, src, nact, xb, w_gate, w_up, w_down)


def _combine_kernel(dst_ref, lpos_ref, x2_ref, g_ref, ys_ref, out_ref, ybuf, sem):
    b = pl.program_id(0)
    tb = x2_ref.shape[0]
    lg = LOCAL_ROWS // GROUP
    nb = dst_ref.shape[0] // lg
    ring = GATHER_AHEAD + 1
    slot = lax.rem(b, ring)

    def start_gather(blk):
        for i in range(lg):
            _group_copy(ys_ref, dst_ref[blk * lg + i], ybuf, lax.rem(blk, ring), i, sem).start()

    @pl.when(b == 0)
    def _():
        for a in range(min(GATHER_AHEAD, nb)):
            start_gather(a)

    @pl.when(b + GATHER_AHEAD < nb)
    def _():
        start_gather(b + GATHER_AHEAD)

    for i in range(lg):
        _group_copy(ys_ref, 0, ybuf, slot, i, sem).wait()
    lpos = lpos_ref[...]
    y = jnp.zeros(x2_ref.shape, F32)
    for c in range(LOCAL_ROWS // SORT_CHUNK):
        m0, m1 = _slot_masks(lpos, c, tb)
        p = jnp.where(m0, 1.0, jnp.where(m1, 1.0, 0.0)).astype(BF16)
        y = y + _dot_tn(p, ybuf[slot, c * SORT_CHUNK:(c + 1) * SORT_CHUNK, :])
    out_ref[...] = _rms(x2_ref[...] + y, g_ref[...])


def _combine(dst, lpos, x2, g_final, ys):
    n, d = x2.shape
    tb = TOKEN_BLOCK
    return pl.pallas_call(
        _combine_kernel,
        grid_spec=pltpu.PrefetchScalarGridSpec(
            num_scalar_prefetch=1,
            grid=(n // tb,),
            in_specs=[pl.BlockSpec((None, 1, TOP_K * tb), lambda i, dst: (i, 0, 0)),
                      pl.BlockSpec((tb, d), lambda i, dst: (i, 0)), pl.BlockSpec((1, d), lambda i, dst: (0, 0)),
                      pl.BlockSpec(memory_space=pl.ANY)],
            out_specs=pl.BlockSpec((tb, d), lambda i, dst: (i, 0)),
            scratch_shapes=[pltpu.VMEM((GATHER_AHEAD + 1, LOCAL_ROWS, d), BF16),
                            pltpu.SemaphoreType.DMA((GATHER_AHEAD + 1,))],
        ),
        out_shape=jax.ShapeDtypeStruct((n, d), F32),
        compiler_params=_params("arbitrary"),
        name="combine",
    )(dst, lpos, x2, g_final, ys)


def kernel(x, mem, norm_mix_g, w_in, conv_w, conv_b, conv_ln_g, conv_ln_b, conv_w_out, hgrn_lb_logits, hgrn_onorm_g, hgrn_w_out, w_mix_out, norm_xa_g, norm_mem_g, xa_w_q, xa_w_k, xa_w_v, xa_w_o, norm_ffn_g, router_group_w, router_group_b, router_expert_w, router_expert_b, moe_w_gate, moe_w_up, moe_w_down, final_norm_g):
    batch, seq, d = x.shape
    n = batch * seq
    assert w_in.shape[0] == 1, "the final RMSNorm is fused into the single layer's combine step"
    cw = conv_w.shape[2]
    kw = hgrn_w_out.shape[1]
    assert kw == HGRN_HEADS * HGRN_DIM and conv_w.shape[1] == CONV_K
    assert seq % TOKEN_BLOCK == 0 and TOKEN_BLOCK % CHUNK == 0 and TOKEN_BLOCK % CONV_ROWS == 0
    assert moe_w_gate.shape[1] == N_EXPERTS and router_group_w.shape[2] == N_GROUPS and TOP_K == 2
    na = TOP_K * TOKEN_BLOCK
    assert LOCAL_ROWS % SORT_CHUNK == 0 and LOCAL_ROWS >= na + N_EXPERTS * (GROUP - 1) + GROUP

    n_tiles = -(-(n // TOKEN_BLOCK) * (na + N_EXPERTS * (GROUP - 1)) // MOE_TILE) + N_EXPERTS
    n_tiles = -(-n_tiles // 16) * 16
    vec = lambda p: p.reshape(1, -1)
    l = 0

    x2d = x.reshape(n, d)
    x1 = _mixer(x2d, vec(norm_mix_g[l]), w_in[l], conv_w[l], vec(conv_b[l]), vec(conv_ln_g[l]), vec(conv_ln_b[l]),
                conv_w_out[l].astype(BF16), hgrn_lb_logits[l:l + 2], vec(hgrn_onorm_g[l]),
                hgrn_w_out[l].astype(BF16), w_mix_out[l].astype(BF16), cw, kw, batch)
    k_bf, v_bf = _memkv(mem.reshape(-1, d), vec(norm_mem_g[l]), xa_w_k[l].astype(BF16), xa_w_v[l].astype(BF16), batch)
    pad = ROUTE_ROWS - N_GROUPS - N_EXPERTS
    w_route_t = jnp.pad(jnp.concatenate([router_group_w[l], router_expert_w[l]], axis=1).T, ((0, pad), (0, 0)))
    b_route = jnp.pad(jnp.concatenate([router_group_b[l], router_expert_b[l]]), (0, pad)).reshape(ROUTE_ROWS, 1)
    x2, xb, cnt, lpos = _attn(x1, vec(norm_xa_g[l]), xa_w_q[l].astype(BF16), k_bf, v_bf, xa_w_o[l].astype(BF16),
                              vec(norm_ffn_g[l]), w_route_t, b_route, batch)
    te, src, dst, nact = _plan(cnt.reshape(-1, 1), n_tiles)
    ys = _experts(te, src, nact, xb, moe_w_gate[l], moe_w_up[l], moe_w_down[l], n_tiles)
    out = _combine(dst, lpos, x2, vec(final_norm_g), ys)
    return out.reshape(batch, seq, d)
```
